```python
import jax, jax.numpy as jnp
from jax import lax
import numpy as np

D_MODEL = 1024
BATCH = 16
SEQ = 4096
DEPTH = 4

CTX_LEN = 256
GRID_W = 64
EPS = 1e-6
NEG_INF = -1e30

HEAD_DIM = 64
ATTN_HEADS = D_MODEL // 128
ATTN_KV_HEADS = ATTN_HEADS // 4
ATTN_GROUP = ATTN_HEADS // ATTN_KV_HEADS
ATTN_WIDTH = ATTN_HEADS * HEAD_DIM
KV_WIDTH = ATTN_KV_HEADS * HEAD_DIM
WINDOW = 128
BLOCK = 128
ROPE_BASE = 10000.0
ROPE_FREQS = HEAD_DIM // 4

SSM_WIDTH = D_MODEL // 2
SSM_GROUP = 16
SSM_GROUPS = SSM_WIDTH // SSM_GROUP
SSM_STATE = 64
DT_MIN = 1e-3
DT_MAX = 1e-1

EVEN_IN = 2 * ATTN_WIDTH + 2 * KV_WIDTH + 2 * SSM_WIDTH
EVEN_MIX = ATTN_WIDTH + SSM_WIDTH

POOL_WIDTH = D_MODEL
POOL_WINDOWS = (2, 4, 8, 16)
POOL_GROUP = POOL_WIDTH // len(POOL_WINDOWS)

kernel_name = 'hybrid_swa_s5_pool_prefix_dit'


def rmsnorm(x, g):
    xf = x.astype(jnp.float32)
    y = xf * lax.rsqrt(jnp.mean(xf * xf, axis=-1, keepdims=True) + EPS)
    return (y * g.astype(jnp.float32)).astype(x.dtype)


def axial_rope_tables(n_tokens):
    rows = n_tokens // GRID_W
    row = jnp.repeat(jnp.arange(rows, dtype=jnp.float32), GRID_W)
    col = jnp.tile(jnp.arange(GRID_W, dtype=jnp.float32), rows)
    inv_freq = ROPE_BASE ** (-jnp.arange(ROPE_FREQS, dtype=jnp.float32) / ROPE_FREQS)
    ang = jnp.stack([row[:, None] * inv_freq, col[:, None] * inv_freq], axis=1)
    ang = ang[:, None]
    return jnp.cos(ang), jnp.sin(ang)


def apply_axial_rope(x, cos, sin):
    b, l, h, _ = x.shape
    xr = x.astype(jnp.float32).reshape(b, l, h, 2, 2, ROPE_FREQS)
    x1, x2 = xr[..., 0, :], xr[..., 1, :]
    out = jnp.stack([x1 * cos - x2 * sin, x2 * cos + x1 * sin], axis=-2)
    return out.reshape(b, l, h, HEAD_DIM).astype(x.dtype)


def window_attention(q, k, v, kc, vc, sink):
    b, l, hkv, g, dh = q.shape
    nb = l // BLOCK
    scale = dh ** -0.5
    qb = q.reshape(b, nb, BLOCK, hkv, g, dh)
    pad = ((0, 0), (BLOCK, BLOCK), (0, 0), (0, 0))
    kp = jnp.pad(k, pad).reshape(b, nb + 2, BLOCK, hkv, dh)
    vp = jnp.pad(v, pad).reshape(b, nb + 2, BLOCK, hkv, dh)
    kb = jnp.concatenate([kp[:, :-2], kp[:, 1:-1], kp[:, 2:]], axis=2)
    vb = jnp.concatenate([vp[:, :-2], vp[:, 1:-1], vp[:, 2:]], axis=2)
    s_loc = jnp.einsum('bnqhgd,bnkhd->bnhgqk', qb, kb).astype(jnp.float32) * scale
    s_ctx = jnp.einsum('bnqhgd,bkhd->bnhgqk', qb, kc).astype(jnp.float32) * scale
    qpos = np.arange(nb)[:, None] * BLOCK + np.arange(BLOCK)[None, :]
    kpos = (np.arange(nb)[:, None] - 1) * BLOCK + np.arange(3 * BLOCK)[None, :]
    valid = ((np.abs(qpos[:, :, None] - kpos[:, None, :]) <= WINDOW)
             & (kpos[:, None, :] >= 0) & (kpos[:, None, :] < l))
    s_loc = jnp.where(valid[None, :, None, None], s_loc, NEG_INF)
    s_sink = jnp.broadcast_to(sink.astype(jnp.float32)[None, None, :, :, None, None],
                              s_loc.shape[:-1] + (1,))
    p = jax.nn.softmax(jnp.concatenate([s_loc, s_ctx, s_sink], axis=-1), axis=-1)
    n_loc = 3 * BLOCK
    n_ctx = kc.shape[1]
    p_loc = p[..., :n_loc].astype(v.dtype)
    p_ctx = p[..., n_loc:n_loc + n_ctx].astype(v.dtype)
    out = (jnp.einsum('bnhgqk,bnkhd->bnqhgd', p_loc, vb)
           + jnp.einsum('bnhgqk,bkhd->bnqhgd', p_ctx, vc))
    return out.reshape(b, l, hkv * g * dh)


def context_attention(qc, kc, vc, sink):
    b, lc, hkv, g, dh = qc.shape
    s = jnp.einsum('bqhgd,bkhd->bhgqk', qc, kc).astype(jnp.float32) * dh ** -0.5
    s_sink = jnp.broadcast_to(sink.astype(jnp.float32)[None, :, :, None, None], s.shape[:-1] + (1,))
    p = jax.nn.softmax(jnp.concatenate([s, s_sink], axis=-1), axis=-1)[..., :lc].astype(vc.dtype)
    return jnp.einsum('bhgqk,bkhd->bqhgd', p, vc).reshape(b, lc, hkv * g * dh)


def s5_discretize(a_re, a_im, log_dt, b_re, b_im):
    lam = lax.complex(a_re.astype(jnp.float32), a_im.astype(jnp.float32))
    dt = jnp.exp(log_dt.astype(jnp.float32))[:, None]
    a_bar = jnp.exp(lam * dt)
    bmat = lax.complex(b_re.astype(jnp.float32), b_im.astype(jnp.float32))
    b_bar = ((a_bar - 1.0) / lam)[..., None] * bmat
    return a_bar, b_bar


def _scan_combine(left, right):
    a_l, h_l = left
    a_r, h_r = right
    return a_l * a_r, a_r * h_l + h_r


def diag_scan(a_bar, bu, h0):
    if h0 is not None:
        bu = bu.at[:, 0].add(a_bar * h0)
    a = jnp.broadcast_to(a_bar, bu.shape)
    _, h = lax.associative_scan(_scan_combine, (a, bu), axis=1)
    return h


def s5_branch(u, uc, a_re, a_im, log_dt, b_re, b_im, c_re, c_im, d_skip, glu_w, glu_b, need_ctx):
    b, l, _ = u.shape
    lc = uc.shape[1]
    ul = u.astype(jnp.float32).reshape(b, l, SSM_GROUPS, SSM_GROUP).astype(jnp.complex64)
    ucg = uc.astype(jnp.float32).reshape(b, lc, SSM_GROUPS, SSM_GROUP).astype(jnp.complex64)
    d = d_skip.astype(jnp.float32)
    y = u.astype(jnp.float32) * d
    yc = uc.astype(jnp.float32) * d if need_ctx else None
    for direction in range(2):
        a_bar, b_bar = s5_discretize(a_re[direction], a_im[direction], log_dt[direction],
                                     b_re[direction], b_im[direction])
        cmat = lax.complex(c_re[direction].astype(jnp.float32), c_im[direction].astype(jnp.float32))
        bu = jnp.einsum('blgc,gpc->blgp', ul, b_bar)
        buc = jnp.einsum('blgc,gpc->blgp', ucg, b_bar)
        if direction == 1:
            bu, buc = bu[:, ::-1], buc[:, ::-1]
        hc = diag_scan(a_bar, buc, None)
        h = diag_scan(a_bar, bu, hc[:, -1])
        if direction == 1:
            h, hc = h[:, ::-1], hc[:, ::-1]
        y = y + jnp.real(jnp.einsum('blgp,gcp->blgc', h, cmat)).reshape(b, l, SSM_WIDTH)
        if need_ctx:
            yc = yc + jnp.real(jnp.einsum('blgp,gcp->blgc', hc, cmat)).reshape(b, lc, SSM_WIDTH)
    gw = glu_w.astype(jnp.float32)
    gb = glu_b.astype(jnp.float32)

    def glu(z):
        z = jax.nn.gelu(z)
        return z * jax.nn.sigmoid(z @ gw + gb)

    out = glu(y).astype(u.dtype)
    out_c = glu(yc).astype(u.dtype) if need_ctx else None
    return out, out_c


def attn_ssm_mixer(a, ac, cos, sin, w_in, w_out, sink, a_re, a_im, log_dt, b_re, b_im,
                   c_re, c_im, d_skip, glu_w, glu_b, need_ctx):
    b, l, _ = a.shape
    lc = ac.shape[1]
    cuts = [int(v) for v in np.cumsum([ATTN_WIDTH, KV_WIDTH, KV_WIDTH, ATTN_WIDTH, SSM_WIDTH])]
    q, k, v, g_attn, u, g_ssm = jnp.split(a @ w_in, cuts, axis=-1)
    qc, kc, vc, g_attn_c, uc, g_ssm_c = jnp.split(ac @ w_in, cuts, axis=-1)
    q = apply_axial_rope(q.reshape(b, l, ATTN_HEADS, HEAD_DIM), cos, sin)
    q = q.reshape(b, l, ATTN_KV_HEADS, ATTN_GROUP, HEAD_DIM)
    k = apply_axial_rope(k.reshape(b, l, ATTN_KV_HEADS, HEAD_DIM), cos, sin)
    v = v.reshape(b, l, ATTN_KV_HEADS, HEAD_DIM)
    kc = kc.reshape(b, lc, ATTN_KV_HEADS, HEAD_DIM)
    vc = vc.reshape(b, lc, ATTN_KV_HEADS, HEAD_DIM)
    sink = sink.reshape(ATTN_KV_HEADS, ATTN_GROUP)
    o_attn = window_attention(q, k, v, kc, vc, sink) * jax.nn.silu(g_attn)
    o_ssm, o_ssm_c = s5_branch(u, uc, a_re, a_im, log_dt, b_re, b_im, c_re, c_im,
                               d_skip, glu_w, glu_b, need_ctx)
    y = jnp.concatenate([o_attn, o_ssm * jax.nn.silu(g_ssm)], axis=-1) @ w_out
    yc = None
    if need_ctx:
        qc = qc.reshape(b, lc, ATTN_KV_HEADS, ATTN_GROUP, HEAD_DIM)
        o_attn_c = context_attention(qc, kc, vc, sink) * jax.nn.silu(g_attn_c)
        yc = jnp.concatenate([o_attn_c, o_ssm_c * jax.nn.silu(g_ssm_c)], axis=-1) @ w_out
    return y, yc


def multiscale_pool(u):
    t_len = u.shape[1]
    uf = u.astype(jnp.float32)
    cs = jnp.pad(jnp.cumsum(uf, axis=1), ((0, 0), (1, 0), (0, 0)))
    pos = np.arange(t_len)
    outs = []
    for gi, w in enumerate(POOL_WINDOWS):
        r = w // 2
        lo = np.clip(pos - r, 0, t_len)
        hi = np.clip(pos + r + 1, 0, t_len)
        inv_cnt = jnp.asarray(1.0 / (hi - lo), dtype=jnp.float32)[None, :, None]
        sl = slice(gi * POOL_GROUP, (gi + 1) * POOL_GROUP)
        csg = cs[..., sl]
        outs.append((csg[:, hi] - csg[:, lo]) * inv_cnt - uf[..., sl])
    return jnp.concatenate(outs, axis=-1)


def pool_mixer(a, w_in, w_out, pool_w, pool_scale):
    u, gate = jnp.split(a @ w_in, 2, axis=-1)
    b, t, _ = u.shape
    p = multiscale_pool(u).reshape(b, t, len(POOL_WINDOWS), POOL_GROUP)
    p = jnp.einsum('btgc,gcd->btgd', p, pool_w.astype(jnp.float32)).reshape(b, t, POOL_WIDTH)
    p = (p * pool_scale.astype(jnp.float32)).astype(a.dtype)
    return (p * jax.nn.silu(gate)) @ w_out


def _fwd_setup_inputs(seed: int = 0) -> dict:
    key = jax.random.key(seed)
    ks = iter(jax.random.split(key, 32))
    n_even = (DEPTH + 1) // 2
    n_odd = DEPTH // 2

    def nrm(shape, scale):
        return jax.random.normal(next(ks), shape, jnp.float32) * scale

    n_idx = jnp.arange(SSM_STATE, dtype=jnp.float32)
    ssm_shape = (n_even, 2, SSM_GROUPS, SSM_STATE)
    return {
        'x': nrm((BATCH, SEQ, D_MODEL), 1.0),
        'c': nrm((BATCH, D_MODEL), 1.0),
        'ctx': nrm((BATCH, CTX_LEN, D_MODEL), 1.0),
        'c_ctx': nrm((D_MODEL,), 1.0),
        'ada_w': nrm((DEPTH, D_MODEL, 3 * D_MODEL), 0.5 * D_MODEL ** -0.5),
        'ada_b': nrm((DEPTH, 3 * D_MODEL), 0.01),
        'norm_g': 1.0 + nrm((DEPTH, D_MODEL), 0.02),
        'even_w_in': nrm((n_even, D_MODEL, EVEN_IN), D_MODEL ** -0.5),
        'even_w_out': nrm((n_even, EVEN_MIX, D_MODEL), EVEN_MIX ** -0.5),
        'attn_sink': nrm((n_even, ATTN_HEADS), 0.5),
        'ssm_a_re': -0.5 + nrm(ssm_shape, 0.01),
        'ssm_a_im': jnp.pi * n_idx + nrm(ssm_shape, 0.01),
        'ssm_log_dt': jax.random.uniform(next(ks), (n_even, 2, SSM_GROUPS), jnp.float32,
                                         np.log(DT_MIN), np.log(DT_MAX)),
        'ssm_b_re': nrm((n_even, 2, SSM_GROUPS, SSM_STATE, SSM_GROUP), (2 * SSM_GROUP) ** -0.5),
        'ssm_b_im': nrm((n_even, 2, SSM_GROUPS, SSM_STATE, SSM_GROUP), (2 * SSM_GROUP) ** -0.5),
        'ssm_c_re': nrm((n_even, 2, SSM_GROUPS, SSM_GROUP, SSM_STATE), (2 * SSM_STATE) ** -0.5),
        'ssm_c_im': nrm((n_even, 2, SSM_GROUPS, SSM_GROUP, SSM_STATE), (2 * SSM_STATE) ** -0.5),
        'ssm_d': nrm((n_even, SSM_WIDTH), 0.5),
        'glu_w': nrm((n_even, SSM_WIDTH, SSM_WIDTH), SSM_WIDTH ** -0.5),
        'glu_b': nrm((n_even, SSM_WIDTH), 0.01),
        'odd_w_in': nrm((n_odd, D_MODEL, 2 * POOL_WIDTH), D_MODEL ** -0.5),
        'odd_w_out': nrm((n_odd, POOL_WIDTH, D_MODEL), POOL_WIDTH ** -0.5),
        'pool_w': nrm((n_odd, len(POOL_WINDOWS), POOL_GROUP, POOL_GROUP), POOL_GROUP ** -0.5),
        'pool_scale': 1.0 + nrm((n_odd, POOL_WIDTH), 0.02),
        'final_g': 1.0 + nrm((D_MODEL,), 0.02),
    }


def _fwd_reference(x, c, ctx, c_ctx, ada_w, ada_b, norm_g, even_w_in, even_w_out, attn_sink,
              ssm_a_re, ssm_a_im, ssm_log_dt, ssm_b_re, ssm_b_im, ssm_c_re, ssm_c_im, ssm_d,
              glu_w, glu_b, odd_w_in, odd_w_out, pool_w, pool_scale, final_g):
    cos, sin = axial_rope_tables(x.shape[1])
    h, hc = x, ctx
    s_lat = jax.nn.silu(c)
    s_ctx = jax.nn.silu(c_ctx)
    for i in range(DEPTH):
        need_ctx = i < DEPTH - 1
        shift, scale, gate = jnp.split((s_lat @ ada_w[i] + ada_b[i])[:, None, :], 3, axis=-1)
        shift_c, scale_c, gate_c = jnp.split(s_ctx @ ada_w[i] + ada_b[i], 3, axis=-1)
        a = rmsnorm(h, norm_g[i]) * (1.0 + scale) + shift
        ac = rmsnorm(hc, norm_g[i]) * (1.0 + scale_c) + shift_c
        j = i // 2
        if i % 2 == 0:
            y, yc = attn_ssm_mixer(a, ac, cos, sin, even_w_in[j], even_w_out[j], attn_sink[j],
                                   ssm_a_re[j], ssm_a_im[j], ssm_log_dt[j], ssm_b_re[j], ssm_b_im[j],
                                   ssm_c_re[j], ssm_c_im[j], ssm_d[j], glu_w[j], glu_b[j], need_ctx)
        else:
            y = pool_mixer(a, odd_w_in[j], odd_w_out[j], pool_w[j], pool_scale[j])
            yc = pool_mixer(ac, odd_w_in[j], odd_w_out[j], pool_w[j], pool_scale[j]) if need_ctx else None
        h = h + gate * y
        if need_ctx:
            hc = hc + gate_c * yc
    return rmsnorm(h, final_g)


import jax as _jax
import jax.numpy as _jnp

TWIN_FORMAT = 'train_step'
FWD_PARAMS = ['x', 'c', 'ctx', 'c_ctx', 'ada_w', 'ada_b', 'norm_g', 'even_w_in', 'even_w_out', 'attn_sink', 'ssm_a_re', 'ssm_a_im', 'ssm_log_dt', 'ssm_b_re', 'ssm_b_im', 'ssm_c_re', 'ssm_c_im', 'ssm_d', 'glu_w', 'glu_b', 'odd_w_in', 'odd_w_out', 'pool_w', 'pool_scale', 'final_g']
TWIN_WEIGHTS = ['c_ctx', 'ada_w', 'ada_b', 'norm_g', 'even_w_in', 'even_w_out', 'attn_sink', 'ssm_a_re', 'ssm_a_im', 'ssm_log_dt', 'ssm_b_re', 'ssm_b_im', 'ssm_c_re', 'ssm_c_im', 'ssm_d', 'glu_w', 'glu_b', 'odd_w_in', 'odd_w_out', 'pool_w', 'pool_scale', 'final_g']
TWIN_DIFF_INPUT = 'x'
TWIN_INPUTS = ['x', 'c', 'ctx', 'c_ctx', 'ada_w', 'ada_b', 'norm_g', 'even_w_in', 'even_w_out', 'attn_sink', 'ssm_a_re', 'ssm_a_im', 'ssm_log_dt', 'ssm_b_re', 'ssm_b_im', 'ssm_c_re', 'ssm_c_im', 'ssm_d', 'glu_w', 'glu_b', 'odd_w_in', 'odd_w_out', 'pool_w', 'pool_scale', 'final_g', 'loss_target', 'm_c_ctx', 'm_ada_w', 'm_ada_b', 'm_norm_g', 'm_even_w_in', 'm_even_w_out', 'm_attn_sink', 'm_ssm_a_re', 'm_ssm_a_im', 'm_ssm_log_dt', 'm_ssm_b_re', 'm_ssm_b_im', 'm_ssm_c_re', 'm_ssm_c_im', 'm_ssm_d', 'm_glu_w', 'm_glu_b', 'm_odd_w_in', 'm_odd_w_out', 'm_pool_w', 'm_pool_scale', 'm_final_g', 'v_c_ctx', 'v_ada_w', 'v_ada_b', 'v_norm_g', 'v_even_w_in', 'v_even_w_out', 'v_attn_sink', 'v_ssm_a_re', 'v_ssm_a_im', 'v_ssm_log_dt', 'v_ssm_b_re', 'v_ssm_b_im', 'v_ssm_c_re', 'v_ssm_c_im', 'v_ssm_d', 'v_glu_w', 'v_glu_b', 'v_odd_w_in', 'v_odd_w_out', 'v_pool_w', 'v_pool_scale', 'v_final_g']
TWIN_OUTPUTS = ['loss', 'grad_x', 'grad_c_ctx', 'grad_ada_w', 'grad_ada_b', 'grad_norm_g', 'grad_even_w_in', 'grad_even_w_out', 'grad_attn_sink', 'grad_ssm_a_re', 'grad_ssm_a_im', 'grad_ssm_log_dt', 'grad_ssm_b_re', 'grad_ssm_b_im', 'grad_ssm_c_re', 'grad_ssm_c_im', 'grad_ssm_d', 'grad_glu_w', 'grad_glu_b', 'grad_odd_w_in', 'grad_odd_w_out', 'grad_pool_w', 'grad_pool_scale', 'grad_final_g', 'delta_c_ctx', 'delta_ada_w', 'delta_ada_b', 'delta_norm_g', 'delta_even_w_in', 'delta_even_w_out', 'delta_attn_sink', 'delta_ssm_a_re', 'delta_ssm_a_im', 'delta_ssm_log_dt', 'delta_ssm_b_re', 'delta_ssm_b_im', 'delta_ssm_c_re', 'delta_ssm_c_im', 'delta_ssm_d', 'delta_glu_w', 'delta_glu_b', 'delta_odd_w_in', 'delta_odd_w_out', 'delta_pool_w', 'delta_pool_scale', 'delta_final_g', 'new_m_c_ctx', 'new_m_ada_w', 'new_m_ada_b', 'new_m_norm_g', 'new_m_even_w_in', 'new_m_even_w_out', 'new_m_attn_sink', 'new_m_ssm_a_re', 'new_m_ssm_a_im', 'new_m_ssm_log_dt', 'new_m_ssm_b_re', 'new_m_ssm_b_im', 'new_m_ssm_c_re', 'new_m_ssm_c_im', 'new_m_ssm_d', 'new_m_glu_w', 'new_m_glu_b', 'new_m_odd_w_in', 'new_m_odd_w_out', 'new_m_pool_w', 'new_m_pool_scale', 'new_m_final_g', 'new_v_c_ctx', 'new_v_ada_w', 'new_v_ada_b', 'new_v_norm_g', 'new_v_even_w_in', 'new_v_even_w_out', 'new_v_attn_sink', 'new_v_ssm_a_re', 'new_v_ssm_a_im', 'new_v_ssm_log_dt', 'new_v_ssm_b_re', 'new_v_ssm_b_im', 'new_v_ssm_c_re', 'new_v_ssm_c_im', 'new_v_ssm_d', 'new_v_glu_w', 'new_v_glu_b', 'new_v_odd_w_in', 'new_v_odd_w_out', 'new_v_pool_w', 'new_v_pool_scale', 'new_v_final_g']
TWIN_LEAF_KINDS = {'loss': 'loss', 'grad_x': 'grad_x', 'grad_c_ctx': 'grad_w', 'grad_ada_w': 'grad_w', 'grad_ada_b': 'grad_w', 'grad_norm_g': 'grad_w', 'grad_even_w_in': 'grad_w', 'grad_even_w_out': 'grad_w', 'grad_attn_sink': 'grad_w', 'grad_ssm_a_re': 'grad_w', 'grad_ssm_a_im': 'grad_w', 'grad_ssm_log_dt': 'grad_w', 'grad_ssm_b_re': 'grad_w', 'grad_ssm_b_im': 'grad_w', 'grad_ssm_c_re': 'grad_w', 'grad_ssm_c_im': 'grad_w', 'grad_ssm_d': 'grad_w', 'grad_glu_w': 'grad_w', 'grad_glu_b': 'grad_w', 'grad_odd_w_in': 'grad_w', 'grad_odd_w_out': 'grad_w', 'grad_pool_w': 'grad_w', 'grad_pool_scale': 'grad_w', 'grad_final_g': 'grad_w', 'delta_c_ctx': 'delta_w', 'delta_ada_w': 'delta_w', 'delta_ada_b': 'delta_w', 'delta_norm_g': 'delta_w', 'delta_even_w_in': 'delta_w', 'delta_even_w_out': 'delta_w', 'delta_attn_sink': 'delta_w', 'delta_ssm_a_re': 'delta_w', 'delta_ssm_a_im': 'delta_w', 'delta_ssm_log_dt': 'delta_w', 'delta_ssm_b_re': 'delta_w', 'delta_ssm_b_im': 'delta_w', 'delta_ssm_c_re': 'delta_w', 'delta_ssm_c_im': 'delta_w', 'delta_ssm_d': 'delta_w', 'delta_glu_w': 'delta_w', 'delta_glu_b': 'delta_w', 'delta_odd_w_in': 'delta_w', 'delta_odd_w_out': 'delta_w', 'delta_pool_w': 'delta_w', 'delta_pool_scale': 'delta_w', 'delta_final_g': 'delta_w', 'new_m_c_ctx': 'new_m', 'new_m_ada_w': 'new_m', 'new_m_ada_b': 'new_m', 'new_m_norm_g': 'new_m', 'new_m_even_w_in': 'new_m', 'new_m_even_w_out': 'new_m', 'new_m_attn_sink': 'new_m', 'new_m_ssm_a_re': 'new_m', 'new_m_ssm_a_im': 'new_m', 'new_m_ssm_log_dt': 'new_m', 'new_m_ssm_b_re': 'new_m', 'new_m_ssm_b_im': 'new_m', 'new_m_ssm_c_re': 'new_m', 'new_m_ssm_c_im': 'new_m', 'new_m_ssm_d': 'new_m', 'new_m_glu_w': 'new_m', 'new_m_glu_b': 'new_m', 'new_m_odd_w_in': 'new_m', 'new_m_odd_w_out': 'new_m', 'new_m_pool_w': 'new_m', 'new_m_pool_scale': 'new_m', 'new_m_final_g': 'new_m', 'new_v_c_ctx': 'new_v', 'new_v_ada_w': 'new_v', 'new_v_ada_b': 'new_v', 'new_v_norm_g': 'new_v', 'new_v_even_w_in': 'new_v', 'new_v_even_w_out': 'new_v', 'new_v_attn_sink': 'new_v', 'new_v_ssm_a_re': 'new_v', 'new_v_ssm_a_im': 'new_v', 'new_v_ssm_log_dt': 'new_v', 'new_v_ssm_b_re': 'new_v', 'new_v_ssm_b_im': 'new_v', 'new_v_ssm_c_re': 'new_v', 'new_v_ssm_c_im': 'new_v', 'new_v_ssm_d': 'new_v', 'new_v_glu_w': 'new_v', 'new_v_glu_b': 'new_v', 'new_v_odd_w_in': 'new_v', 'new_v_odd_w_out': 'new_v', 'new_v_pool_w': 'new_v', 'new_v_pool_scale': 'new_v', 'new_v_final_g': 'new_v'}


def _forward(args):
    return _fwd_reference(*[args[k] for k in FWD_PARAMS])


def _output_shape():
    out = _jax.eval_shape(lambda: _forward(_fwd_setup_inputs(0)))
    return out.shape, out.dtype

N_MICROBATCH = 1
ADAM_LR = 0.001
ADAM_B1 = 0.9
ADAM_B2 = 0.999
ADAM_EPS = 1e-08
ADAM_WD = 0.01
ADAM_STEP = 10
PER_EXAMPLE_BATCH_AXIS = {'x': 0, 'c': 0, 'ctx': 0, 'loss_target': 0}
SHARED_INPUTS = []
_WEIGHT_DTYPES = {'c_ctx': _jnp.float32, 'ada_w': _jnp.float32, 'ada_b': _jnp.float32, 'norm_g': _jnp.float32, 'even_w_in': _jnp.float32, 'even_w_out': _jnp.float32, 'attn_sink': _jnp.float32, 'ssm_a_re': _jnp.float32, 'ssm_a_im': _jnp.float32, 'ssm_log_dt': _jnp.float32, 'ssm_b_re': _jnp.float32, 'ssm_b_im': _jnp.float32, 'ssm_c_re': _jnp.float32, 'ssm_c_im': _jnp.float32, 'ssm_d': _jnp.float32, 'glu_w': _jnp.float32, 'glu_b': _jnp.float32, 'odd_w_in': _jnp.float32, 'odd_w_out': _jnp.float32, 'pool_w': _jnp.float32, 'pool_scale': _jnp.float32, 'final_g': _jnp.float32}
MOMENT_SCALE = {'c_ctx': 9.412196e-03, 'ada_w': 4.645482e-02, 'ada_b': 7.648127e-02, 'norm_g': 4.878924e-02, 'even_w_in': 1.186685e-02, 'even_w_out': 1.161006e-02, 'attn_sink': 1.659299e-04, 'ssm_a_re': 1.444669e-03, 'ssm_a_im': 1.137619e-03, 'ssm_log_dt': 6.537915e-01, 'ssm_b_re': 7.403694e-04, 'ssm_b_im': 6.972295e-04, 'ssm_c_re': 1.384238e-03, 'ssm_c_im': 1.360657e-03, 'ssm_d': 1.842699e-02, 'glu_w': 1.437371e-03, 'glu_b': 4.138380e-03, 'odd_w_in': 4.818546e-02, 'odd_w_out': 4.728972e-02, 'pool_w': 4.742791e-02, 'pool_scale': 4.776570e-02, 'final_g': 6.404196e+01}


def _to_microbatches(a, axis):
    t = _jnp.moveaxis(a, axis, 0)
    t = t.reshape((N_MICROBATCH, t.shape[0] // N_MICROBATCH) + t.shape[1:])
    return _jnp.moveaxis(t, 1, axis + 1)


def setup_inputs(seed: int = 0) -> dict:
    inp = _fwd_setup_inputs(seed)
    key = _jax.random.fold_in(_jax.random.key(seed), 7919)
    shape, _ = _output_shape()
    out = dict(inp)
    out["loss_target"] = _jax.random.normal(_jax.random.fold_in(key, 0), shape, _jnp.float32)
    for i, name in enumerate(TWIN_WEIGHTS):
        w = inp[name].astype(_jnp.float32)
        if MOMENT_SCALE is None:
            s = _jnp.sqrt(_jnp.mean(_jnp.square(w)) + 1e-30)
        else:
            s = MOMENT_SCALE[name]
        km, kv = _jax.random.split(_jax.random.fold_in(key, i + 1))
        out[name] = w
        out["m_" + name] = s * _jax.random.normal(km, w.shape, _jnp.float32)
        out["v_" + name] = (s * s) * _jax.random.uniform(kv, w.shape, _jnp.float32, 0.5, 1.5)
    if N_MICROBATCH > 1:
        for name, axis in PER_EXAMPLE_BATCH_AXIS.items():
            out[name] = _to_microbatches(out[name], axis)
    return {'x': out['x'], 'c': out['c'], 'ctx': out['ctx'], 'c_ctx': out['c_ctx'], 'ada_w': out['ada_w'], 'ada_b': out['ada_b'], 'norm_g': out['norm_g'], 'even_w_in': out['even_w_in'], 'even_w_out': out['even_w_out'], 'attn_sink': out['attn_sink'], 'ssm_a_re': out['ssm_a_re'], 'ssm_a_im': out['ssm_a_im'], 'ssm_log_dt': out['ssm_log_dt'], 'ssm_b_re': out['ssm_b_re'], 'ssm_b_im': out['ssm_b_im'], 'ssm_c_re': out['ssm_c_re'], 'ssm_c_im': out['ssm_c_im'], 'ssm_d': out['ssm_d'], 'glu_w': out['glu_w'], 'glu_b': out['glu_b'], 'odd_w_in': out['odd_w_in'], 'odd_w_out': out['odd_w_out'], 'pool_w': out['pool_w'], 'pool_scale': out['pool_scale'], 'final_g': out['final_g'], 'loss_target': out['loss_target'], 'm_c_ctx': out['m_c_ctx'], 'm_ada_w': out['m_ada_w'], 'm_ada_b': out['m_ada_b'], 'm_norm_g': out['m_norm_g'], 'm_even_w_in': out['m_even_w_in'], 'm_even_w_out': out['m_even_w_out'], 'm_attn_sink': out['m_attn_sink'], 'm_ssm_a_re': out['m_ssm_a_re'], 'm_ssm_a_im': out['m_ssm_a_im'], 'm_ssm_log_dt': out['m_ssm_log_dt'], 'm_ssm_b_re': out['m_ssm_b_re'], 'm_ssm_b_im': out['m_ssm_b_im'], 'm_ssm_c_re': out['m_ssm_c_re'], 'm_ssm_c_im': out['m_ssm_c_im'], 'm_ssm_d': out['m_ssm_d'], 'm_glu_w': out['m_glu_w'], 'm_glu_b': out['m_glu_b'], 'm_odd_w_in': out['m_odd_w_in'], 'm_odd_w_out': out['m_odd_w_out'], 'm_pool_w': out['m_pool_w'], 'm_pool_scale': out['m_pool_scale'], 'm_final_g': out['m_final_g'], 'v_c_ctx': out['v_c_ctx'], 'v_ada_w': out['v_ada_w'], 'v_ada_b': out['v_ada_b'], 'v_norm_g': out['v_norm_g'], 'v_even_w_in': out['v_even_w_in'], 'v_even_w_out': out['v_even_w_out'], 'v_attn_sink': out['v_attn_sink'], 'v_ssm_a_re': out['v_ssm_a_re'], 'v_ssm_a_im': out['v_ssm_a_im'], 'v_ssm_log_dt': out['v_ssm_log_dt'], 'v_ssm_b_re': out['v_ssm_b_re'], 'v_ssm_b_im': out['v_ssm_b_im'], 'v_ssm_c_re': out['v_ssm_c_re'], 'v_ssm_c_im': out['v_ssm_c_im'], 'v_ssm_d': out['v_ssm_d'], 'v_glu_w': out['v_glu_w'], 'v_glu_b': out['v_glu_b'], 'v_odd_w_in': out['v_odd_w_in'], 'v_odd_w_out': out['v_odd_w_out'], 'v_pool_w': out['v_pool_w'], 'v_pool_scale': out['v_pool_scale'], 'v_final_g': out['v_final_g']}


def _loss(weights, diff, rest, loss_target):
    with _jax.named_scope("forward"):
        args = {**rest, TWIN_DIFF_INPUT: diff, **{k: w.astype(_WEIGHT_DTYPES[k]) for k, w in weights.items()}}
        y = _forward(args)
    with _jax.named_scope("loss_head"):
        err = _jnp.square(y.astype(_jnp.float32) - loss_target)
        return 0.5 * _jnp.sum(_jnp.mean(err, axis=-1)) if err.ndim else 0.5 * err


def _adamw(w, g, m, v):
    m = ADAM_B1 * m + (1.0 - ADAM_B1) * g
    v = ADAM_B2 * v + (1.0 - ADAM_B2) * _jnp.square(g)
    m_hat = m / (1.0 - ADAM_B1 ** ADAM_STEP)
    v_hat = v / (1.0 - ADAM_B2 ** ADAM_STEP)
    delta = -ADAM_LR * (m_hat / (_jnp.sqrt(v_hat) + ADAM_EPS) + ADAM_WD * w)
    return delta, m, v


def reference(x, c, ctx, c_ctx, ada_w, ada_b, norm_g, even_w_in, even_w_out, attn_sink, ssm_a_re, ssm_a_im, ssm_log_dt, ssm_b_re, ssm_b_im, ssm_c_re, ssm_c_im, ssm_d, glu_w, glu_b, odd_w_in, odd_w_out, pool_w, pool_scale, final_g, loss_target, m_c_ctx, m_ada_w, m_ada_b, m_norm_g, m_even_w_in, m_even_w_out, m_attn_sink, m_ssm_a_re, m_ssm_a_im, m_ssm_log_dt, m_ssm_b_re, m_ssm_b_im, m_ssm_c_re, m_ssm_c_im, m_ssm_d, m_glu_w, m_glu_b, m_odd_w_in, m_odd_w_out, m_pool_w, m_pool_scale, m_final_g, v_c_ctx, v_ada_w, v_ada_b, v_norm_g, v_even_w_in, v_even_w_out, v_attn_sink, v_ssm_a_re, v_ssm_a_im, v_ssm_log_dt, v_ssm_b_re, v_ssm_b_im, v_ssm_c_re, v_ssm_c_im, v_ssm_d, v_glu_w, v_glu_b, v_odd_w_in, v_odd_w_out, v_pool_w, v_pool_scale, v_final_g):
    given = dict(x=x, c=c, ctx=ctx, c_ctx=c_ctx, ada_w=ada_w, ada_b=ada_b, norm_g=norm_g, even_w_in=even_w_in, even_w_out=even_w_out, attn_sink=attn_sink, ssm_a_re=ssm_a_re, ssm_a_im=ssm_a_im, ssm_log_dt=ssm_log_dt, ssm_b_re=ssm_b_re, ssm_b_im=ssm_b_im, ssm_c_re=ssm_c_re, ssm_c_im=ssm_c_im, ssm_d=ssm_d, glu_w=glu_w, glu_b=glu_b, odd_w_in=odd_w_in, odd_w_out=odd_w_out, pool_w=pool_w, pool_scale=pool_scale, final_g=final_g, loss_target=loss_target, m_c_ctx=m_c_ctx, m_ada_w=m_ada_w, m_ada_b=m_ada_b, m_norm_g=m_norm_g, m_even_w_in=m_even_w_in, m_even_w_out=m_even_w_out, m_attn_sink=m_attn_sink, m_ssm_a_re=m_ssm_a_re, m_ssm_a_im=m_ssm_a_im, m_ssm_log_dt=m_ssm_log_dt, m_ssm_b_re=m_ssm_b_re, m_ssm_b_im=m_ssm_b_im, m_ssm_c_re=m_ssm_c_re, m_ssm_c_im=m_ssm_c_im, m_ssm_d=m_ssm_d, m_glu_w=m_glu_w, m_glu_b=m_glu_b, m_odd_w_in=m_odd_w_in, m_odd_w_out=m_odd_w_out, m_pool_w=m_pool_w, m_pool_scale=m_pool_scale, m_final_g=m_final_g, v_c_ctx=v_c_ctx, v_ada_w=v_ada_w, v_ada_b=v_ada_b, v_norm_g=v_norm_g, v_even_w_in=v_even_w_in, v_even_w_out=v_even_w_out, v_attn_sink=v_attn_sink, v_ssm_a_re=v_ssm_a_re, v_ssm_a_im=v_ssm_a_im, v_ssm_log_dt=v_ssm_log_dt, v_ssm_b_re=v_ssm_b_re, v_ssm_b_im=v_ssm_b_im, v_ssm_c_re=v_ssm_c_re, v_ssm_c_im=v_ssm_c_im, v_ssm_d=v_ssm_d, v_glu_w=v_glu_w, v_glu_b=v_glu_b, v_odd_w_in=v_odd_w_in, v_odd_w_out=v_odd_w_out, v_pool_w=v_pool_w, v_pool_scale=v_pool_scale, v_final_g=v_final_g)
    weights = {n: given[n] for n in TWIN_WEIGHTS}
    shared = {n: given[n] for n in SHARED_INPUTS}
    per_example = {n: given[n] for n in ['x', 'c', 'ctx']}
    grad_fn = _jax.value_and_grad(_loss, argnums=(0, 1))

    def one_microbatch(ex, loss_target):
        ex = dict(ex)
        diff = ex.pop(TWIN_DIFF_INPUT)
        return grad_fn(weights, diff, {**shared, **ex}, loss_target)

    if N_MICROBATCH == 1:
        loss, (grad_w, grad_x) = one_microbatch(per_example, given["loss_target"])
    else:
        def body(carry, xs):
            loss_sum, grad_sum = carry
            l_k, (gw_k, gx_k) = one_microbatch(xs[0], xs[1])
            with _jax.named_scope("update"):
                return (loss_sum + l_k, _jax.tree.map(_jnp.add, grad_sum, gw_k)), gx_k

        init = (_jnp.zeros((), _jnp.float32), _jax.tree.map(_jnp.zeros_like, weights))
        (loss, grad_w), grad_x = _jax.lax.scan(body, init, (per_example, given["loss_target"]))
    with _jax.named_scope("update"):
        delta_w, new_m, new_v = {}, {}, {}
        for n in TWIN_WEIGHTS:
            delta_w[n], new_m[n], new_v[n] = _adamw(weights[n], grad_w[n], given["m_" + n], given["v_" + n])
    return (loss, grad_x, *[grad_w[n] for n in TWIN_WEIGHTS], *[delta_w[n] for n in TWIN_WEIGHTS],
            *[new_m[n] for n in TWIN_WEIGHTS], *[new_v[n] for n in TWIN_WEIGHTS])
```

```python
import functools

import numpy as np
import jax
import jax.numpy as jnp
from jax import lax
from jax.experimental import pallas as pl
from jax.experimental.pallas import tpu as pltpu

F32 = jnp.float32
BF16 = jnp.bfloat16
MESH = pl.DeviceIdType.MESH

D = 1024
DEPTH = 4
EPS = 1e-6
NEG_INF = -1e30
GRID_W = 64
ROPE_BASE = 10000.0
ROPE_FREQS = 16
HEAD_DIM = 64
N_HEADS = 8
N_KV = 2
GROUP = 4
ATTN_W = N_HEADS * HEAD_DIM
KV_W = N_KV * HEAD_DIM
WINDOW = 128
AB = 128
SSM_W = 512
G = 32
C = 16
P = 64
Q = 16
QC = Q * C
P2 = 2 * P
SCAN_W = 1024
POOL_R = (1, 2, 4, 8)
POOL_G = 256
HALO = 8
TM = 256
EVEN_SPLITS = (512, 256, 512, 512, 512)
ODD_SPLITS = (1024, 1024)

ADAM_LR = 0.001
ADAM_B1 = 0.9
ADAM_B2 = 0.999
ADAM_EPS = 1e-08
ADAM_WD = 0.01
ADAM_STEP = 10

MB = 1024 * 1024


def _cp(vmem_mb=48, n_axes=0):
    kw = dict(vmem_limit_bytes=vmem_mb * MB)
    if n_axes:
        kw["dimension_semantics"] = ("arbitrary",) * n_axes
    return pltpu.CompilerParams(**kw)


def _sig(x):
    return 1.0 / (1.0 + jnp.exp(-x))


def _silu_and_grad(x):
    s = _sig(x)
    return x * s, s * (1.0 + x * (1.0 - s))


_GELU_C = 0.7978845608028654
_GELU_A = 0.044715


def _gelu_and_grad(x):
    th = jnp.tanh(_GELU_C * (x + _GELU_A * x * x * x))
    val = 0.5 * x * (1.0 + th)
    grad = 0.5 * (1.0 + th) + 0.5 * x * (1.0 - th * th) * _GELU_C * (1.0 + 3.0 * _GELU_A * x * x)
    return val, grad


def _rms(h):
    r = lax.rsqrt(jnp.mean(h * h, axis=-1, keepdims=True) + EPS)
    return h * r, r


def _dot(a, b):
    return jnp.dot(a, b, preferred_element_type=F32)


def _dot_nt(a, b):
    return lax.dot_general(a, b, (((1,), (1,)), ((), ())), preferred_element_type=F32)


def _dot_tn(a, b):
    return lax.dot_general(a, b, (((0,), (0,)), ((), ())), preferred_element_type=F32)


def _rowsum(x):
    return jnp.sum(x, axis=0, keepdims=True)


def _seg(t):
    return jnp.minimum(t, 1)


def _row_spec(n):
    return pl.BlockSpec((None, TM, n), lambda b, t: (b, t, 0))


def _const_spec(shape):
    nd = len(shape)
    return pl.BlockSpec(shape, lambda b, t: (0,) * nd)


def _mod_spec():
    return pl.BlockSpec((None, None, 8, D), lambda b, t: (b, _seg(t), 0, 0))


def norm_in(h, g, mod, w, splits, name):
    bsz, t_len, _ = h.shape
    n = w.shape[1]
    offs = [int(v) for v in np.cumsum((0,) + tuple(splits))]

    def body(h_ref, g_ref, mod_ref, w_ref, a_ref, *outs):
        xh, _ = _rms(h_ref[...])
        a = xh * g_ref[...] * (1.0 + mod_ref[1:2, :]) + mod_ref[0:1, :]
        ab = a.astype(BF16)
        a_ref[...] = ab
        z = _dot(ab, w_ref[...])
        for o, lo, hi in zip(outs, offs[:-1], offs[1:]):
            o[...] = z[:, lo:hi]

    return pl.pallas_call(
        body, name=name, grid=(bsz, t_len // TM),
        in_specs=[_row_spec(D), _const_spec((1, D)), _mod_spec(), _const_spec((D, n))],
        out_specs=[_row_spec(D)] + [_row_spec(s) for s in splits],
        out_shape=[jax.ShapeDtypeStruct((bsz, t_len, D), BF16)]
        + [jax.ShapeDtypeStruct((bsz, t_len, s), F32) for s in splits],
        compiler_params=_cp(48, 2),
    )(h, g, mod, w)


def norm_in_bwd(dparts, dh_in, h, g, mod, w, name):
    bsz, t_len, _ = h.shape
    n = w.shape[1]
    k = len(dparts)

    def body(*refs):
        parts = refs[:k]
        dh_in_ref, h_ref, g_ref, mod_ref, w_ref, dh_ref, dz_ref, dmod_ref, dg_ref = refs[k:]
        b, t = pl.program_id(0), pl.program_id(1)
        dz = jnp.concatenate([r[...] for r in parts], axis=1).astype(BF16)
        dz_ref[...] = dz
        da = _dot_nt(dz, w_ref[...])
        xh, r = _rms(h_ref[...])
        gg = g_ref[...]
        sc1 = 1.0 + mod_ref[1:2, :]

        @pl.when(t <= 1)
        def _():
            dmod_ref[...] = jnp.zeros_like(dmod_ref)

        @pl.when((b == 0) & (t == 0))
        def _():
            dg_ref[...] = jnp.zeros_like(dg_ref)

        dmod_ref[0:1, :] += _rowsum(da)
        dmod_ref[1:2, :] += _rowsum(da * (xh * gg))
        dg_ref[0:1, :] += _rowsum(da * sc1 * xh)
        dxh = da * gg * sc1
        dh_ref[...] = dh_in_ref[...] + r * (dxh - xh * jnp.mean(dxh * xh, axis=-1, keepdims=True))

    return pl.pallas_call(
        body, name=name, grid=(bsz, t_len // TM),
        in_specs=[_row_spec(p.shape[-1]) for p in dparts]
        + [_row_spec(D), _row_spec(D), _const_spec((1, D)), _mod_spec(), _const_spec((D, n))],
        out_specs=[_row_spec(D), _row_spec(n), _mod_spec(), _const_spec((8, D))],
        out_shape=[jax.ShapeDtypeStruct((bsz, t_len, D), F32), jax.ShapeDtypeStruct((bsz, t_len, n), BF16),
                   jax.ShapeDtypeStruct((bsz, 2, 8, D), F32), jax.ShapeDtypeStruct((8, D), F32)],
        compiler_params=_cp(56, 2),
    )(*dparts, dh_in, h, g, mod, w)


def matmul_tn(a, b, m, n, name, a_col=0, b_col=0):
    rows = a.shape[0]
    tr = 512 if rows % 512 == 0 else rows
    tn = n
    for cand in (1024, 768, 512, 256, 128):
        if n > 1024 and n % cand == 0:
            tn = cand
            break
    nb = n // tn

    def body(a_ref, b_ref, o_ref):
        @pl.when(pl.program_id(1) == 0)
        def _():
            o_ref[...] = jnp.zeros_like(o_ref)
        o_ref[...] += _dot_tn(a_ref[...].astype(BF16), b_ref[...].astype(BF16))

    return pl.pallas_call(
        body, name=name, grid=(nb, rows // tr),
        in_specs=[pl.BlockSpec((tr, m), lambda j, r: (r, a_col)),
                  pl.BlockSpec((tr, tn), lambda j, r: (r, b_col * nb + j))],
        out_specs=pl.BlockSpec((m, tn), lambda j, r: (0, j)),
        out_shape=jax.ShapeDtypeStruct((m, n), F32),
        compiler_params=_cp(48, 2),
    )(a, b)


def even_out(h, mod, o_attn, g_attn, y_ssm, g_ssm, glu_w, glu_b, w_out, name):
    bsz, t_len, _ = h.shape

    def body(h_ref, mod_ref, oa_ref, ga_ref, ys_ref, gs_ref, gw_ref, gb_ref, wo_ref, hn_ref, mix_ref, yo_ref):
        zz, _ = _gelu_and_grad(ys_ref[...])
        s = _dot(zz.astype(BF16), gw_ref[...]) + gb_ref[...]
        o_ssm = zz * _sig(s)
        sa, _ = _silu_and_grad(ga_ref[...])
        ss, _ = _silu_and_grad(gs_ref[...])
        mb = jnp.concatenate([oa_ref[...] * sa, o_ssm * ss], axis=1).astype(BF16)
        mix_ref[...] = mb
        yo = _dot(mb, wo_ref[...])
        yo_ref[...] = yo
        hn_ref[...] = h_ref[...] + mod_ref[2:3, :] * yo

    return pl.pallas_call(
        body, name=name, grid=(bsz, t_len // TM),
        in_specs=[_row_spec(D), _mod_spec(), _row_spec(512), _row_spec(512), _row_spec(512), _row_spec(512),
                  _const_spec((512, 512)), _const_spec((1, 512)), _const_spec((D, D))],
        out_specs=[_row_spec(D), _row_spec(D), _row_spec(D)],
        out_shape=[jax.ShapeDtypeStruct((bsz, t_len, D), F32), jax.ShapeDtypeStruct((bsz, t_len, D), BF16),
                   jax.ShapeDtypeStruct((bsz, t_len, D), F32)],
        compiler_params=_cp(48, 2),
    )(h, mod, o_attn, g_attn, y_ssm, g_ssm, glu_w, glu_b, w_out)


def even_out_bwd(dh, mod, o_attn, g_attn, y_ssm, g_ssm, glu_w, glu_b, w_out, yout, name):
    bsz, t_len, _ = dh.shape

    def body(dh_ref, mod_ref, oa_ref, ga_ref, ys_ref, gs_ref, gw_ref, gb_ref, wo_ref, yo_ref,
             doa_ref, dga_ref, dgs_ref, dys_ref, dyo_ref, zz_ref, ds_ref, dgate_ref, dgb_ref):
        b, t = pl.program_id(0), pl.program_id(1)
        dhv = dh_ref[...]

        @pl.when(t <= 1)
        def _():
            dgate_ref[...] = jnp.zeros_like(dgate_ref)

        @pl.when((b == 0) & (t == 0))
        def _():
            dgb_ref[...] = jnp.zeros_like(dgb_ref)

        dgate_ref[0:1, :] += _rowsum(dhv * yo_ref[...])
        dyb = (mod_ref[2:3, :] * dhv).astype(BF16)
        dyo_ref[...] = dyb
        dmix = _dot_nt(dyb, wo_ref[...])
        sa, dsa = _silu_and_grad(ga_ref[...])
        doa_ref[...] = dmix[:, :512] * sa
        dga_ref[...] = dmix[:, :512] * oa_ref[...] * dsa
        zz, dzz_dy = _gelu_and_grad(ys_ref[...])
        zb = zz.astype(BF16)
        zz_ref[...] = zb
        sg = _sig(_dot(zb, gw_ref[...]) + gb_ref[...])
        ss, dss = _silu_and_grad(gs_ref[...])
        dm = dmix[:, 512:]
        dgs_ref[...] = dm * (zz * sg) * dss
        do = dm * ss
        ds = do * zz * sg * (1.0 - sg)
        dsb = ds.astype(BF16)
        ds_ref[...] = dsb
        dgb_ref[0:1, :] += _rowsum(ds)
        dzz = do * sg + _dot_nt(dsb, gw_ref[...])
        dys_ref[...] = dzz * dzz_dy

    r512 = jax.ShapeDtypeStruct((bsz, t_len, 512), F32)
    return pl.pallas_call(
        body, name=name, grid=(bsz, t_len // TM),
        in_specs=[_row_spec(D), _mod_spec(), _row_spec(512), _row_spec(512), _row_spec(512), _row_spec(512),
                  _const_spec((512, 512)), _const_spec((1, 512)), _const_spec((D, D)), _row_spec(D)],
        out_specs=[_row_spec(512)] * 4 + [_row_spec(D), _row_spec(512), _row_spec(512), _mod_spec(),
                                           _const_spec((8, 512))],
        out_shape=[r512, r512, r512, r512, jax.ShapeDtypeStruct((bsz, t_len, D), BF16),
                   jax.ShapeDtypeStruct((bsz, t_len, 512), BF16), jax.ShapeDtypeStruct((bsz, t_len, 512), BF16),
                   jax.ShapeDtypeStruct((bsz, 2, 8, D), F32), jax.ShapeDtypeStruct((8, 512), F32)],
        compiler_params=_cp(48, 2),
    )(dh, mod, o_attn, g_attn, y_ssm, g_ssm, glu_w, glu_b, w_out, yout)


def _split3_dot(band, x):
    x1 = x.astype(BF16)
    r1 = x - x1.astype(F32)
    x2 = r1.astype(BF16)
    x3 = (r1 - x2.astype(F32)).astype(BF16)
    return _dot(band, x3) + _dot(band, x2) + _dot(band, x1)


def pool_band(x, lc, transpose, name):
    bsz, t_len, _ = x.shape
    assert lc == TM
    hb = TM // HALO

    def body(xp_ref, xc_ref, xn_ref, o_ref):
        t = pl.program_id(1)
        seg_lo = jnp.where(t == 0, 0, lc)
        seg_hi = jnp.where(t == 0, lc, t_len)
        cur = xc_ref[...]
        xh = jnp.concatenate([xp_ref[...], cur, xn_ref[...]], axis=0)
        row_t = t * TM + lax.broadcasted_iota(jnp.int32, (TM, 1), 0)
        col_s = t * TM - HALO + lax.broadcasted_iota(jnp.int32, (1, TM + 2 * HALO), 1)
        row_s = t * TM - HALO + lax.broadcasted_iota(jnp.int32, (TM + 2 * HALO, 1), 0)
        s_ok = (col_s >= seg_lo) & (col_s < seg_hi)
        outs = []
        for gi, r in enumerate(POOL_R):
            band = ((jnp.abs(row_t - col_s) <= r) & s_ok).astype(BF16)
            xg = xh[:, gi * POOL_G:(gi + 1) * POOL_G]
            if transpose:
                cnt_s = jnp.minimum(row_s + r, seg_hi - 1) - jnp.maximum(row_s - r, seg_lo) + 1
                xg = xg * (1.0 / jnp.maximum(cnt_s, 1).astype(F32))
            acc = _split3_dot(band, xg)
            if not transpose:
                cnt_t = jnp.minimum(row_t + r, seg_hi - 1) - jnp.maximum(row_t - r, seg_lo) + 1
                acc = acc * (1.0 / cnt_t.astype(F32))
            outs.append(acc - cur[:, gi * POOL_G:(gi + 1) * POOL_G])
        o_ref[...] = jnp.concatenate(outs, axis=1)

    return pl.pallas_call(
        body, name=name, grid=(bsz, t_len // TM),
        in_specs=[pl.BlockSpec((None, HALO, D), lambda b, t: (b, jnp.maximum(t * hb - 1, 0), 0)),
                  _row_spec(D),
                  pl.BlockSpec((None, HALO, D), lambda b, t: (b, jnp.minimum((t + 1) * hb, t_len // HALO - 1), 0))],
        out_specs=_row_spec(D),
        out_shape=jax.ShapeDtypeStruct((bsz, t_len, D), F32),
        compiler_params=_cp(48, 2),
    )(x, x, x)


def pool_out(h, mod, pm, gate, pool_w, pool_scale, w_out, name):
    bsz, t_len, _ = h.shape

    def body(h_ref, mod_ref, pm_ref, gt_ref, pw_ref, ps_ref, wo_ref, hn_ref, mix_ref, yo_ref):
        pmv = pm_ref[...]
        ppre = jnp.concatenate([_dot(pmv[:, g * POOL_G:(g + 1) * POOL_G].astype(BF16), pw_ref[g])
                                for g in range(4)], axis=1)
        sl, _ = _silu_and_grad(gt_ref[...])
        mb = (ppre * ps_ref[...] * sl).astype(BF16)
        mix_ref[...] = mb
        yo = _dot(mb, wo_ref[...])
        yo_ref[...] = yo
        hn_ref[...] = h_ref[...] + mod_ref[2:3, :] * yo

    return pl.pallas_call(
        body, name=name, grid=(bsz, t_len // TM),
        in_specs=[_row_spec(D), _mod_spec(), _row_spec(D), _row_spec(D), _const_spec((4, POOL_G, POOL_G)),
                  _const_spec((1, D)), _const_spec((D, D))],
        out_specs=[_row_spec(D), _row_spec(D), _row_spec(D)],
        out_shape=[jax.ShapeDtypeStruct((bsz, t_len, D), F32), jax.ShapeDtypeStruct((bsz, t_len, D), BF16),
                   jax.ShapeDtypeStruct((bsz, t_len, D), F32)],
        compiler_params=_cp(48, 2),
    )(h, mod, pm, gate, pool_w, pool_scale, w_out)


def pool_out_bwd(dh, mod, pm, gate, pool_w, pool_scale, w_out, yout, name):
    bsz, t_len, _ = dh.shape

    def body(dh_ref, mod_ref, pm_ref, gt_ref, pw_ref, ps_ref, wo_ref, yo_ref,
             dpm_ref, dgt_ref, dyo_ref, dpp_ref, dgate_ref, dps_ref):
        b, t = pl.program_id(0), pl.program_id(1)
        dhv = dh_ref[...]

        @pl.when(t <= 1)
        def _():
            dgate_ref[...] = jnp.zeros_like(dgate_ref)

        @pl.when((b == 0) & (t == 0))
        def _():
            dps_ref[...] = jnp.zeros_like(dps_ref)

        dgate_ref[0:1, :] += _rowsum(dhv * yo_ref[...])
        dyb = (mod_ref[2:3, :] * dhv).astype(BF16)
        dyo_ref[...] = dyb
        dmix = _dot_nt(dyb, wo_ref[...])
        pmv = pm_ref[...]
        ppre = jnp.concatenate([_dot(pmv[:, g * POOL_G:(g + 1) * POOL_G].astype(BF16), pw_ref[g])
                                for g in range(4)], axis=1)
        ps = ps_ref[...]
        sl, dsl = _silu_and_grad(gt_ref[...])
        dp = dmix * sl
        dgt_ref[...] = dmix * (ppre * ps) * dsl
        dps_ref[0:1, :] += _rowsum(dp * ppre)
        dppb = (dp * ps).astype(BF16)
        dpp_ref[...] = dppb
        dpm_ref[...] = jnp.concatenate([_dot_nt(dppb[:, g * POOL_G:(g + 1) * POOL_G], pw_ref[g])
                                        for g in range(4)], axis=1)

    return pl.pallas_call(
        body, name=name, grid=(bsz, t_len // TM),
        in_specs=[_row_spec(D), _mod_spec(), _row_spec(D), _row_spec(D), _const_spec((4, POOL_G, POOL_G)),
                  _const_spec((1, D)), _const_spec((D, D)), _row_spec(D)],
        out_specs=[_row_spec(D), _row_spec(D), _row_spec(D), _row_spec(D), _mod_spec(), _const_spec((8, D))],
        out_shape=[jax.ShapeDtypeStruct((bsz, t_len, D), F32), jax.ShapeDtypeStruct((bsz, t_len, D), F32),
                   jax.ShapeDtypeStruct((bsz, t_len, D), BF16), jax.ShapeDtypeStruct((bsz, t_len, D), BF16),
                   jax.ShapeDtypeStruct((bsz, 2, 8, D), F32), jax.ShapeDtypeStruct((8, D), F32)],
        compiler_params=_cp(48, 2),
    )(dh, mod, pm, gate, pool_w, pool_scale, w_out, yout)


def loss_head(h, final_g, target, name):
    bsz, t_len, _ = h.shape

    def body(h_ref, g_ref, tg_ref, dh_ref, loss_ref, dg_ref):
        b, t = pl.program_id(0), pl.program_id(1)

        @pl.when((b == 0) & (t == 0))
        def _():
            loss_ref[...] = jnp.zeros_like(loss_ref)
            dg_ref[...] = jnp.zeros_like(dg_ref)

        lat = (t > 0).astype(F32)
        xh, r = _rms(h_ref[...])
        gg = g_ref[...]
        err = (xh * gg - tg_ref[...]) * lat
        loss_ref[...] += 0.5 * jnp.sum(jnp.mean(err * err, axis=-1, keepdims=True))
        dy = err * (1.0 / D)
        dg_ref[0:1, :] += _rowsum(dy * xh)
        dxh = dy * gg
        dh_ref[...] = r * (dxh - xh * jnp.mean(dxh * xh, axis=-1, keepdims=True))

    return pl.pallas_call(
        body, name=name, grid=(bsz, t_len // TM),
        in_specs=[_row_spec(D), _const_spec((1, D)),
                  pl.BlockSpec((None, TM, D), lambda b, t: (b, jnp.maximum(t - 1, 0), 0))],
        out_specs=[_row_spec(D), _const_spec((8, 128)), _const_spec((8, D))],
        out_shape=[jax.ShapeDtypeStruct((bsz, t_len, D), F32), jax.ShapeDtypeStruct((8, 128), F32),
                   jax.ShapeDtypeStruct((8, D), F32)],
        compiler_params=_cp(48, 2),
    )(h, final_g, target)


def _swap16(x):
    n = x.shape[-1]
    ax = x.ndim - 1
    lane = lax.broadcasted_iota(jnp.int32, x.shape, ax)
    return jnp.where((lane % 32) < 16, pltpu.roll(x, n - 16, ax), pltpu.roll(x, 16, ax))


def _rope(x, cos, sin):
    return x * cos + _swap16(x) * sin


def _rope_t(dy, cos, sin):
    return dy * cos + _swap16(dy * sin)


def rope_tables(lc, seq):
    rows = seq // GRID_W
    row = jnp.repeat(jnp.arange(rows, dtype=F32), GRID_W)
    col = jnp.tile(jnp.arange(GRID_W, dtype=F32), rows)
    inv_freq = ROPE_BASE ** (-jnp.arange(ROPE_FREQS, dtype=F32) / ROPE_FREQS)
    ar, ac = row[:, None] * inv_freq, col[:, None] * inv_freq
    cos = jnp.concatenate([jnp.cos(ar), jnp.cos(ar), jnp.cos(ac), jnp.cos(ac)], axis=1)
    sin = jnp.concatenate([-jnp.sin(ar), jnp.sin(ar), -jnp.sin(ac), jnp.sin(ac)], axis=1)
    cos = jnp.concatenate([jnp.ones((lc, HEAD_DIM), F32), cos], axis=0)
    sin = jnp.concatenate([jnp.zeros((lc, HEAD_DIM), F32), sin], axis=0)
    return jnp.tile(cos, (1, 2)), jnp.tile(sin, (1, 2))


def _attn_mask(i, lc, t_len):
    qrow = i * AB + lax.broadcasted_iota(jnp.int32, (AB, 1), 0)
    kloc = (i - 1) * AB + lax.broadcasted_iota(jnp.int32, (1, 3 * AB), 1)
    valid = (qrow >= lc) & (kloc >= lc) & (kloc < t_len) & (jnp.abs(qrow - kloc) <= WINDOW)
    mask = jnp.concatenate([valid, jnp.ones((AB, lc), jnp.bool_)], axis=1)
    return jnp.concatenate([mask] * GROUP, axis=0)


def _attn_specs(t_len, lc):
    nb = t_len // AB
    prev = lambda b, i: (b, jnp.maximum(i - 1, 0), 0)
    cur = lambda b, i: (b, i, 0)
    nxt = lambda b, i: (b, jnp.minimum(i + 1, nb - 1), 0)
    kv = [pl.BlockSpec((None, AB, 2 * KV_W), f) for f in (prev, cur, nxt)]
    kv.append(pl.BlockSpec((None, lc, 2 * KV_W), lambda b, i: (b, 0, 0)))
    tab = [pl.BlockSpec((AB, 128), lambda b, i, f=f: f(b, i)[1:]) for f in (prev, cur, nxt)]
    return kv, tab


def _attn_keys(kvp, kvc, kvn, kvx, cp, cc, cn, sp, sc, sn):
    kk = jnp.concatenate([_rope(kvp[:, :KV_W], cp, sp), _rope(kvc[:, :KV_W], cc, sc),
                          _rope(kvn[:, :KV_W], cn, sn), kvx[:, :KV_W]], axis=0)
    vv = jnp.concatenate([kvp[:, KV_W:], kvc[:, KV_W:], kvn[:, KV_W:], kvx[:, KV_W:]], axis=0)
    return kk, vv


def _stack_heads(x, hk):
    return jnp.concatenate([x[:, (GROUP * hk + g) * HEAD_DIM:(GROUP * hk + g + 1) * HEAD_DIM]
                            for g in range(GROUP)], axis=0)


def _sink_col(sink_ref, hk):
    return jnp.concatenate([jnp.full((AB, 1), sink_ref[GROUP * hk + g], F32) for g in range(GROUP)], axis=0)


def attn_fwd(q, kv, cos, sin, sink, lc, name):
    bsz, t_len, _ = q.shape
    kv_specs, tab_specs = _attn_specs(t_len, lc)
    scale = HEAD_DIM ** -0.5

    def body(sink_ref, q_ref, kvp_ref, kvc_ref, kvn_ref, kvx_ref, cp, cc, cn, sp, sc, sn, o_ref, lse_ref):
        i = pl.program_id(1)
        mask = _attn_mask(i, lc, t_len)
        qr = _rope(q_ref[...], jnp.tile(cc[...], (1, 4)), jnp.tile(sc[...], (1, 4)))
        kk, vv = _attn_keys(kvp_ref[...], kvc_ref[...], kvn_ref[...], kvx_ref[...],
                            cp[...], cc[...], cn[...], sp[...], sc[...], sn[...])
        outs, lses = [], []
        for hk in range(N_KV):
            kh = kk[:, hk * HEAD_DIM:(hk + 1) * HEAD_DIM].astype(BF16)
            vh = vv[:, hk * HEAD_DIM:(hk + 1) * HEAD_DIM].astype(BF16)
            q4 = _stack_heads(qr, hk).astype(BF16)
            s = jnp.where(mask, _dot_nt(q4, kh) * scale, NEG_INF)
            sk = _sink_col(sink_ref, hk)
            m = jnp.maximum(jnp.max(s, axis=-1, keepdims=True), sk)
            p = jnp.exp(s - m)
            l = jnp.sum(p, axis=-1, keepdims=True) + jnp.exp(sk - m)
            o = _dot(p.astype(BF16), vh) / l
            lse = m + jnp.log(l)
            for g in range(GROUP):
                outs.append(o[g * AB:(g + 1) * AB])
                lses.append(lse[g * AB:(g + 1) * AB])
        o_ref[...] = jnp.concatenate(outs, axis=1)
        lse_ref[...] = jnp.concatenate(lses, axis=1)

    return pl.pallas_call(
        body, name=name, grid=(bsz, t_len // AB),
        in_specs=[pl.BlockSpec(memory_space=pltpu.SMEM),
                  pl.BlockSpec((None, AB, ATTN_W), lambda b, i: (b, i, 0))] + kv_specs + tab_specs + tab_specs,
        out_specs=[pl.BlockSpec((None, AB, ATTN_W), lambda b, i: (b, i, 0)),
                   pl.BlockSpec((None, AB, N_HEADS), lambda b, i: (b, i, 0))],
        out_shape=[jax.ShapeDtypeStruct((bsz, t_len, ATTN_W), F32), jax.ShapeDtypeStruct((bsz, t_len, N_HEADS), F32)],
        compiler_params=_cp(48, 2),
    )(sink, q, kv, kv, kv, kv, cos, cos, cos, sin, sin, sin)


def attn_bwd(q, kv, o, lse, do, cos, sin, sink, lc, name):
    bsz, t_len, _ = q.shape
    nb = t_len // AB
    kv_specs, tab_specs = _attn_specs(t_len, lc)
    scale = HEAD_DIM ** -0.5
    blk = lambda w: pl.BlockSpec((None, AB, w), lambda b, i: (b, i, 0))
    full_tab = pl.BlockSpec((t_len, 128), lambda b, i: (0, 0))

    def body(sink_ref, q_ref, kvp_ref, kvc_ref, kvn_ref, kvx_ref, cp, cc, cn, sp, sc, sn, cf, sf,
             o_ref, lse_ref, do_ref, dq_ref, dkv_ref, dsink_ref):
        b, i = pl.program_id(0), pl.program_id(1)

        @pl.when(i == 0)
        def _():
            dkv_ref[...] = jnp.zeros_like(dkv_ref)

        @pl.when((b == 0) & (i == 0))
        def _():
            dsink_ref[...] = jnp.zeros_like(dsink_ref)

        mask = _attn_mask(i, lc, t_len)
        cq, sq = jnp.tile(cc[...], (1, 4)), jnp.tile(sc[...], (1, 4))
        qr = _rope(q_ref[...], cq, sq)
        kk, vv = _attn_keys(kvp_ref[...], kvc_ref[...], kvn_ref[...], kvx_ref[...],
                            cp[...], cc[...], cn[...], sp[...], sc[...], sn[...])
        dov, ov, lsev = do_ref[...], o_ref[...], lse_ref[...]
        dqs, dks, dvs, dsk = [], [], [], []
        for hk in range(N_KV):
            kh = kk[:, hk * HEAD_DIM:(hk + 1) * HEAD_DIM].astype(BF16)
            vh = vv[:, hk * HEAD_DIM:(hk + 1) * HEAD_DIM].astype(BF16)
            q4 = _stack_heads(qr, hk).astype(BF16)
            do4 = _stack_heads(dov, hk)
            o4 = _stack_heads(ov, hk)
            lse4 = jnp.concatenate([lsev[:, GROUP * hk + g:GROUP * hk + g + 1] for g in range(GROUP)], axis=0)
            delta = jnp.sum(do4 * o4, axis=-1, keepdims=True)
            s = jnp.where(mask, _dot_nt(q4, kh) * scale, NEG_INF)
            p = jnp.exp(s - lse4)
            do4b = do4.astype(BF16)
            dp = _dot_nt(do4b, vh)
            ds = (p * (dp - delta) * scale).astype(BF16)
            dq4 = _dot(ds, kh)
            dks.append(_dot_tn(ds, q4))
            dvs.append(_dot_tn(p.astype(BF16), do4b))
            pd = jnp.exp(_sink_col(sink_ref, hk) - lse4) * delta
            for g in range(GROUP):
                dqs.append(dq4[g * AB:(g + 1) * AB])
                dsk.append(-jnp.sum(pd[g * AB:(g + 1) * AB], axis=0, keepdims=True))
        dq_ref[...] = _rope_t(jnp.concatenate(dqs, axis=1), cq, sq)
        dsink_ref[0:1, :] += jnp.concatenate(dsk, axis=1)
        dkv = jnp.concatenate(dks + dvs, axis=1)
        starts = (jnp.maximum(i - 1, 0), i, jnp.minimum(i + 1, nb - 1))
        for j, st in enumerate(starts):
            rows = pl.ds(pl.multiple_of(st * AB, AB), AB)
            dkv_ref[rows, :] += dkv[j * AB:(j + 1) * AB]
        dkv_ref[0:lc, :] += dkv[3 * AB:]

        @pl.when(i == nb - 1)
        def _():
            def unrotate(j, carry):
                rows = pl.ds(pl.multiple_of(j * AB, AB), AB)
                dkv_ref[rows, 0:KV_W] = _rope_t(dkv_ref[rows, 0:KV_W], cf[rows, :], sf[rows, :])
                return carry
            lax.fori_loop(0, nb, unrotate, 0)

    return pl.pallas_call(
        body, name=name, grid=(bsz, nb),
        in_specs=[pl.BlockSpec(memory_space=pltpu.SMEM), blk(ATTN_W)] + kv_specs + tab_specs + tab_specs
        + [full_tab, full_tab, blk(ATTN_W), blk(N_HEADS), blk(ATTN_W)],
        out_specs=[blk(ATTN_W), pl.BlockSpec((None, t_len, 2 * KV_W), lambda b, i: (b, 0, 0)),
                   pl.BlockSpec((8, N_HEADS), lambda b, i: (0, 0))],
        out_shape=[jax.ShapeDtypeStruct((bsz, t_len, ATTN_W), F32), jax.ShapeDtypeStruct((bsz, t_len, 2 * KV_W), F32),
                   jax.ShapeDtypeStruct((8, N_HEADS), F32)],
        compiler_params=_cp(56, 2),
    )(sink, q, kv, kv, kv, kv, cos, cos, cos, sin, sin, sin, cos, sin, o, lse, do)


def _s5_mats_dir(a_re, a_im, log_dt, b_re, b_im, c_re, c_im, flip):
    hp = lax.Precision.HIGHEST
    lam = lax.complex(a_re, a_im)
    ldt = lam * jnp.exp(log_dt)[:, None]
    a_bar = jnp.exp(ldt)
    b_bar = ((a_bar - 1.0) / lam)[..., None] * lax.complex(b_re, b_im)
    cm = lax.complex(c_re, c_im)
    pw = jnp.exp(ldt[..., None] * jnp.arange(Q + 1, dtype=F32))
    k = jnp.real(jnp.einsum('gcp,gpt,gpk->tgck', cm, pw[..., :Q], b_bar, precision=hp))
    tt = np.arange(Q)
    lag = tt[None, :] - tt[:, None]
    kt = k[np.clip(lag, 0, Q - 1)] * jnp.asarray(lag >= 0, F32)[:, :, None, None, None]
    kt = kt.transpose(2, 0, 4, 1, 3)
    ws = pw[..., :Q][..., ::-1][:, :, :, None] * b_bar[:, :, None, :]
    ws = ws.transpose(0, 2, 3, 1)
    wo = cm[:, :, :, None] * pw[:, None, :, 1:]
    wo = wo.transpose(0, 2, 3, 1)
    if flip:
        kt, ws, wo = kt[:, ::-1, :, ::-1, :], ws[:, ::-1], wo[:, :, ::-1]
    kt = kt.reshape(G, QC, QC)
    ws = jnp.concatenate([jnp.real(ws), jnp.imag(ws)], axis=-1).reshape(G, QC, P2)
    wo = jnp.concatenate([jnp.real(wo), -jnp.imag(wo)], axis=1).reshape(G, P2, QC)
    aq = pw[..., Q]
    a1 = jnp.concatenate([jnp.real(aq), jnp.real(aq)], axis=-1).reshape(1, G * P2)
    a2 = jnp.concatenate([-jnp.imag(aq), jnp.imag(aq)], axis=-1).reshape(1, G * P2)
    return kt, ws, wo, a1, a2


def s5_mats(a_re, a_im, log_dt, b_re, b_im, c_re, c_im, d_skip):
    per_dir = [_s5_mats_dir(a_re[d], a_im[d], log_dt[d], b_re[d], b_im[d], c_re[d], c_im[d], d == 1)
               for d in range(2)]
    kt, ws, wo, a1, a2 = (jnp.stack([m[i] for m in per_dir]) for i in range(5))
    dvec = jnp.broadcast_to(d_skip.reshape(G, 1, C), (G, Q, C)).reshape(G, 1, QC)
    return kt, ws, wo, a1, a2, dvec


def _to_groups(u):
    bsz, t_len, _ = u.shape
    return u.reshape(bsz, t_len // Q, Q, G, C).transpose(0, 3, 1, 2, 4).reshape(bsz, G, t_len // Q, QC)


def _from_groups(ug):
    bsz, _, nc, _ = ug.shape
    return ug.reshape(bsz, G, nc, Q, C).transpose(0, 2, 3, 1, 4).reshape(bsz, nc * Q, G * C)


def _gb(shape):
    return pl.BlockSpec((None, None) + shape, lambda g, b: (b, g, 0, 0))


def _gw(shape):
    return pl.BlockSpec((2, None) + shape, lambda g, b: (0, g, 0, 0))


def _gs(nc):
    return pl.BlockSpec((2, None, nc, P2), lambda g, b: (0, b, 0, g))


def s5_chunk_fwd(ug, kt, ws, dvec, name):
    bsz, _, nc, _ = ug.shape

    def body(u_ref, kt_ref, ws_ref, d_ref, y_ref, s_ref):
        u = u_ref[...]
        ub = u.astype(BF16)
        y_ref[...] = u * d_ref[...] + _dot(ub, kt_ref[0]) + _dot(ub, kt_ref[1])
        s_ref[0] = _dot(ub, ws_ref[0])
        s_ref[1] = _dot(ub, ws_ref[1])

    return pl.pallas_call(
        body, name=name, grid=(G, bsz),
        in_specs=[_gb((nc, QC)), _gw((QC, QC)), _gw((QC, P2)), pl.BlockSpec((None, 1, QC), lambda g, b: (g, 0, 0))],
        out_specs=[_gb((nc, QC)), _gs(nc)],
        out_shape=[jax.ShapeDtypeStruct((bsz, G, nc, QC), F32), jax.ShapeDtypeStruct((2, bsz, nc, G * P2), F32)],
        compiler_params=_cp(32, 2),
    )(ug, kt, ws, dvec)


def _swap_halves(h):
    n = h.shape[-1]
    lane = lax.broadcasted_iota(jnp.int32, h.shape, h.ndim - 1)
    return jnp.where((lane % P2) < P, pltpu.roll(h, n - P, h.ndim - 1), pltpu.roll(h, P, h.ndim - 1))


def _chunk_order(i, nc, ncc):
    return i, jnp.where(i < ncc, ncc - 1 - i, nc - 1 - (i - ncc))


def _scan_specs(nc):
    st = pl.BlockSpec((2, None, nc, SCAN_W), lambda b, w: (0, b, 0, w))
    av = pl.BlockSpec((2, 1, SCAN_W), lambda b, w: (0, 0, w))
    return st, av


def s5_scan_fwd(s, a1, a2, ncc, name):
    _, bsz, nc, gw = s.shape
    st, av = _scan_specs(nc)

    def body(s_ref, a1_ref, a2_ref, hp_ref):
        a1v = (a1_ref[0], a1_ref[1])
        a2v = (a2_ref[0], a2_ref[1])

        def step(i, hs):
            out = []
            for d, n in enumerate(_chunk_order(i, nc, ncc)):
                row = pl.ds(n, 1)
                hp_ref[d, row, :] = hs[d]
                out.append(a1v[d] * hs[d] + a2v[d] * _swap_halves(hs[d]) + s_ref[d, row, :])
            return tuple(out)

        z = jnp.zeros((1, SCAN_W), F32)
        lax.fori_loop(0, nc, step, (z, z))

    return pl.pallas_call(
        body, name=name, grid=(bsz, gw // SCAN_W),
        in_specs=[st, av, av], out_specs=st,
        out_shape=jax.ShapeDtypeStruct(s.shape, F32),
        compiler_params=_cp(32, 2),
    )(s, a1, a2)


def s5_scan_bwd(dhp, hp, a1, a2, ncc, name):
    _, bsz, nc, gw = hp.shape
    st, av = _scan_specs(nc)
    acc = pl.BlockSpec((2, None, 8, SCAN_W), lambda b, w: (0, b, 0, w))

    def body(dhp_ref, hp_ref, a1_ref, a2_ref, ds_ref, da1_ref, da2_ref):
        a1v = (a1_ref[0], a1_ref[1])
        a2v = (a2_ref[0], a2_ref[1])

        def step(j, carry):
            gh, da1, da2 = carry
            i = nc - 1 - j
            ngh, nda1, nda2 = [], [], []
            for d, n in enumerate(_chunk_order(i, nc, ncc)):
                row = pl.ds(n, 1)
                ds_ref[d, row, :] = gh[d]
                h = hp_ref[d, row, :]
                nda1.append(da1[d] + gh[d] * h)
                nda2.append(da2[d] + gh[d] * _swap_halves(h))
                ngh.append(dhp_ref[d, row, :] + a1v[d] * gh[d] + _swap_halves(a2v[d] * gh[d]))
            return tuple(ngh), tuple(nda1), tuple(nda2)

        z = jnp.zeros((1, SCAN_W), F32)
        _, da1, da2 = lax.fori_loop(0, nc, step, ((z, z), (z, z), (z, z)))
        for d in range(2):
            da1_ref[d] = jnp.broadcast_to(da1[d], (8, SCAN_W))
            da2_ref[d] = jnp.broadcast_to(da2[d], (8, SCAN_W))

    return pl.pallas_call(
        body, name=name, grid=(bsz, gw // SCAN_W),
        in_specs=[st, st, av, av], out_specs=[st, acc, acc],
        out_shape=[jax.ShapeDtypeStruct(hp.shape, F32), jax.ShapeDtypeStruct((2, bsz, 8, gw), F32),
                   jax.ShapeDtypeStruct((2, bsz, 8, gw), F32)],
        compiler_params=_cp(40, 2),
    )(dhp, hp, a1, a2)


def s5_out_fwd(y1, hp, wo, name):
    bsz, _, nc, _ = y1.shape

    def body(y1_ref, hp_ref, wo_ref, y_ref):
        y_ref[...] = (y1_ref[...] + _dot(hp_ref[0].astype(BF16), wo_ref[0])
                      + _dot(hp_ref[1].astype(BF16), wo_ref[1]))

    return pl.pallas_call(
        body, name=name, grid=(G, bsz),
        in_specs=[_gb((nc, QC)), _gs(nc), _gw((P2, QC))],
        out_specs=_gb((nc, QC)),
        out_shape=jax.ShapeDtypeStruct(y1.shape, F32),
        compiler_params=_cp(32, 2),
    )(y1, hp, wo)


def _acc_init(b, *refs):
    @pl.when(b == 0)
    def _():
        for r in refs:
            r[...] = jnp.zeros_like(r)


def s5_out_bwd(dyg, ug, hp, wo, name):
    bsz, _, nc, _ = dyg.shape

    def body(dy_ref, u_ref, hp_ref, wo_ref, dhp_ref, dwo_ref, dkt_ref, dd_ref):
        _acc_init(pl.program_id(1), dwo_ref, dkt_ref, dd_ref)
        dy, u = dy_ref[...], u_ref[...]
        dyb = dy.astype(BF16)
        for d in range(2):
            dhp_ref[d] = _dot_nt(dyb, wo_ref[d])
            dwo_ref[d] += _dot_tn(hp_ref[d].astype(BF16), dyb)
        dkt_ref[...] += _dot_tn(u.astype(BF16), dyb)
        dd_ref[...] += _rowsum(dy * u)

    return pl.pallas_call(
        body, name=name, grid=(G, bsz),
        in_specs=[_gb((nc, QC)), _gb((nc, QC)), _gs(nc), _gw((P2, QC))],
        out_specs=[_gs(nc), _gw((P2, QC)), pl.BlockSpec((None, QC, QC), lambda g, b: (g, 0, 0)),
                   pl.BlockSpec((None, 1, QC), lambda g, b: (g, 0, 0))],
        out_shape=[jax.ShapeDtypeStruct(hp.shape, F32), jax.ShapeDtypeStruct((2, G, P2, QC), F32),
                   jax.ShapeDtypeStruct((G, QC, QC), F32), jax.ShapeDtypeStruct((G, 1, QC), F32)],
        compiler_params=_cp(32, 2),
    )(dyg, ug, hp, wo)


def s5_chunk_bwd(dyg, ug, ds, kt, ws, dvec, name):
    bsz, _, nc, _ = dyg.shape

    def body(dy_ref, u_ref, ds_ref, kt_ref, ws_ref, d_ref, du_ref, dws_ref):
        _acc_init(pl.program_id(1), dws_ref)
        dy = dy_ref[...]
        dyb = dy.astype(BF16)
        ub = u_ref[...].astype(BF16)
        du = dy * d_ref[...] + _dot_nt(dyb, kt_ref[0]) + _dot_nt(dyb, kt_ref[1])
        for d in range(2):
            dsb = ds_ref[d].astype(BF16)
            du += _dot_nt(dsb, ws_ref[d])
            dws_ref[d] += _dot_tn(ub, dsb)
        du_ref[...] = du

    return pl.pallas_call(
        body, name=name, grid=(G, bsz),
        in_specs=[_gb((nc, QC)), _gb((nc, QC)), _gs(nc), _gw((QC, QC)), _gw((QC, P2)),
                  pl.BlockSpec((None, 1, QC), lambda g, b: (g, 0, 0))],
        out_specs=[_gb((nc, QC)), _gw((QC, P2))],
        out_shape=[jax.ShapeDtypeStruct(dyg.shape, F32), jax.ShapeDtypeStruct((2, G, QC, P2), F32)],
        compiler_params=_cp(32, 2),
    )(dyg, ug, ds, kt, ws, dvec)


def local_step(x, ctx, target, mods, norm_g, final_g, even, odd):
    bsz, seq, _ = x.shape
    lc = ctx.shape[1]
    t_len = lc + seq
    ncc = lc // Q
    cos, sin = rope_tables(lc, seq)
    h = jnp.concatenate([ctx, x], axis=1)
    saved = []
    for i in range(DEPTH):
        j = i // 2
        g = norm_g[i].reshape(1, D)
        if i % 2 == 0:
            w = even[j]
            a, q, kv, g_attn, u, g_ssm = norm_in(h, g, mods[i], w["w_in"], EVEN_SPLITS, f"even_in{j}")
            o_attn, lse = attn_fwd(q, kv, cos, sin, w["sink"], lc, f"attn_fwd{j}")
            mats, mats_vjp = jax.vjp(s5_mats, *w["ssm"])
            kt, ws, wo, a1, a2, dvec = mats
            kt, ws, wo = kt.astype(BF16), ws.astype(BF16), wo.astype(BF16)
            ug = _to_groups(u)
            y1, s = s5_chunk_fwd(ug, kt, ws, dvec, f"s5_chunk_fwd{j}")
            hp = s5_scan_fwd(s, a1, a2, ncc, f"s5_scan_fwd{j}")
            y_ssm = _from_groups(s5_out_fwd(y1, hp, wo, f"s5_out_fwd{j}"))
            h_new, mix, yout = even_out(h, mods[i], o_attn, g_attn, y_ssm, g_ssm, w["glu_w"], w["glu_b"],
                                        w["w_out"], f"even_out{j}")
            saved.append(dict(h=h, a=a, q=q, kv=kv, g_attn=g_attn, g_ssm=g_ssm, o_attn=o_attn, lse=lse, ug=ug,
                              hp=hp, y_ssm=y_ssm, mix=mix, yout=yout, mats=(kt, ws, wo, a1, a2, dvec),
                              mats_vjp=mats_vjp))
        else:
            w = odd[j]
            a, u, gate = norm_in(h, g, mods[i], w["w_in"], ODD_SPLITS, f"odd_in{j}")
            pm = pool_band(u, lc, False, f"pool_band_fwd{j}")
            h_new, mix, yout = pool_out(h, mods[i], pm, gate, w["pool_w"], w["pool_scale"], w["w_out"], f"pool_out{j}")
            saved.append(dict(h=h, a=a, pm=pm, gate=gate, mix=mix, yout=yout))
        h = h_new

    dh, loss_acc, dfg = loss_head(h, final_g.reshape(1, D), target, "loss_head")
    grads = dict(final_g=dfg[0], norm_g=[None] * DEPTH, even=[None, None], odd=[None, None])
    dmods = [None] * DEPTH
    rows = bsz * t_len
    flat = lambda v: v.reshape(rows, v.shape[-1])
    for i in reversed(range(DEPTH)):
        j = i // 2
        sv = saved[i]
        g = norm_g[i].reshape(1, D)
        if i % 2 == 0:
            w = even[j]
            kt, ws, wo, a1, a2, dvec = sv["mats"]
            (d_oattn, d_gattn, d_gssm, d_yssm, dyout, zz, dsg, dgate, dglu_b) = even_out_bwd(
                dh, mods[i], sv["o_attn"], sv["g_attn"], sv["y_ssm"], sv["g_ssm"], w["glu_w"], w["glu_b"],
                w["w_out"], sv["yout"], f"even_out_bwd{j}")
            g_w_out = matmul_tn(flat(sv["mix"]), flat(dyout), D, D, f"even_w_out_grad{j}")
            g_glu_w = matmul_tn(flat(zz), flat(dsg), SSM_W, SSM_W, f"glu_w_grad{j}")
            dq, dkv, dsink = attn_bwd(sv["q"], sv["kv"], sv["o_attn"], sv["lse"], d_oattn, cos, sin, w["sink"], lc,
                                      f"attn_bwd{j}")
            dyg = _to_groups(d_yssm)
            dhp, dwo, dkt, dd = s5_out_bwd(dyg, sv["ug"], sv["hp"], wo, f"s5_out_bwd{j}")
            ds, da1, da2 = s5_scan_bwd(dhp, sv["hp"], a1, a2, ncc, f"s5_scan_bwd{j}")
            dug, dws = s5_chunk_bwd(dyg, sv["ug"], ds, kt, ws, dvec, f"s5_chunk_bwd{j}")
            dkt2 = jnp.stack([dkt, dkt])
            da1 = da1[:, :, 0, :].sum(axis=1).reshape(2, 1, G * P2)
            da2 = da2[:, :, 0, :].sum(axis=1).reshape(2, 1, G * P2)
            g_ssm = sv["mats_vjp"]((dkt2, dws, dwo, da1, da2, dd))
            dparts = [dq, dkv, d_gattn, _from_groups(dug), d_gssm]
            dh, dz, dmod, dg = norm_in_bwd(dparts, dh, sv["h"], g, mods[i], w["w_in"], f"even_in_bwd{j}")
            g_w_in = matmul_tn(flat(sv["a"]), flat(dz), D, dz.shape[-1], f"even_w_in_grad{j}")
            grads["even"][j] = dict(w_in=g_w_in, w_out=g_w_out, sink=dsink[0], ssm=g_ssm, glu_w=g_glu_w,
                                    glu_b=dglu_b[0])
        else:
            w = odd[j]
            dpm, dgt, dyout, dpp, dgate, dps = pool_out_bwd(dh, mods[i], sv["pm"], sv["gate"], w["pool_w"],
                                                            w["pool_scale"], w["w_out"], sv["yout"],
                                                            f"pool_out_bwd{j}")
            g_w_out = matmul_tn(flat(sv["mix"]), flat(dyout), D, D, f"odd_w_out_grad{j}")
            g_pool_w = jnp.stack([matmul_tn(flat(sv["pm"]), flat(dpp), POOL_G, POOL_G, f"pool_w_grad{j}_{gi}",
                                            a_col=gi, b_col=gi) for gi in range(4)])
            du = pool_band(dpm, lc, True, f"pool_band_bwd{j}")
            dh, dz, dmod, dg = norm_in_bwd([du, dgt], dh, sv["h"], g, mods[i], w["w_in"], f"odd_in_bwd{j}")
            g_w_in = matmul_tn(flat(sv["a"]), flat(dz), D, dz.shape[-1], f"odd_w_in_grad{j}")
            grads["odd"][j] = dict(w_in=g_w_in, w_out=g_w_out, pool_w=g_pool_w, pool_scale=dps[0])
        grads["norm_g"][i] = dg[0]
        dmods[i] = jnp.concatenate([dmod[:, :, 0:2, :], dgate[:, :, 0:1, :]], axis=2)
    return loss_acc[0, 0], dh[:, lc:, :], dmods, grads


N_DEV = 8
HBM_SPEC = pl.BlockSpec(memory_space=pltpu.HBM)


def allgather8(x_shard, name):
    m_per, n = x_shard.shape

    def body(x_ref, out_ref, send_sems, recv_sems, local_sem):
        x, y, c = lax.axis_index("x"), lax.axis_index("y"), lax.axis_index("c")
        me, sibling = (x, y, c), (x, y, 1 - c)
        chips = [(1 - x, y), (x, 1 - y), (1 - x, 1 - y)]

        def rows(px, py, pc):
            return out_ref.at[pl.ds((4 * px + 2 * py + pc) * m_per, m_per), :]

        def copy(k, block, to, src=None):
            return pltpu.make_async_remote_copy(
                src_ref=rows(*block) if src is None else src, dst_ref=rows(*block),
                send_sem=send_sems.at[k], recv_sem=recv_sems.at[k], device_id=to, device_id_type=MESH)

        mine = pltpu.make_async_copy(x_ref, rows(*me), local_sem)
        mine.start()
        first = [copy(0, me, sibling, src=x_ref)]
        first += [copy(1 + j, me, (*chip, c), src=x_ref) for j, chip in enumerate(chips)]
        for cp in first:
            cp.start()
        passed = [copy(4 + j, (*chip, c), sibling) for j, chip in enumerate(chips)]
        for j, chip in enumerate(chips):
            copy(1 + j, (*chip, c), me).wait_recv()
            passed[j].start()
        copy(0, sibling, me).wait_recv()
        for j, chip in enumerate(chips):
            copy(4 + j, (*chip, 1 - c), me).wait_recv()
        for cp in first + passed:
            cp.wait_send()
        mine.wait()

    return pl.pallas_call(
        body, name=name,
        out_shape=jax.ShapeDtypeStruct((N_DEV * m_per, n), x_shard.dtype),
        in_specs=[pl.BlockSpec(memory_space=pltpu.VMEM)],
        out_specs=pl.BlockSpec(memory_space=pltpu.VMEM),
        scratch_shapes=[pltpu.SemaphoreType.DMA((7,)), pltpu.SemaphoreType.DMA((7,)), pltpu.SemaphoreType.DMA],
        compiler_params=_cp(56),
    )(x_shard)


def xy_exchange(src, scatter, name):
    shape = src.shape[1:] if scatter else src.shape

    def body(src_ref, out_ref, send_sems, recv_sems, local_sem):
        x, y, c = lax.axis_index("x"), lax.axis_index("y"), lax.axis_index("c")
        my = 2 * x + y
        peers = [(1 - x, y), (x, 1 - y), (1 - x, 1 - y)]

        def piece(pos):
            return src_ref.at[pos] if scatter else src_ref

        def copy(k, px, py):
            return pltpu.make_async_remote_copy(
                src_ref=piece(2 * px + py), dst_ref=out_ref.at[my], send_sem=send_sems.at[k],
                recv_sem=recv_sems.at[k], device_id=(px, py, c), device_id_type=MESH)

        def landing(k, px, py):
            return pltpu.make_async_remote_copy(
                src_ref=piece(my), dst_ref=out_ref.at[2 * px + py], send_sem=send_sems.at[k],
                recv_sem=recv_sems.at[k], device_id=(px, py, c), device_id_type=MESH)

        mine = pltpu.make_async_copy(piece(my), out_ref.at[my], local_sem)
        mine.start()
        sends = [copy(k, px, py) for k, (px, py) in enumerate(peers)]
        for cp in sends:
            cp.start()
        for k, (px, py) in enumerate(peers):
            landing(k, px, py).wait_recv()
        for cp in sends:
            cp.wait_send()
        mine.wait()

    return pl.pallas_call(
        body, name=name,
        out_shape=jax.ShapeDtypeStruct((4,) + tuple(shape), src.dtype),
        in_specs=[HBM_SPEC], out_specs=HBM_SPEC,
        scratch_shapes=[pltpu.SemaphoreType.DMA((3,)), pltpu.SemaphoreType.DMA((3,)), pltpu.SemaphoreType.DMA],
    )(src)


def sibling_exchange(src, name):
    def body(src_ref, out_ref, send_sem, recv_sem):
        peer = (lax.axis_index("x"), lax.axis_index("y"), 1 - lax.axis_index("c"))
        cp = pltpu.make_async_remote_copy(src_ref=src_ref, dst_ref=out_ref, send_sem=send_sem, recv_sem=recv_sem,
                                          device_id=peer, device_id_type=MESH)
        cp.start()
        cp.wait()

    return pl.pallas_call(
        body, name=name, out_shape=jax.ShapeDtypeStruct(src.shape, src.dtype),
        in_specs=[HBM_SPEC], out_specs=HBM_SPEC,
        scratch_shapes=[pltpu.SemaphoreType.DMA, pltpu.SemaphoreType.DMA],
    )(src)


def _row_tile(rows, bytes_per_row, limit):
    best = None
    for tr in range(8, rows + 1, 8):
        if rows % tr == 0 and tr * bytes_per_row <= limit:
            best = tr
    return best if best is not None else rows


def sum_slots(x, name):
    n, rows, cols = x.shape
    tr = _row_tile(rows, n * cols * 4, 4 * MB)

    def body(x_ref, o_ref):
        acc = x_ref[0]
        for k in range(1, n):
            acc = acc + x_ref[k]
        o_ref[...] = acc

    return pl.pallas_call(
        body, name=name, grid=(rows // tr,),
        in_specs=[pl.BlockSpec((n, tr, cols), lambda r: (0, r, 0))],
        out_specs=pl.BlockSpec((tr, cols), lambda r: (r, 0)),
        out_shape=jax.ShapeDtypeStruct((rows, cols), F32),
        compiler_params=_cp(32, 1),
    )(x)


ADA_COLS = 3 * D // 4
C_ROWS = 8


def ada_fwd(c_all, ada_w, ada_b_cols, name):
    nrow = c_all.shape[0]

    def body(c_ref, w_ref, b_ref, o_ref):
        s, _ = _silu_and_grad(c_ref[...])
        o_ref[...] = _dot(s.astype(BF16), w_ref[...].astype(BF16)) + b_ref[...]

    return pl.pallas_call(
        body, name=name, grid=(DEPTH,),
        in_specs=[pl.BlockSpec((nrow, D), lambda i: (0, 0)), pl.BlockSpec((None, D, ADA_COLS), lambda i: (i, 0, 0)),
                  pl.BlockSpec((None, 1, ADA_COLS), lambda i: (i, 0, 0))],
        out_specs=pl.BlockSpec((None, nrow, ADA_COLS), lambda i: (i, 0, 0)),
        out_shape=jax.ShapeDtypeStruct((DEPTH, nrow, ADA_COLS), F32),
        compiler_params=_cp(32, 1),
    )(c_all, ada_w, ada_b_cols)


def ada_bwd(c_all, d_cols, ada_w, name):
    nrow = c_all.shape[0]

    def body(c_ref, d_ref, w_ref, gw_ref, ds_ref):
        @pl.when(pl.program_id(0) == 0)
        def _():
            ds_ref[...] = jnp.zeros_like(ds_ref)
        s, _ = _silu_and_grad(c_ref[...])
        dl = d_ref[...]
        gw_ref[...] = _dot_tn(s.astype(BF16), dl.astype(BF16))
        rid = lax.broadcasted_iota(jnp.int32, (nrow, 1), 0) % C_ROWS
        dctx = jnp.where((rid == 2) | (rid == 3), dl, 0.0).astype(BF16)
        ds_ref[0:1, :] += _rowsum(_dot_nt(dctx, w_ref[...].astype(BF16)))

    return pl.pallas_call(
        body, name=name, grid=(DEPTH,),
        in_specs=[pl.BlockSpec((nrow, D), lambda i: (0, 0)), pl.BlockSpec((None, nrow, ADA_COLS), lambda i: (i, 0, 0)),
                  pl.BlockSpec((None, D, ADA_COLS), lambda i: (i, 0, 0))],
        out_specs=[pl.BlockSpec((None, D, ADA_COLS), lambda i: (i, 0, 0)), pl.BlockSpec((8, D), lambda i: (0, 0))],
        out_shape=[jax.ShapeDtypeStruct((DEPTH, D, ADA_COLS), F32), jax.ShapeDtypeStruct((8, D), F32)],
        compiler_params=_cp(32, 1),
    )(c_all, d_cols, ada_w)


def ada_bias_grad(d_all, name):
    nrow = d_all.shape[1]

    def body(d_ref, o_ref):
        o_ref[...] = jnp.broadcast_to(_rowsum(d_ref[...]), o_ref.shape)

    return pl.pallas_call(
        body, name=name, grid=(DEPTH,),
        in_specs=[pl.BlockSpec((None, nrow, 3 * D), lambda i: (i, 0, 0))],
        out_specs=pl.BlockSpec((None, 8, 3 * D), lambda i: (i, 0, 0)),
        out_shape=jax.ShapeDtypeStruct((DEPTH, 8, 3 * D), F32),
        compiler_params=_cp(32, 1),
    )(d_all)


def silu_chain(ds, c, name):
    def body(ds_ref, c_ref, o_ref):
        _, dsl = _silu_and_grad(c_ref[...])
        o_ref[...] = ds_ref[...] * dsl

    return pl.pallas_call(body, name=name, out_shape=jax.ShapeDtypeStruct(ds.shape, F32))(ds, c)


def _flat_cols(shape):
    size = int(np.prod(shape))
    for cols in (1024, 128):
        if size % cols == 0:
            return cols
    return shape[-1]


def adamw(w, m, v, grads, name):
    shape = w.shape
    cols = _flat_cols(shape)
    as2d = lambda a: a.reshape(-1, cols)
    rows = w.size // cols
    tr = _row_tile(rows, cols * 4, MB)
    k = len(grads)

    def body(*refs):
        w_ref, m_ref, v_ref = refs[:3]
        g_refs = refs[3:3 + k]
        g_out, d_out, m_out, v_out = refs[3 + k:]
        g = g_refs[0][...]
        for r in g_refs[1:]:
            g = g + r[...]
        g_out[...] = g
        mn = ADAM_B1 * m_ref[...] + (1.0 - ADAM_B1) * g
        vn = ADAM_B2 * v_ref[...] + (1.0 - ADAM_B2) * (g * g)
        m_out[...] = mn
        v_out[...] = vn
        m_hat = mn / (1.0 - ADAM_B1 ** ADAM_STEP)
        v_hat = vn / (1.0 - ADAM_B2 ** ADAM_STEP)
        d_out[...] = -ADAM_LR * (m_hat / (jnp.sqrt(v_hat) + ADAM_EPS) + ADAM_WD * w_ref[...])

    spec = pl.BlockSpec((tr, cols), lambda r: (r, 0))
    outs = pl.pallas_call(
        body, name=name, grid=(rows // tr,),
        in_specs=[spec] * (3 + k), out_specs=[spec] * 4,
        out_shape=[jax.ShapeDtypeStruct((rows, cols), F32)] * 4,
        compiler_params=_cp(32, 1),
    )(as2d(w), as2d(m), as2d(v), *[as2d(g) for g in grads])
    return tuple(o.reshape(shape) for o in outs)


BIG = (("even_w_in", (2, D, 576), 2), ("even_w_out", (2, 256, D), 1), ("glu_w", (2, 128, SSM_W), 1),
       ("odd_w_in", (2, D, 512), 2), ("odd_w_out", (2, 256, D), 1), ("pool_w", (2, 4, 64, POOL_G), 2))
BIG_COLS = 1024


def _full_shape(shard, axis):
    return tuple(4 * s if a == axis else s for a, s in enumerate(shard))


def _to_shards(full, shard, axis):
    return jnp.moveaxis(full.reshape(shard[:axis] + (4,) + shard[axis:]), axis, 0)


def _from_shards(stacked, shard, axis):
    return jnp.moveaxis(stacked, 0, axis).reshape(_full_shape(shard, axis))


def pack_big_local(arrs):
    return jnp.concatenate([a.reshape(-1) for a in arrs]).reshape(-1, BIG_COLS)


def unpack_big_local(flat):
    out, off = [], 0
    flat = flat.reshape(-1)
    for _, shard, _ in BIG:
        n = int(np.prod(shard))
        out.append(flat[off:off + n].reshape(shard))
        off += n
    return out


def unpack_big_gathered(g):
    out, off = [], 0
    g = g.reshape(4, -1)
    for _, shard, axis in BIG:
        n = int(np.prod(shard))
        out.append(_from_shards(g[:, off:off + n].reshape((4,) + shard), shard, axis))
        off += n
    return out


SMALL = (("ds_ctx", (D,)), ("norm_g", (DEPTH, D)), ("final_g", (D,)), ("attn_sink", (2, N_HEADS)),
         ("ssm_a_re", (2, 2, G, P)), ("ssm_a_im", (2, 2, G, P)), ("ssm_log_dt", (2, 2, G)),
         ("ssm_b_re", (2, 2, G, P, C)), ("ssm_b_im", (2, 2, G, P, C)), ("ssm_c_re", (2, 2, G, C, P)),
         ("ssm_c_im", (2, 2, G, C, P)), ("ssm_d", (2, SSM_W)), ("glu_b", (2, SSM_W)), ("pool_scale", (2, D)))
SMALL_PAD = 8 * 128


def pack_small(vals):
    flat = jnp.concatenate([vals[n].reshape(-1) for n, _ in SMALL])
    pad = (-flat.shape[0]) % SMALL_PAD
    return jnp.pad(flat, (0, pad)).reshape(-1, 128)


def unpack_small(packed):
    flat, out, off = packed.reshape(-1), {}, 0
    for n, shape in SMALL:
        size = int(np.prod(shape))
        out[n] = flat[off:off + size].reshape(shape)
        off += size
    return out


WEIGHT_NAMES = ('c_ctx', 'ada_w', 'ada_b', 'norm_g', 'even_w_in', 'even_w_out', 'attn_sink', 'ssm_a_re', 'ssm_a_im',
                'ssm_log_dt', 'ssm_b_re', 'ssm_b_im', 'ssm_c_re', 'ssm_c_im', 'ssm_d', 'glu_w', 'glu_b', 'odd_w_in',
                'odd_w_out', 'pool_w', 'pool_scale', 'final_g')
SSM_NAMES = ('ssm_a_re', 'ssm_a_im', 'ssm_log_dt', 'ssm_b_re', 'ssm_b_im', 'ssm_c_re', 'ssm_c_im', 'ssm_d')


def kernel(x, c, ctx, c_ctx, ada_w, ada_b, norm_g, even_w_in, even_w_out, attn_sink, ssm_a_re, ssm_a_im, ssm_log_dt, ssm_b_re, ssm_b_im, ssm_c_re, ssm_c_im, ssm_d, glu_w, glu_b, odd_w_in, odd_w_out, pool_w, pool_scale, final_g, loss_target, m_c_ctx, m_ada_w, m_ada_b, m_norm_g, m_even_w_in, m_even_w_out, m_attn_sink, m_ssm_a_re, m_ssm_a_im, m_ssm_log_dt, m_ssm_b_re, m_ssm_b_im, m_ssm_c_re, m_ssm_c_im, m_ssm_d, m_glu_w, m_glu_b, m_odd_w_in, m_odd_w_out, m_pool_w, m_pool_scale, m_final_g, v_c_ctx, v_ada_w, v_ada_b, v_norm_g, v_even_w_in, v_even_w_out, v_attn_sink, v_ssm_a_re, v_ssm_a_im, v_ssm_log_dt, v_ssm_b_re, v_ssm_b_im, v_ssm_c_re, v_ssm_c_im, v_ssm_d, v_glu_w, v_glu_b, v_odd_w_in, v_odd_w_out, v_pool_w, v_pool_scale, v_final_g):
    env = dict(locals())
    weights = {n: env[n] for n in WEIGHT_NAMES}
    bsz = x.shape[0]
    ax, ay, ac = lax.axis_index("x"), lax.axis_index("y"), lax.axis_index("c")
    pos = 2 * ax + ay
    dev = 2 * pos + ac

    c_rows = jnp.concatenate([c, c_ctx.reshape(1, D), c_ctx.reshape(1, D), jnp.zeros((C_ROWS - bsz - 2, D), F32)])
    c_all = allgather8(c_rows, "gather_c")
    ada_b_cols = lax.dynamic_slice(ada_b, (0, pos * ADA_COLS), (DEPTH, ADA_COLS)).reshape(DEPTH, 1, ADA_COLS)
    mod_cols = ada_fwd(c_all, ada_w, ada_b_cols, "ada_fwd")
    nrow = N_DEV * C_ROWS
    misc = jnp.concatenate([mod_cols.reshape(DEPTH * nrow, ADA_COLS),
                            jnp.pad(pool_scale, ((0, 6), (0, ADA_COLS - pool_scale.shape[1])))])
    misc_all = allgather8(misc, "gather_mod").reshape(4, 2, DEPTH * nrow + 8, ADA_COLS)[:, 0]
    mod_full = misc_all[:, :DEPTH * nrow].reshape(4, DEPTH, nrow, ADA_COLS).transpose(1, 2, 0, 3)
    mod_mine = lax.dynamic_slice(mod_full.reshape(DEPTH, nrow, 3 * D), (0, dev * C_ROWS, 0), (DEPTH, C_ROWS, 3 * D))
    mods = []
    for i in range(DEPTH):
        lat = mod_mine[i, :bsz].reshape(bsz, 1, 3, D)
        con = jnp.broadcast_to(mod_mine[i, bsz].reshape(1, 1, 3, D), (bsz, 1, 3, D))
        mods.append(jnp.pad(jnp.concatenate([con, lat], axis=1), ((0, 0), (0, 0), (0, 5), (0, 0))))
    pool_scale_full = misc_all[:, DEPTH * nrow:DEPTH * nrow + 2, :pool_scale.shape[1]].transpose(1, 0, 2).reshape(2, D)

    w_local = pack_big_local([weights[n].astype(BF16) for n, _, _ in BIG])
    full = dict(zip([n for n, _, _ in BIG], unpack_big_gathered(xy_exchange(w_local, False, "gather_weights"))))
    even = [dict(w_in=full["even_w_in"][j], w_out=full["even_w_out"][j], sink=attn_sink[j],
                 ssm=tuple(weights[n][j] for n in SSM_NAMES), glu_w=full["glu_w"][j],
                 glu_b=glu_b[j].reshape(1, SSM_W)) for j in range(2)]
    odd = [dict(w_in=full["odd_w_in"][j], w_out=full["odd_w_out"][j], pool_w=full["pool_w"][j],
                pool_scale=pool_scale_full[j].reshape(1, D)) for j in range(2)]

    loss_local, grad_x, dmods, grads = local_step(x, ctx, loss_target, mods, norm_g, final_g, even, odd)
    loss = lax.psum(loss_local, ("x", "y", "c"))

    d_rows = jnp.stack([jnp.concatenate([dm[:, 1].reshape(bsz, 3 * D), dm[:, 0].reshape(bsz, 3 * D),
                                         jnp.zeros((C_ROWS - 2 * bsz, 3 * D), F32)]) for dm in dmods])
    d_all = allgather8(d_rows.reshape(DEPTH * C_ROWS, 3 * D), "gather_dmod")
    d_all = d_all.reshape(N_DEV, DEPTH, C_ROWS, 3 * D).transpose(1, 0, 2, 3).reshape(DEPTH, nrow, 3 * D)
    d_cols = lax.dynamic_slice(d_all, (0, 0, pos * ADA_COLS), (DEPTH, nrow, ADA_COLS))
    g_ada_w, ds_ctx = ada_bwd(c_all, d_cols, ada_w, "ada_bwd")
    g_ada_b = ada_bias_grad(d_all, "ada_bias_grad")[:, 0]

    small = dict(ds_ctx=ds_ctx[0] * (ac == 0).astype(F32), norm_g=jnp.stack(grads["norm_g"]), final_g=grads["final_g"],
                 attn_sink=jnp.stack([grads["even"][j]["sink"] for j in range(2)]),
                 glu_b=jnp.stack([grads["even"][j]["glu_b"] for j in range(2)]),
                 pool_scale=jnp.stack([grads["odd"][j]["pool_scale"] for j in range(2)]))
    for k, n in enumerate(SSM_NAMES):
        small[n] = jnp.stack([grads["even"][j]["ssm"][k] for j in range(2)])
    packed = pack_small(small)
    small_sum = sum_slots(allgather8(packed, "gather_small").reshape(N_DEV, packed.shape[0], 128), "sum_small")
    g_small = unpack_small(small_sum)
    g_small["c_ctx"] = silu_chain(g_small.pop("ds_ctx").reshape(1, D), c_ctx.reshape(1, D), "c_ctx_grad").reshape(D)
    g_small["ada_b"] = g_ada_b
    g_small["pool_scale"] = lax.dynamic_slice(g_small["pool_scale"], (0, pos * 256), (2, 256))

    big_full = dict(even_w_in=jnp.stack([grads["even"][j]["w_in"] for j in range(2)]),
                    even_w_out=jnp.stack([grads["even"][j]["w_out"] for j in range(2)]),
                    glu_w=jnp.stack([grads["even"][j]["glu_w"] for j in range(2)]),
                    odd_w_in=jnp.stack([grads["odd"][j]["w_in"] for j in range(2)]),
                    odd_w_out=jnp.stack([grads["odd"][j]["w_out"] for j in range(2)]),
                    pool_w=jnp.stack([grads["odd"][j]["pool_w"] for j in range(2)]))
    by_pos = jnp.concatenate([_to_shards(big_full[n], shard, axis).reshape(4, -1) for n, shard, axis in BIG], axis=1)
    by_pos = by_pos.reshape(4, -1, BIG_COLS)
    mine4 = sum_slots(xy_exchange(by_pos, True, "scatter_grads"), "sum_positions")
    other4 = sibling_exchange(mine4, "swap_cores")
    g_mine = dict(zip([n for n, _, _ in BIG], unpack_big_local(mine4)))
    g_other = dict(zip([n for n, _, _ in BIG], unpack_big_local(other4)))

    results = {}
    for n in WEIGHT_NAMES:
        if n in g_mine:
            gs = [g_mine[n], g_other[n]]
        elif n == "ada_w":
            gs = [g_ada_w]
        else:
            gs = [g_small[n]]
        results[n] = adamw(weights[n], env["m_" + n], env["v_" + n], gs, "adamw_" + n)
    outs = [loss, grad_x]
    for k in range(4):
        outs += [results[n][k] for n in WEIGHT_NAMES]
    return tuple(outs)
```

```python
import functools

import numpy as np
import jax
import jax.numpy as jnp
from jax import lax
from jax.experimental import pallas as pl
from jax.experimental.pallas import tpu as pltpu

F32 = jnp.float32
BF16 = jnp.bfloat16
MESH = pl.DeviceIdType.MESH

D = 1024
DEPTH = 4
EPS = 1e-6
NEG_INF = -1e30
GRID_W = 64
ROPE_BASE = 10000.0
ROPE_FREQS = 16
HEAD_DIM = 64
N_HEADS = 8
N_KV = 2
GROUP = 4
ATTN_W = N_HEADS * HEAD_DIM
KV_W = N_KV * HEAD_DIM
WINDOW = 128
AB = 128
SSM_W = 512
G = 32
C = 16
P = 64
Q = 16
QC = Q * C
P2 = 2 * P
SCAN_G = 16
POOL_R = (1, 2, 4, 8)
POOL_G = 256
HALO = 8
TM = 256
EVEN_SPLITS = (512, 256, 512, 512, 512)
ODD_SPLITS = (1024, 1024)

ADAM_LR = 0.001
ADAM_B1 = 0.9
ADAM_B2 = 0.999
ADAM_EPS = 1e-08
ADAM_WD = 0.01
ADAM_STEP = 10

MB = 1024 * 1024


def _cp(vmem_mb=48, n_axes=0):
    kw = dict(vmem_limit_bytes=vmem_mb * MB)
    if n_axes:
        kw["dimension_semantics"] = ("arbitrary",) * n_axes
    return pltpu.CompilerParams(**kw)


def _sig(x):
    return 1.0 / (1.0 + jnp.exp(-x))


def _silu_and_grad(x):
    s = _sig(x)
    return x * s, s * (1.0 + x * (1.0 - s))


_GELU_C = 0.7978845608028654
_GELU_A = 0.044715


def _gelu_and_grad(x):
    th = jnp.tanh(_GELU_C * (x + _GELU_A * x * x * x))
    val = 0.5 * x * (1.0 + th)
    grad = 0.5 * (1.0 + th) + 0.5 * x * (1.0 - th * th) * _GELU_C * (1.0 + 3.0 * _GELU_A * x * x)
    return val, grad


def _rms(h):
    r = lax.rsqrt(jnp.mean(h * h, axis=-1, keepdims=True) + EPS)
    return h * r, r


def _dot(a, b):
    return jnp.dot(a, b, preferred_element_type=F32)


def _dot_nt(a, b):
    return lax.dot_general(a, b, (((1,), (1,)), ((), ())), preferred_element_type=F32)


def _dot_tn(a, b):
    return lax.dot_general(a, b, (((0,), (0,)), ((), ())), preferred_element_type=F32)


def _rowsum(x):
    return jnp.sum(x, axis=0, keepdims=True)


def _seg(t):
    return jnp.minimum(t, 1)


def _row_spec(n):
    return pl.BlockSpec((None, TM, n), lambda b, t: (b, t, 0))


def _const_spec(shape):
    nd = len(shape)
    return pl.BlockSpec(shape, lambda b, t: (0,) * nd)


def _mod_spec():
    return pl.BlockSpec((None, None, 8, D), lambda b, t: (b, _seg(t), 0, 0))


def norm_in(h, g, mod, w, splits, name):
    bsz, t_len, _ = h.shape
    n = w.shape[1]
    offs = [int(v) for v in np.cumsum((0,) + tuple(splits))]

    def body(h_ref, g_ref, mod_ref, w_ref, a_ref, *outs):
        xh, _ = _rms(h_ref[...])
        a = xh * g_ref[...] * (1.0 + mod_ref[1:2, :]) + mod_ref[0:1, :]
        ab = a.astype(BF16)
        a_ref[...] = ab
        z = _dot(ab, w_ref[...])
        for o, lo, hi in zip(outs, offs[:-1], offs[1:]):
            o[...] = z[:, lo:hi]

    return pl.pallas_call(
        body, name=name, grid=(bsz, t_len // TM),
        in_specs=[_row_spec(D), _const_spec((1, D)), _mod_spec(), _const_spec((D, n))],
        out_specs=[_row_spec(D)] + [_row_spec(s) for s in splits],
        out_shape=[jax.ShapeDtypeStruct((bsz, t_len, D), BF16)]
        + [jax.ShapeDtypeStruct((bsz, t_len, s), F32) for s in splits],
        compiler_params=_cp(48, 2),
    )(h, g, mod, w)


def norm_in_bwd(dparts, dh_in, h, g, mod, w, name):
    bsz, t_len, _ = h.shape
    n = w.shape[1]
    k = len(dparts)

    def body(*refs):
        parts = refs[:k]
        dh_in_ref, h_ref, g_ref, mod_ref, w_ref, dh_ref, dz_ref, dmod_ref, dg_ref = refs[k:]
        b, t = pl.program_id(0), pl.program_id(1)
        dz = jnp.concatenate([r[...] for r in parts], axis=1).astype(BF16)
        dz_ref[...] = dz
        da = _dot_nt(dz, w_ref[...])
        xh, r = _rms(h_ref[...])
        gg = g_ref[...]
        sc1 = 1.0 + mod_ref[1:2, :]

        @pl.when(t <= 1)
        def _():
            dmod_ref[...] = jnp.zeros_like(dmod_ref)

        @pl.when((b == 0) & (t == 0))
        def _():
            dg_ref[...] = jnp.zeros_like(dg_ref)

        dmod_ref[0:1, :] += _rowsum(da)
        dmod_ref[1:2, :] += _rowsum(da * (xh * gg))
        dg_ref[0:1, :] += _rowsum(da * sc1 * xh)
        dxh = da * gg * sc1
        dh_ref[...] = dh_in_ref[...] + r * (dxh - xh * jnp.mean(dxh * xh, axis=-1, keepdims=True))

    return pl.pallas_call(
        body, name=name, grid=(bsz, t_len // TM),
        in_specs=[_row_spec(p.shape[-1]) for p in dparts]
        + [_row_spec(D), _row_spec(D), _const_spec((1, D)), _mod_spec(), _const_spec((D, n))],
        out_specs=[_row_spec(D), _row_spec(n), _mod_spec(), _const_spec((8, D))],
        out_shape=[jax.ShapeDtypeStruct((bsz, t_len, D), F32), jax.ShapeDtypeStruct((bsz, t_len, n), BF16),
                   jax.ShapeDtypeStruct((bsz, 2, 8, D), F32), jax.ShapeDtypeStruct((8, D), F32)],
        compiler_params=_cp(56, 2),
    )(*dparts, dh_in, h, g, mod, w)


def matmul_tn(a, b, m, n, name, a_col=0, b_col=0):
    rows = a.shape[0]
    tr = 512 if rows % 512 == 0 else rows
    tn = n
    for cand in (1024, 768, 512, 256, 128):
        if n > 1024 and n % cand == 0:
            tn = cand
            break
    nb = n // tn

    def body(a_ref, b_ref, o_ref):
        @pl.when(pl.program_id(1) == 0)
        def _():
            o_ref[...] = jnp.zeros_like(o_ref)
        o_ref[...] += _dot_tn(a_ref[...].astype(BF16), b_ref[...].astype(BF16))

    return pl.pallas_call(
        body, name=name, grid=(nb, rows // tr),
        in_specs=[pl.BlockSpec((tr, m), lambda j, r: (r, a_col)),
                  pl.BlockSpec((tr, tn), lambda j, r: (r, b_col * nb + j))],
        out_specs=pl.BlockSpec((m, tn), lambda j, r: (0, j)),
        out_shape=jax.ShapeDtypeStruct((m, n), F32),
        compiler_params=_cp(48, 2),
    )(a, b)


def even_out(h, mod, o_attn, g_attn, y_ssm, g_ssm, glu_w, glu_b, w_out, name):
    bsz, t_len, _ = h.shape

    def body(h_ref, mod_ref, oa_ref, ga_ref, ys_ref, gs_ref, gw_ref, gb_ref, wo_ref, hn_ref, mix_ref, yo_ref):
        zz, _ = _gelu_and_grad(ys_ref[...])
        s = _dot(zz.astype(BF16), gw_ref[...]) + gb_ref[...]
        o_ssm = zz * _sig(s)
        sa, _ = _silu_and_grad(ga_ref[...])
        ss, _ = _silu_and_grad(gs_ref[...])
        mb = jnp.concatenate([oa_ref[...] * sa, o_ssm * ss], axis=1).astype(BF16)
        mix_ref[...] = mb
        yo = _dot(mb, wo_ref[...])
        yo_ref[...] = yo
        hn_ref[...] = h_ref[...] + mod_ref[2:3, :] * yo

    return pl.pallas_call(
        body, name=name, grid=(bsz, t_len // TM),
        in_specs=[_row_spec(D), _mod_spec(), _row_spec(512), _row_spec(512), _row_spec(512), _row_spec(512),
                  _const_spec((512, 512)), _const_spec((1, 512)), _const_spec((D, D))],
        out_specs=[_row_spec(D), _row_spec(D), _row_spec(D)],
        out_shape=[jax.ShapeDtypeStruct((bsz, t_len, D), F32), jax.ShapeDtypeStruct((bsz, t_len, D), BF16),
                   jax.ShapeDtypeStruct((bsz, t_len, D), F32)],
        compiler_params=_cp(48, 2),
    )(h, mod, o_attn, g_attn, y_ssm, g_ssm, glu_w, glu_b, w_out)


def even_out_bwd(dh, mod, o_attn, g_attn, y_ssm, g_ssm, glu_w, glu_b, w_out, yout, name):
    bsz, t_len, _ = dh.shape

    def body(dh_ref, mod_ref, oa_ref, ga_ref, ys_ref, gs_ref, gw_ref, gb_ref, wo_ref, yo_ref,
             doa_ref, dga_ref, dgs_ref, dys_ref, dyo_ref, zz_ref, ds_ref, dgate_ref, dgb_ref):
        b, t = pl.program_id(0), pl.program_id(1)
        dhv = dh_ref[...]

        @pl.when(t <= 1)
        def _():
            dgate_ref[...] = jnp.zeros_like(dgate_ref)

        @pl.when((b == 0) & (t == 0))
        def _():
            dgb_ref[...] = jnp.zeros_like(dgb_ref)

        dgate_ref[0:1, :] += _rowsum(dhv * yo_ref[...])
        dyb = (mod_ref[2:3, :] * dhv).astype(BF16)
        dyo_ref[...] = dyb
        dmix = _dot_nt(dyb, wo_ref[...])
        sa, dsa = _silu_and_grad(ga_ref[...])
        doa_ref[...] = dmix[:, :512] * sa
        dga_ref[...] = dmix[:, :512] * oa_ref[...] * dsa
        zz, dzz_dy = _gelu_and_grad(ys_ref[...])
        zb = zz.astype(BF16)
        zz_ref[...] = zb
        sg = _sig(_dot(zb, gw_ref[...]) + gb_ref[...])
        ss, dss = _silu_and_grad(gs_ref[...])
        dm = dmix[:, 512:]
        dgs_ref[...] = dm * (zz * sg) * dss
        do = dm * ss
        ds = do * zz * sg * (1.0 - sg)
        dsb = ds.astype(BF16)
        ds_ref[...] = dsb
        dgb_ref[0:1, :] += _rowsum(ds)
        dzz = do * sg + _dot_nt(dsb, gw_ref[...])
        dys_ref[...] = dzz * dzz_dy

    r512 = jax.ShapeDtypeStruct((bsz, t_len, 512), F32)
    return pl.pallas_call(
        body, name=name, grid=(bsz, t_len // TM),
        in_specs=[_row_spec(D), _mod_spec(), _row_spec(512), _row_spec(512), _row_spec(512), _row_spec(512),
                  _const_spec((512, 512)), _const_spec((1, 512)), _const_spec((D, D)), _row_spec(D)],
        out_specs=[_row_spec(512)] * 4 + [_row_spec(D), _row_spec(512), _row_spec(512), _mod_spec(),
                                           _const_spec((8, 512))],
        out_shape=[r512, r512, r512, r512, jax.ShapeDtypeStruct((bsz, t_len, D), BF16),
                   jax.ShapeDtypeStruct((bsz, t_len, 512), BF16), jax.ShapeDtypeStruct((bsz, t_len, 512), BF16),
                   jax.ShapeDtypeStruct((bsz, 2, 8, D), F32), jax.ShapeDtypeStruct((8, 512), F32)],
        compiler_params=_cp(48, 2),
    )(dh, mod, o_attn, g_attn, y_ssm, g_ssm, glu_w, glu_b, w_out, yout)


def _split3_dot(band, x):
    x1 = x.astype(BF16)
    r1 = x - x1.astype(F32)
    x2 = r1.astype(BF16)
    x3 = (r1 - x2.astype(F32)).astype(BF16)
    return _dot(band, x3) + _dot(band, x2) + _dot(band, x1)


def pool_band(x, lc, transpose, name):
    bsz, t_len, _ = x.shape
    assert lc == TM
    hb = TM // HALO

    def body(xp_ref, xc_ref, xn_ref, o_ref):
        t = pl.program_id(1)
        seg_lo = jnp.where(t == 0, 0, lc)
        seg_hi = jnp.where(t == 0, lc, t_len)
        cur = xc_ref[...]
        xh = jnp.concatenate([xp_ref[...], cur, xn_ref[...]], axis=0)
        row_t = t * TM + lax.broadcasted_iota(jnp.int32, (TM, 1), 0)
        col_s = t * TM - HALO + lax.broadcasted_iota(jnp.int32, (1, TM + 2 * HALO), 1)
        row_s = t * TM - HALO + lax.broadcasted_iota(jnp.int32, (TM + 2 * HALO, 1), 0)
        s_ok = (col_s >= seg_lo) & (col_s < seg_hi)
        outs = []
        for gi, r in enumerate(POOL_R):
            band = ((jnp.abs(row_t - col_s) <= r) & s_ok).astype(BF16)
            xg = xh[:, gi * POOL_G:(gi + 1) * POOL_G]
            if transpose:
                cnt_s = jnp.minimum(row_s + r, seg_hi - 1) - jnp.maximum(row_s - r, seg_lo) + 1
                xg = xg * (1.0 / jnp.maximum(cnt_s, 1).astype(F32))
            acc = _split3_dot(band, xg)
            if not transpose:
                cnt_t = jnp.minimum(row_t + r, seg_hi - 1) - jnp.maximum(row_t - r, seg_lo) + 1
                acc = acc * (1.0 / cnt_t.astype(F32))
            outs.append(acc - cur[:, gi * POOL_G:(gi + 1) * POOL_G])
        o_ref[...] = jnp.concatenate(outs, axis=1)

    return pl.pallas_call(
        body, name=name, grid=(bsz, t_len // TM),
        in_specs=[pl.BlockSpec((None, HALO, D), lambda b, t: (b, jnp.maximum(t * hb - 1, 0), 0)),
                  _row_spec(D),
                  pl.BlockSpec((None, HALO, D), lambda b, t: (b, jnp.minimum((t + 1) * hb, t_len // HALO - 1), 0))],
        out_specs=_row_spec(D),
        out_shape=jax.ShapeDtypeStruct((bsz, t_len, D), F32),
        compiler_params=_cp(48, 2),
    )(x, x, x)


def pool_out(h, mod, pm, gate, pool_w, pool_scale, w_out, name):
    bsz, t_len, _ = h.shape

    def body(h_ref, mod_ref, pm_ref, gt_ref, pw_ref, ps_ref, wo_ref, hn_ref, mix_ref, yo_ref):
        pmv = pm_ref[...]
        ppre = jnp.concatenate([_dot(pmv[:, g * POOL_G:(g + 1) * POOL_G].astype(BF16), pw_ref[g])
                                for g in range(4)], axis=1)
        sl, _ = _silu_and_grad(gt_ref[...])
        mb = (ppre * ps_ref[...] * sl).astype(BF16)
        mix_ref[...] = mb
        yo = _dot(mb, wo_ref[...])
        yo_ref[...] = yo
        hn_ref[...] = h_ref[...] + mod_ref[2:3, :] * yo

    return pl.pallas_call(
        body, name=name, grid=(bsz, t_len // TM),
        in_specs=[_row_spec(D), _mod_spec(), _row_spec(D), _row_spec(D), _const_spec((4, POOL_G, POOL_G)),
                  _const_spec((1, D)), _const_spec((D, D))],
        out_specs=[_row_spec(D), _row_spec(D), _row_spec(D)],
        out_shape=[jax.ShapeDtypeStruct((bsz, t_len, D), F32), jax.ShapeDtypeStruct((bsz, t_len, D), BF16),
                   jax.ShapeDtypeStruct((bsz, t_len, D), F32)],
        compiler_params=_cp(48, 2),
    )(h, mod, pm, gate, pool_w, pool_scale, w_out)


def pool_out_bwd(dh, mod, pm, gate, pool_w, pool_scale, w_out, yout, name):
    bsz, t_len, _ = dh.shape

    def body(dh_ref, mod_ref, pm_ref, gt_ref, pw_ref, ps_ref, wo_ref, yo_ref,
             dpm_ref, dgt_ref, dyo_ref, dpp_ref, dgate_ref, dps_ref):
        b, t = pl.program_id(0), pl.program_id(1)
        dhv = dh_ref[...]

        @pl.when(t <= 1)
        def _():
            dgate_ref[...] = jnp.zeros_like(dgate_ref)

        @pl.when((b == 0) & (t == 0))
        def _():
            dps_ref[...] = jnp.zeros_like(dps_ref)

        dgate_ref[0:1, :] += _rowsum(dhv * yo_ref[...])
        dyb = (mod_ref[2:3, :] * dhv).astype(BF16)
        dyo_ref[...] = dyb
        dmix = _dot_nt(dyb, wo_ref[...])
        pmv = pm_ref[...]
        ppre = jnp.concatenate([_dot(pmv[:, g * POOL_G:(g + 1) * POOL_G].astype(BF16), pw_ref[g])
                                for g in range(4)], axis=1)
        ps = ps_ref[...]
        sl, dsl = _silu_and_grad(gt_ref[...])
        dp = dmix * sl
        dgt_ref[...] = dmix * (ppre * ps) * dsl
        dps_ref[0:1, :] += _rowsum(dp * ppre)
        dppb = (dp * ps).astype(BF16)
        dpp_ref[...] = dppb
        dpm_ref[...] = jnp.concatenate([_dot_nt(dppb[:, g * POOL_G:(g + 1) * POOL_G], pw_ref[g])
                                        for g in range(4)], axis=1)

    return pl.pallas_call(
        body, name=name, grid=(bsz, t_len // TM),
        in_specs=[_row_spec(D), _mod_spec(), _row_spec(D), _row_spec(D), _const_spec((4, POOL_G, POOL_G)),
                  _const_spec((1, D)), _const_spec((D, D)), _row_spec(D)],
        out_specs=[_row_spec(D), _row_spec(D), _row_spec(D), _row_spec(D), _mod_spec(), _const_spec((8, D))],
        out_shape=[jax.ShapeDtypeStruct((bsz, t_len, D), F32), jax.ShapeDtypeStruct((bsz, t_len, D), F32),
                   jax.ShapeDtypeStruct((bsz, t_len, D), BF16), jax.ShapeDtypeStruct((bsz, t_len, D), BF16),
                   jax.ShapeDtypeStruct((bsz, 2, 8, D), F32), jax.ShapeDtypeStruct((8, D), F32)],
        compiler_params=_cp(48, 2),
    )(dh, mod, pm, gate, pool_w, pool_scale, w_out, yout)


def loss_head(h, final_g, target, name):
    bsz, t_len, _ = h.shape

    def body(h_ref, g_ref, tg_ref, dh_ref, loss_ref, dg_ref):
        b, t = pl.program_id(0), pl.program_id(1)

        @pl.when((b == 0) & (t == 0))
        def _():
            loss_ref[...] = jnp.zeros_like(loss_ref)
            dg_ref[...] = jnp.zeros_like(dg_ref)

        lat = (t > 0).astype(F32)
        xh, r = _rms(h_ref[...])
        gg = g_ref[...]
        err = (xh * gg - tg_ref[...]) * lat
        loss_ref[...] += 0.5 * jnp.sum(jnp.mean(err * err, axis=-1, keepdims=True))
        dy = err * (1.0 / D)
        dg_ref[0:1, :] += _rowsum(dy * xh)
        dxh = dy * gg
        dh_ref[...] = r * (dxh - xh * jnp.mean(dxh * xh, axis=-1, keepdims=True))

    return pl.pallas_call(
        body, name=name, grid=(bsz, t_len // TM),
        in_specs=[_row_spec(D), _const_spec((1, D)),
                  pl.BlockSpec((None, TM, D), lambda b, t: (b, jnp.maximum(t - 1, 0), 0))],
        out_specs=[_row_spec(D), _const_spec((8, 128)), _const_spec((8, D))],
        out_shape=[jax.ShapeDtypeStruct((bsz, t_len, D), F32), jax.ShapeDtypeStruct((8, 128), F32),
                   jax.ShapeDtypeStruct((8, D), F32)],
        compiler_params=_cp(48, 2),
    )(h, final_g, target)


def _swap16(x):
    n = x.shape[-1]
    ax = x.ndim - 1
    lane = lax.broadcasted_iota(jnp.int32, x.shape, ax)
    return jnp.where((lane % 32) < 16, pltpu.roll(x, n - 16, ax), pltpu.roll(x, 16, ax))


def _rope(x, cos, sin):
    return x * cos + _swap16(x) * sin


def _rope_t(dy, cos, sin):
    return dy * cos + _swap16(dy * sin)


def rope_tables(lc, seq):
    rows = seq // GRID_W
    row = jnp.repeat(jnp.arange(rows, dtype=F32), GRID_W)
    col = jnp.tile(jnp.arange(GRID_W, dtype=F32), rows)
    inv_freq = ROPE_BASE ** (-jnp.arange(ROPE_FREQS, dtype=F32) / ROPE_FREQS)
    ar, ac = row[:, None] * inv_freq, col[:, None] * inv_freq
    cos = jnp.concatenate([jnp.cos(ar), jnp.cos(ar), jnp.cos(ac), jnp.cos(ac)], axis=1)
    sin = jnp.concatenate([-jnp.sin(ar), jnp.sin(ar), -jnp.sin(ac), jnp.sin(ac)], axis=1)
    cos = jnp.concatenate([jnp.ones((lc, HEAD_DIM), F32), cos], axis=0)
    sin = jnp.concatenate([jnp.zeros((lc, HEAD_DIM), F32), sin], axis=0)
    return jnp.tile(cos, (1, 2)), jnp.tile(sin, (1, 2))


def _attn_mask(i, lc, t_len):
    qrow = i * AB + lax.broadcasted_iota(jnp.int32, (AB, 1), 0)
    kloc = (i - 1) * AB + lax.broadcasted_iota(jnp.int32, (1, 3 * AB), 1)
    valid = (qrow >= lc) & (kloc >= lc) & (kloc < t_len) & (jnp.abs(qrow - kloc) <= WINDOW)
    mask = jnp.concatenate([valid, jnp.ones((AB, lc), jnp.bool_)], axis=1)
    return jnp.concatenate([mask] * GROUP, axis=0)


def _attn_specs(t_len, lc):
    nb = t_len // AB
    prev = lambda b, i: (b, jnp.maximum(i - 1, 0), 0)
    cur = lambda b, i: (b, i, 0)
    nxt = lambda b, i: (b, jnp.minimum(i + 1, nb - 1), 0)
    kv = [pl.BlockSpec((None, AB, 2 * KV_W), f) for f in (prev, cur, nxt)]
    kv.append(pl.BlockSpec((None, lc, 2 * KV_W), lambda b, i: (b, 0, 0)))
    tab = [pl.BlockSpec((AB, 128), lambda b, i, f=f: f(b, i)[1:]) for f in (prev, cur, nxt)]
    return kv, tab


def _attn_keys(kvp, kvc, kvn, kvx, cp, cc, cn, sp, sc, sn):
    kk = jnp.concatenate([_rope(kvp[:, :KV_W], cp, sp), _rope(kvc[:, :KV_W], cc, sc),
                          _rope(kvn[:, :KV_W], cn, sn), kvx[:, :KV_W]], axis=0)
    vv = jnp.concatenate([kvp[:, KV_W:], kvc[:, KV_W:], kvn[:, KV_W:], kvx[:, KV_W:]], axis=0)
    return kk, vv


def _stack_heads(x, hk):
    return jnp.concatenate([x[:, (GROUP * hk + g) * HEAD_DIM:(GROUP * hk + g + 1) * HEAD_DIM]
                            for g in range(GROUP)], axis=0)


def _sink_col(sink_ref, hk):
    return jnp.concatenate([jnp.full((AB, 1), sink_ref[GROUP * hk + g], F32) for g in range(GROUP)], axis=0)


def attn_fwd(q, kv, cos, sin, sink, lc, name):
    bsz, t_len, _ = q.shape
    kv_specs, tab_specs = _attn_specs(t_len, lc)
    scale = HEAD_DIM ** -0.5

    def body(sink_ref, q_ref, kvp_ref, kvc_ref, kvn_ref, kvx_ref, cp, cc, cn, sp, sc, sn, o_ref, lse_ref):
        i = pl.program_id(1)
        mask = _attn_mask(i, lc, t_len)
        qr = _rope(q_ref[...], jnp.tile(cc[...], (1, 4)), jnp.tile(sc[...], (1, 4)))
        kk, vv = _attn_keys(kvp_ref[...], kvc_ref[...], kvn_ref[...], kvx_ref[...],
                            cp[...], cc[...], cn[...], sp[...], sc[...], sn[...])
        outs, lses = [], []
        for hk in range(N_KV):
            kh = kk[:, hk * HEAD_DIM:(hk + 1) * HEAD_DIM].astype(BF16)
            vh = vv[:, hk * HEAD_DIM:(hk + 1) * HEAD_DIM].astype(BF16)
            q4 = _stack_heads(qr, hk).astype(BF16)
            s = jnp.where(mask, _dot_nt(q4, kh) * scale, NEG_INF)
            sk = _sink_col(sink_ref, hk)
            m = jnp.maximum(jnp.max(s, axis=-1, keepdims=True), sk)
            p = jnp.exp(s - m)
            l = jnp.sum(p, axis=-1, keepdims=True) + jnp.exp(sk - m)
            o = _dot(p.astype(BF16), vh) / l
            lse = m + jnp.log(l)
            for g in range(GROUP):
                outs.append(o[g * AB:(g + 1) * AB])
                lses.append(lse[g * AB:(g + 1) * AB])
        o_ref[...] = jnp.concatenate(outs, axis=1)
        lse_ref[...] = jnp.concatenate(lses, axis=1)

    return pl.pallas_call(
        body, name=name, grid=(bsz, t_len // AB),
        in_specs=[pl.BlockSpec(memory_space=pltpu.SMEM),
                  pl.BlockSpec((None, AB, ATTN_W), lambda b, i: (b, i, 0))] + kv_specs + tab_specs + tab_specs,
        out_specs=[pl.BlockSpec((None, AB, ATTN_W), lambda b, i: (b, i, 0)),
                   pl.BlockSpec((None, AB, N_HEADS), lambda b, i: (b, i, 0))],
        out_shape=[jax.ShapeDtypeStruct((bsz, t_len, ATTN_W), F32), jax.ShapeDtypeStruct((bsz, t_len, N_HEADS), F32)],
        compiler_params=_cp(48, 2),
    )(sink, q, kv, kv, kv, kv, cos, cos, cos, sin, sin, sin)


def attn_bwd(q, kv, o, lse, do, cos, sin, sink, lc, name):
    bsz, t_len, _ = q.shape
    nb = t_len // AB
    kv_specs, tab_specs = _attn_specs(t_len, lc)
    scale = HEAD_DIM ** -0.5
    blk = lambda w: pl.BlockSpec((None, AB, w), lambda b, i: (b, i, 0))
    full_tab = pl.BlockSpec((t_len, 128), lambda b, i: (0, 0))

    def body(sink_ref, q_ref, kvp_ref, kvc_ref, kvn_ref, kvx_ref, cp, cc, cn, sp, sc, sn, cf, sf,
             o_ref, lse_ref, do_ref, dq_ref, dkv_ref, dsink_ref):
        b, i = pl.program_id(0), pl.program_id(1)

        @pl.when(i == 0)
        def _():
            dkv_ref[...] = jnp.zeros_like(dkv_ref)

        @pl.when((b == 0) & (i == 0))
        def _():
            dsink_ref[...] = jnp.zeros_like(dsink_ref)

        mask = _attn_mask(i, lc, t_len)
        cq, sq = jnp.tile(cc[...], (1, 4)), jnp.tile(sc[...], (1, 4))
        qr = _rope(q_ref[...], cq, sq)
        kk, vv = _attn_keys(kvp_ref[...], kvc_ref[...], kvn_ref[...], kvx_ref[...],
                            cp[...], cc[...], cn[...], sp[...], sc[...], sn[...])
        dov, ov, lsev = do_ref[...], o_ref[...], lse_ref[...]
        dqs, dks, dvs, dsk = [], [], [], []
        for hk in range(N_KV):
            kh = kk[:, hk * HEAD_DIM:(hk + 1) * HEAD_DIM].astype(BF16)
            vh = vv[:, hk * HEAD_DIM:(hk + 1) * HEAD_DIM].astype(BF16)
            q4 = _stack_heads(qr, hk).astype(BF16)
            do4 = _stack_heads(dov, hk)
            o4 = _stack_heads(ov, hk)
            lse4 = jnp.concatenate([lsev[:, GROUP * hk + g:GROUP * hk + g + 1] for g in range(GROUP)], axis=0)
            delta = jnp.sum(do4 * o4, axis=-1, keepdims=True)
            s = jnp.where(mask, _dot_nt(q4, kh) * scale, NEG_INF)
            p = jnp.exp(s - lse4)
            do4b = do4.astype(BF16)
            dp = _dot_nt(do4b, vh)
            ds = (p * (dp - delta) * scale).astype(BF16)
            dq4 = _dot(ds, kh)
            dks.append(_dot_tn(ds, q4))
            dvs.append(_dot_tn(p.astype(BF16), do4b))
            pd = jnp.exp(_sink_col(sink_ref, hk) - lse4) * delta
            for g in range(GROUP):
                dqs.append(dq4[g * AB:(g + 1) * AB])
                dsk.append(-jnp.sum(pd[g * AB:(g + 1) * AB], axis=0, keepdims=True))
        dq_ref[...] = _rope_t(jnp.concatenate(dqs, axis=1), cq, sq)
        dsink_ref[0:1, :] += jnp.concatenate(dsk, axis=1)
        dkv = jnp.concatenate(dks + dvs, axis=1)
        starts = (jnp.maximum(i - 1, 0), i, jnp.minimum(i + 1, nb - 1))
        for j, st in enumerate(starts):
            rows = pl.ds(pl.multiple_of(st * AB, AB), AB)
            dkv_ref[rows, :] += dkv[j * AB:(j + 1) * AB]
        dkv_ref[0:lc, :] += dkv[3 * AB:]

        @pl.when(i == nb - 1)
        def _():
            def unrotate(j, carry):
                rows = pl.ds(pl.multiple_of(j * AB, AB), AB)
                dkv_ref[rows, 0:KV_W] = _rope_t(dkv_ref[rows, 0:KV_W], cf[rows, :], sf[rows, :])
                return carry
            lax.fori_loop(0, nb, unrotate, 0)

    return pl.pallas_call(
        body, name=name, grid=(bsz, nb),
        in_specs=[pl.BlockSpec(memory_space=pltpu.SMEM), blk(ATTN_W)] + kv_specs + tab_specs + tab_specs
        + [full_tab, full_tab, blk(ATTN_W), blk(N_HEADS), blk(ATTN_W)],
        out_specs=[blk(ATTN_W), pl.BlockSpec((None, t_len, 2 * KV_W), lambda b, i: (b, 0, 0)),
                   pl.BlockSpec((8, N_HEADS), lambda b, i: (0, 0))],
        out_shape=[jax.ShapeDtypeStruct((bsz, t_len, ATTN_W), F32), jax.ShapeDtypeStruct((bsz, t_len, 2 * KV_W), F32),
                   jax.ShapeDtypeStruct((8, N_HEADS), F32)],
        compiler_params=_cp(56, 2),
    )(sink, q, kv, kv, kv, kv, cos, cos, cos, sin, sin, sin, cos, sin, o, lse, do)


def _s5_mats_dir(a_re, a_im, log_dt, b_re, b_im, c_re, c_im, flip):
    hp = lax.Precision.HIGHEST
    lam = lax.complex(a_re, a_im)
    ldt = lam * jnp.exp(log_dt)[:, None]
    a_bar = jnp.exp(ldt)
    b_bar = ((a_bar - 1.0) / lam)[..., None] * lax.complex(b_re, b_im)
    cm = lax.complex(c_re, c_im)
    tt = np.arange(Q)
    powers = lambda e: jnp.exp(ldt[..., None] * jnp.asarray(e, F32))
    k = jnp.real(jnp.einsum('gcp,gpt,gpk->tgck', cm, powers(tt), b_bar, precision=hp))
    lag = (tt[:, None] - tt[None, :]) if flip else (tt[None, :] - tt[:, None])
    onehot = (lag[:, :, None] == tt[None, None, :]).astype(np.float32)
    kt = jnp.einsum('abt,tgck->gakbc', onehot, k, precision=hp).reshape(G, QC, QC)
    ws = powers(tt if flip else Q - 1 - tt)[:, :, :, None] * b_bar[:, :, None, :]
    ws = ws.transpose(0, 2, 3, 1)
    wo = cm[:, :, :, None] * powers(Q - tt if flip else tt + 1)[:, None, :, :]
    wo = wo.transpose(0, 2, 3, 1)
    ws = jnp.concatenate([jnp.real(ws), jnp.imag(ws)], axis=-1).reshape(G, QC, P2)
    wo = jnp.concatenate([jnp.real(wo), -jnp.imag(wo)], axis=1).reshape(G, P2, QC)
    a1, a2 = _pair_forms(powers([Q]))
    return kt, ws, wo, a1, a2


def _pair_forms(z):
    re, im = jnp.real(z), jnp.imag(z)
    k = z.shape[-1]
    a1 = jnp.concatenate([re, re], axis=1).transpose(2, 0, 1).reshape(k, G * P2)
    a2 = jnp.concatenate([-im, im], axis=1).transpose(2, 0, 1).reshape(k, G * P2)
    return a1, a2


def s5_mats(a_re, a_im, log_dt, b_re, b_im, c_re, c_im, d_skip):
    per_dir = [_s5_mats_dir(a_re[d], a_im[d], log_dt[d], b_re[d], b_im[d], c_re[d], c_im[d], d == 1)
               for d in range(2)]
    kt, ws, wo, a1, a2 = (jnp.stack([m[i] for m in per_dir]) for i in range(5))
    dvec = jnp.broadcast_to(d_skip.reshape(G, 1, C), (G, Q, C)).reshape(G, 1, QC)
    return kt, ws, wo, a1, a2, dvec


def _to_groups(u):
    bsz, t_len, _ = u.shape
    return u.reshape(bsz, t_len // Q, Q, G, C).transpose(0, 3, 1, 2, 4).reshape(bsz, G, t_len // Q, QC)


def _from_groups(ug):
    bsz, _, nc, _ = ug.shape
    return ug.reshape(bsz, G, nc, Q, C).transpose(0, 2, 3, 1, 4).reshape(bsz, nc * Q, G * C)


def _gb(shape):
    return pl.BlockSpec((None, None) + shape, lambda g, b: (b, g, 0, 0))


def _gw(shape):
    return pl.BlockSpec((2, None) + shape, lambda g, b: (0, g, 0, 0))


def _gs(nc):
    return pl.BlockSpec((2, None, nc, P2), lambda g, b: (0, b, 0, g))


def s5_chunk_fwd(ug, kt, ws, dvec, name):
    bsz, _, nc, _ = ug.shape

    def body(u_ref, kt_ref, ws_ref, d_ref, y_ref, s_ref):
        u = u_ref[...]
        ub = u.astype(BF16)
        y_ref[...] = u * d_ref[...] + _dot(ub, kt_ref[0]) + _dot(ub, kt_ref[1])
        s_ref[0] = _dot(ub, ws_ref[0])
        s_ref[1] = _dot(ub, ws_ref[1])

    return pl.pallas_call(
        body, name=name, grid=(G, bsz),
        in_specs=[_gb((nc, QC)), _gw((QC, QC)), _gw((QC, P2)), pl.BlockSpec((None, 1, QC), lambda g, b: (g, 0, 0))],
        out_specs=[_gb((nc, QC)), _gs(nc)],
        out_shape=[jax.ShapeDtypeStruct((bsz, G, nc, QC), F32), jax.ShapeDtypeStruct((2, bsz, nc, G * P2), F32)],
        compiler_params=_cp(32, 2),
    )(ug, kt, ws, dvec)


def s5_scan(s, a1, a2, ncc, reverse, name, hp=None):
    _, bsz, nc, gw = s.shape
    as_rows = lambda v: v.reshape(v.shape[:-1] + (G, P2))
    st = pl.BlockSpec((2, None, nc, SCAN_G, P2), lambda b, w: (0, b, 0, w, 0))
    av = pl.BlockSpec((2, SCAN_G, P2), lambda b, w: (0, w, 0))
    acc = pl.BlockSpec((2, None, SCAN_G, P2), lambda b, w: (0, b, w, 0))
    with_da = hp is not None

    def body(*refs):
        if with_da:
            s_ref, a1_ref, a2_ref, hp_ref, out_ref, da1_ref, da2_ref = refs
        else:
            s_ref, a1_ref, a2_ref, out_ref = refs
        a1v = (a1_ref[0], a1_ref[1])
        a2v = (a2_ref[0], a2_ref[1])
        swap = lambda h: pltpu.roll(h, P, 1)

        def step(j, carry):
            i = nc - 1 - j if reverse else j
            order = (i, jnp.where(i < ncc, ncc - 1 - i, nc - 1 - (i - ncc)))
            hs, da1, da2 = carry
            nh, n1, n2 = [], [], []
            for d, n in enumerate(order):
                h = hs[d]
                out_ref[d, n] = h
                nh.append(a1v[d] * h + a2v[d] * swap(h) + s_ref[d, n])
                if with_da:
                    hv = hp_ref[d, n]
                    n1.append(da1[d] + h * hv)
                    n2.append(da2[d] + h * swap(hv))
            return tuple(nh), tuple(n1), tuple(n2)

        z = jnp.zeros((SCAN_G, P2), F32)
        zz = (z, z) if with_da else ()
        _, da1, da2 = lax.fori_loop(0, nc, step, ((z, z), zz, zz))
        if with_da:
            for d in range(2):
                da1_ref[d] = da1[d]
                da2_ref[d] = da2[d]

    out_shape = [jax.ShapeDtypeStruct((2, bsz, nc, G, P2), F32)]
    out_specs = [st]
    ins = [as_rows(s), as_rows(a1[:, 0]), as_rows(a2[:, 0])]
    in_specs = [st, av, av]
    if with_da:
        ins.append(as_rows(hp))
        in_specs.append(st)
        out_shape += [jax.ShapeDtypeStruct((2, bsz, G, P2), F32)] * 2
        out_specs += [acc, acc]
    res = pl.pallas_call(
        body, name=name, grid=(bsz, G // SCAN_G), in_specs=in_specs, out_specs=out_specs, out_shape=out_shape,
        compiler_params=_cp(48, 2),
    )(*ins)
    out = res[0].reshape(s.shape)
    return (out, res[1].reshape(2, bsz, gw), res[2].reshape(2, bsz, gw)) if with_da else out


def s5_out_fwd(y1, hp, wo, name):
    bsz, _, nc, _ = y1.shape

    def body(y1_ref, hp_ref, wo_ref, y_ref):
        y_ref[...] = (y1_ref[...] + _dot(hp_ref[0].astype(BF16), wo_ref[0])
                      + _dot(hp_ref[1].astype(BF16), wo_ref[1]))

    return pl.pallas_call(
        body, name=name, grid=(G, bsz),
        in_specs=[_gb((nc, QC)), _gs(nc), _gw((P2, QC))],
        out_specs=_gb((nc, QC)),
        out_shape=jax.ShapeDtypeStruct(y1.shape, F32),
        compiler_params=_cp(32, 2),
    )(y1, hp, wo)


def _acc_init(b, *refs):
    @pl.when(b == 0)
    def _():
        for r in refs:
            r[...] = jnp.zeros_like(r)


def s5_out_bwd(dyg, ug, hp, wo, name):
    bsz, _, nc, _ = dyg.shape

    def body(dy_ref, u_ref, hp_ref, wo_ref, dhp_ref, dwo_ref, dkt_ref, dd_ref):
        _acc_init(pl.program_id(1), dwo_ref, dkt_ref, dd_ref)
        dy, u = dy_ref[...], u_ref[...]
        dyb = dy.astype(BF16)
        for d in range(2):
            dhp_ref[d] = _dot_nt(dyb, wo_ref[d])
            dwo_ref[d] += _dot_tn(hp_ref[d].astype(BF16), dyb)
        dkt_ref[...] += _dot_tn(u.astype(BF16), dyb)
        dd_ref[...] += _rowsum(dy * u)

    return pl.pallas_call(
        body, name=name, grid=(G, bsz),
        in_specs=[_gb((nc, QC)), _gb((nc, QC)), _gs(nc), _gw((P2, QC))],
        out_specs=[_gs(nc), _gw((P2, QC)), pl.BlockSpec((None, QC, QC), lambda g, b: (g, 0, 0)),
                   pl.BlockSpec((None, 1, QC), lambda g, b: (g, 0, 0))],
        out_shape=[jax.ShapeDtypeStruct(hp.shape, F32), jax.ShapeDtypeStruct((2, G, P2, QC), F32),
                   jax.ShapeDtypeStruct((G, QC, QC), F32), jax.ShapeDtypeStruct((G, 1, QC), F32)],
        compiler_params=_cp(32, 2),
    )(dyg, ug, hp, wo)


def s5_chunk_bwd(dyg, ug, ds, kt, ws, dvec, name):
    bsz, _, nc, _ = dyg.shape

    def body(dy_ref, u_ref, ds_ref, kt_ref, ws_ref, d_ref, du_ref, dws_ref):
        _acc_init(pl.program_id(1), dws_ref)
        dy = dy_ref[...]
        dyb = dy.astype(BF16)
        ub = u_ref[...].astype(BF16)
        du = dy * d_ref[...] + _dot_nt(dyb, kt_ref[0]) + _dot_nt(dyb, kt_ref[1])
        for d in range(2):
            dsb = ds_ref[d].astype(BF16)
            du += _dot_nt(dsb, ws_ref[d])
            dws_ref[d] += _dot_tn(ub, dsb)
        du_ref[...] = du

    return pl.pallas_call(
        body, name=name, grid=(G, bsz),
        in_specs=[_gb((nc, QC)), _gb((nc, QC)), _gs(nc), _gw((QC, QC)), _gw((QC, P2)),
                  pl.BlockSpec((None, 1, QC), lambda g, b: (g, 0, 0))],
        out_specs=[_gb((nc, QC)), _gw((QC, P2))],
        out_shape=[jax.ShapeDtypeStruct(dyg.shape, F32), jax.ShapeDtypeStruct((2, G, QC, P2), F32)],
        compiler_params=_cp(32, 2),
    )(dyg, ug, ds, kt, ws, dvec)


def local_step(x, ctx, target, mods, norm_g, final_g, even, odd):
    bsz, seq, _ = x.shape
    lc = ctx.shape[1]
    t_len = lc + seq
    ncc = lc // Q
    cos, sin = rope_tables(lc, seq)
    h = jnp.concatenate([ctx, x], axis=1)
    saved = []
    for i in range(DEPTH):
        j = i // 2
        g = norm_g[i].reshape(1, D)
        if i % 2 == 0:
            w = even[j]
            a, q, kv, g_attn, u, g_ssm = norm_in(h, g, mods[i], w["w_in"], EVEN_SPLITS, f"even_in{j}")
            o_attn, lse = attn_fwd(q, kv, cos, sin, w["sink"], lc, f"attn_fwd{j}")
            mats, mats_vjp = jax.vjp(s5_mats, *w["ssm"])
            kt, ws, wo, a1, a2, dvec = mats
            kt, ws, wo = kt.astype(BF16), ws.astype(BF16), wo.astype(BF16)
            ug = _to_groups(u)
            y1, s = s5_chunk_fwd(ug, kt, ws, dvec, f"s5_chunk_fwd{j}")
            hp = s5_scan(s, a1, a2, ncc, False, f"s5_scan_fwd{j}")
            y_ssm = _from_groups(s5_out_fwd(y1, hp, wo, f"s5_out_fwd{j}"))
            h_new, mix, yout = even_out(h, mods[i], o_attn, g_attn, y_ssm, g_ssm, w["glu_w"], w["glu_b"],
                                        w["w_out"], f"even_out{j}")
            saved.append(dict(h=h, a=a, q=q, kv=kv, g_attn=g_attn, g_ssm=g_ssm, o_attn=o_attn, lse=lse, ug=ug,
                              hp=hp, y_ssm=y_ssm, mix=mix, yout=yout, mats=(kt, ws, wo, a1, a2, dvec),
                              mats_vjp=mats_vjp))
        else:
            w = odd[j]
            a, u, gate = norm_in(h, g, mods[i], w["w_in"], ODD_SPLITS, f"odd_in{j}")
            pm = pool_band(u, lc, False, f"pool_band_fwd{j}")
            h_new, mix, yout = pool_out(h, mods[i], pm, gate, w["pool_w"], w["pool_scale"], w["w_out"], f"pool_out{j}")
            saved.append(dict(h=h, a=a, pm=pm, gate=gate, mix=mix, yout=yout))
        h = h_new

    dh, loss_acc, dfg = loss_head(h, final_g.reshape(1, D), target, "loss_head")
    grads = dict(final_g=dfg[0], norm_g=[None] * DEPTH, even=[None, None], odd=[None, None])
    dmods = [None] * DEPTH
    rows = bsz * t_len
    flat = lambda v: v.reshape(rows, v.shape[-1])
    for i in reversed(range(DEPTH)):
        j = i // 2
        sv = saved[i]
        g = norm_g[i].reshape(1, D)
        if i % 2 == 0:
            w = even[j]
            kt, ws, wo, a1, a2, dvec = sv["mats"]
            (d_oattn, d_gattn, d_gssm, d_yssm, dyout, zz, dsg, dgate, dglu_b) = even_out_bwd(
                dh, mods[i], sv["o_attn"], sv["g_attn"], sv["y_ssm"], sv["g_ssm"], w["glu_w"], w["glu_b"],
                w["w_out"], sv["yout"], f"even_out_bwd{j}")
            g_w_out = matmul_tn(flat(sv["mix"]), flat(dyout), D, D, f"even_w_out_grad{j}")
            g_glu_w = matmul_tn(flat(zz), flat(dsg), SSM_W, SSM_W, f"glu_w_grad{j}")
            dq, dkv, dsink = attn_bwd(sv["q"], sv["kv"], sv["o_attn"], sv["lse"], d_oattn, cos, sin, w["sink"], lc,
                                      f"attn_bwd{j}")
            dyg = _to_groups(d_yssm)
            dhp, dwo, dkt, dd = s5_out_bwd(dyg, sv["ug"], sv["hp"], wo, f"s5_out_bwd{j}")
            ds, da1, da2 = s5_scan(dhp, a1, -a2, ncc, True, f"s5_scan_bwd{j}", hp=sv["hp"])
            dug, dws = s5_chunk_bwd(dyg, sv["ug"], ds, kt, ws, dvec, f"s5_chunk_bwd{j}")
            dkt2 = jnp.stack([dkt, dkt])
            da1 = da1.sum(axis=1).reshape(2, 1, G * P2)
            da2 = da2.sum(axis=1).reshape(2, 1, G * P2)
            g_ssm = sv["mats_vjp"]((dkt2, dws, dwo, da1, da2, dd))
            dparts = [dq, dkv, d_gattn, _from_groups(dug), d_gssm]
            dh, dz, dmod, dg = norm_in_bwd(dparts, dh, sv["h"], g, mods[i], w["w_in"], f"even_in_bwd{j}")
            g_w_in = matmul_tn(flat(sv["a"]), flat(dz), D, dz.shape[-1], f"even_w_in_grad{j}")
            grads["even"][j] = dict(w_in=g_w_in, w_out=g_w_out, sink=dsink[0], ssm=g_ssm, glu_w=g_glu_w,
                                    glu_b=dglu_b[0])
        else:
            w = odd[j]
            dpm, dgt, dyout, dpp, dgate, dps = pool_out_bwd(dh, mods[i], sv["pm"], sv["gate"], w["pool_w"],
                                                            w["pool_scale"], w["w_out"], sv["yout"],
                                                            f"pool_out_bwd{j}")
            g_w_out = matmul_tn(flat(sv["mix"]), flat(dyout), D, D, f"odd_w_out_grad{j}")
            g_pool_w = jnp.stack([matmul_tn(flat(sv["pm"]), flat(dpp), POOL_G, POOL_G, f"pool_w_grad{j}_{gi}",
                                            a_col=gi, b_col=gi) for gi in range(4)])
            du = pool_band(dpm, lc, True, f"pool_band_bwd{j}")
            dh, dz, dmod, dg = norm_in_bwd([du, dgt], dh, sv["h"], g, mods[i], w["w_in"], f"odd_in_bwd{j}")
            g_w_in = matmul_tn(flat(sv["a"]), flat(dz), D, dz.shape[-1], f"odd_w_in_grad{j}")
            grads["odd"][j] = dict(w_in=g_w_in, w_out=g_w_out, pool_w=g_pool_w, pool_scale=dps[0])
        grads["norm_g"][i] = dg[0]
        dmods[i] = jnp.concatenate([dmod[:, :, 0:2, :], dgate[:, :, 0:1, :]], axis=2)
    return loss_acc[0, 0], dh[:, lc:, :], dmods, grads


N_DEV = 8
HBM_SPEC = pl.BlockSpec(memory_space=pltpu.HBM)


def allgather8(x_shard, name):
    m_per, n = x_shard.shape

    def body(x_ref, out_ref, send_sems, recv_sems, local_sem):
        x, y, c = lax.axis_index("x"), lax.axis_index("y"), lax.axis_index("c")
        me, sibling = (x, y, c), (x, y, 1 - c)
        chips = [(1 - x, y), (x, 1 - y), (1 - x, 1 - y)]

        def rows(px, py, pc):
            return out_ref.at[pl.ds((4 * px + 2 * py + pc) * m_per, m_per), :]

        def copy(k, block, to, src=None):
            return pltpu.make_async_remote_copy(
                src_ref=rows(*block) if src is None else src, dst_ref=rows(*block),
                send_sem=send_sems.at[k], recv_sem=recv_sems.at[k], device_id=to, device_id_type=MESH)

        mine = pltpu.make_async_copy(x_ref, rows(*me), local_sem)
        mine.start()
        first = [copy(0, me, sibling, src=x_ref)]
        first += [copy(1 + j, me, (*chip, c), src=x_ref) for j, chip in enumerate(chips)]
        for cp in first:
            cp.start()
        passed = [copy(4 + j, (*chip, c), sibling) for j, chip in enumerate(chips)]
        for j, chip in enumerate(chips):
            copy(1 + j, (*chip, c), me).wait_recv()
            passed[j].start()
        copy(0, sibling, me).wait_recv()
        for j, chip in enumerate(chips):
            copy(4 + j, (*chip, 1 - c), me).wait_recv()
        for cp in first + passed:
            cp.wait_send()
        mine.wait()

    return pl.pallas_call(
        body, name=name,
        out_shape=jax.ShapeDtypeStruct((N_DEV * m_per, n), x_shard.dtype),
        in_specs=[pl.BlockSpec(memory_space=pltpu.VMEM)],
        out_specs=pl.BlockSpec(memory_space=pltpu.VMEM),
        scratch_shapes=[pltpu.SemaphoreType.DMA((7,)), pltpu.SemaphoreType.DMA((7,)), pltpu.SemaphoreType.DMA],
        compiler_params=_cp(56),
    )(x_shard)


def xy_exchange(src, scatter, name):
    shape = src.shape[1:] if scatter else src.shape

    def body(src_ref, out_ref, send_sems, recv_sems, local_sem):
        x, y, c = lax.axis_index("x"), lax.axis_index("y"), lax.axis_index("c")
        my = 2 * x + y
        peers = [(1 - x, y), (x, 1 - y), (1 - x, 1 - y)]

        def piece(pos):
            return src_ref.at[pos] if scatter else src_ref

        def copy(k, px, py):
            return pltpu.make_async_remote_copy(
                src_ref=piece(2 * px + py), dst_ref=out_ref.at[my], send_sem=send_sems.at[k],
                recv_sem=recv_sems.at[k], device_id=(px, py, c), device_id_type=MESH)

        def landing(k, px, py):
            return pltpu.make_async_remote_copy(
                src_ref=piece(my), dst_ref=out_ref.at[2 * px + py], send_sem=send_sems.at[k],
                recv_sem=recv_sems.at[k], device_id=(px, py, c), device_id_type=MESH)

        mine = pltpu.make_async_copy(piece(my), out_ref.at[my], local_sem)
        mine.start()
        sends = [copy(k, px, py) for k, (px, py) in enumerate(peers)]
        for cp in sends:
            cp.start()
        for k, (px, py) in enumerate(peers):
            landing(k, px, py).wait_recv()
        for cp in sends:
            cp.wait_send()
        mine.wait()

    return pl.pallas_call(
        body, name=name,
        out_shape=jax.ShapeDtypeStruct((4,) + tuple(shape), src.dtype),
        in_specs=[HBM_SPEC], out_specs=HBM_SPEC,
        scratch_shapes=[pltpu.SemaphoreType.DMA((3,)), pltpu.SemaphoreType.DMA((3,)), pltpu.SemaphoreType.DMA],
    )(src)


def sibling_exchange(src, name):
    def body(src_ref, out_ref, send_sem, recv_sem):
        peer = (lax.axis_index("x"), lax.axis_index("y"), 1 - lax.axis_index("c"))
        cp = pltpu.make_async_remote_copy(src_ref=src_ref, dst_ref=out_ref, send_sem=send_sem, recv_sem=recv_sem,
                                          device_id=peer, device_id_type=MESH)
        cp.start()
        cp.wait()

    return pl.pallas_call(
        body, name=name, out_shape=jax.ShapeDtypeStruct(src.shape, src.dtype),
        in_specs=[HBM_SPEC], out_specs=HBM_SPEC,
        scratch_shapes=[pltpu.SemaphoreType.DMA, pltpu.SemaphoreType.DMA],
    )(src)


def _row_tile(rows, bytes_per_row, limit):
    best = None
    for tr in range(8, rows + 1, 8):
        if rows % tr == 0 and tr * bytes_per_row <= limit:
            best = tr
    return best if best is not None else rows


def sum_slots(x, name):
    n, rows, cols = x.shape
    tr = _row_tile(rows, n * cols * 4, 4 * MB)

    def body(x_ref, o_ref):
        acc = x_ref[0]
        for k in range(1, n):
            acc = acc + x_ref[k]
        o_ref[...] = acc

    return pl.pallas_call(
        body, name=name, grid=(rows // tr,),
        in_specs=[pl.BlockSpec((n, tr, cols), lambda r: (0, r, 0))],
        out_specs=pl.BlockSpec((tr, cols), lambda r: (r, 0)),
        out_shape=jax.ShapeDtypeStruct((rows, cols), F32),
        compiler_params=_cp(32, 1),
    )(x)


ADA_COLS = 3 * D // 4
C_ROWS = 8


def ada_fwd(c_all, ada_w, ada_b_cols, name):
    nrow = c_all.shape[0]

    def body(c_ref, w_ref, b_ref, o_ref):
        s, _ = _silu_and_grad(c_ref[...])
        o_ref[...] = _dot(s.astype(BF16), w_ref[...].astype(BF16)) + b_ref[...]

    return pl.pallas_call(
        body, name=name, grid=(DEPTH,),
        in_specs=[pl.BlockSpec((nrow, D), lambda i: (0, 0)), pl.BlockSpec((None, D, ADA_COLS), lambda i: (i, 0, 0)),
                  pl.BlockSpec((None, 1, ADA_COLS), lambda i: (i, 0, 0))],
        out_specs=pl.BlockSpec((None, nrow, ADA_COLS), lambda i: (i, 0, 0)),
        out_shape=jax.ShapeDtypeStruct((DEPTH, nrow, ADA_COLS), F32),
        compiler_params=_cp(32, 1),
    )(c_all, ada_w, ada_b_cols)


def ada_bwd(c_all, d_cols, ada_w, name):
    nrow = c_all.shape[0]

    def body(c_ref, d_ref, w_ref, gw_ref, ds_ref):
        @pl.when(pl.program_id(0) == 0)
        def _():
            ds_ref[...] = jnp.zeros_like(ds_ref)
        s, _ = _silu_and_grad(c_ref[...])
        dl = d_ref[...]
        gw_ref[...] = _dot_tn(s.astype(BF16), dl.astype(BF16))
        rid = lax.broadcasted_iota(jnp.int32, (nrow, 1), 0) % C_ROWS
        dctx = jnp.where((rid == 2) | (rid == 3), dl, 0.0).astype(BF16)
        ds_ref[0:1, :] += _rowsum(_dot_nt(dctx, w_ref[...].astype(BF16)))

    return pl.pallas_call(
        body, name=name, grid=(DEPTH,),
        in_specs=[pl.BlockSpec((nrow, D), lambda i: (0, 0)), pl.BlockSpec((None, nrow, ADA_COLS), lambda i: (i, 0, 0)),
                  pl.BlockSpec((None, D, ADA_COLS), lambda i: (i, 0, 0))],
        out_specs=[pl.BlockSpec((None, D, ADA_COLS), lambda i: (i, 0, 0)), pl.BlockSpec((8, D), lambda i: (0, 0))],
        out_shape=[jax.ShapeDtypeStruct((DEPTH, D, ADA_COLS), F32), jax.ShapeDtypeStruct((8, D), F32)],
        compiler_params=_cp(32, 1),
    )(c_all, d_cols, ada_w)


def ada_bias_grad(d_all, name):
    nrow = d_all.shape[1]

    def body(d_ref, o_ref):
        o_ref[...] = jnp.broadcast_to(_rowsum(d_ref[...]), o_ref.shape)

    return pl.pallas_call(
        body, name=name, grid=(DEPTH,),
        in_specs=[pl.BlockSpec((None, nrow, 3 * D), lambda i: (i, 0, 0))],
        out_specs=pl.BlockSpec((None, 8, 3 * D), lambda i: (i, 0, 0)),
        out_shape=jax.ShapeDtypeStruct((DEPTH, 8, 3 * D), F32),
        compiler_params=_cp(32, 1),
    )(d_all)


def silu_chain(ds, c, name):
    def body(ds_ref, c_ref, o_ref):
        _, dsl = _silu_and_grad(c_ref[...])
        o_ref[...] = ds_ref[...] * dsl

    return pl.pallas_call(body, name=name, out_shape=jax.ShapeDtypeStruct(ds.shape, F32))(ds, c)


def _flat_cols(shape):
    size = int(np.prod(shape))
    for cols in (1024, 128):
        if size % cols == 0:
            return cols
    return shape[-1]


def adamw(w, m, v, grads, name):
    shape = w.shape
    cols = _flat_cols(shape)
    as2d = lambda a: a.reshape(-1, cols)
    rows = w.size // cols
    tr = _row_tile(rows, cols * 4, MB)
    k = len(grads)

    def body(*refs):
        w_ref, m_ref, v_ref = refs[:3]
        g_refs = refs[3:3 + k]
        g_out, d_out, m_out, v_out = refs[3 + k:]
        g = g_refs[0][...]
        for r in g_refs[1:]:
            g = g + r[...]
        g_out[...] = g
        mn = ADAM_B1 * m_ref[...] + (1.0 - ADAM_B1) * g
        vn = ADAM_B2 * v_ref[...] + (1.0 - ADAM_B2) * (g * g)
        m_out[...] = mn
        v_out[...] = vn
        m_hat = mn / (1.0 - ADAM_B1 ** ADAM_STEP)
        v_hat = vn / (1.0 - ADAM_B2 ** ADAM_STEP)
        d_out[...] = -ADAM_LR * (m_hat / (jnp.sqrt(v_hat) + ADAM_EPS) + ADAM_WD * w_ref[...])

    spec = pl.BlockSpec((tr, cols), lambda r: (r, 0))
    outs = pl.pallas_call(
        body, name=name, grid=(rows // tr,),
        in_specs=[spec] * (3 + k), out_specs=[spec] * 4,
        out_shape=[jax.ShapeDtypeStruct((rows, cols), F32)] * 4,
        compiler_params=_cp(32, 1),
    )(as2d(w), as2d(m), as2d(v), *[as2d(g) for g in grads])
    return tuple(o.reshape(shape) for o in outs)


BIG = (("even_w_in", (2, D, 576), 2), ("even_w_out", (2, 256, D), 1), ("glu_w", (2, 128, SSM_W), 1),
       ("odd_w_in", (2, D, 512), 2), ("odd_w_out", (2, 256, D), 1), ("pool_w", (2, 4, 64, POOL_G), 2))
BIG_COLS = 1024


def _full_shape(shard, axis):
    return tuple(4 * s if a == axis else s for a, s in enumerate(shard))


def _to_shards(full, shard, axis):
    return jnp.moveaxis(full.reshape(shard[:axis] + (4,) + shard[axis:]), axis, 0)


def _from_shards(stacked, shard, axis):
    return jnp.moveaxis(stacked, 0, axis).reshape(_full_shape(shard, axis))


def pack_big_local(arrs):
    return jnp.concatenate([a.reshape(-1) for a in arrs]).reshape(-1, BIG_COLS)


def unpack_big_local(flat):
    out, off = [], 0
    flat = flat.reshape(-1)
    for _, shard, _ in BIG:
        n = int(np.prod(shard))
        out.append(flat[off:off + n].reshape(shard))
        off += n
    return out


def unpack_big_gathered(g):
    out, off = [], 0
    g = g.reshape(4, -1)
    for _, shard, axis in BIG:
        n = int(np.prod(shard))
        out.append(_from_shards(g[:, off:off + n].reshape((4,) + shard), shard, axis))
        off += n
    return out


SMALL = (("ds_ctx", (D,)), ("norm_g", (DEPTH, D)), ("final_g", (D,)), ("attn_sink", (2, N_HEADS)),
         ("ssm_a_re", (2, 2, G, P)), ("ssm_a_im", (2, 2, G, P)), ("ssm_log_dt", (2, 2, G)),
         ("ssm_b_re", (2, 2, G, P, C)), ("ssm_b_im", (2, 2, G, P, C)), ("ssm_c_re", (2, 2, G, C, P)),
         ("ssm_c_im", (2, 2, G, C, P)), ("ssm_d", (2, SSM_W)), ("glu_b", (2, SSM_W)), ("pool_scale", (2, D)))
SMALL_PAD = 8 * 128


def pack_small(vals):
    flat = jnp.concatenate([vals[n].reshape(-1) for n, _ in SMALL])
    pad = (-flat.shape[0]) % SMALL_PAD
    return jnp.pad(flat, (0, pad)).reshape(-1, 128)


def unpack_small(packed):
    flat, out, off = packed.reshape(-1), {}, 0
    for n, shape in SMALL:
        size = int(np.prod(shape))
        out[n] = flat[off:off + size].reshape(shape)
        off += size
    return out


WEIGHT_NAMES = ('c_ctx', 'ada_w', 'ada_b', 'norm_g', 'even_w_in', 'even_w_out', 'attn_sink', 'ssm_a_re', 'ssm_a_im',
                'ssm_log_dt', 'ssm_b_re', 'ssm_b_im', 'ssm_c_re', 'ssm_c_im', 'ssm_d', 'glu_w', 'glu_b', 'odd_w_in',
                'odd_w_out', 'pool_w', 'pool_scale', 'final_g')
SSM_NAMES = ('ssm_a_re', 'ssm_a_im', 'ssm_log_dt', 'ssm_b_re', 'ssm_b_im', 'ssm_c_re', 'ssm_c_im', 'ssm_d')


def kernel(x, c, ctx, c_ctx, ada_w, ada_b, norm_g, even_w_in, even_w_out, attn_sink, ssm_a_re, ssm_a_im, ssm_log_dt, ssm_b_re, ssm_b_im, ssm_c_re, ssm_c_im, ssm_d, glu_w, glu_b, odd_w_in, odd_w_out, pool_w, pool_scale, final_g, loss_target, m_c_ctx, m_ada_w, m_ada_b, m_norm_g, m_even_w_in, m_even_w_out, m_attn_sink, m_ssm_a_re, m_ssm_a_im, m_ssm_log_dt, m_ssm_b_re, m_ssm_b_im, m_ssm_c_re, m_ssm_c_im, m_ssm_d, m_glu_w, m_glu_b, m_odd_w_in, m_odd_w_out, m_pool_w, m_pool_scale, m_final_g, v_c_ctx, v_ada_w, v_ada_b, v_norm_g, v_even_w_in, v_even_w_out, v_attn_sink, v_ssm_a_re, v_ssm_a_im, v_ssm_log_dt, v_ssm_b_re, v_ssm_b_im, v_ssm_c_re, v_ssm_c_im, v_ssm_d, v_glu_w, v_glu_b, v_odd_w_in, v_odd_w_out, v_pool_w, v_pool_scale, v_final_g):
    env = dict(locals())
    weights = {n: env[n] for n in WEIGHT_NAMES}
    bsz = x.shape[0]
    ax, ay, ac = lax.axis_index("x"), lax.axis_index("y"), lax.axis_index("c")
    pos = 2 * ax + ay
    dev = 2 * pos + ac

    c_rows = jnp.concatenate([c, c_ctx.reshape(1, D), c_ctx.reshape(1, D), jnp.zeros((C_ROWS - bsz - 2, D), F32)])
    c_all = allgather8(c_rows, "gather_c")
    ada_b_cols = lax.dynamic_slice(ada_b, (0, pos * ADA_COLS), (DEPTH, ADA_COLS)).reshape(DEPTH, 1, ADA_COLS)
    mod_cols = ada_fwd(c_all, ada_w, ada_b_cols, "ada_fwd")
    nrow = N_DEV * C_ROWS
    misc = jnp.concatenate([mod_cols.reshape(DEPTH * nrow, ADA_COLS),
                            jnp.pad(pool_scale, ((0, 6), (0, ADA_COLS - pool_scale.shape[1])))])
    misc_all = allgather8(misc, "gather_mod").reshape(4, 2, DEPTH * nrow + 8, ADA_COLS)[:, 0]
    mod_full = misc_all[:, :DEPTH * nrow].reshape(4, DEPTH, nrow, ADA_COLS).transpose(1, 2, 0, 3)
    mod_mine = lax.dynamic_slice(mod_full.reshape(DEPTH, nrow, 3 * D), (0, dev * C_ROWS, 0), (DEPTH, C_ROWS, 3 * D))
    mods = []
    for i in range(DEPTH):
        lat = mod_mine[i, :bsz].reshape(bsz, 1, 3, D)
        con = jnp.broadcast_to(mod_mine[i, bsz].reshape(1, 1, 3, D), (bsz, 1, 3, D))
        mods.append(jnp.pad(jnp.concatenate([con, lat], axis=1), ((0, 0), (0, 0), (0, 5), (0, 0))))
    pool_scale_full = misc_all[:, DEPTH * nrow:DEPTH * nrow + 2, :pool_scale.shape[1]].transpose(1, 0, 2).reshape(2, D)

    w_local = pack_big_local([weights[n].astype(BF16) for n, _, _ in BIG])
    full = dict(zip([n for n, _, _ in BIG], unpack_big_gathered(xy_exchange(w_local, False, "gather_weights"))))
    even = [dict(w_in=full["even_w_in"][j], w_out=full["even_w_out"][j], sink=attn_sink[j],
                 ssm=tuple(weights[n][j] for n in SSM_NAMES), glu_w=full["glu_w"][j],
                 glu_b=glu_b[j].reshape(1, SSM_W)) for j in range(2)]
    odd = [dict(w_in=full["odd_w_in"][j], w_out=full["odd_w_out"][j], pool_w=full["pool_w"][j],
                pool_scale=pool_scale_full[j].reshape(1, D)) for j in range(2)]

    loss_local, grad_x, dmods, grads = local_step(x, ctx, loss_target, mods, norm_g, final_g, even, odd)
    loss = lax.psum(loss_local, ("x", "y", "c"))

    d_rows = jnp.stack([jnp.concatenate([dm[:, 1].reshape(bsz, 3 * D), dm[:, 0].reshape(bsz, 3 * D),
                                         jnp.zeros((C_ROWS - 2 * bsz, 3 * D), F32)]) for dm in dmods])
    d_all = allgather8(d_rows.reshape(DEPTH * C_ROWS, 3 * D), "gather_dmod")
    d_all = d_all.reshape(N_DEV, DEPTH, C_ROWS, 3 * D).transpose(1, 0, 2, 3).reshape(DEPTH, nrow, 3 * D)
    d_cols = lax.dynamic_slice(d_all, (0, 0, pos * ADA_COLS), (DEPTH, nrow, ADA_COLS))
    g_ada_w, ds_ctx = ada_bwd(c_all, d_cols, ada_w, "ada_bwd")
    g_ada_b = ada_bias_grad(d_all, "ada_bias_grad")[:, 0]

    small = dict(ds_ctx=ds_ctx[0] * (ac == 0).astype(F32), norm_g=jnp.stack(grads["norm_g"]), final_g=grads["final_g"],
                 attn_sink=jnp.stack([grads["even"][j]["sink"] for j in range(2)]),
                 glu_b=jnp.stack([grads["even"][j]["glu_b"] for j in range(2)]),
                 pool_scale=jnp.stack([grads["odd"][j]["pool_scale"] for j in range(2)]))
    for k, n in enumerate(SSM_NAMES):
        small[n] = jnp.stack([grads["even"][j]["ssm"][k] for j in range(2)])
    packed = pack_small(small)
    small_sum = sum_slots(allgather8(packed, "gather_small").reshape(N_DEV, packed.shape[0], 128), "sum_small")
    g_small = unpack_small(small_sum)
    g_small["c_ctx"] = silu_chain(g_small.pop("ds_ctx").reshape(1, D), c_ctx.reshape(1, D), "c_ctx_grad").reshape(D)
    g_small["ada_b"] = g_ada_b
    g_small["pool_scale"] = lax.dynamic_slice(g_small["pool_scale"], (0, pos * 256), (2, 256))

    big_full = dict(even_w_in=jnp.stack([grads["even"][j]["w_in"] for j in range(2)]),
                    even_w_out=jnp.stack([grads["even"][j]["w_out"] for j in range(2)]),
                    glu_w=jnp.stack([grads["even"][j]["glu_w"] for j in range(2)]),
                    odd_w_in=jnp.stack([grads["odd"][j]["w_in"] for j in range(2)]),
                    odd_w_out=jnp.stack([grads["odd"][j]["w_out"] for j in range(2)]),
                    pool_w=jnp.stack([grads["odd"][j]["pool_w"] for j in range(2)]))
    by_pos = jnp.concatenate([_to_shards(big_full[n], shard, axis).reshape(4, -1) for n, shard, axis in BIG], axis=1)
    by_pos = by_pos.reshape(4, -1, BIG_COLS)
    mine4 = sum_slots(xy_exchange(by_pos, True, "scatter_grads"), "sum_positions")
    other4 = sibling_exchange(mine4, "swap_cores")
    g_mine = dict(zip([n for n, _, _ in BIG], unpack_big_local(mine4)))
    g_other = dict(zip([n for n, _, _ in BIG], unpack_big_local(other4)))

    results = {}
    for n in WEIGHT_NAMES:
        if n in g_mine:
            gs = [g_mine[n], g_other[n]]
        elif n == "ada_w":
            gs = [g_ada_w]
        else:
            gs = [g_small[n]]
        results[n] = adamw(weights[n], env["m_" + n], env["v_" + n], gs, "adamw_" + n)
    outs = [loss, grad_x]
    for k in range(4):
        outs += [results[n][k] for n in WEIGHT_NAMES]
    return tuple(outs)
```

```python
import functools

import numpy as np
import jax
import jax.numpy as jnp
from jax import lax
from jax.experimental import pallas as pl
from jax.experimental.pallas import tpu as pltpu

F32 = jnp.float32
BF16 = jnp.bfloat16
MESH = pl.DeviceIdType.MESH

D = 1024
DEPTH = 4
EPS = 1e-6
NEG_INF = -1e30
GRID_W = 64
ROPE_BASE = 10000.0
ROPE_FREQS = 16
HEAD_DIM = 64
N_HEADS = 8
N_KV = 2
GROUP = 4
ATTN_W = N_HEADS * HEAD_DIM
KV_W = N_KV * HEAD_DIM
WINDOW = 128
AB = 128
SSM_W = 512
G = 32
C = 16
P = 64
Q = 16
QC = Q * C
P2 = 2 * P
SCAN_G = 16
POOL_R = (1, 2, 4, 8)
POOL_G = 256
HALO = 8
TM = 256
EVEN_SPLITS = (512, 256, 512, 512, 512)
ODD_SPLITS = (1024, 1024)

ADAM_LR = 0.001
ADAM_B1 = 0.9
ADAM_B2 = 0.999
ADAM_EPS = 1e-08
ADAM_WD = 0.01
ADAM_STEP = 10

MB = 1024 * 1024


def _cp(vmem_mb=48, n_axes=0):
    kw = dict(vmem_limit_bytes=vmem_mb * MB)
    if n_axes:
        kw["dimension_semantics"] = ("arbitrary",) * n_axes
    return pltpu.CompilerParams(**kw)


def _sig(x):
    return 1.0 / (1.0 + jnp.exp(-x))


def _silu_and_grad(x):
    s = _sig(x)
    return x * s, s * (1.0 + x * (1.0 - s))


_GELU_C = 0.7978845608028654
_GELU_A = 0.044715


def _gelu_and_grad(x):
    th = jnp.tanh(_GELU_C * (x + _GELU_A * x * x * x))
    val = 0.5 * x * (1.0 + th)
    grad = 0.5 * (1.0 + th) + 0.5 * x * (1.0 - th * th) * _GELU_C * (1.0 + 3.0 * _GELU_A * x * x)
    return val, grad


def _rms(h):
    r = lax.rsqrt(jnp.mean(h * h, axis=-1, keepdims=True) + EPS)
    return h * r, r


def _dot(a, b):
    return jnp.dot(a, b, preferred_element_type=F32)


def _dot_nt(a, b):
    return lax.dot_general(a, b, (((1,), (1,)), ((), ())), preferred_element_type=F32)


def _dot_tn(a, b):
    return lax.dot_general(a, b, (((0,), (0,)), ((), ())), preferred_element_type=F32)


def _rowsum(x):
    return jnp.sum(x, axis=0, keepdims=True)


def _seg(t):
    return jnp.minimum(t, 1)


def _row_spec(n):
    return pl.BlockSpec((None, TM, n), lambda b, t: (b, t, 0))


def _const_spec(shape):
    nd = len(shape)
    return pl.BlockSpec(shape, lambda b, t: (0,) * nd)


def _mod_spec():
    return pl.BlockSpec((None, None, 8, D), lambda b, t: (b, _seg(t), 0, 0))


def norm_in(h, g, mod, w, splits, name):
    bsz, t_len, _ = h.shape
    n = w.shape[1]
    offs = [int(v) for v in np.cumsum((0,) + tuple(splits))]

    def body(h_ref, g_ref, mod_ref, w_ref, a_ref, *outs):
        xh, _ = _rms(h_ref[...])
        a = xh * g_ref[...] * (1.0 + mod_ref[1:2, :]) + mod_ref[0:1, :]
        ab = a.astype(BF16)
        a_ref[...] = ab
        z = _dot(ab, w_ref[...])
        for o, lo, hi in zip(outs, offs[:-1], offs[1:]):
            o[...] = z[:, lo:hi]

    return pl.pallas_call(
        body, name=name, grid=(bsz, t_len // TM),
        in_specs=[_row_spec(D), _const_spec((1, D)), _mod_spec(), _const_spec((D, n))],
        out_specs=[_row_spec(D)] + [_row_spec(s) for s in splits],
        out_shape=[jax.ShapeDtypeStruct((bsz, t_len, D), BF16)]
        + [jax.ShapeDtypeStruct((bsz, t_len, s), F32) for s in splits],
        compiler_params=_cp(48, 2),
    )(h, g, mod, w)


def norm_in_bwd(dparts, dh_in, h, g, mod, w, name):
    bsz, t_len, _ = h.shape
    n = w.shape[1]
    k = len(dparts)

    def body(*refs):
        parts = refs[:k]
        dh_in_ref, h_ref, g_ref, mod_ref, w_ref, dh_ref, dz_ref, dmod_ref, dg_ref = refs[k:]
        b, t = pl.program_id(0), pl.program_id(1)
        dz = jnp.concatenate([r[...] for r in parts], axis=1).astype(BF16)
        dz_ref[...] = dz
        da = _dot_nt(dz, w_ref[...])
        xh, r = _rms(h_ref[...])
        gg = g_ref[...]
        sc1 = 1.0 + mod_ref[1:2, :]

        @pl.when(t <= 1)
        def _():
            dmod_ref[...] = jnp.zeros_like(dmod_ref)

        @pl.when((b == 0) & (t == 0))
        def _():
            dg_ref[...] = jnp.zeros_like(dg_ref)

        dmod_ref[0:1, :] += _rowsum(da)
        dmod_ref[1:2, :] += _rowsum(da * (xh * gg))
        dg_ref[0:1, :] += _rowsum(da * sc1 * xh)
        dxh = da * gg * sc1
        dh_ref[...] = dh_in_ref[...] + r * (dxh - xh * jnp.mean(dxh * xh, axis=-1, keepdims=True))

    return pl.pallas_call(
        body, name=name, grid=(bsz, t_len // TM),
        in_specs=[_row_spec(p.shape[-1]) for p in dparts]
        + [_row_spec(D), _row_spec(D), _const_spec((1, D)), _mod_spec(), _const_spec((D, n))],
        out_specs=[_row_spec(D), _row_spec(n), _mod_spec(), _const_spec((8, D))],
        out_shape=[jax.ShapeDtypeStruct((bsz, t_len, D), F32), jax.ShapeDtypeStruct((bsz, t_len, n), BF16),
                   jax.ShapeDtypeStruct((bsz, 2, 8, D), F32), jax.ShapeDtypeStruct((8, D), F32)],
        compiler_params=_cp(56, 2),
    )(*dparts, dh_in, h, g, mod, w)


def matmul_tn(a, b, m, n, name, a_col=0, b_col=0):
    rows = a.shape[0]
    tr = 512 if rows % 512 == 0 else rows
    tn = n
    for cand in (1024, 768, 512, 256, 128):
        if n > 1024 and n % cand == 0:
            tn = cand
            break
    nb = n // tn

    def body(a_ref, b_ref, o_ref):
        @pl.when(pl.program_id(1) == 0)
        def _():
            o_ref[...] = jnp.zeros_like(o_ref)
        o_ref[...] += _dot_tn(a_ref[...].astype(BF16), b_ref[...].astype(BF16))

    return pl.pallas_call(
        body, name=name, grid=(nb, rows // tr),
        in_specs=[pl.BlockSpec((tr, m), lambda j, r: (r, a_col)),
                  pl.BlockSpec((tr, tn), lambda j, r: (r, b_col * nb + j))],
        out_specs=pl.BlockSpec((m, tn), lambda j, r: (0, j)),
        out_shape=jax.ShapeDtypeStruct((m, n), F32),
        compiler_params=_cp(48, 2),
    )(a, b)


def even_out(h, mod, o_attn, g_attn, y_ssm, g_ssm, glu_w, glu_b, w_out, name):
    bsz, t_len, _ = h.shape

    def body(h_ref, mod_ref, oa_ref, ga_ref, ys_ref, gs_ref, gw_ref, gb_ref, wo_ref, hn_ref, mix_ref, yo_ref):
        zz, _ = _gelu_and_grad(ys_ref[...])
        s = _dot(zz.astype(BF16), gw_ref[...]) + gb_ref[...]
        o_ssm = zz * _sig(s)
        sa, _ = _silu_and_grad(ga_ref[...])
        ss, _ = _silu_and_grad(gs_ref[...])
        mb = jnp.concatenate([oa_ref[...] * sa, o_ssm * ss], axis=1).astype(BF16)
        mix_ref[...] = mb
        yo = _dot(mb, wo_ref[...])
        yo_ref[...] = yo
        hn_ref[...] = h_ref[...] + mod_ref[2:3, :] * yo

    return pl.pallas_call(
        body, name=name, grid=(bsz, t_len // TM),
        in_specs=[_row_spec(D), _mod_spec(), _row_spec(512), _row_spec(512), _row_spec(512), _row_spec(512),
                  _const_spec((512, 512)), _const_spec((1, 512)), _const_spec((D, D))],
        out_specs=[_row_spec(D), _row_spec(D), _row_spec(D)],
        out_shape=[jax.ShapeDtypeStruct((bsz, t_len, D), F32), jax.ShapeDtypeStruct((bsz, t_len, D), BF16),
                   jax.ShapeDtypeStruct((bsz, t_len, D), F32)],
        compiler_params=_cp(48, 2),
    )(h, mod, o_attn, g_attn, y_ssm, g_ssm, glu_w, glu_b, w_out)


def even_out_bwd(dh, mod, o_attn, g_attn, y_ssm, g_ssm, glu_w, glu_b, w_out, yout, name):
    bsz, t_len, _ = dh.shape

    def body(dh_ref, mod_ref, oa_ref, ga_ref, ys_ref, gs_ref, gw_ref, gb_ref, wo_ref, yo_ref,
             doa_ref, dga_ref, dgs_ref, dys_ref, dyo_ref, zz_ref, ds_ref, dgate_ref, dgb_ref):
        b, t = pl.program_id(0), pl.program_id(1)
        dhv = dh_ref[...]

        @pl.when(t <= 1)
        def _():
            dgate_ref[...] = jnp.zeros_like(dgate_ref)

        @pl.when((b == 0) & (t == 0))
        def _():
            dgb_ref[...] = jnp.zeros_like(dgb_ref)

        dgate_ref[0:1, :] += _rowsum(dhv * yo_ref[...])
        dyb = (mod_ref[2:3, :] * dhv).astype(BF16)
        dyo_ref[...] = dyb
        dmix = _dot_nt(dyb, wo_ref[...])
        sa, dsa = _silu_and_grad(ga_ref[...])
        doa_ref[...] = dmix[:, :512] * sa
        dga_ref[...] = dmix[:, :512] * oa_ref[...] * dsa
        zz, dzz_dy = _gelu_and_grad(ys_ref[...])
        zb = zz.astype(BF16)
        zz_ref[...] = zb
        sg = _sig(_dot(zb, gw_ref[...]) + gb_ref[...])
        ss, dss = _silu_and_grad(gs_ref[...])
        dm = dmix[:, 512:]
        dgs_ref[...] = dm * (zz * sg) * dss
        do = dm * ss
        ds = do * zz * sg * (1.0 - sg)
        dsb = ds.astype(BF16)
        ds_ref[...] = dsb
        dgb_ref[0:1, :] += _rowsum(ds)
        dzz = do * sg + _dot_nt(dsb, gw_ref[...])
        dys_ref[...] = dzz * dzz_dy

    r512 = jax.ShapeDtypeStruct((bsz, t_len, 512), F32)
    return pl.pallas_call(
        body, name=name, grid=(bsz, t_len // TM),
        in_specs=[_row_spec(D), _mod_spec(), _row_spec(512), _row_spec(512), _row_spec(512), _row_spec(512),
                  _const_spec((512, 512)), _const_spec((1, 512)), _const_spec((D, D)), _row_spec(D)],
        out_specs=[_row_spec(512)] * 4 + [_row_spec(D), _row_spec(512), _row_spec(512), _mod_spec(),
                                           _const_spec((8, 512))],
        out_shape=[r512, r512, r512, r512, jax.ShapeDtypeStruct((bsz, t_len, D), BF16),
                   jax.ShapeDtypeStruct((bsz, t_len, 512), BF16), jax.ShapeDtypeStruct((bsz, t_len, 512), BF16),
                   jax.ShapeDtypeStruct((bsz, 2, 8, D), F32), jax.ShapeDtypeStruct((8, 512), F32)],
        compiler_params=_cp(48, 2),
    )(dh, mod, o_attn, g_attn, y_ssm, g_ssm, glu_w, glu_b, w_out, yout)


def _split3_dot(band, x):
    x1 = x.astype(BF16)
    r1 = x - x1.astype(F32)
    x2 = r1.astype(BF16)
    x3 = (r1 - x2.astype(F32)).astype(BF16)
    return _dot(band, x3) + _dot(band, x2) + _dot(band, x1)


def pool_band(x, lc, transpose, name):
    bsz, t_len, _ = x.shape
    assert lc == TM
    hb = TM // HALO

    def body(xp_ref, xc_ref, xn_ref, o_ref):
        t = pl.program_id(1)
        seg_lo = jnp.where(t == 0, 0, lc)
        seg_hi = jnp.where(t == 0, lc, t_len)
        cur = xc_ref[...]
        xh = jnp.concatenate([xp_ref[...], cur, xn_ref[...]], axis=0)
        row_t = t * TM + lax.broadcasted_iota(jnp.int32, (TM, 1), 0)
        col_s = t * TM - HALO + lax.broadcasted_iota(jnp.int32, (1, TM + 2 * HALO), 1)
        row_s = t * TM - HALO + lax.broadcasted_iota(jnp.int32, (TM + 2 * HALO, 1), 0)
        s_ok = (col_s >= seg_lo) & (col_s < seg_hi)
        outs = []
        for gi, r in enumerate(POOL_R):
            band = ((jnp.abs(row_t - col_s) <= r) & s_ok).astype(BF16)
            xg = xh[:, gi * POOL_G:(gi + 1) * POOL_G]
            if transpose:
                cnt_s = jnp.minimum(row_s + r, seg_hi - 1) - jnp.maximum(row_s - r, seg_lo) + 1
                xg = xg * (1.0 / jnp.maximum(cnt_s, 1).astype(F32))
            acc = _split3_dot(band, xg)
            if not transpose:
                cnt_t = jnp.minimum(row_t + r, seg_hi - 1) - jnp.maximum(row_t - r, seg_lo) + 1
                acc = acc * (1.0 / cnt_t.astype(F32))
            outs.append(acc - cur[:, gi * POOL_G:(gi + 1) * POOL_G])
        o_ref[...] = jnp.concatenate(outs, axis=1)

    return pl.pallas_call(
        body, name=name, grid=(bsz, t_len // TM),
        in_specs=[pl.BlockSpec((None, HALO, D), lambda b, t: (b, jnp.maximum(t * hb - 1, 0), 0)),
                  _row_spec(D),
                  pl.BlockSpec((None, HALO, D), lambda b, t: (b, jnp.minimum((t + 1) * hb, t_len // HALO - 1), 0))],
        out_specs=_row_spec(D),
        out_shape=jax.ShapeDtypeStruct((bsz, t_len, D), F32),
        compiler_params=_cp(48, 2),
    )(x, x, x)


def pool_out(h, mod, pm, gate, pool_w, pool_scale, w_out, name):
    bsz, t_len, _ = h.shape

    def body(h_ref, mod_ref, pm_ref, gt_ref, pw_ref, ps_ref, wo_ref, hn_ref, mix_ref, yo_ref):
        pmv = pm_ref[...]
        ppre = jnp.concatenate([_dot(pmv[:, g * POOL_G:(g + 1) * POOL_G].astype(BF16), pw_ref[g])
                                for g in range(4)], axis=1)
        sl, _ = _silu_and_grad(gt_ref[...])
        mb = (ppre * ps_ref[...] * sl).astype(BF16)
        mix_ref[...] = mb
        yo = _dot(mb, wo_ref[...])
        yo_ref[...] = yo
        hn_ref[...] = h_ref[...] + mod_ref[2:3, :] * yo

    return pl.pallas_call(
        body, name=name, grid=(bsz, t_len // TM),
        in_specs=[_row_spec(D), _mod_spec(), _row_spec(D), _row_spec(D), _const_spec((4, POOL_G, POOL_G)),
                  _const_spec((1, D)), _const_spec((D, D))],
        out_specs=[_row_spec(D), _row_spec(D), _row_spec(D)],
        out_shape=[jax.ShapeDtypeStruct((bsz, t_len, D), F32), jax.ShapeDtypeStruct((bsz, t_len, D), BF16),
                   jax.ShapeDtypeStruct((bsz, t_len, D), F32)],
        compiler_params=_cp(48, 2),
    )(h, mod, pm, gate, pool_w, pool_scale, w_out)


def pool_out_bwd(dh, mod, pm, gate, pool_w, pool_scale, w_out, yout, name):
    bsz, t_len, _ = dh.shape

    def body(dh_ref, mod_ref, pm_ref, gt_ref, pw_ref, ps_ref, wo_ref, yo_ref,
             dpm_ref, dgt_ref, dyo_ref, dpp_ref, dgate_ref, dps_ref):
        b, t = pl.program_id(0), pl.program_id(1)
        dhv = dh_ref[...]

        @pl.when(t <= 1)
        def _():
            dgate_ref[...] = jnp.zeros_like(dgate_ref)

        @pl.when((b == 0) & (t == 0))
        def _():
            dps_ref[...] = jnp.zeros_like(dps_ref)

        dgate_ref[0:1, :] += _rowsum(dhv * yo_ref[...])
        dyb = (mod_ref[2:3, :] * dhv).astype(BF16)
        dyo_ref[...] = dyb
        dmix = _dot_nt(dyb, wo_ref[...])
        pmv = pm_ref[...]
        ppre = jnp.concatenate([_dot(pmv[:, g * POOL_G:(g + 1) * POOL_G].astype(BF16), pw_ref[g])
                                for g in range(4)], axis=1)
        ps = ps_ref[...]
        sl, dsl = _silu_and_grad(gt_ref[...])
        dp = dmix * sl
        dgt_ref[...] = dmix * (ppre * ps) * dsl
        dps_ref[0:1, :] += _rowsum(dp * ppre)
        dppb = (dp * ps).astype(BF16)
        dpp_ref[...] = dppb
        dpm_ref[...] = jnp.concatenate([_dot_nt(dppb[:, g * POOL_G:(g + 1) * POOL_G], pw_ref[g])
                                        for g in range(4)], axis=1)

    return pl.pallas_call(
        body, name=name, grid=(bsz, t_len // TM),
        in_specs=[_row_spec(D), _mod_spec(), _row_spec(D), _row_spec(D), _const_spec((4, POOL_G, POOL_G)),
                  _const_spec((1, D)), _const_spec((D, D)), _row_spec(D)],
        out_specs=[_row_spec(D), _row_spec(D), _row_spec(D), _row_spec(D), _mod_spec(), _const_spec((8, D))],
        out_shape=[jax.ShapeDtypeStruct((bsz, t_len, D), F32), jax.ShapeDtypeStruct((bsz, t_len, D), F32),
                   jax.ShapeDtypeStruct((bsz, t_len, D), BF16), jax.ShapeDtypeStruct((bsz, t_len, D), BF16),
                   jax.ShapeDtypeStruct((bsz, 2, 8, D), F32), jax.ShapeDtypeStruct((8, D), F32)],
        compiler_params=_cp(48, 2),
    )(dh, mod, pm, gate, pool_w, pool_scale, w_out, yout)


def loss_head(h, final_g, target, name):
    bsz, t_len, _ = h.shape

    def body(h_ref, g_ref, tg_ref, dh_ref, loss_ref, dg_ref):
        b, t = pl.program_id(0), pl.program_id(1)

        @pl.when((b == 0) & (t == 0))
        def _():
            loss_ref[...] = jnp.zeros_like(loss_ref)
            dg_ref[...] = jnp.zeros_like(dg_ref)

        lat = (t > 0).astype(F32)
        xh, r = _rms(h_ref[...])
        gg = g_ref[...]
        err = (xh * gg - tg_ref[...]) * lat
        loss_ref[...] += 0.5 * jnp.sum(jnp.mean(err * err, axis=-1, keepdims=True))
        dy = err * (1.0 / D)
        dg_ref[0:1, :] += _rowsum(dy * xh)
        dxh = dy * gg
        dh_ref[...] = r * (dxh - xh * jnp.mean(dxh * xh, axis=-1, keepdims=True))

    return pl.pallas_call(
        body, name=name, grid=(bsz, t_len // TM),
        in_specs=[_row_spec(D), _const_spec((1, D)),
                  pl.BlockSpec((None, TM, D), lambda b, t: (b, jnp.maximum(t - 1, 0), 0))],
        out_specs=[_row_spec(D), _const_spec((8, 128)), _const_spec((8, D))],
        out_shape=[jax.ShapeDtypeStruct((bsz, t_len, D), F32), jax.ShapeDtypeStruct((8, 128), F32),
                   jax.ShapeDtypeStruct((8, D), F32)],
        compiler_params=_cp(48, 2),
    )(h, final_g, target)


def _swap16(x):
    n = x.shape[-1]
    ax = x.ndim - 1
    lane = lax.broadcasted_iota(jnp.int32, x.shape, ax)
    return jnp.where((lane % 32) < 16, pltpu.roll(x, n - 16, ax), pltpu.roll(x, 16, ax))


def _rope(x, cos, sin):
    return x * cos + _swap16(x) * sin


def _rope_t(dy, cos, sin):
    return dy * cos + _swap16(dy * sin)


def rope_tables(lc, seq):
    rows = seq // GRID_W
    row = jnp.repeat(jnp.arange(rows, dtype=F32), GRID_W)
    col = jnp.tile(jnp.arange(GRID_W, dtype=F32), rows)
    inv_freq = ROPE_BASE ** (-jnp.arange(ROPE_FREQS, dtype=F32) / ROPE_FREQS)
    ar, ac = row[:, None] * inv_freq, col[:, None] * inv_freq
    cos = jnp.concatenate([jnp.cos(ar), jnp.cos(ar), jnp.cos(ac), jnp.cos(ac)], axis=1)
    sin = jnp.concatenate([-jnp.sin(ar), jnp.sin(ar), -jnp.sin(ac), jnp.sin(ac)], axis=1)
    cos = jnp.concatenate([jnp.ones((lc, HEAD_DIM), F32), cos], axis=0)
    sin = jnp.concatenate([jnp.zeros((lc, HEAD_DIM), F32), sin], axis=0)
    return jnp.tile(cos, (1, 2)), jnp.tile(sin, (1, 2))


def _attn_mask(i, lc, t_len):
    qrow = i * AB + lax.broadcasted_iota(jnp.int32, (AB, 1), 0)
    kloc = (i - 1) * AB + lax.broadcasted_iota(jnp.int32, (1, 3 * AB), 1)
    valid = (qrow >= lc) & (kloc >= lc) & (kloc < t_len) & (jnp.abs(qrow - kloc) <= WINDOW)
    mask = jnp.concatenate([valid, jnp.ones((AB, lc), jnp.bool_)], axis=1)
    return jnp.concatenate([mask] * GROUP, axis=0)


def _attn_specs(t_len, lc):
    nb = t_len // AB
    prev = lambda b, i: (b, jnp.maximum(i - 1, 0), 0)
    cur = lambda b, i: (b, i, 0)
    nxt = lambda b, i: (b, jnp.minimum(i + 1, nb - 1), 0)
    kv = [pl.BlockSpec((None, AB, 2 * KV_W), f) for f in (prev, cur, nxt)]
    kv.append(pl.BlockSpec((None, lc, 2 * KV_W), lambda b, i: (b, 0, 0)))
    tab = [pl.BlockSpec((AB, 128), lambda b, i, f=f: f(b, i)[1:]) for f in (prev, cur, nxt)]
    return kv, tab


def _attn_keys(kvp, kvc, kvn, kvx, cp, cc, cn, sp, sc, sn):
    kk = jnp.concatenate([_rope(kvp[:, :KV_W], cp, sp), _rope(kvc[:, :KV_W], cc, sc),
                          _rope(kvn[:, :KV_W], cn, sn), kvx[:, :KV_W]], axis=0)
    vv = jnp.concatenate([kvp[:, KV_W:], kvc[:, KV_W:], kvn[:, KV_W:], kvx[:, KV_W:]], axis=0)
    return kk, vv


def _stack_heads(x, hk):
    return jnp.concatenate([x[:, (GROUP * hk + g) * HEAD_DIM:(GROUP * hk + g + 1) * HEAD_DIM]
                            for g in range(GROUP)], axis=0)


def _sink_col(sink_ref, hk):
    return jnp.concatenate([jnp.full((AB, 1), sink_ref[GROUP * hk + g], F32) for g in range(GROUP)], axis=0)


def attn_fwd(q, kv, cos, sin, sink, lc, name):
    bsz, t_len, _ = q.shape
    kv_specs, tab_specs = _attn_specs(t_len, lc)
    scale = HEAD_DIM ** -0.5

    def body(sink_ref, q_ref, kvp_ref, kvc_ref, kvn_ref, kvx_ref, cp, cc, cn, sp, sc, sn, o_ref, lse_ref):
        i = pl.program_id(1)
        mask = _attn_mask(i, lc, t_len)
        qr = _rope(q_ref[...], jnp.tile(cc[...], (1, 4)), jnp.tile(sc[...], (1, 4)))
        kk, vv = _attn_keys(kvp_ref[...], kvc_ref[...], kvn_ref[...], kvx_ref[...],
                            cp[...], cc[...], cn[...], sp[...], sc[...], sn[...])
        outs, lses = [], []
        for hk in range(N_KV):
            kh = kk[:, hk * HEAD_DIM:(hk + 1) * HEAD_DIM].astype(BF16)
            vh = vv[:, hk * HEAD_DIM:(hk + 1) * HEAD_DIM].astype(BF16)
            q4 = _stack_heads(qr, hk).astype(BF16)
            s = jnp.where(mask, _dot_nt(q4, kh) * scale, NEG_INF)
            sk = _sink_col(sink_ref, hk)
            m = jnp.maximum(jnp.max(s, axis=-1, keepdims=True), sk)
            p = jnp.exp(s - m)
            l = jnp.sum(p, axis=-1, keepdims=True) + jnp.exp(sk - m)
            o = _dot(p.astype(BF16), vh) / l
            lse = m + jnp.log(l)
            for g in range(GROUP):
                outs.append(o[g * AB:(g + 1) * AB])
                lses.append(lse[g * AB:(g + 1) * AB])
        o_ref[...] = jnp.concatenate(outs, axis=1)
        lse_ref[...] = jnp.concatenate(lses, axis=1)

    return pl.pallas_call(
        body, name=name, grid=(bsz, t_len // AB),
        in_specs=[pl.BlockSpec(memory_space=pltpu.SMEM),
                  pl.BlockSpec((None, AB, ATTN_W), lambda b, i: (b, i, 0))] + kv_specs + tab_specs + tab_specs,
        out_specs=[pl.BlockSpec((None, AB, ATTN_W), lambda b, i: (b, i, 0)),
                   pl.BlockSpec((None, AB, N_HEADS), lambda b, i: (b, i, 0))],
        out_shape=[jax.ShapeDtypeStruct((bsz, t_len, ATTN_W), F32), jax.ShapeDtypeStruct((bsz, t_len, N_HEADS), F32)],
        compiler_params=_cp(48, 2),
    )(sink, q, kv, kv, kv, kv, cos, cos, cos, sin, sin, sin)


def attn_bwd(q, kv, o, lse, do, cos, sin, sink, lc, name):
    bsz, t_len, _ = q.shape
    nb = t_len // AB
    kv_specs, tab_specs = _attn_specs(t_len, lc)
    scale = HEAD_DIM ** -0.5
    blk = lambda w: pl.BlockSpec((None, AB, w), lambda b, i: (b, i, 0))
    full_tab = pl.BlockSpec((t_len, 128), lambda b, i: (0, 0))

    def body(sink_ref, q_ref, kvp_ref, kvc_ref, kvn_ref, kvx_ref, cp, cc, cn, sp, sc, sn, cf, sf,
             o_ref, lse_ref, do_ref, dq_ref, dkv_ref, dsink_ref):
        b, i = pl.program_id(0), pl.program_id(1)

        @pl.when(i == 0)
        def _():
            dkv_ref[...] = jnp.zeros_like(dkv_ref)

        @pl.when((b == 0) & (i == 0))
        def _():
            dsink_ref[...] = jnp.zeros_like(dsink_ref)

        mask = _attn_mask(i, lc, t_len)
        cq, sq = jnp.tile(cc[...], (1, 4)), jnp.tile(sc[...], (1, 4))
        qr = _rope(q_ref[...], cq, sq)
        kk, vv = _attn_keys(kvp_ref[...], kvc_ref[...], kvn_ref[...], kvx_ref[...],
                            cp[...], cc[...], cn[...], sp[...], sc[...], sn[...])
        dov, ov, lsev = do_ref[...], o_ref[...], lse_ref[...]
        dqs, dks, dvs, dsk = [], [], [], []
        for hk in range(N_KV):
            kh = kk[:, hk * HEAD_DIM:(hk + 1) * HEAD_DIM].astype(BF16)
            vh = vv[:, hk * HEAD_DIM:(hk + 1) * HEAD_DIM].astype(BF16)
            q4 = _stack_heads(qr, hk).astype(BF16)
            do4 = _stack_heads(dov, hk)
            o4 = _stack_heads(ov, hk)
            lse4 = jnp.concatenate([lsev[:, GROUP * hk + g:GROUP * hk + g + 1] for g in range(GROUP)], axis=0)
            delta = jnp.sum(do4 * o4, axis=-1, keepdims=True)
            s = jnp.where(mask, _dot_nt(q4, kh) * scale, NEG_INF)
            p = jnp.exp(s - lse4)
            do4b = do4.astype(BF16)
            dp = _dot_nt(do4b, vh)
            ds = (p * (dp - delta) * scale).astype(BF16)
            dq4 = _dot(ds, kh)
            dks.append(_dot_tn(ds, q4))
            dvs.append(_dot_tn(p.astype(BF16), do4b))
            pd = jnp.exp(_sink_col(sink_ref, hk) - lse4) * delta
            for g in range(GROUP):
                dqs.append(dq4[g * AB:(g + 1) * AB])
                dsk.append(-jnp.sum(pd[g * AB:(g + 1) * AB], axis=0, keepdims=True))
        dq_ref[...] = _rope_t(jnp.concatenate(dqs, axis=1), cq, sq)
        dsink_ref[0:1, :] += jnp.concatenate(dsk, axis=1)
        dkv = jnp.concatenate(dks + dvs, axis=1)
        starts = (jnp.maximum(i - 1, 0), i, jnp.minimum(i + 1, nb - 1))
        for j, st in enumerate(starts):
            rows = pl.ds(pl.multiple_of(st * AB, AB), AB)
            dkv_ref[rows, :] += dkv[j * AB:(j + 1) * AB]
        dkv_ref[0:lc, :] += dkv[3 * AB:]

        @pl.when(i == nb - 1)
        def _():
            def unrotate(j, carry):
                rows = pl.ds(pl.multiple_of(j * AB, AB), AB)
                dkv_ref[rows, 0:KV_W] = _rope_t(dkv_ref[rows, 0:KV_W], cf[rows, :], sf[rows, :])
                return carry
            lax.fori_loop(0, nb, unrotate, 0)

    return pl.pallas_call(
        body, name=name, grid=(bsz, nb),
        in_specs=[pl.BlockSpec(memory_space=pltpu.SMEM), blk(ATTN_W)] + kv_specs + tab_specs + tab_specs
        + [full_tab, full_tab, blk(ATTN_W), blk(N_HEADS), blk(ATTN_W)],
        out_specs=[blk(ATTN_W), pl.BlockSpec((None, t_len, 2 * KV_W), lambda b, i: (b, 0, 0)),
                   pl.BlockSpec((8, N_HEADS), lambda b, i: (0, 0))],
        out_shape=[jax.ShapeDtypeStruct((bsz, t_len, ATTN_W), F32), jax.ShapeDtypeStruct((bsz, t_len, 2 * KV_W), F32),
                   jax.ShapeDtypeStruct((8, N_HEADS), F32)],
        compiler_params=_cp(56, 2),
    )(sink, q, kv, kv, kv, kv, cos, cos, cos, sin, sin, sin, cos, sin, o, lse, do)


def _s5_mats_dir(a_re, a_im, log_dt, b_re, b_im, c_re, c_im, flip):
    hp = lax.Precision.HIGHEST
    lam = lax.complex(a_re, a_im)
    ldt = lam * jnp.exp(log_dt)[:, None]
    a_bar = jnp.exp(ldt)
    b_bar = ((a_bar - 1.0) / lam)[..., None] * lax.complex(b_re, b_im)
    cm = lax.complex(c_re, c_im)
    tt = np.arange(Q)
    powers = lambda e: jnp.exp(ldt[..., None] * jnp.asarray(e, F32))
    k = jnp.real(jnp.einsum('gcp,gpt,gpk->gktc', cm, powers(Q - 1 - tt if flip else tt), b_bar, precision=hp))
    k = k.reshape(G, C, QC)
    slabs = []
    for t1 in range(Q):
        if flip:
            sh = (Q - 1 - t1) * C
            slabs.append(jnp.pad(k, ((0, 0), (0, 0), (0, sh)))[..., sh:])
        else:
            slabs.append(jnp.pad(k, ((0, 0), (0, 0), (t1 * C, 0)))[..., :QC])
    kt = jnp.stack(slabs, axis=1).reshape(G, QC, QC)
    ws = powers(tt if flip else Q - 1 - tt)[:, :, :, None] * b_bar[:, :, None, :]
    ws = ws.transpose(0, 2, 3, 1)
    wo = cm[:, :, :, None] * powers(Q - tt if flip else tt + 1)[:, None, :, :]
    wo = wo.transpose(0, 2, 3, 1)
    ws = jnp.concatenate([jnp.real(ws), jnp.imag(ws)], axis=-1).reshape(G, QC, P2)
    wo = jnp.concatenate([jnp.real(wo), -jnp.imag(wo)], axis=1).reshape(G, P2, QC)
    a1, a2 = _pair_forms(powers([Q]))
    return kt, ws, wo, a1, a2


def _pair_forms(z):
    re, im = jnp.real(z), jnp.imag(z)
    k = z.shape[-1]
    a1 = jnp.concatenate([re, re], axis=1).transpose(2, 0, 1).reshape(k, G * P2)
    a2 = jnp.concatenate([-im, im], axis=1).transpose(2, 0, 1).reshape(k, G * P2)
    return a1, a2


def s5_mats(a_re, a_im, log_dt, b_re, b_im, c_re, c_im, d_skip):
    per_dir = [_s5_mats_dir(a_re[d], a_im[d], log_dt[d], b_re[d], b_im[d], c_re[d], c_im[d], d == 1)
               for d in range(2)]
    kt, ws, wo, a1, a2 = (jnp.stack([m[i] for m in per_dir]) for i in range(5))
    dvec = jnp.broadcast_to(d_skip.reshape(G, 1, C), (G, Q, C)).reshape(G, 1, QC)
    return kt, ws, wo, a1, a2, dvec


def _to_groups(u):
    bsz, t_len, _ = u.shape
    return u.reshape(bsz, t_len // Q, Q, G, C).transpose(0, 3, 1, 2, 4).reshape(bsz, G, t_len // Q, QC)


def _from_groups(ug):
    bsz, _, nc, _ = ug.shape
    return ug.reshape(bsz, G, nc, Q, C).transpose(0, 2, 3, 1, 4).reshape(bsz, nc * Q, G * C)


def _gb(shape):
    return pl.BlockSpec((None, None) + shape, lambda g, b: (b, g, 0, 0))


def _gw(shape):
    return pl.BlockSpec((2, None) + shape, lambda g, b: (0, g, 0, 0))


def _gs(nc):
    return pl.BlockSpec((2, None, nc, P2), lambda g, b: (0, b, 0, g))


def s5_chunk_fwd(ug, kt, ws, dvec, name):
    bsz, _, nc, _ = ug.shape

    def body(u_ref, kt_ref, ws_ref, d_ref, y_ref, s_ref):
        u = u_ref[...]
        ub = u.astype(BF16)
        y_ref[...] = u * d_ref[...] + _dot(ub, kt_ref[0]) + _dot(ub, kt_ref[1])
        s_ref[0] = _dot(ub, ws_ref[0])
        s_ref[1] = _dot(ub, ws_ref[1])

    return pl.pallas_call(
        body, name=name, grid=(G, bsz),
        in_specs=[_gb((nc, QC)), _gw((QC, QC)), _gw((QC, P2)), pl.BlockSpec((None, 1, QC), lambda g, b: (g, 0, 0))],
        out_specs=[_gb((nc, QC)), _gs(nc)],
        out_shape=[jax.ShapeDtypeStruct((bsz, G, nc, QC), F32), jax.ShapeDtypeStruct((2, bsz, nc, G * P2), F32)],
        compiler_params=_cp(32, 2),
    )(ug, kt, ws, dvec)


def s5_scan(s, a1, a2, ncc, reverse, name, hp=None):
    _, bsz, nc, gw = s.shape
    as_rows = lambda v: v.reshape(v.shape[:-1] + (G, P2))
    st = pl.BlockSpec((2, None, nc, SCAN_G, P2), lambda b, w: (0, b, 0, w, 0))
    av = pl.BlockSpec((2, SCAN_G, P2), lambda b, w: (0, w, 0))
    acc = pl.BlockSpec((2, None, SCAN_G, P2), lambda b, w: (0, b, w, 0))
    with_da = hp is not None

    def body(*refs):
        if with_da:
            s_ref, a1_ref, a2_ref, hp_ref, out_ref, da1_ref, da2_ref = refs
        else:
            s_ref, a1_ref, a2_ref, out_ref = refs
        a1v = (a1_ref[0], a1_ref[1])
        a2v = (a2_ref[0], a2_ref[1])
        swap = lambda h: pltpu.roll(h, P, 1)

        def step(j, carry):
            i = nc - 1 - j if reverse else j
            order = (i, jnp.where(i < ncc, ncc - 1 - i, nc - 1 - (i - ncc)))
            hs, da1, da2 = carry
            nh, n1, n2 = [], [], []
            for d, n in enumerate(order):
                h = hs[d]
                out_ref[d, n] = h
                nh.append(a1v[d] * h + a2v[d] * swap(h) + s_ref[d, n])
                if with_da:
                    hv = hp_ref[d, n]
                    n1.append(da1[d] + h * hv)
                    n2.append(da2[d] + h * swap(hv))
            return tuple(nh), tuple(n1), tuple(n2)

        z = jnp.zeros((SCAN_G, P2), F32)
        zz = (z, z) if with_da else ()
        _, da1, da2 = lax.fori_loop(0, nc, step, ((z, z), zz, zz))
        if with_da:
            for d in range(2):
                da1_ref[d] = da1[d]
                da2_ref[d] = da2[d]

    out_shape = [jax.ShapeDtypeStruct((2, bsz, nc, G, P2), F32)]
    out_specs = [st]
    ins = [as_rows(s), as_rows(a1[:, 0]), as_rows(a2[:, 0])]
    in_specs = [st, av, av]
    if with_da:
        ins.append(as_rows(hp))
        in_specs.append(st)
        out_shape += [jax.ShapeDtypeStruct((2, bsz, G, P2), F32)] * 2
        out_specs += [acc, acc]
    res = pl.pallas_call(
        body, name=name, grid=(bsz, G // SCAN_G), in_specs=in_specs, out_specs=out_specs, out_shape=out_shape,
        compiler_params=_cp(48, 2),
    )(*ins)
    out = res[0].reshape(s.shape)
    return (out, res[1].reshape(2, bsz, gw), res[2].reshape(2, bsz, gw)) if with_da else out


def s5_out_fwd(y1, hp, wo, name):
    bsz, _, nc, _ = y1.shape

    def body(y1_ref, hp_ref, wo_ref, y_ref):
        y_ref[...] = (y1_ref[...] + _dot(hp_ref[0].astype(BF16), wo_ref[0])
                      + _dot(hp_ref[1].astype(BF16), wo_ref[1]))

    return pl.pallas_call(
        body, name=name, grid=(G, bsz),
        in_specs=[_gb((nc, QC)), _gs(nc), _gw((P2, QC))],
        out_specs=_gb((nc, QC)),
        out_shape=jax.ShapeDtypeStruct(y1.shape, F32),
        compiler_params=_cp(32, 2),
    )(y1, hp, wo)


def _acc_init(b, *refs):
    @pl.when(b == 0)
    def _():
        for r in refs:
            r[...] = jnp.zeros_like(r)


def s5_out_bwd(dyg, ug, hp, wo, name):
    bsz, _, nc, _ = dyg.shape

    def body(dy_ref, u_ref, hp_ref, wo_ref, dhp_ref, dwo_ref, dkt_ref, dd_ref):
        _acc_init(pl.program_id(1), dwo_ref, dkt_ref, dd_ref)
        dy, u = dy_ref[...], u_ref[...]
        dyb = dy.astype(BF16)
        for d in range(2):
            dhp_ref[d] = _dot_nt(dyb, wo_ref[d])
            dwo_ref[d] += _dot_tn(hp_ref[d].astype(BF16), dyb)
        dkt_ref[...] += _dot_tn(u.astype(BF16), dyb)
        dd_ref[...] += _rowsum(dy * u)

    return pl.pallas_call(
        body, name=name, grid=(G, bsz),
        in_specs=[_gb((nc, QC)), _gb((nc, QC)), _gs(nc), _gw((P2, QC))],
        out_specs=[_gs(nc), _gw((P2, QC)), pl.BlockSpec((None, QC, QC), lambda g, b: (g, 0, 0)),
                   pl.BlockSpec((None, 1, QC), lambda g, b: (g, 0, 0))],
        out_shape=[jax.ShapeDtypeStruct(hp.shape, F32), jax.ShapeDtypeStruct((2, G, P2, QC), F32),
                   jax.ShapeDtypeStruct((G, QC, QC), F32), jax.ShapeDtypeStruct((G, 1, QC), F32)],
        compiler_params=_cp(32, 2),
    )(dyg, ug, hp, wo)


def s5_chunk_bwd(dyg, ug, ds, kt, ws, dvec, name):
    bsz, _, nc, _ = dyg.shape

    def body(dy_ref, u_ref, ds_ref, kt_ref, ws_ref, d_ref, du_ref, dws_ref):
        _acc_init(pl.program_id(1), dws_ref)
        dy = dy_ref[...]
        dyb = dy.astype(BF16)
        ub = u_ref[...].astype(BF16)
        du = dy * d_ref[...] + _dot_nt(dyb, kt_ref[0]) + _dot_nt(dyb, kt_ref[1])
        for d in range(2):
            dsb = ds_ref[d].astype(BF16)
            du += _dot_nt(dsb, ws_ref[d])
            dws_ref[d] += _dot_tn(ub, dsb)
        du_ref[...] = du

    return pl.pallas_call(
        body, name=name, grid=(G, bsz),
        in_specs=[_gb((nc, QC)), _gb((nc, QC)), _gs(nc), _gw((QC, QC)), _gw((QC, P2)),
                  pl.BlockSpec((None, 1, QC), lambda g, b: (g, 0, 0))],
        out_specs=[_gb((nc, QC)), _gw((QC, P2))],
        out_shape=[jax.ShapeDtypeStruct(dyg.shape, F32), jax.ShapeDtypeStruct((2, G, QC, P2), F32)],
        compiler_params=_cp(32, 2),
    )(dyg, ug, ds, kt, ws, dvec)


def local_step(x, ctx, target, mods, norm_g, final_g, even, odd):
    bsz, seq, _ = x.shape
    lc = ctx.shape[1]
    t_len = lc + seq
    ncc = lc // Q
    cos, sin = rope_tables(lc, seq)
    h = jnp.concatenate([ctx, x], axis=1)
    saved = []
    for i in range(DEPTH):
        j = i // 2
        g = norm_g[i].reshape(1, D)
        if i % 2 == 0:
            w = even[j]
            a, q, kv, g_attn, u, g_ssm = norm_in(h, g, mods[i], w["w_in"], EVEN_SPLITS, f"even_in{j}")
            o_attn, lse = attn_fwd(q, kv, cos, sin, w["sink"], lc, f"attn_fwd{j}")
            mats, mats_vjp = jax.vjp(s5_mats, *w["ssm"])
            kt, ws, wo, a1, a2, dvec = mats
            kt, ws, wo = kt.astype(BF16), ws.astype(BF16), wo.astype(BF16)
            ug = _to_groups(u)
            y1, s = s5_chunk_fwd(ug, kt, ws, dvec, f"s5_chunk_fwd{j}")
            hp = s5_scan(s, a1, a2, ncc, False, f"s5_scan_fwd{j}")
            y_ssm = _from_groups(s5_out_fwd(y1, hp, wo, f"s5_out_fwd{j}"))
            h_new, mix, yout = even_out(h, mods[i], o_attn, g_attn, y_ssm, g_ssm, w["glu_w"], w["glu_b"],
                                        w["w_out"], f"even_out{j}")
            saved.append(dict(h=h, a=a, q=q, kv=kv, g_attn=g_attn, g_ssm=g_ssm, o_attn=o_attn, lse=lse, ug=ug,
                              hp=hp, y_ssm=y_ssm, mix=mix, yout=yout, mats=(kt, ws, wo, a1, a2, dvec),
                              mats_vjp=mats_vjp))
        else:
            w = odd[j]
            a, u, gate = norm_in(h, g, mods[i], w["w_in"], ODD_SPLITS, f"odd_in{j}")
            pm = pool_band(u, lc, False, f"pool_band_fwd{j}")
            h_new, mix, yout = pool_out(h, mods[i], pm, gate, w["pool_w"], w["pool_scale"], w["w_out"], f"pool_out{j}")
            saved.append(dict(h=h, a=a, pm=pm, gate=gate, mix=mix, yout=yout))
        h = h_new

    dh, loss_acc, dfg = loss_head(h, final_g.reshape(1, D), target, "loss_head")
    grads = dict(final_g=dfg[0], norm_g=[None] * DEPTH, even=[None, None], odd=[None, None])
    dmods = [None] * DEPTH
    rows = bsz * t_len
    flat = lambda v: v.reshape(rows, v.shape[-1])
    for i in reversed(range(DEPTH)):
        j = i // 2
        sv = saved[i]
        g = norm_g[i].reshape(1, D)
        if i % 2 == 0:
            w = even[j]
            kt, ws, wo, a1, a2, dvec = sv["mats"]
            (d_oattn, d_gattn, d_gssm, d_yssm, dyout, zz, dsg, dgate, dglu_b) = even_out_bwd(
                dh, mods[i], sv["o_attn"], sv["g_attn"], sv["y_ssm"], sv["g_ssm"], w["glu_w"], w["glu_b"],
                w["w_out"], sv["yout"], f"even_out_bwd{j}")
            g_w_out = matmul_tn(flat(sv["mix"]), flat(dyout), D, D, f"even_w_out_grad{j}")
            g_glu_w = matmul_tn(flat(zz), flat(dsg), SSM_W, SSM_W, f"glu_w_grad{j}")
            dq, dkv, dsink = attn_bwd(sv["q"], sv["kv"], sv["o_attn"], sv["lse"], d_oattn, cos, sin, w["sink"], lc,
                                      f"attn_bwd{j}")
            dyg = _to_groups(d_yssm)
            dhp, dwo, dkt, dd = s5_out_bwd(dyg, sv["ug"], sv["hp"], wo, f"s5_out_bwd{j}")
            ds, da1, da2 = s5_scan(dhp, a1, -a2, ncc, True, f"s5_scan_bwd{j}", hp=sv["hp"])
            dug, dws = s5_chunk_bwd(dyg, sv["ug"], ds, kt, ws, dvec, f"s5_chunk_bwd{j}")
            dkt2 = jnp.stack([dkt, dkt])
            da1 = da1.sum(axis=1).reshape(2, 1, G * P2)
            da2 = da2.sum(axis=1).reshape(2, 1, G * P2)
            g_ssm = sv["mats_vjp"]((dkt2, dws, dwo, da1, da2, dd))
            dparts = [dq, dkv, d_gattn, _from_groups(dug), d_gssm]
            dh, dz, dmod, dg = norm_in_bwd(dparts, dh, sv["h"], g, mods[i], w["w_in"], f"even_in_bwd{j}")
            g_w_in = matmul_tn(flat(sv["a"]), flat(dz), D, dz.shape[-1], f"even_w_in_grad{j}")
            grads["even"][j] = dict(w_in=g_w_in, w_out=g_w_out, sink=dsink[0], ssm=g_ssm, glu_w=g_glu_w,
                                    glu_b=dglu_b[0])
        else:
            w = odd[j]
            dpm, dgt, dyout, dpp, dgate, dps = pool_out_bwd(dh, mods[i], sv["pm"], sv["gate"], w["pool_w"],
                                                            w["pool_scale"], w["w_out"], sv["yout"],
                                                            f"pool_out_bwd{j}")
            g_w_out = matmul_tn(flat(sv["mix"]), flat(dyout), D, D, f"odd_w_out_grad{j}")
            g_pool_w = jnp.stack([matmul_tn(flat(sv["pm"]), flat(dpp), POOL_G, POOL_G, f"pool_w_grad{j}_{gi}",
                                            a_col=gi, b_col=gi) for gi in range(4)])
            du = pool_band(dpm, lc, True, f"pool_band_bwd{j}")
            dh, dz, dmod, dg = norm_in_bwd([du, dgt], dh, sv["h"], g, mods[i], w["w_in"], f"odd_in_bwd{j}")
            g_w_in = matmul_tn(flat(sv["a"]), flat(dz), D, dz.shape[-1], f"odd_w_in_grad{j}")
            grads["odd"][j] = dict(w_in=g_w_in, w_out=g_w_out, pool_w=g_pool_w, pool_scale=dps[0])
        grads["norm_g"][i] = dg[0]
        dmods[i] = jnp.concatenate([dmod[:, :, 0:2, :], dgate[:, :, 0:1, :]], axis=2)
    return loss_acc[0, 0], dh[:, lc:, :], dmods, grads


N_DEV = 8
HBM_SPEC = pl.BlockSpec(memory_space=pltpu.HBM)


def allgather8(x_shard, name):
    m_per, n = x_shard.shape

    def body(x_ref, out_ref, send_sems, recv_sems, local_sem):
        x, y, c = lax.axis_index("x"), lax.axis_index("y"), lax.axis_index("c")
        me, sibling = (x, y, c), (x, y, 1 - c)
        chips = [(1 - x, y), (x, 1 - y), (1 - x, 1 - y)]

        def rows(px, py, pc):
            return out_ref.at[pl.ds((4 * px + 2 * py + pc) * m_per, m_per), :]

        def copy(k, block, to, src=None):
            return pltpu.make_async_remote_copy(
                src_ref=rows(*block) if src is None else src, dst_ref=rows(*block),
                send_sem=send_sems.at[k], recv_sem=recv_sems.at[k], device_id=to, device_id_type=MESH)

        mine = pltpu.make_async_copy(x_ref, rows(*me), local_sem)
        mine.start()
        first = [copy(0, me, sibling, src=x_ref)]
        first += [copy(1 + j, me, (*chip, c), src=x_ref) for j, chip in enumerate(chips)]
        for cp in first:
            cp.start()
        passed = [copy(4 + j, (*chip, c), sibling) for j, chip in enumerate(chips)]
        for j, chip in enumerate(chips):
            copy(1 + j, (*chip, c), me).wait_recv()
            passed[j].start()
        copy(0, sibling, me).wait_recv()
        for j, chip in enumerate(chips):
            copy(4 + j, (*chip, 1 - c), me).wait_recv()
        for cp in first + passed:
            cp.wait_send()
        mine.wait()

    return pl.pallas_call(
        body, name=name,
        out_shape=jax.ShapeDtypeStruct((N_DEV * m_per, n), x_shard.dtype),
        in_specs=[pl.BlockSpec(memory_space=pltpu.VMEM)],
        out_specs=pl.BlockSpec(memory_space=pltpu.VMEM),
        scratch_shapes=[pltpu.SemaphoreType.DMA((7,)), pltpu.SemaphoreType.DMA((7,)), pltpu.SemaphoreType.DMA],
        compiler_params=_cp(56),
    )(x_shard)


def xy_exchange(srcs, scatter, name):
    n = len(srcs)
    shapes = [tuple(s.shape[1:]) if scatter else tuple(s.shape) for s in srcs]

    def body(*refs):
        src_refs, out_refs = refs[:n], refs[n:2 * n]
        send_sems, recv_sems, local_sems = refs[2 * n:]
        x, y, c = lax.axis_index("x"), lax.axis_index("y"), lax.axis_index("c")
        my = 2 * x + y
        peers = [(1 - x, y), (x, 1 - y), (1 - x, 1 - y)]

        def piece(i, pos):
            return src_refs[i].at[pos] if scatter else src_refs[i]

        def copy(i, k, src_pos, dst_pos):
            px, py = peers[k]
            return pltpu.make_async_remote_copy(
                src_ref=piece(i, src_pos), dst_ref=out_refs[i].at[dst_pos], send_sem=send_sems.at[3 * i + k],
                recv_sem=recv_sems.at[3 * i + k], device_id=(px, py, c), device_id_type=MESH)

        local = [pltpu.make_async_copy(piece(i, my), out_refs[i].at[my], local_sems.at[i]) for i in range(n)]
        sends = [copy(i, k, 2 * px + py, my) for i in range(n) for k, (px, py) in enumerate(peers)]
        for cp in local + sends:
            cp.start()
        for i in range(n):
            for k, (px, py) in enumerate(peers):
                copy(i, k, my, 2 * px + py).wait_recv()
        for cp in sends:
            cp.wait_send()
        for cp in local:
            cp.wait()

    return pl.pallas_call(
        body, name=name,
        out_shape=[jax.ShapeDtypeStruct((4,) + sh, s.dtype) for sh, s in zip(shapes, srcs)],
        in_specs=[HBM_SPEC] * n, out_specs=[HBM_SPEC] * n,
        scratch_shapes=[pltpu.SemaphoreType.DMA((3 * n,)), pltpu.SemaphoreType.DMA((3 * n,)),
                        pltpu.SemaphoreType.DMA((n,))],
    )(*srcs)


def sibling_exchange(srcs, name):
    n = len(srcs)

    def body(*refs):
        src_refs, out_refs = refs[:n], refs[n:2 * n]
        send_sems, recv_sems = refs[2 * n:]
        peer = (lax.axis_index("x"), lax.axis_index("y"), 1 - lax.axis_index("c"))
        cps = [pltpu.make_async_remote_copy(src_ref=src_refs[i], dst_ref=out_refs[i], send_sem=send_sems.at[i],
                                            recv_sem=recv_sems.at[i], device_id=peer, device_id_type=MESH)
               for i in range(n)]
        for cp in cps:
            cp.start()
        for cp in cps:
            cp.wait()

    return pl.pallas_call(
        body, name=name, out_shape=[jax.ShapeDtypeStruct(s.shape, s.dtype) for s in srcs],
        in_specs=[HBM_SPEC] * n, out_specs=[HBM_SPEC] * n,
        scratch_shapes=[pltpu.SemaphoreType.DMA((n,)), pltpu.SemaphoreType.DMA((n,))],
    )(*srcs)


def _row_tile(rows, bytes_per_row, limit):
    best = None
    for tr in range(8, rows + 1, 8):
        if rows % tr == 0 and tr * bytes_per_row <= limit:
            best = tr
    return best if best is not None else rows


def sum_slots(x, name):
    n, rows, cols = x.shape
    tr = _row_tile(rows, n * cols * 4, 4 * MB)

    def body(x_ref, o_ref):
        acc = x_ref[0]
        for k in range(1, n):
            acc = acc + x_ref[k]
        o_ref[...] = acc

    return pl.pallas_call(
        body, name=name, grid=(rows // tr,),
        in_specs=[pl.BlockSpec((n, tr, cols), lambda r: (0, r, 0))],
        out_specs=pl.BlockSpec((tr, cols), lambda r: (r, 0)),
        out_shape=jax.ShapeDtypeStruct((rows, cols), F32),
        compiler_params=_cp(32, 1),
    )(x)


ADA_COLS = 3 * D // 4
C_ROWS = 8


def ada_fwd(c_all, ada_w, ada_b_cols, name):
    nrow = c_all.shape[0]

    def body(c_ref, w_ref, b_ref, o_ref):
        s, _ = _silu_and_grad(c_ref[...])
        o_ref[...] = _dot(s.astype(BF16), w_ref[...].astype(BF16)) + b_ref[...]

    return pl.pallas_call(
        body, name=name, grid=(DEPTH,),
        in_specs=[pl.BlockSpec((nrow, D), lambda i: (0, 0)), pl.BlockSpec((None, D, ADA_COLS), lambda i: (i, 0, 0)),
                  pl.BlockSpec((None, 1, ADA_COLS), lambda i: (i, 0, 0))],
        out_specs=pl.BlockSpec((None, nrow, ADA_COLS), lambda i: (i, 0, 0)),
        out_shape=jax.ShapeDtypeStruct((DEPTH, nrow, ADA_COLS), F32),
        compiler_params=_cp(32, 1),
    )(c_all, ada_w, ada_b_cols)


def ada_bwd(c_all, d_cols, ada_w, name):
    nrow = c_all.shape[0]

    def body(c_ref, d_ref, w_ref, gw_ref, ds_ref):
        @pl.when(pl.program_id(0) == 0)
        def _():
            ds_ref[...] = jnp.zeros_like(ds_ref)
        s, _ = _silu_and_grad(c_ref[...])
        dl = d_ref[...]
        gw_ref[...] = _dot_tn(s.astype(BF16), dl.astype(BF16))
        rid = lax.broadcasted_iota(jnp.int32, (nrow, 1), 0) % C_ROWS
        dctx = jnp.where((rid == 2) | (rid == 3), dl, 0.0).astype(BF16)
        ds_ref[0:1, :] += _rowsum(_dot_nt(dctx, w_ref[...].astype(BF16)))

    return pl.pallas_call(
        body, name=name, grid=(DEPTH,),
        in_specs=[pl.BlockSpec((nrow, D), lambda i: (0, 0)), pl.BlockSpec((None, nrow, ADA_COLS), lambda i: (i, 0, 0)),
                  pl.BlockSpec((None, D, ADA_COLS), lambda i: (i, 0, 0))],
        out_specs=[pl.BlockSpec((None, D, ADA_COLS), lambda i: (i, 0, 0)), pl.BlockSpec((8, D), lambda i: (0, 0))],
        out_shape=[jax.ShapeDtypeStruct((DEPTH, D, ADA_COLS), F32), jax.ShapeDtypeStruct((8, D), F32)],
        compiler_params=_cp(32, 1),
    )(c_all, d_cols, ada_w)


def ada_bias_grad(d_all, name):
    nrow = d_all.shape[1]

    def body(d_ref, o_ref):
        o_ref[...] = jnp.broadcast_to(_rowsum(d_ref[...]), o_ref.shape)

    return pl.pallas_call(
        body, name=name, grid=(DEPTH,),
        in_specs=[pl.BlockSpec((None, nrow, 3 * D), lambda i: (i, 0, 0))],
        out_specs=pl.BlockSpec((None, 8, 3 * D), lambda i: (i, 0, 0)),
        out_shape=jax.ShapeDtypeStruct((DEPTH, 8, 3 * D), F32),
        compiler_params=_cp(32, 1),
    )(d_all)


def silu_chain(ds, c, name):
    def body(ds_ref, c_ref, o_ref):
        _, dsl = _silu_and_grad(c_ref[...])
        o_ref[...] = ds_ref[...] * dsl

    return pl.pallas_call(body, name=name, out_shape=jax.ShapeDtypeStruct(ds.shape, F32))(ds, c)


def _flat_cols(shape):
    size = int(np.prod(shape))
    if shape[-1] >= 128:
        return shape[-1]
    for cols in (1024, 128):
        if size % cols == 0:
            return cols
    return shape[-1]


def adamw(w, m, v, grads, name):
    shape = w.shape
    cols = _flat_cols(shape)
    as2d = lambda a: a.reshape(-1, cols)
    rows = w.size // cols
    tr = _row_tile(rows, cols * 4, MB)
    k = len(grads)

    def body(*refs):
        w_ref, m_ref, v_ref = refs[:3]
        g_refs = refs[3:3 + k]
        g_out, d_out, m_out, v_out = refs[3 + k:]
        g = g_refs[0][...]
        for r in g_refs[1:]:
            g = g + r[...]
        g_out[...] = g
        mn = ADAM_B1 * m_ref[...] + (1.0 - ADAM_B1) * g
        vn = ADAM_B2 * v_ref[...] + (1.0 - ADAM_B2) * (g * g)
        m_out[...] = mn
        v_out[...] = vn
        m_hat = mn / (1.0 - ADAM_B1 ** ADAM_STEP)
        v_hat = vn / (1.0 - ADAM_B2 ** ADAM_STEP)
        d_out[...] = -ADAM_LR * (m_hat / (jnp.sqrt(v_hat) + ADAM_EPS) + ADAM_WD * w_ref[...])

    spec = pl.BlockSpec((tr, cols), lambda r: (r, 0))
    outs = pl.pallas_call(
        body, name=name, grid=(rows // tr,),
        in_specs=[spec] * (3 + k), out_specs=[spec] * 4,
        out_shape=[jax.ShapeDtypeStruct((rows, cols), F32)] * 4,
        compiler_params=_cp(32, 1),
    )(as2d(w), as2d(m), as2d(v), *[as2d(g) for g in grads])
    return tuple(o.reshape(shape) for o in outs)


BIG = (("even_w_in", (2, D, 576), 2), ("even_w_out", (2, 256, D), 1), ("glu_w", (2, 128, SSM_W), 1),
       ("odd_w_in", (2, D, 512), 2), ("odd_w_out", (2, 256, D), 1), ("pool_w", (2, 4, 64, POOL_G), 2))


def _full_shape(shard, axis):
    return tuple(4 * s if a == axis else s for a, s in enumerate(shard))


def _to_shards(full, shard, axis):
    return jnp.moveaxis(full.reshape(shard[:axis] + (4,) + shard[axis:]), axis, 0)


def _from_shards(stacked, shard, axis):
    return jnp.moveaxis(stacked, 0, axis).reshape(_full_shape(shard, axis))


SMALL = (("ds_ctx", (D,)), ("norm_g", (DEPTH, D)), ("final_g", (D,)), ("attn_sink", (2, N_HEADS)),
         ("ssm_a_re", (2, 2, G, P)), ("ssm_a_im", (2, 2, G, P)), ("ssm_log_dt", (2, 2, G)),
         ("ssm_b_re", (2, 2, G, P, C)), ("ssm_b_im", (2, 2, G, P, C)), ("ssm_c_re", (2, 2, G, C, P)),
         ("ssm_c_im", (2, 2, G, C, P)), ("ssm_d", (2, SSM_W)), ("glu_b", (2, SSM_W)), ("pool_scale", (2, D)))
SMALL_PAD = 8 * 128


def pack_small(vals):
    flat = jnp.concatenate([vals[n].reshape(-1) for n, _ in SMALL])
    pad = (-flat.shape[0]) % SMALL_PAD
    return jnp.pad(flat, (0, pad)).reshape(-1, 128)


def unpack_small(packed):
    flat, out, off = packed.reshape(-1), {}, 0
    for n, shape in SMALL:
        size = int(np.prod(shape))
        out[n] = flat[off:off + size].reshape(shape)
        off += size
    return out


WEIGHT_NAMES = ('c_ctx', 'ada_w', 'ada_b', 'norm_g', 'even_w_in', 'even_w_out', 'attn_sink', 'ssm_a_re', 'ssm_a_im',
                'ssm_log_dt', 'ssm_b_re', 'ssm_b_im', 'ssm_c_re', 'ssm_c_im', 'ssm_d', 'glu_w', 'glu_b', 'odd_w_in',
                'odd_w_out', 'pool_w', 'pool_scale', 'final_g')
SSM_NAMES = ('ssm_a_re', 'ssm_a_im', 'ssm_log_dt', 'ssm_b_re', 'ssm_b_im', 'ssm_c_re', 'ssm_c_im', 'ssm_d')


def kernel(x, c, ctx, c_ctx, ada_w, ada_b, norm_g, even_w_in, even_w_out, attn_sink, ssm_a_re, ssm_a_im, ssm_log_dt, ssm_b_re, ssm_b_im, ssm_c_re, ssm_c_im, ssm_d, glu_w, glu_b, odd_w_in, odd_w_out, pool_w, pool_scale, final_g, loss_target, m_c_ctx, m_ada_w, m_ada_b, m_norm_g, m_even_w_in, m_even_w_out, m_attn_sink, m_ssm_a_re, m_ssm_a_im, m_ssm_log_dt, m_ssm_b_re, m_ssm_b_im, m_ssm_c_re, m_ssm_c_im, m_ssm_d, m_glu_w, m_glu_b, m_odd_w_in, m_odd_w_out, m_pool_w, m_pool_scale, m_final_g, v_c_ctx, v_ada_w, v_ada_b, v_norm_g, v_even_w_in, v_even_w_out, v_attn_sink, v_ssm_a_re, v_ssm_a_im, v_ssm_log_dt, v_ssm_b_re, v_ssm_b_im, v_ssm_c_re, v_ssm_c_im, v_ssm_d, v_glu_w, v_glu_b, v_odd_w_in, v_odd_w_out, v_pool_w, v_pool_scale, v_final_g):
    env = dict(locals())
    weights = {n: env[n] for n in WEIGHT_NAMES}
    bsz = x.shape[0]
    ax, ay, ac = lax.axis_index("x"), lax.axis_index("y"), lax.axis_index("c")
    pos = 2 * ax + ay
    dev = 2 * pos + ac

    c_rows = jnp.concatenate([c, c_ctx.reshape(1, D), c_ctx.reshape(1, D), jnp.zeros((C_ROWS - bsz - 2, D), F32)])
    c_all = allgather8(c_rows, "gather_c")
    ada_b_cols = lax.dynamic_slice(ada_b, (0, pos * ADA_COLS), (DEPTH, ADA_COLS)).reshape(DEPTH, 1, ADA_COLS)
    mod_cols = ada_fwd(c_all, ada_w, ada_b_cols, "ada_fwd")
    nrow = N_DEV * C_ROWS
    misc = jnp.concatenate([mod_cols.reshape(DEPTH * nrow, ADA_COLS),
                            jnp.pad(pool_scale, ((0, 6), (0, ADA_COLS - pool_scale.shape[1])))])
    misc_all = allgather8(misc, "gather_mod").reshape(4, 2, DEPTH * nrow + 8, ADA_COLS)[:, 0]
    mod_full = misc_all[:, :DEPTH * nrow].reshape(4, DEPTH, nrow, ADA_COLS).transpose(1, 2, 0, 3)
    mod_mine = lax.dynamic_slice(mod_full.reshape(DEPTH, nrow, 3 * D), (0, dev * C_ROWS, 0), (DEPTH, C_ROWS, 3 * D))
    mods = []
    for i in range(DEPTH):
        lat = mod_mine[i, :bsz].reshape(bsz, 1, 3, D)
        con = jnp.broadcast_to(mod_mine[i, bsz].reshape(1, 1, 3, D), (bsz, 1, 3, D))
        mods.append(jnp.pad(jnp.concatenate([con, lat], axis=1), ((0, 0), (0, 0), (0, 5), (0, 0))))
    pool_scale_full = misc_all[:, DEPTH * nrow:DEPTH * nrow + 2, :pool_scale.shape[1]].transpose(1, 0, 2).reshape(2, D)

    gathered = xy_exchange([weights[n].astype(BF16) for n, _, _ in BIG], False, "gather_weights")
    full = {n: _from_shards(g, shard, axis) for (n, shard, axis), g in zip(BIG, gathered)}
    even = [dict(w_in=full["even_w_in"][j], w_out=full["even_w_out"][j], sink=attn_sink[j],
                 ssm=tuple(weights[n][j] for n in SSM_NAMES), glu_w=full["glu_w"][j],
                 glu_b=glu_b[j].reshape(1, SSM_W)) for j in range(2)]
    odd = [dict(w_in=full["odd_w_in"][j], w_out=full["odd_w_out"][j], pool_w=full["pool_w"][j],
                pool_scale=pool_scale_full[j].reshape(1, D)) for j in range(2)]

    loss_local, grad_x, dmods, grads = local_step(x, ctx, loss_target, mods, norm_g, final_g, even, odd)
    loss = lax.psum(loss_local, ("x", "y", "c"))

    d_rows = jnp.stack([jnp.concatenate([dm[:, 1].reshape(bsz, 3 * D), dm[:, 0].reshape(bsz, 3 * D),
                                         jnp.zeros((C_ROWS - 2 * bsz, 3 * D), F32)]) for dm in dmods])
    d_all = allgather8(d_rows.reshape(DEPTH * C_ROWS, 3 * D), "gather_dmod")
    d_all = d_all.reshape(N_DEV, DEPTH, C_ROWS, 3 * D).transpose(1, 0, 2, 3).reshape(DEPTH, nrow, 3 * D)
    d_cols = lax.dynamic_slice(d_all, (0, 0, pos * ADA_COLS), (DEPTH, nrow, ADA_COLS))
    g_ada_w, ds_ctx = ada_bwd(c_all, d_cols, ada_w, "ada_bwd")
    g_ada_b = ada_bias_grad(d_all, "ada_bias_grad")[:, 0]

    small = dict(ds_ctx=ds_ctx[0] * (ac == 0).astype(F32), norm_g=jnp.stack(grads["norm_g"]), final_g=grads["final_g"],
                 attn_sink=jnp.stack([grads["even"][j]["sink"] for j in range(2)]),
                 glu_b=jnp.stack([grads["even"][j]["glu_b"] for j in range(2)]),
                 pool_scale=jnp.stack([grads["odd"][j]["pool_scale"] for j in range(2)]))
    for k, n in enumerate(SSM_NAMES):
        small[n] = jnp.stack([grads["even"][j]["ssm"][k] for j in range(2)])
    packed = pack_small(small)
    small_sum = sum_slots(allgather8(packed, "gather_small").reshape(N_DEV, packed.shape[0], 128), "sum_small")
    g_small = unpack_small(small_sum)
    g_small["c_ctx"] = silu_chain(g_small.pop("ds_ctx").reshape(1, D), c_ctx.reshape(1, D), "c_ctx_grad").reshape(D)
    g_small["ada_b"] = g_ada_b
    g_small["pool_scale"] = lax.dynamic_slice(g_small["pool_scale"], (0, pos * 256), (2, 256))

    big_full = dict(even_w_in=jnp.stack([grads["even"][j]["w_in"] for j in range(2)]),
                    even_w_out=jnp.stack([grads["even"][j]["w_out"] for j in range(2)]),
                    glu_w=jnp.stack([grads["even"][j]["glu_w"] for j in range(2)]),
                    odd_w_in=jnp.stack([grads["odd"][j]["w_in"] for j in range(2)]),
                    odd_w_out=jnp.stack([grads["odd"][j]["w_out"] for j in range(2)]),
                    pool_w=jnp.stack([grads["odd"][j]["pool_w"] for j in range(2)]))
    landed = xy_exchange([_to_shards(big_full[n], shard, axis) for n, shard, axis in BIG], True, "scatter_grads")
    mine4 = [sum_slots(r.reshape(4, -1, r.shape[-1]), "sum_positions_" + n).reshape(shard)
             for (n, shard, _), r in zip(BIG, landed)]
    other4 = sibling_exchange(mine4, "swap_cores")
    g_mine = dict(zip([n for n, _, _ in BIG], mine4))
    g_other = dict(zip([n for n, _, _ in BIG], other4))

    results = {}
    for n in WEIGHT_NAMES:
        if n in g_mine:
            gs = [g_mine[n], g_other[n]]
        elif n == "ada_w":
            gs = [g_ada_w]
        else:
            gs = [g_small[n]]
        results[n] = adamw(weights[n], env["m_" + n], env["v_" + n], gs, "adamw_" + n)
    outs = [loss, grad_x]
    for k in range(4):
        outs += [results[n][k] for n in WEIGHT_NAMES]
    return tuple(outs)
```

```python
import functools

import numpy as np
import jax
import jax.numpy as jnp
from jax import lax
from jax.experimental import pallas as pl
from jax.experimental.pallas import tpu as pltpu

F32 = jnp.float32
BF16 = jnp.bfloat16
MESH = pl.DeviceIdType.MESH

D = 1024
DEPTH = 4
EPS = 1e-6
NEG_INF = -1e30
GRID_W = 64
ROPE_BASE = 10000.0
ROPE_FREQS = 16
HEAD_DIM = 64
N_HEADS = 8
N_KV = 2
GROUP = 4
ATTN_W = N_HEADS * HEAD_DIM
KV_W = N_KV * HEAD_DIM
WINDOW = 128
AB = 128
SSM_W = 512
G = 32
C = 16
P = 64
Q = 16
QC = Q * C
P2 = 2 * P
SCAN_G = 16
POOL_R = (1, 2, 4, 8)
POOL_G = 256
HALO = 8
TM = 256
EVEN_SPLITS = (512, 256, 512, 512, 512)
ODD_SPLITS = (1024, 1024)

ADAM_LR = 0.001
ADAM_B1 = 0.9
ADAM_B2 = 0.999
ADAM_EPS = 1e-08
ADAM_WD = 0.01
ADAM_STEP = 10

MB = 1024 * 1024


def _cp(vmem_mb=48, n_axes=0):
    kw = dict(vmem_limit_bytes=vmem_mb * MB)
    if n_axes:
        kw["dimension_semantics"] = ("arbitrary",) * n_axes
    return pltpu.CompilerParams(**kw)


def _sig(x):
    return 1.0 / (1.0 + jnp.exp(-x))


def _silu_and_grad(x):
    s = _sig(x)
    return x * s, s * (1.0 + x * (1.0 - s))


_GELU_C = 0.7978845608028654
_GELU_A = 0.044715


def _gelu_and_grad(x):
    th = jnp.tanh(_GELU_C * (x + _GELU_A * x * x * x))
    val = 0.5 * x * (1.0 + th)
    grad = 0.5 * (1.0 + th) + 0.5 * x * (1.0 - th * th) * _GELU_C * (1.0 + 3.0 * _GELU_A * x * x)
    return val, grad


def _rms(h):
    r = lax.rsqrt(jnp.mean(h * h, axis=-1, keepdims=True) + EPS)
    return h * r, r


def _dot(a, b):
    return jnp.dot(a, b, preferred_element_type=F32)


def _dot_nt(a, b):
    return lax.dot_general(a, b, (((1,), (1,)), ((), ())), preferred_element_type=F32)


def _dot_tn(a, b):
    return lax.dot_general(a, b, (((0,), (0,)), ((), ())), preferred_element_type=F32)


def _rowsum(x):
    return jnp.sum(x, axis=0, keepdims=True)


def _seg(t):
    return jnp.minimum(t, 1)


def _row_spec(n):
    return pl.BlockSpec((None, TM, n), lambda b, t: (b, t, 0))


def _const_spec(shape):
    nd = len(shape)
    return pl.BlockSpec(shape, lambda b, t: (0,) * nd)


def _mod_spec():
    return pl.BlockSpec((None, None, 8, D), lambda b, t: (b, _seg(t), 0, 0))


def norm_in(h, g, mod, w, splits, name):
    bsz, t_len, _ = h.shape
    n = w.shape[1]
    offs = [int(v) for v in np.cumsum((0,) + tuple(splits))]

    def body(h_ref, g_ref, mod_ref, w_ref, a_ref, *outs):
        xh, _ = _rms(h_ref[...])
        a = xh * g_ref[...] * (1.0 + mod_ref[1:2, :]) + mod_ref[0:1, :]
        ab = a.astype(BF16)
        a_ref[...] = ab
        z = _dot(ab, w_ref[...])
        for o, lo, hi in zip(outs, offs[:-1], offs[1:]):
            o[...] = z[:, lo:hi]

    return pl.pallas_call(
        body, name=name, grid=(bsz, t_len // TM),
        in_specs=[_row_spec(D), _const_spec((1, D)), _mod_spec(), _const_spec((D, n))],
        out_specs=[_row_spec(D)] + [_row_spec(s) for s in splits],
        out_shape=[jax.ShapeDtypeStruct((bsz, t_len, D), BF16)]
        + [jax.ShapeDtypeStruct((bsz, t_len, s), F32) for s in splits],
        compiler_params=_cp(48, 2),
    )(h, g, mod, w)


def norm_in_bwd(dparts, dh_in, h, g, mod, w, name, skip=None):
    bsz, t_len, _ = h.shape
    n = w.shape[1]
    k = len(dparts)
    extra = [] if skip is None else [skip[1], skip[2]]

    def body(*refs):
        parts = [r[...] for r in refs[:k]]
        if skip is not None:
            parts[skip[0]] = parts[skip[0]] + refs[k][...] * refs[k + 1][...]
        dh_in_ref, h_ref, g_ref, mod_ref, w_ref, dh_ref, dz_ref, dmod_ref, dg_ref = refs[k + len(extra):]
        b, t = pl.program_id(0), pl.program_id(1)
        dz = jnp.concatenate(parts, axis=1).astype(BF16)
        dz_ref[...] = dz
        da = _dot_nt(dz, w_ref[...])
        xh, r = _rms(h_ref[...])
        gg = g_ref[...]
        sc1 = 1.0 + mod_ref[1:2, :]

        @pl.when(t <= 1)
        def _():
            dmod_ref[...] = jnp.zeros_like(dmod_ref)

        @pl.when((b == 0) & (t == 0))
        def _():
            dg_ref[...] = jnp.zeros_like(dg_ref)

        dmod_ref[0:1, :] += _rowsum(da)
        dmod_ref[1:2, :] += _rowsum(da * (xh * gg))
        dg_ref[0:1, :] += _rowsum(da * sc1 * xh)
        dxh = da * gg * sc1
        dh_ref[...] = dh_in_ref[...] + r * (dxh - xh * jnp.mean(dxh * xh, axis=-1, keepdims=True))

    return pl.pallas_call(
        body, name=name, grid=(bsz, t_len // TM),
        in_specs=[_row_spec(p.shape[-1]) for p in dparts]
        + ([_row_spec(extra[0].shape[-1]), _const_spec(extra[1].shape)] if extra else [])
        + [_row_spec(D), _row_spec(D), _const_spec((1, D)), _mod_spec(), _const_spec((D, n))],
        out_specs=[_row_spec(D), _row_spec(n), _mod_spec(), _const_spec((8, D))],
        out_shape=[jax.ShapeDtypeStruct((bsz, t_len, D), F32), jax.ShapeDtypeStruct((bsz, t_len, n), BF16),
                   jax.ShapeDtypeStruct((bsz, 2, 8, D), F32), jax.ShapeDtypeStruct((8, D), F32)],
        compiler_params=_cp(56, 2),
    )(*dparts, *extra, dh_in, h, g, mod, w)


def matmul_tn(a, b, m, n, name, a_col=0, b_col=0):
    rows = a.shape[0]
    tr = 512 if rows % 512 == 0 else rows
    tn = n
    for cand in (1024, 768, 512, 256, 128):
        if n > 1024 and n % cand == 0:
            tn = cand
            break
    nb = n // tn

    def body(a_ref, b_ref, o_ref):
        @pl.when(pl.program_id(1) == 0)
        def _():
            o_ref[...] = jnp.zeros_like(o_ref)
        o_ref[...] += _dot_tn(a_ref[...].astype(BF16), b_ref[...].astype(BF16))

    return pl.pallas_call(
        body, name=name, grid=(nb, rows // tr),
        in_specs=[pl.BlockSpec((tr, m), lambda j, r: (r, a_col)),
                  pl.BlockSpec((tr, tn), lambda j, r: (r, b_col * nb + j))],
        out_specs=pl.BlockSpec((m, tn), lambda j, r: (0, j)),
        out_shape=jax.ShapeDtypeStruct((m, n), F32),
        compiler_params=_cp(48, 2),
    )(a, b)


def even_out(h, mod, o_attn, g_attn, y_scan, u, d_skip, g_ssm, glu_w, glu_b, w_out, name):
    bsz, t_len, _ = h.shape

    def body(h_ref, mod_ref, oa_ref, ga_ref, ys_ref, u_ref, dk_ref, gs_ref, gw_ref, gb_ref, wo_ref,
             hn_ref, mix_ref, yo_ref):
        zz, _ = _gelu_and_grad(ys_ref[...] + u_ref[...] * dk_ref[...])
        s = _dot(zz.astype(BF16), gw_ref[...]) + gb_ref[...]
        o_ssm = zz * _sig(s)
        sa, _ = _silu_and_grad(ga_ref[...])
        ss, _ = _silu_and_grad(gs_ref[...])
        mb = jnp.concatenate([oa_ref[...] * sa, o_ssm * ss], axis=1).astype(BF16)
        mix_ref[...] = mb
        yo = _dot(mb, wo_ref[...])
        yo_ref[...] = yo
        hn_ref[...] = h_ref[...] + mod_ref[2:3, :] * yo

    return pl.pallas_call(
        body, name=name, grid=(bsz, t_len // TM),
        in_specs=[_row_spec(D), _mod_spec(), _row_spec(512), _row_spec(512), _row_spec(512), _row_spec(512),
                  _const_spec((1, 512)), _row_spec(512), _const_spec((512, 512)), _const_spec((1, 512)),
                  _const_spec((D, D))],
        out_specs=[_row_spec(D), _row_spec(D), _row_spec(D)],
        out_shape=[jax.ShapeDtypeStruct((bsz, t_len, D), F32), jax.ShapeDtypeStruct((bsz, t_len, D), BF16),
                   jax.ShapeDtypeStruct((bsz, t_len, D), F32)],
        compiler_params=_cp(48, 2),
    )(h, mod, o_attn, g_attn, y_scan, u, d_skip, g_ssm, glu_w, glu_b, w_out)


def even_out_bwd(dh, mod, o_attn, g_attn, y_scan, u, d_skip, g_ssm, glu_w, glu_b, w_out, yout, name):
    bsz, t_len, _ = dh.shape

    def body(dh_ref, mod_ref, oa_ref, ga_ref, ys_ref, u_ref, dk_ref, gs_ref, gw_ref, gb_ref, wo_ref, yo_ref,
             doa_ref, dga_ref, dgs_ref, dys_ref, dyo_ref, zz_ref, ds_ref, dgate_ref, dgb_ref):
        b, t = pl.program_id(0), pl.program_id(1)
        dhv = dh_ref[...]

        @pl.when(t <= 1)
        def _():
            dgate_ref[...] = jnp.zeros_like(dgate_ref)

        @pl.when((b == 0) & (t == 0))
        def _():
            dgb_ref[...] = jnp.zeros_like(dgb_ref)

        dgate_ref[0:1, :] += _rowsum(dhv * yo_ref[...])
        dyb = (mod_ref[2:3, :] * dhv).astype(BF16)
        dyo_ref[...] = dyb
        dmix = _dot_nt(dyb, wo_ref[...])
        sa, dsa = _silu_and_grad(ga_ref[...])
        doa_ref[...] = dmix[:, :512] * sa
        dga_ref[...] = dmix[:, :512] * oa_ref[...] * dsa
        uv = u_ref[...]
        zz, dzz_dy = _gelu_and_grad(ys_ref[...] + uv * dk_ref[...])
        zb = zz.astype(BF16)
        zz_ref[...] = zb
        sg = _sig(_dot(zb, gw_ref[...]) + gb_ref[...])
        ss, dss = _silu_and_grad(gs_ref[...])
        dm = dmix[:, 512:]
        dgs_ref[...] = dm * (zz * sg) * dss
        do = dm * ss
        ds = do * zz * sg * (1.0 - sg)
        dsb = ds.astype(BF16)
        ds_ref[...] = dsb
        dgb_ref[0:1, :] += _rowsum(ds)
        dys = (do * sg + _dot_nt(dsb, gw_ref[...])) * dzz_dy
        dys_ref[...] = dys
        dgb_ref[1:2, :] += _rowsum(dys * uv)

    r512 = jax.ShapeDtypeStruct((bsz, t_len, 512), F32)
    return pl.pallas_call(
        body, name=name, grid=(bsz, t_len // TM),
        in_specs=[_row_spec(D), _mod_spec(), _row_spec(512), _row_spec(512), _row_spec(512), _row_spec(512),
                  _const_spec((1, 512)), _row_spec(512), _const_spec((512, 512)), _const_spec((1, 512)),
                  _const_spec((D, D)), _row_spec(D)],
        out_specs=[_row_spec(512)] * 4 + [_row_spec(D), _row_spec(512), _row_spec(512), _mod_spec(),
                                           _const_spec((8, 512))],
        out_shape=[r512, r512, r512, r512, jax.ShapeDtypeStruct((bsz, t_len, D), BF16),
                   jax.ShapeDtypeStruct((bsz, t_len, 512), BF16), jax.ShapeDtypeStruct((bsz, t_len, 512), BF16),
                   jax.ShapeDtypeStruct((bsz, 2, 8, D), F32), jax.ShapeDtypeStruct((8, 512), F32)],
        compiler_params=_cp(48, 2),
    )(dh, mod, o_attn, g_attn, y_scan, u, d_skip, g_ssm, glu_w, glu_b, w_out, yout)


def _split3_dot(band, x):
    x1 = x.astype(BF16)
    r1 = x - x1.astype(F32)
    x2 = r1.astype(BF16)
    x3 = (r1 - x2.astype(F32)).astype(BF16)
    return _dot(band, x3) + _dot(band, x2) + _dot(band, x1)


def pool_band(x, lc, transpose, name):
    bsz, t_len, _ = x.shape
    assert lc == TM
    hb = TM // HALO

    def body(xp_ref, xc_ref, xn_ref, o_ref):
        t = pl.program_id(1)
        seg_lo = jnp.where(t == 0, 0, lc)
        seg_hi = jnp.where(t == 0, lc, t_len)
        cur = xc_ref[...]
        xh = jnp.concatenate([xp_ref[...], cur, xn_ref[...]], axis=0)
        row_t = t * TM + lax.broadcasted_iota(jnp.int32, (TM, 1), 0)
        col_s = t * TM - HALO + lax.broadcasted_iota(jnp.int32, (1, TM + 2 * HALO), 1)
        row_s = t * TM - HALO + lax.broadcasted_iota(jnp.int32, (TM + 2 * HALO, 1), 0)
        s_ok = (col_s >= seg_lo) & (col_s < seg_hi)
        outs = []
        for gi, r in enumerate(POOL_R):
            band = ((jnp.abs(row_t - col_s) <= r) & s_ok).astype(BF16)
            xg = xh[:, gi * POOL_G:(gi + 1) * POOL_G]
            if transpose:
                cnt_s = jnp.minimum(row_s + r, seg_hi - 1) - jnp.maximum(row_s - r, seg_lo) + 1
                xg = xg * (1.0 / jnp.maximum(cnt_s, 1).astype(F32))
            acc = _split3_dot(band, xg)
            if not transpose:
                cnt_t = jnp.minimum(row_t + r, seg_hi - 1) - jnp.maximum(row_t - r, seg_lo) + 1
                acc = acc * (1.0 / cnt_t.astype(F32))
            outs.append(acc - cur[:, gi * POOL_G:(gi + 1) * POOL_G])
        o_ref[...] = jnp.concatenate(outs, axis=1)

    return pl.pallas_call(
        body, name=name, grid=(bsz, t_len // TM),
        in_specs=[pl.BlockSpec((None, HALO, D), lambda b, t: (b, jnp.maximum(t * hb - 1, 0), 0)),
                  _row_spec(D),
                  pl.BlockSpec((None, HALO, D), lambda b, t: (b, jnp.minimum((t + 1) * hb, t_len // HALO - 1), 0))],
        out_specs=_row_spec(D),
        out_shape=jax.ShapeDtypeStruct((bsz, t_len, D), F32),
        compiler_params=_cp(48, 2),
    )(x, x, x)


def pool_out(h, mod, pm, gate, pool_w, pool_scale, w_out, name):
    bsz, t_len, _ = h.shape

    def body(h_ref, mod_ref, pm_ref, gt_ref, pw_ref, ps_ref, wo_ref, hn_ref, mix_ref, yo_ref):
        pmv = pm_ref[...]
        ppre = jnp.concatenate([_dot(pmv[:, g * POOL_G:(g + 1) * POOL_G].astype(BF16), pw_ref[g])
                                for g in range(4)], axis=1)
        sl, _ = _silu_and_grad(gt_ref[...])
        mb = (ppre * ps_ref[...] * sl).astype(BF16)
        mix_ref[...] = mb
        yo = _dot(mb, wo_ref[...])
        yo_ref[...] = yo
        hn_ref[...] = h_ref[...] + mod_ref[2:3, :] * yo

    return pl.pallas_call(
        body, name=name, grid=(bsz, t_len // TM),
        in_specs=[_row_spec(D), _mod_spec(), _row_spec(D), _row_spec(D), _const_spec((4, POOL_G, POOL_G)),
                  _const_spec((1, D)), _const_spec((D, D))],
        out_specs=[_row_spec(D), _row_spec(D), _row_spec(D)],
        out_shape=[jax.ShapeDtypeStruct((bsz, t_len, D), F32), jax.ShapeDtypeStruct((bsz, t_len, D), BF16),
                   jax.ShapeDtypeStruct((bsz, t_len, D), F32)],
        compiler_params=_cp(48, 2),
    )(h, mod, pm, gate, pool_w, pool_scale, w_out)


def pool_out_bwd(dh, mod, pm, gate, pool_w, pool_scale, w_out, yout, name):
    bsz, t_len, _ = dh.shape

    def body(dh_ref, mod_ref, pm_ref, gt_ref, pw_ref, ps_ref, wo_ref, yo_ref,
             dpm_ref, dgt_ref, dyo_ref, dpp_ref, dgate_ref, dps_ref):
        b, t = pl.program_id(0), pl.program_id(1)
        dhv = dh_ref[...]

        @pl.when(t <= 1)
        def _():
            dgate_ref[...] = jnp.zeros_like(dgate_ref)

        @pl.when((b == 0) & (t == 0))
        def _():
            dps_ref[...] = jnp.zeros_like(dps_ref)

        dgate_ref[0:1, :] += _rowsum(dhv * yo_ref[...])
        dyb = (mod_ref[2:3, :] * dhv).astype(BF16)
        dyo_ref[...] = dyb
        dmix = _dot_nt(dyb, wo_ref[...])
        pmv = pm_ref[...]
        ppre = jnp.concatenate([_dot(pmv[:, g * POOL_G:(g + 1) * POOL_G].astype(BF16), pw_ref[g])
                                for g in range(4)], axis=1)
        ps = ps_ref[...]
        sl, dsl = _silu_and_grad(gt_ref[...])
        dp = dmix * sl
        dgt_ref[...] = dmix * (ppre * ps) * dsl
        dps_ref[0:1, :] += _rowsum(dp * ppre)
        dppb = (dp * ps).astype(BF16)
        dpp_ref[...] = dppb
        dpm_ref[...] = jnp.concatenate([_dot_nt(dppb[:, g * POOL_G:(g + 1) * POOL_G], pw_ref[g])
                                        for g in range(4)], axis=1)

    return pl.pallas_call(
        body, name=name, grid=(bsz, t_len // TM),
        in_specs=[_row_spec(D), _mod_spec(), _row_spec(D), _row_spec(D), _const_spec((4, POOL_G, POOL_G)),
                  _const_spec((1, D)), _const_spec((D, D)), _row_spec(D)],
        out_specs=[_row_spec(D), _row_spec(D), _row_spec(D), _row_spec(D), _mod_spec(), _const_spec((8, D))],
        out_shape=[jax.ShapeDtypeStruct((bsz, t_len, D), F32), jax.ShapeDtypeStruct((bsz, t_len, D), F32),
                   jax.ShapeDtypeStruct((bsz, t_len, D), BF16), jax.ShapeDtypeStruct((bsz, t_len, D), BF16),
                   jax.ShapeDtypeStruct((bsz, 2, 8, D), F32), jax.ShapeDtypeStruct((8, D), F32)],
        compiler_params=_cp(48, 2),
    )(dh, mod, pm, gate, pool_w, pool_scale, w_out, yout)


def loss_head(h, final_g, target, name):
    bsz, t_len, _ = h.shape

    def body(h_ref, g_ref, tg_ref, dh_ref, loss_ref, dg_ref):
        b, t = pl.program_id(0), pl.program_id(1)

        @pl.when((b == 0) & (t == 0))
        def _():
            loss_ref[...] = jnp.zeros_like(loss_ref)
            dg_ref[...] = jnp.zeros_like(dg_ref)

        lat = (t > 0).astype(F32)
        xh, r = _rms(h_ref[...])
        gg = g_ref[...]
        err = (xh * gg - tg_ref[...]) * lat
        loss_ref[...] += 0.5 * jnp.sum(jnp.mean(err * err, axis=-1, keepdims=True))
        dy = err * (1.0 / D)
        dg_ref[0:1, :] += _rowsum(dy * xh)
        dxh = dy * gg
        dh_ref[...] = r * (dxh - xh * jnp.mean(dxh * xh, axis=-1, keepdims=True))

    return pl.pallas_call(
        body, name=name, grid=(bsz, t_len // TM),
        in_specs=[_row_spec(D), _const_spec((1, D)),
                  pl.BlockSpec((None, TM, D), lambda b, t: (b, jnp.maximum(t - 1, 0), 0))],
        out_specs=[_row_spec(D), _const_spec((8, 128)), _const_spec((8, D))],
        out_shape=[jax.ShapeDtypeStruct((bsz, t_len, D), F32), jax.ShapeDtypeStruct((8, 128), F32),
                   jax.ShapeDtypeStruct((8, D), F32)],
        compiler_params=_cp(48, 2),
    )(h, final_g, target)


def _swap16(x):
    n = x.shape[-1]
    ax = x.ndim - 1
    lane = lax.broadcasted_iota(jnp.int32, x.shape, ax)
    return jnp.where((lane % 32) < 16, pltpu.roll(x, n - 16, ax), pltpu.roll(x, 16, ax))


def _rope(x, cos, sin):
    return x * cos + _swap16(x) * sin


def _rope_t(dy, cos, sin):
    return dy * cos + _swap16(dy * sin)


def rope_tables(lc, seq):
    rows = seq // GRID_W
    row = jnp.repeat(jnp.arange(rows, dtype=F32), GRID_W)
    col = jnp.tile(jnp.arange(GRID_W, dtype=F32), rows)
    inv_freq = ROPE_BASE ** (-jnp.arange(ROPE_FREQS, dtype=F32) / ROPE_FREQS)
    ar, ac = row[:, None] * inv_freq, col[:, None] * inv_freq
    cos = jnp.concatenate([jnp.cos(ar), jnp.cos(ar), jnp.cos(ac), jnp.cos(ac)], axis=1)
    sin = jnp.concatenate([-jnp.sin(ar), jnp.sin(ar), -jnp.sin(ac), jnp.sin(ac)], axis=1)
    cos = jnp.concatenate([jnp.ones((lc, HEAD_DIM), F32), cos], axis=0)
    sin = jnp.concatenate([jnp.zeros((lc, HEAD_DIM), F32), sin], axis=0)
    return jnp.tile(cos, (1, 2)), jnp.tile(sin, (1, 2))


def _attn_mask(i, lc, t_len):
    qrow = i * AB + lax.broadcasted_iota(jnp.int32, (AB, 1), 0)
    kloc = (i - 1) * AB + lax.broadcasted_iota(jnp.int32, (1, 3 * AB), 1)
    valid = (qrow >= lc) & (kloc >= lc) & (kloc < t_len) & (jnp.abs(qrow - kloc) <= WINDOW)
    mask = jnp.concatenate([valid, jnp.ones((AB, lc), jnp.bool_)], axis=1)
    return jnp.concatenate([mask] * GROUP, axis=0)


def _attn_specs(t_len, lc):
    nb = t_len // AB
    prev = lambda b, i: (b, jnp.maximum(i - 1, 0), 0)
    cur = lambda b, i: (b, i, 0)
    nxt = lambda b, i: (b, jnp.minimum(i + 1, nb - 1), 0)
    kv = [pl.BlockSpec((None, AB, 2 * KV_W), f) for f in (prev, cur, nxt)]
    kv.append(pl.BlockSpec((None, lc, 2 * KV_W), lambda b, i: (b, 0, 0)))
    tab = [pl.BlockSpec((AB, 128), lambda b, i, f=f: f(b, i)[1:]) for f in (prev, cur, nxt)]
    return kv, tab


def _attn_keys(kvp, kvc, kvn, kvx, cp, cc, cn, sp, sc, sn):
    kk = jnp.concatenate([_rope(kvp[:, :KV_W], cp, sp), _rope(kvc[:, :KV_W], cc, sc),
                          _rope(kvn[:, :KV_W], cn, sn), kvx[:, :KV_W]], axis=0)
    vv = jnp.concatenate([kvp[:, KV_W:], kvc[:, KV_W:], kvn[:, KV_W:], kvx[:, KV_W:]], axis=0)
    return kk, vv


def _stack_heads(x, hk):
    return jnp.concatenate([x[:, (GROUP * hk + g) * HEAD_DIM:(GROUP * hk + g + 1) * HEAD_DIM]
                            for g in range(GROUP)], axis=0)


def _sink_col(sink_ref, hk):
    return jnp.concatenate([jnp.full((AB, 1), sink_ref[GROUP * hk + g], F32) for g in range(GROUP)], axis=0)


def attn_fwd(q, kv, cos, sin, sink, lc, name):
    bsz, t_len, _ = q.shape
    kv_specs, tab_specs = _attn_specs(t_len, lc)
    scale = HEAD_DIM ** -0.5

    def body(sink_ref, q_ref, kvp_ref, kvc_ref, kvn_ref, kvx_ref, cp, cc, cn, sp, sc, sn, o_ref, lse_ref):
        i = pl.program_id(1)
        mask = _attn_mask(i, lc, t_len)
        qr = _rope(q_ref[...], jnp.tile(cc[...], (1, 4)), jnp.tile(sc[...], (1, 4)))
        kk, vv = _attn_keys(kvp_ref[...], kvc_ref[...], kvn_ref[...], kvx_ref[...],
                            cp[...], cc[...], cn[...], sp[...], sc[...], sn[...])
        outs, lses = [], []
        for hk in range(N_KV):
            kh = kk[:, hk * HEAD_DIM:(hk + 1) * HEAD_DIM].astype(BF16)
            vh = vv[:, hk * HEAD_DIM:(hk + 1) * HEAD_DIM].astype(BF16)
            q4 = _stack_heads(qr, hk).astype(BF16)
            s = jnp.where(mask, _dot_nt(q4, kh) * scale, NEG_INF)
            sk = _sink_col(sink_ref, hk)
            m = jnp.maximum(jnp.max(s, axis=-1, keepdims=True), sk)
            p = jnp.exp(s - m)
            l = jnp.sum(p, axis=-1, keepdims=True) + jnp.exp(sk - m)
            o = _dot(p.astype(BF16), vh) / l
            lse = m + jnp.log(l)
            for g in range(GROUP):
                outs.append(o[g * AB:(g + 1) * AB])
                lses.append(lse[g * AB:(g + 1) * AB])
        o_ref[...] = jnp.concatenate(outs, axis=1)
        lse_ref[...] = jnp.concatenate(lses, axis=1)

    return pl.pallas_call(
        body, name=name, grid=(bsz, t_len // AB),
        in_specs=[pl.BlockSpec(memory_space=pltpu.SMEM),
                  pl.BlockSpec((None, AB, ATTN_W), lambda b, i: (b, i, 0))] + kv_specs + tab_specs + tab_specs,
        out_specs=[pl.BlockSpec((None, AB, ATTN_W), lambda b, i: (b, i, 0)),
                   pl.BlockSpec((None, AB, N_HEADS), lambda b, i: (b, i, 0))],
        out_shape=[jax.ShapeDtypeStruct((bsz, t_len, ATTN_W), F32), jax.ShapeDtypeStruct((bsz, t_len, N_HEADS), F32)],
        compiler_params=_cp(48, 2),
    )(sink, q, kv, kv, kv, kv, cos, cos, cos, sin, sin, sin)


def attn_bwd(q, kv, o, lse, do, cos, sin, sink, lc, name):
    bsz, t_len, _ = q.shape
    nb = t_len // AB
    kv_specs, tab_specs = _attn_specs(t_len, lc)
    scale = HEAD_DIM ** -0.5
    blk = lambda w: pl.BlockSpec((None, AB, w), lambda b, i: (b, i, 0))
    full_tab = pl.BlockSpec((t_len, 128), lambda b, i: (0, 0))

    def body(sink_ref, q_ref, kvp_ref, kvc_ref, kvn_ref, kvx_ref, cp, cc, cn, sp, sc, sn, cf, sf,
             o_ref, lse_ref, do_ref, dq_ref, dkv_ref, dsink_ref):
        b, i = pl.program_id(0), pl.program_id(1)

        @pl.when(i == 0)
        def _():
            dkv_ref[...] = jnp.zeros_like(dkv_ref)

        @pl.when((b == 0) & (i == 0))
        def _():
            dsink_ref[...] = jnp.zeros_like(dsink_ref)

        mask = _attn_mask(i, lc, t_len)
        cq, sq = jnp.tile(cc[...], (1, 4)), jnp.tile(sc[...], (1, 4))
        qr = _rope(q_ref[...], cq, sq)
        kk, vv = _attn_keys(kvp_ref[...], kvc_ref[...], kvn_ref[...], kvx_ref[...],
                            cp[...], cc[...], cn[...], sp[...], sc[...], sn[...])
        dov, ov, lsev = do_ref[...], o_ref[...], lse_ref[...]
        dqs, dks, dvs, dsk = [], [], [], []
        for hk in range(N_KV):
            kh = kk[:, hk * HEAD_DIM:(hk + 1) * HEAD_DIM].astype(BF16)
            vh = vv[:, hk * HEAD_DIM:(hk + 1) * HEAD_DIM].astype(BF16)
            q4 = _stack_heads(qr, hk).astype(BF16)
            do4 = _stack_heads(dov, hk)
            o4 = _stack_heads(ov, hk)
            lse4 = jnp.concatenate([lsev[:, GROUP * hk + g:GROUP * hk + g + 1] for g in range(GROUP)], axis=0)
            delta = jnp.sum(do4 * o4, axis=-1, keepdims=True)
            s = jnp.where(mask, _dot_nt(q4, kh) * scale, NEG_INF)
            p = jnp.exp(s - lse4)
            do4b = do4.astype(BF16)
            dp = _dot_nt(do4b, vh)
            ds = (p * (dp - delta) * scale).astype(BF16)
            dq4 = _dot(ds, kh)
            dks.append(_dot_tn(ds, q4))
            dvs.append(_dot_tn(p.astype(BF16), do4b))
            pd = jnp.exp(_sink_col(sink_ref, hk) - lse4) * delta
            for g in range(GROUP):
                dqs.append(dq4[g * AB:(g + 1) * AB])
                dsk.append(-jnp.sum(pd[g * AB:(g + 1) * AB], axis=0, keepdims=True))
        dq_ref[...] = _rope_t(jnp.concatenate(dqs, axis=1), cq, sq)
        dsink_ref[0:1, :] += jnp.concatenate(dsk, axis=1)
        dkv = jnp.concatenate(dks + dvs, axis=1)
        starts = (jnp.maximum(i - 1, 0), i, jnp.minimum(i + 1, nb - 1))
        for j, st in enumerate(starts):
            rows = pl.ds(pl.multiple_of(st * AB, AB), AB)
            dkv_ref[rows, :] += dkv[j * AB:(j + 1) * AB]
        dkv_ref[0:lc, :] += dkv[3 * AB:]

        @pl.when(i == nb - 1)
        def _():
            def unrotate(j, carry):
                rows = pl.ds(pl.multiple_of(j * AB, AB), AB)
                dkv_ref[rows, 0:KV_W] = _rope_t(dkv_ref[rows, 0:KV_W], cf[rows, :], sf[rows, :])
                return carry
            lax.fori_loop(0, nb, unrotate, 0)

    return pl.pallas_call(
        body, name=name, grid=(bsz, nb),
        in_specs=[pl.BlockSpec(memory_space=pltpu.SMEM), blk(ATTN_W)] + kv_specs + tab_specs + tab_specs
        + [full_tab, full_tab, blk(ATTN_W), blk(N_HEADS), blk(ATTN_W)],
        out_specs=[blk(ATTN_W), pl.BlockSpec((None, t_len, 2 * KV_W), lambda b, i: (b, 0, 0)),
                   pl.BlockSpec((8, N_HEADS), lambda b, i: (0, 0))],
        out_shape=[jax.ShapeDtypeStruct((bsz, t_len, ATTN_W), F32), jax.ShapeDtypeStruct((bsz, t_len, 2 * KV_W), F32),
                   jax.ShapeDtypeStruct((8, N_HEADS), F32)],
        compiler_params=_cp(56, 2),
    )(sink, q, kv, kv, kv, kv, cos, cos, cos, sin, sin, sin, cos, sin, o, lse, do)


def _s5_mats_dir(a_re, a_im, log_dt, b_re, b_im, c_re, c_im, flip):
    hp = lax.Precision.HIGHEST
    lam = lax.complex(a_re, a_im)
    ldt = lam * jnp.exp(log_dt)[:, None]
    a_bar = jnp.exp(ldt)
    b_bar = ((a_bar - 1.0) / lam)[..., None] * lax.complex(b_re, b_im)
    cm = lax.complex(c_re, c_im)
    tt = np.arange(Q)
    powers = lambda e: jnp.exp(ldt[..., None] * jnp.asarray(e, F32))
    k = jnp.real(jnp.einsum('gcp,gpt,gpk->gktc', cm, powers(Q - 1 - tt if flip else tt), b_bar, precision=hp))
    k = k.reshape(G, C, QC)
    slabs = []
    for t1 in range(Q):
        if flip:
            sh = (Q - 1 - t1) * C
            slabs.append(jnp.pad(k, ((0, 0), (0, 0), (0, sh)))[..., sh:])
        else:
            slabs.append(jnp.pad(k, ((0, 0), (0, 0), (t1 * C, 0)))[..., :QC])
    kt = jnp.stack(slabs, axis=1).reshape(G, QC, QC)
    ws = powers(tt if flip else Q - 1 - tt)[:, :, :, None] * b_bar[:, :, None, :]
    ws = ws.transpose(0, 2, 3, 1)
    wo = cm[:, :, :, None] * powers(Q - tt if flip else tt + 1)[:, None, :, :]
    wo = wo.transpose(0, 2, 3, 1)
    ws = jnp.concatenate([jnp.real(ws), jnp.imag(ws)], axis=-1).reshape(G, QC, P2)
    wo = jnp.concatenate([jnp.real(wo), -jnp.imag(wo)], axis=1).reshape(G, P2, QC)
    a1, a2 = _pair_forms(powers([Q]))
    return kt, ws, wo, a1, a2


def _pair_forms(z):
    re, im = jnp.real(z), jnp.imag(z)
    k = z.shape[-1]
    a1 = jnp.concatenate([re, re], axis=1).transpose(2, 0, 1).reshape(k, G * P2)
    a2 = jnp.concatenate([-im, im], axis=1).transpose(2, 0, 1).reshape(k, G * P2)
    return a1, a2


def s5_mats(a_re, a_im, log_dt, b_re, b_im, c_re, c_im):
    per_dir = [_s5_mats_dir(a_re[d], a_im[d], log_dt[d], b_re[d], b_im[d], c_re[d], c_im[d], d == 1)
               for d in range(2)]
    return tuple(jnp.stack([m[i] for m in per_dir]) for i in range(5))


GH = G // 8
RT = 16 * Q


def _perm_consts():
    r = np.arange(RT)
    rows = np.zeros((RT, RT), np.float32)
    rows[(r % Q) * 16 + r // Q, r] = 1.0
    q = np.arange(8 * 128)
    lanes = np.zeros((8 * 128, 8 * 128), np.float32)
    lanes[q, ((q % 128) // C) * 128 + (q // 128) * C + q % C] = 1.0
    return jnp.asarray(rows, BF16), jnp.asarray(lanes, BF16)


def to_groups(x, name):
    bsz, t_len, _ = x.shape
    nc = t_len // Q
    rows, lanes = _perm_consts()

    def body(x_ref, r_ref, p_ref, o_ref, w_ref):
        for j in range(t_len // RT):
            pt = _dot(r_ref[...], x_ref[j * RT:(j + 1) * RT, :].astype(BF16)).astype(BF16)
            for t in range(Q):
                w_ref[j * 16:(j + 1) * 16, t * SSM_W:(t + 1) * SSM_W] = pt[t * 16:(t + 1) * 16, :]
        for gh in range(GH):
            for th in range(2):
                inp = jnp.concatenate([w_ref[:, (th * 8 + tl) * SSM_W + gh * 128:(th * 8 + tl) * SSM_W + (gh + 1) * 128]
                                       for tl in range(8)], axis=1)
                out = _dot(inp, p_ref[...]).astype(BF16)
                for gl in range(8):
                    o_ref[gh * 8 + gl, :, th * 128:(th + 1) * 128] = out[:, gl * 128:(gl + 1) * 128]

    return pl.pallas_call(
        body, name=name, grid=(bsz,),
        in_specs=[pl.BlockSpec((None, t_len, SSM_W), lambda b: (b, 0, 0)), pl.BlockSpec((RT, RT), lambda b: (0, 0)),
                  pl.BlockSpec((1024, 1024), lambda b: (0, 0))],
        out_specs=pl.BlockSpec((None, G, nc, QC), lambda b: (b, 0, 0, 0)),
        out_shape=jax.ShapeDtypeStruct((bsz, G, nc, QC), BF16),
        scratch_shapes=[pltpu.VMEM((nc, Q * SSM_W), BF16)],
        compiler_params=_cp(56, 1),
    )(x, rows, lanes)


def from_groups(xg, name):
    bsz, _, nc, _ = xg.shape
    t_len = nc * Q
    rows, lanes = _perm_consts()

    def body(x_ref, r_ref, p_ref, o_ref, whi_ref, wlo_ref):
        gh = pl.program_id(1)
        for th in range(2):
            inp = jnp.concatenate([x_ref[gl, :, th * 128:(th + 1) * 128] for gl in range(8)], axis=1)
            hi = inp.astype(BF16)
            lo = (inp - hi.astype(F32)).astype(BF16)
            whi_ref[gh, :, th * 1024:(th + 1) * 1024] = _dot(hi, p_ref[...]).astype(BF16)
            wlo_ref[gh, :, th * 1024:(th + 1) * 1024] = _dot(lo, p_ref[...]).astype(BF16)

        @pl.when(gh == GH - 1)
        def _():
            for j in range(t_len // RT):
                def tile(w_ref):
                    return jnp.concatenate(
                        [jnp.concatenate([w_ref[k, j * 16:(j + 1) * 16, t * 128:(t + 1) * 128] for k in range(GH)],
                                         axis=1) for t in range(Q)], axis=0)
                o_ref[j * RT:(j + 1) * RT, :] = _dot(r_ref[...], tile(whi_ref)) + _dot(r_ref[...], tile(wlo_ref))

    return pl.pallas_call(
        body, name=name, grid=(bsz, GH),
        in_specs=[pl.BlockSpec((None, 8, nc, QC), lambda b, k: (b, k, 0, 0)),
                  pl.BlockSpec((RT, RT), lambda b, k: (0, 0)), pl.BlockSpec((1024, 1024), lambda b, k: (0, 0))],
        out_specs=pl.BlockSpec((None, t_len, SSM_W), lambda b, k: (b, 0, 0)),
        out_shape=jax.ShapeDtypeStruct((bsz, t_len, SSM_W), F32),
        scratch_shapes=[pltpu.VMEM((GH, nc, Q * 128), BF16), pltpu.VMEM((GH, nc, Q * 128), BF16)],
        compiler_params=_cp(56, 2),
    )(xg, rows, lanes)


def _gb(shape):
    return pl.BlockSpec((None, None) + shape, lambda g, b: (b, g, 0, 0))


def _gw(shape):
    return pl.BlockSpec((2, None) + shape, lambda g, b: (0, g, 0, 0))


def _gs(nc):
    return pl.BlockSpec((2, None, nc, P2), lambda g, b: (0, b, 0, g))


def s5_chunk_fwd(ug, kt, ws, name):
    bsz, _, nc, _ = ug.shape

    def body(u_ref, kt_ref, ws_ref, y_ref, s_ref):
        ub = u_ref[...]
        y_ref[...] = _dot(ub, kt_ref[0]) + _dot(ub, kt_ref[1])
        s_ref[0] = _dot(ub, ws_ref[0])
        s_ref[1] = _dot(ub, ws_ref[1])

    return pl.pallas_call(
        body, name=name, grid=(G, bsz),
        in_specs=[_gb((nc, QC)), _gw((QC, QC)), _gw((QC, P2))],
        out_specs=[_gb((nc, QC)), _gs(nc)],
        out_shape=[jax.ShapeDtypeStruct((bsz, G, nc, QC), F32), jax.ShapeDtypeStruct((2, bsz, nc, G * P2), F32)],
        compiler_params=_cp(32, 2),
    )(ug, kt, ws)


def s5_scan(s, a1, a2, ncc, reverse, name, hp=None):
    _, bsz, nc, gw = s.shape
    as_rows = lambda v: v.reshape(v.shape[:-1] + (G, P2))
    st = pl.BlockSpec((2, None, nc, SCAN_G, P2), lambda b, w: (0, b, 0, w, 0))
    av = pl.BlockSpec((2, SCAN_G, P2), lambda b, w: (0, w, 0))
    acc = pl.BlockSpec((2, None, SCAN_G, P2), lambda b, w: (0, b, w, 0))
    with_da = hp is not None

    def body(*refs):
        if with_da:
            s_ref, a1_ref, a2_ref, hp_ref, out_ref, da1_ref, da2_ref = refs
        else:
            s_ref, a1_ref, a2_ref, out_ref = refs
        a1v = (a1_ref[0], a1_ref[1])
        a2v = (a2_ref[0], a2_ref[1])
        swap = lambda h: pltpu.roll(h, P, 1)

        def step(j, carry):
            i = nc - 1 - j if reverse else j
            order = (i, jnp.where(i < ncc, ncc - 1 - i, nc - 1 - (i - ncc)))
            hs, da1, da2 = carry
            nh, n1, n2 = [], [], []
            for d, n in enumerate(order):
                h = hs[d]
                out_ref[d, n] = h
                nh.append(a1v[d] * h + a2v[d] * swap(h) + s_ref[d, n])
                if with_da:
                    hv = hp_ref[d, n]
                    n1.append(da1[d] + h * hv)
                    n2.append(da2[d] + h * swap(hv))
            return tuple(nh), tuple(n1), tuple(n2)

        z = jnp.zeros((SCAN_G, P2), F32)
        zz = (z, z) if with_da else ()
        _, da1, da2 = lax.fori_loop(0, nc, step, ((z, z), zz, zz))
        if with_da:
            for d in range(2):
                da1_ref[d] = da1[d]
                da2_ref[d] = da2[d]

    out_shape = [jax.ShapeDtypeStruct((2, bsz, nc, G, P2), F32)]
    out_specs = [st]
    ins = [as_rows(s), as_rows(a1[:, 0]), as_rows(a2[:, 0])]
    in_specs = [st, av, av]
    if with_da:
        ins.append(as_rows(hp))
        in_specs.append(st)
        out_shape += [jax.ShapeDtypeStruct((2, bsz, G, P2), F32)] * 2
        out_specs += [acc, acc]
    res = pl.pallas_call(
        body, name=name, grid=(bsz, G // SCAN_G), in_specs=in_specs, out_specs=out_specs, out_shape=out_shape,
        compiler_params=_cp(48, 2),
    )(*ins)
    out = res[0].reshape(s.shape)
    return (out, res[1].reshape(2, bsz, gw), res[2].reshape(2, bsz, gw)) if with_da else out


def s5_out_fwd(y1, hp, wo, name):
    bsz, _, nc, _ = y1.shape

    def body(y1_ref, hp_ref, wo_ref, y_ref):
        y_ref[...] = (y1_ref[...] + _dot(hp_ref[0].astype(BF16), wo_ref[0])
                      + _dot(hp_ref[1].astype(BF16), wo_ref[1]))

    return pl.pallas_call(
        body, name=name, grid=(G, bsz),
        in_specs=[_gb((nc, QC)), _gs(nc), _gw((P2, QC))],
        out_specs=_gb((nc, QC)),
        out_shape=jax.ShapeDtypeStruct(y1.shape, F32),
        compiler_params=_cp(32, 2),
    )(y1, hp, wo)


def _acc_init(b, *refs):
    @pl.when(b == 0)
    def _():
        for r in refs:
            r[...] = jnp.zeros_like(r)


def s5_out_bwd(dyg, ug, hp, wo, name):
    bsz, _, nc, _ = dyg.shape

    def body(dy_ref, u_ref, hp_ref, wo_ref, dhp_ref, dwo_ref, dkt_ref):
        _acc_init(pl.program_id(1), dwo_ref, dkt_ref)
        dyb = dy_ref[...]
        for d in range(2):
            dhp_ref[d] = _dot_nt(dyb, wo_ref[d])
            dwo_ref[d] += _dot_tn(hp_ref[d].astype(BF16), dyb)
        dkt_ref[...] += _dot_tn(u_ref[...], dyb)

    return pl.pallas_call(
        body, name=name, grid=(G, bsz),
        in_specs=[_gb((nc, QC)), _gb((nc, QC)), _gs(nc), _gw((P2, QC))],
        out_specs=[_gs(nc), _gw((P2, QC)), pl.BlockSpec((None, QC, QC), lambda g, b: (g, 0, 0))],
        out_shape=[jax.ShapeDtypeStruct(hp.shape, F32), jax.ShapeDtypeStruct((2, G, P2, QC), F32),
                   jax.ShapeDtypeStruct((G, QC, QC), F32)],
        compiler_params=_cp(32, 2),
    )(dyg, ug, hp, wo)


def s5_chunk_bwd(dyg, ug, ds, kt, ws, name):
    bsz, _, nc, _ = dyg.shape

    def body(dy_ref, u_ref, ds_ref, kt_ref, ws_ref, du_ref, dws_ref):
        _acc_init(pl.program_id(1), dws_ref)
        dyb = dy_ref[...]
        du = _dot_nt(dyb, kt_ref[0]) + _dot_nt(dyb, kt_ref[1])
        for d in range(2):
            dsb = ds_ref[d].astype(BF16)
            du += _dot_nt(dsb, ws_ref[d])
            dws_ref[d] += _dot_tn(u_ref[...], dsb)
        du_ref[...] = du

    return pl.pallas_call(
        body, name=name, grid=(G, bsz),
        in_specs=[_gb((nc, QC)), _gb((nc, QC)), _gs(nc), _gw((QC, QC)), _gw((QC, P2))],
        out_specs=[_gb((nc, QC)), _gw((QC, P2))],
        out_shape=[jax.ShapeDtypeStruct(dyg.shape, F32), jax.ShapeDtypeStruct((2, G, QC, P2), F32)],
        compiler_params=_cp(32, 2),
    )(dyg, ug, ds, kt, ws)


def local_step(x, ctx, target, mods, norm_g, final_g, even, odd):
    bsz, seq, _ = x.shape
    lc = ctx.shape[1]
    t_len = lc + seq
    ncc = lc // Q
    cos, sin = rope_tables(lc, seq)
    h = jnp.concatenate([ctx, x], axis=1)
    saved = []
    for i in range(DEPTH):
        j = i // 2
        g = norm_g[i].reshape(1, D)
        if i % 2 == 0:
            w = even[j]
            a, q, kv, g_attn, u, g_ssm = norm_in(h, g, mods[i], w["w_in"], EVEN_SPLITS, f"even_in{j}")
            o_attn, lse = attn_fwd(q, kv, cos, sin, w["sink"], lc, f"attn_fwd{j}")
            mats, mats_vjp = jax.vjp(s5_mats, *w["ssm"][:7])
            kt, ws, wo, a1, a2 = mats
            kt, ws, wo = kt.astype(BF16), ws.astype(BF16), wo.astype(BF16)
            d_skip = w["ssm"][7].reshape(1, SSM_W)
            ug = to_groups(u, f"u_to_groups{j}")
            y1, s = s5_chunk_fwd(ug, kt, ws, f"s5_chunk_fwd{j}")
            hp = s5_scan(s, a1, a2, ncc, False, f"s5_scan_fwd{j}")
            y_scan = from_groups(s5_out_fwd(y1, hp, wo, f"s5_out_fwd{j}"), f"y_from_groups{j}")
            h_new, mix, yout = even_out(h, mods[i], o_attn, g_attn, y_scan, u, d_skip, g_ssm, w["glu_w"], w["glu_b"],
                                        w["w_out"], f"even_out{j}")
            saved.append(dict(h=h, a=a, q=q, kv=kv, g_attn=g_attn, g_ssm=g_ssm, o_attn=o_attn, lse=lse, ug=ug, u=u,
                              hp=hp, y_scan=y_scan, mix=mix, yout=yout, mats=(kt, ws, wo, a1, a2), d_skip=d_skip,
                              mats_vjp=mats_vjp))
        else:
            w = odd[j]
            a, u, gate = norm_in(h, g, mods[i], w["w_in"], ODD_SPLITS, f"odd_in{j}")
            pm = pool_band(u, lc, False, f"pool_band_fwd{j}")
            h_new, mix, yout = pool_out(h, mods[i], pm, gate, w["pool_w"], w["pool_scale"], w["w_out"], f"pool_out{j}")
            saved.append(dict(h=h, a=a, pm=pm, gate=gate, mix=mix, yout=yout))
        h = h_new

    dh, loss_acc, dfg = loss_head(h, final_g.reshape(1, D), target, "loss_head")
    grads = dict(final_g=dfg[0], norm_g=[None] * DEPTH, even=[None, None], odd=[None, None])
    dmods = [None] * DEPTH
    rows = bsz * t_len
    flat = lambda v: v.reshape(rows, v.shape[-1])
    for i in reversed(range(DEPTH)):
        j = i // 2
        sv = saved[i]
        g = norm_g[i].reshape(1, D)
        if i % 2 == 0:
            w = even[j]
            kt, ws, wo, a1, a2 = sv["mats"]
            (d_oattn, d_gattn, d_gssm, d_yssm, dyout, zz, dsg, dgate, dglu_b) = even_out_bwd(
                dh, mods[i], sv["o_attn"], sv["g_attn"], sv["y_scan"], sv["u"], sv["d_skip"], sv["g_ssm"], w["glu_w"],
                w["glu_b"], w["w_out"], sv["yout"], f"even_out_bwd{j}")
            g_w_out = matmul_tn(flat(sv["mix"]), flat(dyout), D, D, f"even_w_out_grad{j}")
            g_glu_w = matmul_tn(flat(zz), flat(dsg), SSM_W, SSM_W, f"glu_w_grad{j}")
            dq, dkv, dsink = attn_bwd(sv["q"], sv["kv"], sv["o_attn"], sv["lse"], d_oattn, cos, sin, w["sink"], lc,
                                      f"attn_bwd{j}")
            dyg = to_groups(d_yssm, f"dy_to_groups{j}")
            dhp, dwo, dkt = s5_out_bwd(dyg, sv["ug"], sv["hp"], wo, f"s5_out_bwd{j}")
            ds, da1, da2 = s5_scan(dhp, a1, -a2, ncc, True, f"s5_scan_bwd{j}", hp=sv["hp"])
            dug, dws = s5_chunk_bwd(dyg, sv["ug"], ds, kt, ws, f"s5_chunk_bwd{j}")
            dkt2 = jnp.stack([dkt, dkt])
            da1 = da1.sum(axis=1).reshape(2, 1, G * P2)
            da2 = da2.sum(axis=1).reshape(2, 1, G * P2)
            g_ssm = sv["mats_vjp"]((dkt2, dws, dwo, da1, da2)) + (dglu_b[1],)
            dparts = [dq, dkv, d_gattn, from_groups(dug, f"du_from_groups{j}"), d_gssm]
            dh, dz, dmod, dg = norm_in_bwd(dparts, dh, sv["h"], g, mods[i], w["w_in"], f"even_in_bwd{j}",
                                           skip=(3, d_yssm, sv["d_skip"]))
            g_w_in = matmul_tn(flat(sv["a"]), flat(dz), D, dz.shape[-1], f"even_w_in_grad{j}")
            grads["even"][j] = dict(w_in=g_w_in, w_out=g_w_out, sink=dsink[0], ssm=g_ssm, glu_w=g_glu_w,
                                    glu_b=dglu_b[0])
        else:
            w = odd[j]
            dpm, dgt, dyout, dpp, dgate, dps = pool_out_bwd(dh, mods[i], sv["pm"], sv["gate"], w["pool_w"],
                                                            w["pool_scale"], w["w_out"], sv["yout"],
                                                            f"pool_out_bwd{j}")
            g_w_out = matmul_tn(flat(sv["mix"]), flat(dyout), D, D, f"odd_w_out_grad{j}")
            g_pool_w = jnp.stack([matmul_tn(flat(sv["pm"]), flat(dpp), POOL_G, POOL_G, f"pool_w_grad{j}_{gi}",
                                            a_col=gi, b_col=gi) for gi in range(4)])
            du = pool_band(dpm, lc, True, f"pool_band_bwd{j}")
            dh, dz, dmod, dg = norm_in_bwd([du, dgt], dh, sv["h"], g, mods[i], w["w_in"], f"odd_in_bwd{j}")
            g_w_in = matmul_tn(flat(sv["a"]), flat(dz), D, dz.shape[-1], f"odd_w_in_grad{j}")
            grads["odd"][j] = dict(w_in=g_w_in, w_out=g_w_out, pool_w=g_pool_w, pool_scale=dps[0])
        grads["norm_g"][i] = dg[0]
        dmods[i] = jnp.concatenate([dmod[:, :, 0:2, :], dgate[:, :, 0:1, :]], axis=2)
    return loss_acc[0, 0], dh[:, lc:, :], dmods, grads


N_DEV = 8
HBM_SPEC = pl.BlockSpec(memory_space=pltpu.HBM)


def allgather8(x_shard, name):
    m_per, n = x_shard.shape

    def body(x_ref, out_ref, send_sems, recv_sems, local_sem):
        x, y, c = lax.axis_index("x"), lax.axis_index("y"), lax.axis_index("c")
        me, sibling = (x, y, c), (x, y, 1 - c)
        chips = [(1 - x, y), (x, 1 - y), (1 - x, 1 - y)]

        def rows(px, py, pc):
            return out_ref.at[pl.ds((4 * px + 2 * py + pc) * m_per, m_per), :]

        def copy(k, block, to, src=None):
            return pltpu.make_async_remote_copy(
                src_ref=rows(*block) if src is None else src, dst_ref=rows(*block),
                send_sem=send_sems.at[k], recv_sem=recv_sems.at[k], device_id=to, device_id_type=MESH)

        mine = pltpu.make_async_copy(x_ref, rows(*me), local_sem)
        mine.start()
        first = [copy(0, me, sibling, src=x_ref)]
        first += [copy(1 + j, me, (*chip, c), src=x_ref) for j, chip in enumerate(chips)]
        for cp in first:
            cp.start()
        passed = [copy(4 + j, (*chip, c), sibling) for j, chip in enumerate(chips)]
        for j, chip in enumerate(chips):
            copy(1 + j, (*chip, c), me).wait_recv()
            passed[j].start()
        copy(0, sibling, me).wait_recv()
        for j, chip in enumerate(chips):
            copy(4 + j, (*chip, 1 - c), me).wait_recv()
        for cp in first + passed:
            cp.wait_send()
        mine.wait()

    return pl.pallas_call(
        body, name=name,
        out_shape=jax.ShapeDtypeStruct((N_DEV * m_per, n), x_shard.dtype),
        in_specs=[pl.BlockSpec(memory_space=pltpu.VMEM)],
        out_specs=pl.BlockSpec(memory_space=pltpu.VMEM),
        scratch_shapes=[pltpu.SemaphoreType.DMA((7,)), pltpu.SemaphoreType.DMA((7,)), pltpu.SemaphoreType.DMA],
        compiler_params=_cp(56),
    )(x_shard)


def xy_exchange(srcs, scatter, name):
    n = len(srcs)
    shapes = [tuple(s.shape[1:]) if scatter else tuple(s.shape) for s in srcs]

    def body(*refs):
        src_refs, out_refs = refs[:n], refs[n:2 * n]
        send_sems, recv_sems, local_sems = refs[2 * n:]
        x, y, c = lax.axis_index("x"), lax.axis_index("y"), lax.axis_index("c")
        my = 2 * x + y
        peers = [(1 - x, y), (x, 1 - y), (1 - x, 1 - y)]

        def piece(i, pos):
            return src_refs[i].at[pos] if scatter else src_refs[i]

        def copy(i, k, src_pos, dst_pos):
            px, py = peers[k]
            return pltpu.make_async_remote_copy(
                src_ref=piece(i, src_pos), dst_ref=out_refs[i].at[dst_pos], send_sem=send_sems.at[3 * i + k],
                recv_sem=recv_sems.at[3 * i + k], device_id=(px, py, c), device_id_type=MESH)

        local = [pltpu.make_async_copy(piece(i, my), out_refs[i].at[my], local_sems.at[i]) for i in range(n)]
        sends = [copy(i, k, 2 * px + py, my) for i in range(n) for k, (px, py) in enumerate(peers)]
        for cp in local + sends:
            cp.start()
        for i in range(n):
            for k, (px, py) in enumerate(peers):
                copy(i, k, my, 2 * px + py).wait_recv()
        for cp in sends:
            cp.wait_send()
        for cp in local:
            cp.wait()

    return pl.pallas_call(
        body, name=name,
        out_shape=[jax.ShapeDtypeStruct((4,) + sh, s.dtype) for sh, s in zip(shapes, srcs)],
        in_specs=[HBM_SPEC] * n, out_specs=[HBM_SPEC] * n,
        scratch_shapes=[pltpu.SemaphoreType.DMA((3 * n,)), pltpu.SemaphoreType.DMA((3 * n,)),
                        pltpu.SemaphoreType.DMA((n,))],
    )(*srcs)


def sibling_exchange(srcs, name):
    n = len(srcs)

    def body(*refs):
        src_refs, out_refs = refs[:n], refs[n:2 * n]
        send_sems, recv_sems = refs[2 * n:]
        peer = (lax.axis_index("x"), lax.axis_index("y"), 1 - lax.axis_index("c"))
        cps = [pltpu.make_async_remote_copy(src_ref=src_refs[i], dst_ref=out_refs[i], send_sem=send_sems.at[i],
                                            recv_sem=recv_sems.at[i], device_id=peer, device_id_type=MESH)
               for i in range(n)]
        for cp in cps:
            cp.start()
        for cp in cps:
            cp.wait()

    return pl.pallas_call(
        body, name=name, out_shape=[jax.ShapeDtypeStruct(s.shape, s.dtype) for s in srcs],
        in_specs=[HBM_SPEC] * n, out_specs=[HBM_SPEC] * n,
        scratch_shapes=[pltpu.SemaphoreType.DMA((n,)), pltpu.SemaphoreType.DMA((n,))],
    )(*srcs)


def _row_tile(rows, bytes_per_row, limit):
    best = None
    for tr in range(8, rows + 1, 8):
        if rows % tr == 0 and tr * bytes_per_row <= limit:
            best = tr
    return best if best is not None else rows


def sum_slots(x, name):
    n, rows, cols = x.shape
    tr = _row_tile(rows, n * cols * 4, 4 * MB)

    def body(x_ref, o_ref):
        acc = x_ref[0]
        for k in range(1, n):
            acc = acc + x_ref[k]
        o_ref[...] = acc

    return pl.pallas_call(
        body, name=name, grid=(rows // tr,),
        in_specs=[pl.BlockSpec((n, tr, cols), lambda r: (0, r, 0))],
        out_specs=pl.BlockSpec((tr, cols), lambda r: (r, 0)),
        out_shape=jax.ShapeDtypeStruct((rows, cols), F32),
        compiler_params=_cp(32, 1),
    )(x)


ADA_COLS = 3 * D // 4
C_ROWS = 8


def ada_fwd(c_all, ada_w, ada_b_cols, name):
    nrow = c_all.shape[0]

    def body(c_ref, w_ref, b_ref, o_ref):
        s, _ = _silu_and_grad(c_ref[...])
        o_ref[...] = _dot(s.astype(BF16), w_ref[...].astype(BF16)) + b_ref[...]

    return pl.pallas_call(
        body, name=name, grid=(DEPTH,),
        in_specs=[pl.BlockSpec((nrow, D), lambda i: (0, 0)), pl.BlockSpec((None, D, ADA_COLS), lambda i: (i, 0, 0)),
                  pl.BlockSpec((None, 1, ADA_COLS), lambda i: (i, 0, 0))],
        out_specs=pl.BlockSpec((None, nrow, ADA_COLS), lambda i: (i, 0, 0)),
        out_shape=jax.ShapeDtypeStruct((DEPTH, nrow, ADA_COLS), F32),
        compiler_params=_cp(32, 1),
    )(c_all, ada_w, ada_b_cols)


def ada_bwd(c_all, d_cols, ada_w, name):
    nrow = c_all.shape[0]

    def body(c_ref, d_ref, w_ref, gw_ref, ds_ref):
        @pl.when(pl.program_id(0) == 0)
        def _():
            ds_ref[...] = jnp.zeros_like(ds_ref)
        s, _ = _silu_and_grad(c_ref[...])
        dl = d_ref[...]
        gw_ref[...] = _dot_tn(s.astype(BF16), dl.astype(BF16))
        rid = lax.broadcasted_iota(jnp.int32, (nrow, 1), 0) % C_ROWS
        dctx = jnp.where((rid == 2) | (rid == 3), dl, 0.0).astype(BF16)
        ds_ref[0:1, :] += _rowsum(_dot_nt(dctx, w_ref[...].astype(BF16)))

    return pl.pallas_call(
        body, name=name, grid=(DEPTH,),
        in_specs=[pl.BlockSpec((nrow, D), lambda i: (0, 0)), pl.BlockSpec((None, nrow, ADA_COLS), lambda i: (i, 0, 0)),
                  pl.BlockSpec((None, D, ADA_COLS), lambda i: (i, 0, 0))],
        out_specs=[pl.BlockSpec((None, D, ADA_COLS), lambda i: (i, 0, 0)), pl.BlockSpec((8, D), lambda i: (0, 0))],
        out_shape=[jax.ShapeDtypeStruct((DEPTH, D, ADA_COLS), F32), jax.ShapeDtypeStruct((8, D), F32)],
        compiler_params=_cp(32, 1),
    )(c_all, d_cols, ada_w)


def ada_bias_grad(d_all, name):
    nrow = d_all.shape[1]

    def body(d_ref, o_ref):
        o_ref[...] = jnp.broadcast_to(_rowsum(d_ref[...]), o_ref.shape)

    return pl.pallas_call(
        body, name=name, grid=(DEPTH,),
        in_specs=[pl.BlockSpec((None, nrow, 3 * D), lambda i: (i, 0, 0))],
        out_specs=pl.BlockSpec((None, 8, 3 * D), lambda i: (i, 0, 0)),
        out_shape=jax.ShapeDtypeStruct((DEPTH, 8, 3 * D), F32),
        compiler_params=_cp(32, 1),
    )(d_all)


def silu_chain(ds, c, name):
    def body(ds_ref, c_ref, o_ref):
        _, dsl = _silu_and_grad(c_ref[...])
        o_ref[...] = ds_ref[...] * dsl

    return pl.pallas_call(body, name=name, out_shape=jax.ShapeDtypeStruct(ds.shape, F32))(ds, c)


def _flat_cols(shape):
    size = int(np.prod(shape))
    if shape[-1] >= 128:
        return shape[-1]
    for cols in (1024, 128):
        if size % cols == 0:
            return cols
    return shape[-1]


def adamw(w, m, v, grads, name):
    shape = w.shape
    cols = _flat_cols(shape)
    as2d = lambda a: a.reshape(-1, cols)
    rows = w.size // cols
    tr = _row_tile(rows, cols * 4, MB)
    k = len(grads)

    def body(*refs):
        w_ref, m_ref, v_ref = refs[:3]
        g_refs = refs[3:3 + k]
        g_out, d_out, m_out, v_out = refs[3 + k:]
        g = g_refs[0][...]
        for r in g_refs[1:]:
            g = g + r[...]
        g_out[...] = g
        mn = ADAM_B1 * m_ref[...] + (1.0 - ADAM_B1) * g
        vn = ADAM_B2 * v_ref[...] + (1.0 - ADAM_B2) * (g * g)
        m_out[...] = mn
        v_out[...] = vn
        m_hat = mn / (1.0 - ADAM_B1 ** ADAM_STEP)
        v_hat = vn / (1.0 - ADAM_B2 ** ADAM_STEP)
        d_out[...] = -ADAM_LR * (m_hat / (jnp.sqrt(v_hat) + ADAM_EPS) + ADAM_WD * w_ref[...])

    spec = pl.BlockSpec((tr, cols), lambda r: (r, 0))
    outs = pl.pallas_call(
        body, name=name, grid=(rows // tr,),
        in_specs=[spec] * (3 + k), out_specs=[spec] * 4,
        out_shape=[jax.ShapeDtypeStruct((rows, cols), F32)] * 4,
        compiler_params=_cp(32, 1),
    )(as2d(w), as2d(m), as2d(v), *[as2d(g) for g in grads])
    return tuple(o.reshape(shape) for o in outs)


BIG = (("even_w_in", (2, D, 576), 2), ("even_w_out", (2, 256, D), 1), ("glu_w", (2, 128, SSM_W), 1),
       ("odd_w_in", (2, D, 512), 2), ("odd_w_out", (2, 256, D), 1), ("pool_w", (2, 4, 64, POOL_G), 2))


def _full_shape(shard, axis):
    return tuple(4 * s if a == axis else s for a, s in enumerate(shard))


def _to_shards(full, shard, axis):
    return jnp.moveaxis(full.reshape(shard[:axis] + (4,) + shard[axis:]), axis, 0)


def _from_shards(stacked, shard, axis):
    return jnp.moveaxis(stacked, 0, axis).reshape(_full_shape(shard, axis))


SMALL = (("ds_ctx", (D,)), ("norm_g", (DEPTH, D)), ("final_g", (D,)), ("attn_sink", (2, N_HEADS)),
         ("ssm_a_re", (2, 2, G, P)), ("ssm_a_im", (2, 2, G, P)), ("ssm_log_dt", (2, 2, G)),
         ("ssm_b_re", (2, 2, G, P, C)), ("ssm_b_im", (2, 2, G, P, C)), ("ssm_c_re", (2, 2, G, C, P)),
         ("ssm_c_im", (2, 2, G, C, P)), ("ssm_d", (2, SSM_W)), ("glu_b", (2, SSM_W)), ("pool_scale", (2, D)))
SMALL_PAD = 8 * 128


def pack_small(vals):
    flat = jnp.concatenate([vals[n].reshape(-1) for n, _ in SMALL])
    pad = (-flat.shape[0]) % SMALL_PAD
    return jnp.pad(flat, (0, pad)).reshape(-1, 128)


def unpack_small(packed):
    flat, out, off = packed.reshape(-1), {}, 0
    for n, shape in SMALL:
        size = int(np.prod(shape))
        out[n] = flat[off:off + size].reshape(shape)
        off += size
    return out


WEIGHT_NAMES = ('c_ctx', 'ada_w', 'ada_b', 'norm_g', 'even_w_in', 'even_w_out', 'attn_sink', 'ssm_a_re', 'ssm_a_im',
                'ssm_log_dt', 'ssm_b_re', 'ssm_b_im', 'ssm_c_re', 'ssm_c_im', 'ssm_d', 'glu_w', 'glu_b', 'odd_w_in',
                'odd_w_out', 'pool_w', 'pool_scale', 'final_g')
SSM_NAMES = ('ssm_a_re', 'ssm_a_im', 'ssm_log_dt', 'ssm_b_re', 'ssm_b_im', 'ssm_c_re', 'ssm_c_im', 'ssm_d')


def kernel(x, c, ctx, c_ctx, ada_w, ada_b, norm_g, even_w_in, even_w_out, attn_sink, ssm_a_re, ssm_a_im, ssm_log_dt, ssm_b_re, ssm_b_im, ssm_c_re, ssm_c_im, ssm_d, glu_w, glu_b, odd_w_in, odd_w_out, pool_w, pool_scale, final_g, loss_target, m_c_ctx, m_ada_w, m_ada_b, m_norm_g, m_even_w_in, m_even_w_out, m_attn_sink, m_ssm_a_re, m_ssm_a_im, m_ssm_log_dt, m_ssm_b_re, m_ssm_b_im, m_ssm_c_re, m_ssm_c_im, m_ssm_d, m_glu_w, m_glu_b, m_odd_w_in, m_odd_w_out, m_pool_w, m_pool_scale, m_final_g, v_c_ctx, v_ada_w, v_ada_b, v_norm_g, v_even_w_in, v_even_w_out, v_attn_sink, v_ssm_a_re, v_ssm_a_im, v_ssm_log_dt, v_ssm_b_re, v_ssm_b_im, v_ssm_c_re, v_ssm_c_im, v_ssm_d, v_glu_w, v_glu_b, v_odd_w_in, v_odd_w_out, v_pool_w, v_pool_scale, v_final_g):
    env = dict(locals())
    weights = {n: env[n] for n in WEIGHT_NAMES}
    bsz = x.shape[0]
    ax, ay, ac = lax.axis_index("x"), lax.axis_index("y"), lax.axis_index("c")
    pos = 2 * ax + ay
    dev = 2 * pos + ac

    c_rows = jnp.concatenate([c, c_ctx.reshape(1, D), c_ctx.reshape(1, D), jnp.zeros((C_ROWS - bsz - 2, D), F32)])
    c_all = allgather8(c_rows, "gather_c")
    ada_b_cols = lax.dynamic_slice(ada_b, (0, pos * ADA_COLS), (DEPTH, ADA_COLS)).reshape(DEPTH, 1, ADA_COLS)
    mod_cols = ada_fwd(c_all, ada_w, ada_b_cols, "ada_fwd")
    nrow = N_DEV * C_ROWS
    misc = jnp.concatenate([mod_cols.reshape(DEPTH * nrow, ADA_COLS),
                            jnp.pad(pool_scale, ((0, 6), (0, ADA_COLS - pool_scale.shape[1])))])
    misc_all = allgather8(misc, "gather_mod").reshape(4, 2, DEPTH * nrow + 8, ADA_COLS)[:, 0]
    mod_full = misc_all[:, :DEPTH * nrow].reshape(4, DEPTH, nrow, ADA_COLS).transpose(1, 2, 0, 3)
    mod_mine = lax.dynamic_slice(mod_full.reshape(DEPTH, nrow, 3 * D), (0, dev * C_ROWS, 0), (DEPTH, C_ROWS, 3 * D))
    mods = []
    for i in range(DEPTH):
        lat = mod_mine[i, :bsz].reshape(bsz, 1, 3, D)
        con = jnp.broadcast_to(mod_mine[i, bsz].reshape(1, 1, 3, D), (bsz, 1, 3, D))
        mods.append(jnp.pad(jnp.concatenate([con, lat], axis=1), ((0, 0), (0, 0), (0, 5), (0, 0))))
    pool_scale_full = misc_all[:, DEPTH * nrow:DEPTH * nrow + 2, :pool_scale.shape[1]].transpose(1, 0, 2).reshape(2, D)

    gathered = xy_exchange([weights[n].astype(BF16) for n, _, _ in BIG], False, "gather_weights")
    full = {n: _from_shards(g, shard, axis) for (n, shard, axis), g in zip(BIG, gathered)}
    even = [dict(w_in=full["even_w_in"][j], w_out=full["even_w_out"][j], sink=attn_sink[j],
                 ssm=tuple(weights[n][j] for n in SSM_NAMES), glu_w=full["glu_w"][j],
                 glu_b=glu_b[j].reshape(1, SSM_W)) for j in range(2)]
    odd = [dict(w_in=full["odd_w_in"][j], w_out=full["odd_w_out"][j], pool_w=full["pool_w"][j],
                pool_scale=pool_scale_full[j].reshape(1, D)) for j in range(2)]

    loss_local, grad_x, dmods, grads = local_step(x, ctx, loss_target, mods, norm_g, final_g, even, odd)
    loss = lax.psum(loss_local, ("x", "y", "c"))

    d_rows = jnp.stack([jnp.concatenate([dm[:, 1].reshape(bsz, 3 * D), dm[:, 0].reshape(bsz, 3 * D),
                                         jnp.zeros((C_ROWS - 2 * bsz, 3 * D), F32)]) for dm in dmods])
    d_all = allgather8(d_rows.reshape(DEPTH * C_ROWS, 3 * D), "gather_dmod")
    d_all = d_all.reshape(N_DEV, DEPTH, C_ROWS, 3 * D).transpose(1, 0, 2, 3).reshape(DEPTH, nrow, 3 * D)
    d_cols = lax.dynamic_slice(d_all, (0, 0, pos * ADA_COLS), (DEPTH, nrow, ADA_COLS))
    g_ada_w, ds_ctx = ada_bwd(c_all, d_cols, ada_w, "ada_bwd")
    g_ada_b = ada_bias_grad(d_all, "ada_bias_grad")[:, 0]

    small = dict(ds_ctx=ds_ctx[0] * (ac == 0).astype(F32), norm_g=jnp.stack(grads["norm_g"]), final_g=grads["final_g"],
                 attn_sink=jnp.stack([grads["even"][j]["sink"] for j in range(2)]),
                 glu_b=jnp.stack([grads["even"][j]["glu_b"] for j in range(2)]),
                 pool_scale=jnp.stack([grads["odd"][j]["pool_scale"] for j in range(2)]))
    for k, n in enumerate(SSM_NAMES):
        small[n] = jnp.stack([grads["even"][j]["ssm"][k] for j in range(2)])
    packed = pack_small(small)
    small_sum = sum_slots(allgather8(packed, "gather_small").reshape(N_DEV, packed.shape[0], 128), "sum_small")
    g_small = unpack_small(small_sum)
    g_small["c_ctx"] = silu_chain(g_small.pop("ds_ctx").reshape(1, D), c_ctx.reshape(1, D), "c_ctx_grad").reshape(D)
    g_small["ada_b"] = g_ada_b
    g_small["pool_scale"] = lax.dynamic_slice(g_small["pool_scale"], (0, pos * 256), (2, 256))

    big_full = dict(even_w_in=jnp.stack([grads["even"][j]["w_in"] for j in range(2)]),
                    even_w_out=jnp.stack([grads["even"][j]["w_out"] for j in range(2)]),
                    glu_w=jnp.stack([grads["even"][j]["glu_w"] for j in range(2)]),
                    odd_w_in=jnp.stack([grads["odd"][j]["w_in"] for j in range(2)]),
                    odd_w_out=jnp.stack([grads["odd"][j]["w_out"] for j in range(2)]),
                    pool_w=jnp.stack([grads["odd"][j]["pool_w"] for j in range(2)]))
    landed = xy_exchange([_to_shards(big_full[n], shard, axis) for n, shard, axis in BIG], True, "scatter_grads")
    mine4 = [sum_slots(r.reshape(4, -1, r.shape[-1]), "sum_positions_" + n).reshape(shard)
             for (n, shard, _), r in zip(BIG, landed)]
    other4 = sibling_exchange(mine4, "swap_cores")
    g_mine = dict(zip([n for n, _, _ in BIG], mine4))
    g_other = dict(zip([n for n, _, _ in BIG], other4))

    results = {}
    for n in WEIGHT_NAMES:
        if n in g_mine:
            gs = [g_mine[n], g_other[n]]
        elif n == "ada_w":
            gs = [g_ada_w]
        else:
            gs = [g_small[n]]
        results[n] = adamw(weights[n], env["m_" + n], env["v_" + n], gs, "adamw_" + n)
    outs = [loss, grad_x]
    for k in range(4):
        outs += [results[n][k] for n in WEIGHT_NAMES]
    return tuple(outs)
```

```python
import functools

import numpy as np
import jax
import jax.numpy as jnp
from jax import lax
from jax.experimental import pallas as pl
from jax.experimental.pallas import tpu as pltpu

F32 = jnp.float32
BF16 = jnp.bfloat16
MESH = pl.DeviceIdType.MESH

D = 1024
DEPTH = 4
EPS = 1e-6
NEG_INF = -1e30
GRID_W = 64
ROPE_BASE = 10000.0
ROPE_FREQS = 16
HEAD_DIM = 64
N_HEADS = 8
N_KV = 2
GROUP = 4
ATTN_W = N_HEADS * HEAD_DIM
KV_W = N_KV * HEAD_DIM
WINDOW = 128
AB = 128
SSM_W = 512
G = 32
C = 16
P = 64
Q = 16
QC = Q * C
P2 = 2 * P
SCAN_G = 16
POOL_R = (1, 2, 4, 8)
POOL_G = 256
HALO = 8
TM = 256
EVEN_SPLITS = (512, 256, 512, 512, 512)
ODD_SPLITS = (1024, 1024)

ADAM_LR = 0.001
ADAM_B1 = 0.9
ADAM_B2 = 0.999
ADAM_EPS = 1e-08
ADAM_WD = 0.01
ADAM_STEP = 10

MB = 1024 * 1024


def _cp(vmem_mb=48, n_axes=0):
    kw = dict(vmem_limit_bytes=vmem_mb * MB)
    if n_axes:
        kw["dimension_semantics"] = ("arbitrary",) * n_axes
    return pltpu.CompilerParams(**kw)


def _sig(x):
    return 1.0 / (1.0 + jnp.exp(-x))


def _silu_and_grad(x):
    s = _sig(x)
    return x * s, s * (1.0 + x * (1.0 - s))


_GELU_C = 0.7978845608028654
_GELU_A = 0.044715


def _gelu_and_grad(x):
    th = jnp.tanh(_GELU_C * (x + _GELU_A * x * x * x))
    val = 0.5 * x * (1.0 + th)
    grad = 0.5 * (1.0 + th) + 0.5 * x * (1.0 - th * th) * _GELU_C * (1.0 + 3.0 * _GELU_A * x * x)
    return val, grad


def _rms(h):
    r = lax.rsqrt(jnp.mean(h * h, axis=-1, keepdims=True) + EPS)
    return h * r, r


def _dot(a, b):
    return jnp.dot(a, b, preferred_element_type=F32)


def _dot_nt(a, b):
    return lax.dot_general(a, b, (((1,), (1,)), ((), ())), preferred_element_type=F32)


def _dot_tn(a, b):
    return lax.dot_general(a, b, (((0,), (0,)), ((), ())), preferred_element_type=F32)


def _rowsum(x):
    return jnp.sum(x, axis=0, keepdims=True)


def _seg(t):
    return jnp.minimum(t, 1)


def _row_spec(n):
    return pl.BlockSpec((None, TM, n), lambda b, t: (b, t, 0))


def _const_spec(shape):
    nd = len(shape)
    return pl.BlockSpec(shape, lambda b, t: (0,) * nd)


def _mod_spec():
    return pl.BlockSpec((None, None, 8, D), lambda b, t: (b, _seg(t), 0, 0))


def norm_in(h, g, mod, w, splits, name):
    bsz, t_len, _ = h.shape
    n = w.shape[1]
    offs = [int(v) for v in np.cumsum((0,) + tuple(splits))]

    def body(h_ref, g_ref, mod_ref, w_ref, a_ref, *outs):
        xh, _ = _rms(h_ref[...])
        a = xh * g_ref[...] * (1.0 + mod_ref[1:2, :]) + mod_ref[0:1, :]
        ab = a.astype(BF16)
        a_ref[...] = ab
        z = _dot(ab, w_ref[...])
        for o, lo, hi in zip(outs, offs[:-1], offs[1:]):
            o[...] = z[:, lo:hi]

    return pl.pallas_call(
        body, name=name, grid=(bsz, t_len // TM),
        in_specs=[_row_spec(D), _const_spec((1, D)), _mod_spec(), _const_spec((D, n))],
        out_specs=[_row_spec(D)] + [_row_spec(s) for s in splits],
        out_shape=[jax.ShapeDtypeStruct((bsz, t_len, D), BF16)]
        + [jax.ShapeDtypeStruct((bsz, t_len, s), F32) for s in splits],
        compiler_params=_cp(48, 2),
    )(h, g, mod, w)


def norm_in_bwd(dparts, dh_in, h, g, mod, w, name, skip=None):
    bsz, t_len, _ = h.shape
    n = w.shape[1]
    k = len(dparts)
    extra = [] if skip is None else [skip[1], skip[2]]

    def body(*refs):
        parts = [r[...] for r in refs[:k]]
        if skip is not None:
            parts[skip[0]] = parts[skip[0]] + refs[k][...] * refs[k + 1][...]
        dh_in_ref, h_ref, g_ref, mod_ref, w_ref, dh_ref, dz_ref, dmod_ref, dg_ref = refs[k + len(extra):]
        b, t = pl.program_id(0), pl.program_id(1)
        dz = jnp.concatenate(parts, axis=1).astype(BF16)
        dz_ref[...] = dz
        da = _dot_nt(dz, w_ref[...])
        xh, r = _rms(h_ref[...])
        gg = g_ref[...]
        sc1 = 1.0 + mod_ref[1:2, :]

        @pl.when(t <= 1)
        def _():
            dmod_ref[...] = jnp.zeros_like(dmod_ref)

        @pl.when((b == 0) & (t == 0))
        def _():
            dg_ref[...] = jnp.zeros_like(dg_ref)

        dmod_ref[0:1, :] += _rowsum(da)
        dmod_ref[1:2, :] += _rowsum(da * (xh * gg))
        dg_ref[0:1, :] += _rowsum(da * sc1 * xh)
        dxh = da * gg * sc1
        dh_ref[...] = dh_in_ref[...] + r * (dxh - xh * jnp.mean(dxh * xh, axis=-1, keepdims=True))

    return pl.pallas_call(
        body, name=name, grid=(bsz, t_len // TM),
        in_specs=[_row_spec(p.shape[-1]) for p in dparts]
        + ([_row_spec(extra[0].shape[-1]), _const_spec(extra[1].shape)] if extra else [])
        + [_row_spec(D), _row_spec(D), _const_spec((1, D)), _mod_spec(), _const_spec((D, n))],
        out_specs=[_row_spec(D), _row_spec(n), _mod_spec(), _const_spec((8, D))],
        out_shape=[jax.ShapeDtypeStruct((bsz, t_len, D), F32), jax.ShapeDtypeStruct((bsz, t_len, n), BF16),
                   jax.ShapeDtypeStruct((bsz, 2, 8, D), F32), jax.ShapeDtypeStruct((8, D), F32)],
        compiler_params=_cp(56, 2),
    )(*dparts, *extra, dh_in, h, g, mod, w)


def matmul_tn(a, b, m, n, name, a_col=0, b_col=0):
    rows = a.shape[0]
    tr = 512 if rows % 512 == 0 else rows
    tn = n
    for cand in (1024, 768, 512, 256, 128):
        if n > 1024 and n % cand == 0:
            tn = cand
            break
    nb = n // tn

    def body(a_ref, b_ref, o_ref):
        @pl.when(pl.program_id(1) == 0)
        def _():
            o_ref[...] = jnp.zeros_like(o_ref)
        o_ref[...] += _dot_tn(a_ref[...].astype(BF16), b_ref[...].astype(BF16))

    return pl.pallas_call(
        body, name=name, grid=(nb, rows // tr),
        in_specs=[pl.BlockSpec((tr, m), lambda j, r: (r, a_col)),
                  pl.BlockSpec((tr, tn), lambda j, r: (r, b_col * nb + j))],
        out_specs=pl.BlockSpec((m, tn), lambda j, r: (0, j)),
        out_shape=jax.ShapeDtypeStruct((m, n), F32),
        compiler_params=_cp(48, 2),
    )(a, b)


def even_out(h, mod, o_attn, g_attn, y_scan, u, d_skip, g_ssm, glu_w, glu_b, w_out, name):
    bsz, t_len, _ = h.shape

    def body(h_ref, mod_ref, oa_ref, ga_ref, ys_ref, u_ref, dk_ref, gs_ref, gw_ref, gb_ref, wo_ref,
             hn_ref, mix_ref, yo_ref):
        zz, _ = _gelu_and_grad(ys_ref[...] + u_ref[...] * dk_ref[...])
        s = _dot(zz.astype(BF16), gw_ref[...]) + gb_ref[...]
        o_ssm = zz * _sig(s)
        sa, _ = _silu_and_grad(ga_ref[...])
        ss, _ = _silu_and_grad(gs_ref[...])
        mb = jnp.concatenate([oa_ref[...] * sa, o_ssm * ss], axis=1).astype(BF16)
        mix_ref[...] = mb
        yo = _dot(mb, wo_ref[...])
        yo_ref[...] = yo
        hn_ref[...] = h_ref[...] + mod_ref[2:3, :] * yo

    return pl.pallas_call(
        body, name=name, grid=(bsz, t_len // TM),
        in_specs=[_row_spec(D), _mod_spec(), _row_spec(512), _row_spec(512), _row_spec(512), _row_spec(512),
                  _const_spec((1, 512)), _row_spec(512), _const_spec((512, 512)), _const_spec((1, 512)),
                  _const_spec((D, D))],
        out_specs=[_row_spec(D), _row_spec(D), _row_spec(D)],
        out_shape=[jax.ShapeDtypeStruct((bsz, t_len, D), F32), jax.ShapeDtypeStruct((bsz, t_len, D), BF16),
                   jax.ShapeDtypeStruct((bsz, t_len, D), F32)],
        compiler_params=_cp(48, 2),
    )(h, mod, o_attn, g_attn, y_scan, u, d_skip, g_ssm, glu_w, glu_b, w_out)


def even_out_bwd(dh, mod, o_attn, g_attn, y_scan, u, d_skip, g_ssm, glu_w, glu_b, w_out, yout, name):
    bsz, t_len, _ = dh.shape

    def body(dh_ref, mod_ref, oa_ref, ga_ref, ys_ref, u_ref, dk_ref, gs_ref, gw_ref, gb_ref, wo_ref, yo_ref,
             doa_ref, dga_ref, dgs_ref, dys_ref, dyo_ref, zz_ref, ds_ref, dgate_ref, dgb_ref):
        b, t = pl.program_id(0), pl.program_id(1)
        dhv = dh_ref[...]

        @pl.when(t <= 1)
        def _():
            dgate_ref[...] = jnp.zeros_like(dgate_ref)

        @pl.when((b == 0) & (t == 0))
        def _():
            dgb_ref[...] = jnp.zeros_like(dgb_ref)

        dgate_ref[0:1, :] += _rowsum(dhv * yo_ref[...])
        dyb = (mod_ref[2:3, :] * dhv).astype(BF16)
        dyo_ref[...] = dyb
        dmix = _dot_nt(dyb, wo_ref[...])
        sa, dsa = _silu_and_grad(ga_ref[...])
        doa_ref[...] = dmix[:, :512] * sa
        dga_ref[...] = dmix[:, :512] * oa_ref[...] * dsa
        uv = u_ref[...]
        zz, dzz_dy = _gelu_and_grad(ys_ref[...] + uv * dk_ref[...])
        zb = zz.astype(BF16)
        zz_ref[...] = zb
        sg = _sig(_dot(zb, gw_ref[...]) + gb_ref[...])
        ss, dss = _silu_and_grad(gs_ref[...])
        dm = dmix[:, 512:]
        dgs_ref[...] = dm * (zz * sg) * dss
        do = dm * ss
        ds = do * zz * sg * (1.0 - sg)
        dsb = ds.astype(BF16)
        ds_ref[...] = dsb
        dgb_ref[0:1, :] += _rowsum(ds)
        dys = (do * sg + _dot_nt(dsb, gw_ref[...])) * dzz_dy
        dys_ref[...] = dys
        dgb_ref[1:2, :] += _rowsum(dys * uv)

    r512 = jax.ShapeDtypeStruct((bsz, t_len, 512), F32)
    return pl.pallas_call(
        body, name=name, grid=(bsz, t_len // TM),
        in_specs=[_row_spec(D), _mod_spec(), _row_spec(512), _row_spec(512), _row_spec(512), _row_spec(512),
                  _const_spec((1, 512)), _row_spec(512), _const_spec((512, 512)), _const_spec((1, 512)),
                  _const_spec((D, D)), _row_spec(D)],
        out_specs=[_row_spec(512)] * 4 + [_row_spec(D), _row_spec(512), _row_spec(512), _mod_spec(),
                                           _const_spec((8, 512))],
        out_shape=[r512, r512, r512, r512, jax.ShapeDtypeStruct((bsz, t_len, D), BF16),
                   jax.ShapeDtypeStruct((bsz, t_len, 512), BF16), jax.ShapeDtypeStruct((bsz, t_len, 512), BF16),
                   jax.ShapeDtypeStruct((bsz, 2, 8, D), F32), jax.ShapeDtypeStruct((8, 512), F32)],
        compiler_params=_cp(48, 2),
    )(dh, mod, o_attn, g_attn, y_scan, u, d_skip, g_ssm, glu_w, glu_b, w_out, yout)


def _split3_dot(band, x):
    x1 = x.astype(BF16)
    r1 = x - x1.astype(F32)
    x2 = r1.astype(BF16)
    x3 = (r1 - x2.astype(F32)).astype(BF16)
    return _dot(band, x3) + _dot(band, x2) + _dot(band, x1)


def pool_band(x, lc, transpose, name):
    bsz, t_len, _ = x.shape
    assert lc == TM
    hb = TM // HALO

    def body(xp_ref, xc_ref, xn_ref, o_ref):
        t = pl.program_id(1)
        seg_lo = jnp.where(t == 0, 0, lc)
        seg_hi = jnp.where(t == 0, lc, t_len)
        cur = xc_ref[...]
        xh = jnp.concatenate([xp_ref[...], cur, xn_ref[...]], axis=0)
        row_t = t * TM + lax.broadcasted_iota(jnp.int32, (TM, 1), 0)
        col_s = t * TM - HALO + lax.broadcasted_iota(jnp.int32, (1, TM + 2 * HALO), 1)
        row_s = t * TM - HALO + lax.broadcasted_iota(jnp.int32, (TM + 2 * HALO, 1), 0)
        s_ok = (col_s >= seg_lo) & (col_s < seg_hi)
        outs = []
        for gi, r in enumerate(POOL_R):
            band = ((jnp.abs(row_t - col_s) <= r) & s_ok).astype(BF16)
            xg = xh[:, gi * POOL_G:(gi + 1) * POOL_G]
            if transpose:
                cnt_s = jnp.minimum(row_s + r, seg_hi - 1) - jnp.maximum(row_s - r, seg_lo) + 1
                xg = xg * (1.0 / jnp.maximum(cnt_s, 1).astype(F32))
            acc = _split3_dot(band, xg)
            if not transpose:
                cnt_t = jnp.minimum(row_t + r, seg_hi - 1) - jnp.maximum(row_t - r, seg_lo) + 1
                acc = acc * (1.0 / cnt_t.astype(F32))
            outs.append(acc - cur[:, gi * POOL_G:(gi + 1) * POOL_G])
        o_ref[...] = jnp.concatenate(outs, axis=1)

    return pl.pallas_call(
        body, name=name, grid=(bsz, t_len // TM),
        in_specs=[pl.BlockSpec((None, HALO, D), lambda b, t: (b, jnp.maximum(t * hb - 1, 0), 0)),
                  _row_spec(D),
                  pl.BlockSpec((None, HALO, D), lambda b, t: (b, jnp.minimum((t + 1) * hb, t_len // HALO - 1), 0))],
        out_specs=_row_spec(D),
        out_shape=jax.ShapeDtypeStruct((bsz, t_len, D), F32),
        compiler_params=_cp(48, 2),
    )(x, x, x)


def pool_out(h, mod, pm, gate, pool_w, pool_scale, w_out, name):
    bsz, t_len, _ = h.shape

    def body(h_ref, mod_ref, pm_ref, gt_ref, pw_ref, ps_ref, wo_ref, hn_ref, mix_ref, yo_ref):
        pmv = pm_ref[...]
        ppre = jnp.concatenate([_dot(pmv[:, g * POOL_G:(g + 1) * POOL_G].astype(BF16), pw_ref[g])
                                for g in range(4)], axis=1)
        sl, _ = _silu_and_grad(gt_ref[...])
        mb = (ppre * ps_ref[...] * sl).astype(BF16)
        mix_ref[...] = mb
        yo = _dot(mb, wo_ref[...])
        yo_ref[...] = yo
        hn_ref[...] = h_ref[...] + mod_ref[2:3, :] * yo

    return pl.pallas_call(
        body, name=name, grid=(bsz, t_len // TM),
        in_specs=[_row_spec(D), _mod_spec(), _row_spec(D), _row_spec(D), _const_spec((4, POOL_G, POOL_G)),
                  _const_spec((1, D)), _const_spec((D, D))],
        out_specs=[_row_spec(D), _row_spec(D), _row_spec(D)],
        out_shape=[jax.ShapeDtypeStruct((bsz, t_len, D), F32), jax.ShapeDtypeStruct((bsz, t_len, D), BF16),
                   jax.ShapeDtypeStruct((bsz, t_len, D), F32)],
        compiler_params=_cp(48, 2),
    )(h, mod, pm, gate, pool_w, pool_scale, w_out)


def pool_out_bwd(dh, mod, pm, gate, pool_w, pool_scale, w_out, yout, name):
    bsz, t_len, _ = dh.shape

    def body(dh_ref, mod_ref, pm_ref, gt_ref, pw_ref, ps_ref, wo_ref, yo_ref,
             dpm_ref, dgt_ref, dyo_ref, dpp_ref, dgate_ref, dps_ref):
        b, t = pl.program_id(0), pl.program_id(1)
        dhv = dh_ref[...]

        @pl.when(t <= 1)
        def _():
            dgate_ref[...] = jnp.zeros_like(dgate_ref)

        @pl.when((b == 0) & (t == 0))
        def _():
            dps_ref[...] = jnp.zeros_like(dps_ref)

        dgate_ref[0:1, :] += _rowsum(dhv * yo_ref[...])
        dyb = (mod_ref[2:3, :] * dhv).astype(BF16)
        dyo_ref[...] = dyb
        dmix = _dot_nt(dyb, wo_ref[...])
        pmv = pm_ref[...]
        ppre = jnp.concatenate([_dot(pmv[:, g * POOL_G:(g + 1) * POOL_G].astype(BF16), pw_ref[g])
                                for g in range(4)], axis=1)
        ps = ps_ref[...]
        sl, dsl = _silu_and_grad(gt_ref[...])
        dp = dmix * sl
        dgt_ref[...] = dmix * (ppre * ps) * dsl
        dps_ref[0:1, :] += _rowsum(dp * ppre)
        dppb = (dp * ps).astype(BF16)
        dpp_ref[...] = dppb
        dpm_ref[...] = jnp.concatenate([_dot_nt(dppb[:, g * POOL_G:(g + 1) * POOL_G], pw_ref[g])
                                        for g in range(4)], axis=1)

    return pl.pallas_call(
        body, name=name, grid=(bsz, t_len // TM),
        in_specs=[_row_spec(D), _mod_spec(), _row_spec(D), _row_spec(D), _const_spec((4, POOL_G, POOL_G)),
                  _const_spec((1, D)), _const_spec((D, D)), _row_spec(D)],
        out_specs=[_row_spec(D), _row_spec(D), _row_spec(D), _row_spec(D), _mod_spec(), _const_spec((8, D))],
        out_shape=[jax.ShapeDtypeStruct((bsz, t_len, D), F32), jax.ShapeDtypeStruct((bsz, t_len, D), F32),
                   jax.ShapeDtypeStruct((bsz, t_len, D), BF16), jax.ShapeDtypeStruct((bsz, t_len, D), BF16),
                   jax.ShapeDtypeStruct((bsz, 2, 8, D), F32), jax.ShapeDtypeStruct((8, D), F32)],
        compiler_params=_cp(48, 2),
    )(dh, mod, pm, gate, pool_w, pool_scale, w_out, yout)


def loss_head(h, final_g, target, name):
    bsz, t_len, _ = h.shape

    def body(h_ref, g_ref, tg_ref, dh_ref, loss_ref, dg_ref):
        b, t = pl.program_id(0), pl.program_id(1)

        @pl.when((b == 0) & (t == 0))
        def _():
            loss_ref[...] = jnp.zeros_like(loss_ref)
            dg_ref[...] = jnp.zeros_like(dg_ref)

        lat = (t > 0).astype(F32)
        xh, r = _rms(h_ref[...])
        gg = g_ref[...]
        err = (xh * gg - tg_ref[...]) * lat
        loss_ref[...] += 0.5 * jnp.sum(jnp.mean(err * err, axis=-1, keepdims=True))
        dy = err * (1.0 / D)
        dg_ref[0:1, :] += _rowsum(dy * xh)
        dxh = dy * gg
        dh_ref[...] = r * (dxh - xh * jnp.mean(dxh * xh, axis=-1, keepdims=True))

    return pl.pallas_call(
        body, name=name, grid=(bsz, t_len // TM),
        in_specs=[_row_spec(D), _const_spec((1, D)),
                  pl.BlockSpec((None, TM, D), lambda b, t: (b, jnp.maximum(t - 1, 0), 0))],
        out_specs=[_row_spec(D), _const_spec((8, 128)), _const_spec((8, D))],
        out_shape=[jax.ShapeDtypeStruct((bsz, t_len, D), F32), jax.ShapeDtypeStruct((8, 128), F32),
                   jax.ShapeDtypeStruct((8, D), F32)],
        compiler_params=_cp(48, 2),
    )(h, final_g, target)


def _swap16(x):
    n = x.shape[-1]
    ax = x.ndim - 1
    lane = lax.broadcasted_iota(jnp.int32, x.shape, ax)
    return jnp.where((lane % 32) < 16, pltpu.roll(x, n - 16, ax), pltpu.roll(x, 16, ax))


def _rope(x, cos, sin):
    return x * cos + _swap16(x) * sin


def _rope_t(dy, cos, sin):
    return dy * cos + _swap16(dy * sin)


def rope_tables(lc, seq):
    rows = seq // GRID_W
    row = jnp.repeat(jnp.arange(rows, dtype=F32), GRID_W)
    col = jnp.tile(jnp.arange(GRID_W, dtype=F32), rows)
    inv_freq = ROPE_BASE ** (-jnp.arange(ROPE_FREQS, dtype=F32) / ROPE_FREQS)
    ar, ac = row[:, None] * inv_freq, col[:, None] * inv_freq
    cos = jnp.concatenate([jnp.cos(ar), jnp.cos(ar), jnp.cos(ac), jnp.cos(ac)], axis=1)
    sin = jnp.concatenate([-jnp.sin(ar), jnp.sin(ar), -jnp.sin(ac), jnp.sin(ac)], axis=1)
    cos = jnp.concatenate([jnp.ones((lc, HEAD_DIM), F32), cos], axis=0)
    sin = jnp.concatenate([jnp.zeros((lc, HEAD_DIM), F32), sin], axis=0)
    return jnp.tile(cos, (1, 2)), jnp.tile(sin, (1, 2))


def _attn_mask(i, lc, t_len):
    qrow = i * AB + lax.broadcasted_iota(jnp.int32, (AB, 1), 0)
    kloc = (i - 1) * AB + lax.broadcasted_iota(jnp.int32, (1, 3 * AB), 1)
    valid = (qrow >= lc) & (kloc >= lc) & (kloc < t_len) & (jnp.abs(qrow - kloc) <= WINDOW)
    mask = jnp.concatenate([valid, jnp.ones((AB, lc), jnp.bool_)], axis=1)
    return jnp.concatenate([mask] * GROUP, axis=0)


def _attn_specs(t_len, lc):
    nb = t_len // AB
    prev = lambda b, i: (b, jnp.maximum(i - 1, 0), 0)
    cur = lambda b, i: (b, i, 0)
    nxt = lambda b, i: (b, jnp.minimum(i + 1, nb - 1), 0)
    kv = [pl.BlockSpec((None, AB, 2 * KV_W), f) for f in (prev, cur, nxt)]
    kv.append(pl.BlockSpec((None, lc, 2 * KV_W), lambda b, i: (b, 0, 0)))
    tab = [pl.BlockSpec((AB, 128), lambda b, i, f=f: f(b, i)[1:]) for f in (prev, cur, nxt)]
    return kv, tab


def _attn_keys(kvp, kvc, kvn, kvx, cp, cc, cn, sp, sc, sn):
    kk = jnp.concatenate([_rope(kvp[:, :KV_W], cp, sp), _rope(kvc[:, :KV_W], cc, sc),
                          _rope(kvn[:, :KV_W], cn, sn), kvx[:, :KV_W]], axis=0)
    vv = jnp.concatenate([kvp[:, KV_W:], kvc[:, KV_W:], kvn[:, KV_W:], kvx[:, KV_W:]], axis=0)
    return kk, vv


def _stack_heads(x, hk):
    return jnp.concatenate([x[:, (GROUP * hk + g) * HEAD_DIM:(GROUP * hk + g + 1) * HEAD_DIM]
                            for g in range(GROUP)], axis=0)


def _sink_col(sink_ref, hk):
    return jnp.concatenate([jnp.full((AB, 1), sink_ref[GROUP * hk + g], F32) for g in range(GROUP)], axis=0)


def attn_fwd(q, kv, cos, sin, sink, lc, name, carry=()):
    bsz, t_len, _ = q.shape
    nb = t_len // AB
    kv_specs, tab_specs = _attn_specs(t_len, lc)
    scale = HEAD_DIM ** -0.5
    nc = len(carry)

    def body(sink_ref, q_ref, kvp_ref, kvc_ref, kvn_ref, kvx_ref, cp, cc, cn, sp, sc, sn, *rest):
        o_ref, lse_ref = rest[nc:nc + 2]
        b, i = pl.program_id(0), pl.program_id(1)
        if nc:
            start, wait = _xy_copies(rest[:nc], rest[nc + 2:2 * nc + 2], *rest[2 * nc + 2:], False)
            pl.when((b == 0) & (i == 0))(start)
        mask = _attn_mask(i, lc, t_len)
        qr = _rope(q_ref[...], jnp.tile(cc[...], (1, 4)), jnp.tile(sc[...], (1, 4)))
        kk, vv = _attn_keys(kvp_ref[...], kvc_ref[...], kvn_ref[...], kvx_ref[...],
                            cp[...], cc[...], cn[...], sp[...], sc[...], sn[...])
        outs, lses = [], []
        for hk in range(N_KV):
            kh = kk[:, hk * HEAD_DIM:(hk + 1) * HEAD_DIM].astype(BF16)
            vh = vv[:, hk * HEAD_DIM:(hk + 1) * HEAD_DIM].astype(BF16)
            q4 = _stack_heads(qr, hk).astype(BF16)
            s = jnp.where(mask, _dot_nt(q4, kh) * scale, NEG_INF)
            sk = _sink_col(sink_ref, hk)
            m = jnp.maximum(jnp.max(s, axis=-1, keepdims=True), sk)
            p = jnp.exp(s - m)
            l = jnp.sum(p, axis=-1, keepdims=True) + jnp.exp(sk - m)
            o = _dot(p.astype(BF16), vh) / l
            lse = m + jnp.log(l)
            for g in range(GROUP):
                outs.append(o[g * AB:(g + 1) * AB])
                lses.append(lse[g * AB:(g + 1) * AB])
        o_ref[...] = jnp.concatenate(outs, axis=1)
        lse_ref[...] = jnp.concatenate(lses, axis=1)
        if nc:
            pl.when((b == bsz - 1) & (i == nb - 1))(wait)

    return pl.pallas_call(
        body, name=name, grid=(bsz, nb),
        in_specs=[pl.BlockSpec(memory_space=pltpu.SMEM),
                  pl.BlockSpec((None, AB, ATTN_W), lambda b, i: (b, i, 0))] + kv_specs + tab_specs + tab_specs
        + [HBM_SPEC] * nc,
        out_specs=[pl.BlockSpec((None, AB, ATTN_W), lambda b, i: (b, i, 0)),
                   pl.BlockSpec((None, AB, N_HEADS), lambda b, i: (b, i, 0))] + [HBM_SPEC] * nc,
        out_shape=[jax.ShapeDtypeStruct((bsz, t_len, ATTN_W), F32), jax.ShapeDtypeStruct((bsz, t_len, N_HEADS), F32)]
        + _xy_out_shapes(carry, False),
        scratch_shapes=_xy_sems(nc) if nc else [],
        compiler_params=_cp(48, 2),
    )(sink, q, kv, kv, kv, kv, cos, cos, cos, sin, sin, sin, *carry)


def attn_bwd(q, kv, o, lse, do, cos, sin, sink, lc, name):
    bsz, t_len, _ = q.shape
    nb = t_len // AB
    kv_specs, tab_specs = _attn_specs(t_len, lc)
    scale = HEAD_DIM ** -0.5
    blk = lambda w: pl.BlockSpec((None, AB, w), lambda b, i: (b, i, 0))
    full_tab = pl.BlockSpec((t_len, 128), lambda b, i: (0, 0))

    def body(sink_ref, q_ref, kvp_ref, kvc_ref, kvn_ref, kvx_ref, cp, cc, cn, sp, sc, sn, cf, sf,
             o_ref, lse_ref, do_ref, dq_ref, dkv_ref, dsink_ref):
        b, i = pl.program_id(0), pl.program_id(1)

        @pl.when(i == 0)
        def _():
            dkv_ref[...] = jnp.zeros_like(dkv_ref)

        @pl.when((b == 0) & (i == 0))
        def _():
            dsink_ref[...] = jnp.zeros_like(dsink_ref)

        mask = _attn_mask(i, lc, t_len)
        cq, sq = jnp.tile(cc[...], (1, 4)), jnp.tile(sc[...], (1, 4))
        qr = _rope(q_ref[...], cq, sq)
        kk, vv = _attn_keys(kvp_ref[...], kvc_ref[...], kvn_ref[...], kvx_ref[...],
                            cp[...], cc[...], cn[...], sp[...], sc[...], sn[...])
        dov, ov, lsev = do_ref[...], o_ref[...], lse_ref[...]
        dqs, dks, dvs, dsk = [], [], [], []
        for hk in range(N_KV):
            kh = kk[:, hk * HEAD_DIM:(hk + 1) * HEAD_DIM].astype(BF16)
            vh = vv[:, hk * HEAD_DIM:(hk + 1) * HEAD_DIM].astype(BF16)
            q4 = _stack_heads(qr, hk).astype(BF16)
            do4 = _stack_heads(dov, hk)
            o4 = _stack_heads(ov, hk)
            lse4 = jnp.concatenate([lsev[:, GROUP * hk + g:GROUP * hk + g + 1] for g in range(GROUP)], axis=0)
            delta = jnp.sum(do4 * o4, axis=-1, keepdims=True)
            s = jnp.where(mask, _dot_nt(q4, kh) * scale, NEG_INF)
            p = jnp.exp(s - lse4)
            do4b = do4.astype(BF16)
            dp = _dot_nt(do4b, vh)
            ds = (p * (dp - delta) * scale).astype(BF16)
            dq4 = _dot(ds, kh)
            dks.append(_dot_tn(ds, q4))
            dvs.append(_dot_tn(p.astype(BF16), do4b))
            pd = jnp.exp(_sink_col(sink_ref, hk) - lse4) * delta
            for g in range(GROUP):
                dqs.append(dq4[g * AB:(g + 1) * AB])
                dsk.append(-jnp.sum(pd[g * AB:(g + 1) * AB], axis=0, keepdims=True))
        dq_ref[...] = _rope_t(jnp.concatenate(dqs, axis=1), cq, sq)
        dsink_ref[0:1, :] += jnp.concatenate(dsk, axis=1)
        dkv = jnp.concatenate(dks + dvs, axis=1)
        starts = (jnp.maximum(i - 1, 0), i, jnp.minimum(i + 1, nb - 1))
        for j, st in enumerate(starts):
            rows = pl.ds(pl.multiple_of(st * AB, AB), AB)
            dkv_ref[rows, :] += dkv[j * AB:(j + 1) * AB]
        dkv_ref[0:lc, :] += dkv[3 * AB:]

        @pl.when(i == nb - 1)
        def _():
            def unrotate(j, carry):
                rows = pl.ds(pl.multiple_of(j * AB, AB), AB)
                dkv_ref[rows, 0:KV_W] = _rope_t(dkv_ref[rows, 0:KV_W], cf[rows, :], sf[rows, :])
                return carry
            lax.fori_loop(0, nb, unrotate, 0)

    return pl.pallas_call(
        body, name=name, grid=(bsz, nb),
        in_specs=[pl.BlockSpec(memory_space=pltpu.SMEM), blk(ATTN_W)] + kv_specs + tab_specs + tab_specs
        + [full_tab, full_tab, blk(ATTN_W), blk(N_HEADS), blk(ATTN_W)],
        out_specs=[blk(ATTN_W), pl.BlockSpec((None, t_len, 2 * KV_W), lambda b, i: (b, 0, 0)),
                   pl.BlockSpec((8, N_HEADS), lambda b, i: (0, 0))],
        out_shape=[jax.ShapeDtypeStruct((bsz, t_len, ATTN_W), F32), jax.ShapeDtypeStruct((bsz, t_len, 2 * KV_W), F32),
                   jax.ShapeDtypeStruct((8, N_HEADS), F32)],
        compiler_params=_cp(56, 2),
    )(sink, q, kv, kv, kv, kv, cos, cos, cos, sin, sin, sin, cos, sin, o, lse, do)


def _s5_mats_dir(a_re, a_im, log_dt, b_re, b_im, c_re, c_im, flip):
    hp = lax.Precision.HIGHEST
    lam = lax.complex(a_re, a_im)
    ldt = lam * jnp.exp(log_dt)[:, None]
    a_bar = jnp.exp(ldt)
    b_bar = ((a_bar - 1.0) / lam)[..., None] * lax.complex(b_re, b_im)
    cm = lax.complex(c_re, c_im)
    tt = np.arange(Q)
    powers = lambda e: jnp.exp(ldt[..., None] * jnp.asarray(e, F32))
    k = jnp.real(jnp.einsum('gcp,gpt,gpk->gktc', cm, powers(Q - 1 - tt if flip else tt), b_bar, precision=hp))
    k = k.reshape(G, C, QC)
    slabs = []
    for t1 in range(Q):
        if flip:
            sh = (Q - 1 - t1) * C
            slabs.append(jnp.pad(k, ((0, 0), (0, 0), (0, sh)))[..., sh:])
        else:
            slabs.append(jnp.pad(k, ((0, 0), (0, 0), (t1 * C, 0)))[..., :QC])
    kt = jnp.stack(slabs, axis=1).reshape(G, QC, QC)
    ws = powers(tt if flip else Q - 1 - tt)[:, :, :, None] * b_bar[:, :, None, :]
    ws = ws.transpose(0, 2, 3, 1)
    wo = cm[:, :, :, None] * powers(Q - tt if flip else tt + 1)[:, None, :, :]
    wo = wo.transpose(0, 2, 3, 1)
    ws = jnp.concatenate([jnp.real(ws), jnp.imag(ws)], axis=-1).reshape(G, QC, P2)
    wo = jnp.concatenate([jnp.real(wo), -jnp.imag(wo)], axis=1).reshape(G, P2, QC)
    a1, a2 = _pair_forms(powers([Q]))
    return kt, ws, wo, a1, a2


def _pair_forms(z):
    re, im = jnp.real(z), jnp.imag(z)
    k = z.shape[-1]
    a1 = jnp.concatenate([re, re], axis=1).transpose(2, 0, 1).reshape(k, G * P2)
    a2 = jnp.concatenate([-im, im], axis=1).transpose(2, 0, 1).reshape(k, G * P2)
    return a1, a2


def s5_mats(a_re, a_im, log_dt, b_re, b_im, c_re, c_im):
    per_dir = [_s5_mats_dir(a_re[d], a_im[d], log_dt[d], b_re[d], b_im[d], c_re[d], c_im[d], d == 1)
               for d in range(2)]
    return tuple(jnp.stack([m[i] for m in per_dir]) for i in range(5))


GH = G // 8
RT = 16 * Q


def _perm_consts():
    r = np.arange(RT)
    rows = np.zeros((RT, RT), np.float32)
    rows[(r % Q) * 16 + r // Q, r] = 1.0
    q = np.arange(8 * 128)
    lanes = np.zeros((8 * 128, 8 * 128), np.float32)
    lanes[q, ((q % 128) // C) * 128 + (q // 128) * C + q % C] = 1.0
    return jnp.asarray(rows, BF16), jnp.asarray(lanes, BF16)


def to_groups(x, name):
    bsz, t_len, _ = x.shape
    nc = t_len // Q
    rows, lanes = _perm_consts()

    def body(x_ref, r_ref, p_ref, o_ref, w_ref):
        for j in range(t_len // RT):
            pt = _dot(r_ref[...], x_ref[j * RT:(j + 1) * RT, :].astype(BF16)).astype(BF16)
            for t in range(Q):
                w_ref[j * 16:(j + 1) * 16, t * SSM_W:(t + 1) * SSM_W] = pt[t * 16:(t + 1) * 16, :]
        for gh in range(GH):
            for th in range(2):
                inp = jnp.concatenate([w_ref[:, (th * 8 + tl) * SSM_W + gh * 128:(th * 8 + tl) * SSM_W + (gh + 1) * 128]
                                       for tl in range(8)], axis=1)
                out = _dot(inp, p_ref[...]).astype(BF16)
                for gl in range(8):
                    o_ref[gh * 8 + gl, :, th * 128:(th + 1) * 128] = out[:, gl * 128:(gl + 1) * 128]

    return pl.pallas_call(
        body, name=name, grid=(bsz,),
        in_specs=[pl.BlockSpec((None, t_len, SSM_W), lambda b: (b, 0, 0)), pl.BlockSpec((RT, RT), lambda b: (0, 0)),
                  pl.BlockSpec((1024, 1024), lambda b: (0, 0))],
        out_specs=pl.BlockSpec((None, G, nc, QC), lambda b: (b, 0, 0, 0)),
        out_shape=jax.ShapeDtypeStruct((bsz, G, nc, QC), BF16),
        scratch_shapes=[pltpu.VMEM((nc, Q * SSM_W), BF16)],
        compiler_params=_cp(56, 1),
    )(x, rows, lanes)


def from_groups(xg, name):
    bsz, _, nc, _ = xg.shape
    t_len = nc * Q
    rows, lanes = _perm_consts()

    def body(x_ref, r_ref, p_ref, o_ref, whi_ref, wlo_ref):
        gh = pl.program_id(1)
        for th in range(2):
            inp = jnp.concatenate([x_ref[gl, :, th * 128:(th + 1) * 128] for gl in range(8)], axis=1)
            hi = inp.astype(BF16)
            lo = (inp - hi.astype(F32)).astype(BF16)
            whi_ref[gh, :, th * 1024:(th + 1) * 1024] = _dot(hi, p_ref[...]).astype(BF16)
            wlo_ref[gh, :, th * 1024:(th + 1) * 1024] = _dot(lo, p_ref[...]).astype(BF16)

        @pl.when(gh == GH - 1)
        def _():
            for j in range(t_len // RT):
                def tile(w_ref):
                    return jnp.concatenate(
                        [jnp.concatenate([w_ref[k, j * 16:(j + 1) * 16, t * 128:(t + 1) * 128] for k in range(GH)],
                                         axis=1) for t in range(Q)], axis=0)
                o_ref[j * RT:(j + 1) * RT, :] = _dot(r_ref[...], tile(whi_ref)) + _dot(r_ref[...], tile(wlo_ref))

    return pl.pallas_call(
        body, name=name, grid=(bsz, GH),
        in_specs=[pl.BlockSpec((None, 8, nc, QC), lambda b, k: (b, k, 0, 0)),
                  pl.BlockSpec((RT, RT), lambda b, k: (0, 0)), pl.BlockSpec((1024, 1024), lambda b, k: (0, 0))],
        out_specs=pl.BlockSpec((None, t_len, SSM_W), lambda b, k: (b, 0, 0)),
        out_shape=jax.ShapeDtypeStruct((bsz, t_len, SSM_W), F32),
        scratch_shapes=[pltpu.VMEM((GH, nc, Q * 128), BF16), pltpu.VMEM((GH, nc, Q * 128), BF16)],
        compiler_params=_cp(56, 2),
    )(xg, rows, lanes)


def _gb(shape):
    return pl.BlockSpec((None, None) + shape, lambda g, b: (b, g, 0, 0))


def _gw(shape):
    return pl.BlockSpec((2, None) + shape, lambda g, b: (0, g, 0, 0))


def _gs(nc):
    return pl.BlockSpec((2, None, nc, P2), lambda g, b: (0, b, 0, g))


def s5_chunk_fwd(ug, kt, ws, name):
    bsz, _, nc, _ = ug.shape

    def body(u_ref, kt_ref, ws_ref, y_ref, s_ref):
        ub = u_ref[...]
        y_ref[...] = _dot(ub, kt_ref[0]) + _dot(ub, kt_ref[1])
        s_ref[0] = _dot(ub, ws_ref[0])
        s_ref[1] = _dot(ub, ws_ref[1])

    return pl.pallas_call(
        body, name=name, grid=(G, bsz),
        in_specs=[_gb((nc, QC)), _gw((QC, QC)), _gw((QC, P2))],
        out_specs=[_gb((nc, QC)), _gs(nc)],
        out_shape=[jax.ShapeDtypeStruct((bsz, G, nc, QC), F32), jax.ShapeDtypeStruct((2, bsz, nc, G * P2), F32)],
        compiler_params=_cp(32, 2),
    )(ug, kt, ws)


def s5_scan(s, a1, a2, ncc, reverse, name, hp=None):
    _, bsz, nc, gw = s.shape
    as_rows = lambda v: v.reshape(v.shape[:-1] + (G, P2))
    st = pl.BlockSpec((2, None, nc, SCAN_G, P2), lambda b, w: (0, b, 0, w, 0))
    av = pl.BlockSpec((2, SCAN_G, P2), lambda b, w: (0, w, 0))
    acc = pl.BlockSpec((2, None, SCAN_G, P2), lambda b, w: (0, b, w, 0))
    with_da = hp is not None

    def body(*refs):
        if with_da:
            s_ref, a1_ref, a2_ref, hp_ref, out_ref, da1_ref, da2_ref = refs
        else:
            s_ref, a1_ref, a2_ref, out_ref = refs
        a1v = (a1_ref[0], a1_ref[1])
        a2v = (a2_ref[0], a2_ref[1])
        swap = lambda h: pltpu.roll(h, P, 1)

        def step(j, carry):
            i = nc - 1 - j if reverse else j
            order = (i, jnp.where(i < ncc, ncc - 1 - i, nc - 1 - (i - ncc)))
            hs, da1, da2 = carry
            nh, n1, n2 = [], [], []
            for d, n in enumerate(order):
                h = hs[d]
                out_ref[d, n] = h
                nh.append(a1v[d] * h + a2v[d] * swap(h) + s_ref[d, n])
                if with_da:
                    hv = hp_ref[d, n]
                    n1.append(da1[d] + h * hv)
                    n2.append(da2[d] + h * swap(hv))
            return tuple(nh), tuple(n1), tuple(n2)

        z = jnp.zeros((SCAN_G, P2), F32)
        zz = (z, z) if with_da else ()
        _, da1, da2 = lax.fori_loop(0, nc, step, ((z, z), zz, zz))
        if with_da:
            for d in range(2):
                da1_ref[d] = da1[d]
                da2_ref[d] = da2[d]

    out_shape = [jax.ShapeDtypeStruct((2, bsz, nc, G, P2), F32)]
    out_specs = [st]
    ins = [as_rows(s), as_rows(a1[:, 0]), as_rows(a2[:, 0])]
    in_specs = [st, av, av]
    if with_da:
        ins.append(as_rows(hp))
        in_specs.append(st)
        out_shape += [jax.ShapeDtypeStruct((2, bsz, G, P2), F32)] * 2
        out_specs += [acc, acc]
    res = pl.pallas_call(
        body, name=name, grid=(bsz, G // SCAN_G), in_specs=in_specs, out_specs=out_specs, out_shape=out_shape,
        compiler_params=_cp(48, 2),
    )(*ins)
    out = res[0].reshape(s.shape)
    return (out, res[1].reshape(2, bsz, gw), res[2].reshape(2, bsz, gw)) if with_da else out


def s5_out_fwd(y1, hp, wo, name):
    bsz, _, nc, _ = y1.shape

    def body(y1_ref, hp_ref, wo_ref, y_ref):
        y_ref[...] = (y1_ref[...] + _dot(hp_ref[0].astype(BF16), wo_ref[0])
                      + _dot(hp_ref[1].astype(BF16), wo_ref[1]))

    return pl.pallas_call(
        body, name=name, grid=(G, bsz),
        in_specs=[_gb((nc, QC)), _gs(nc), _gw((P2, QC))],
        out_specs=_gb((nc, QC)),
        out_shape=jax.ShapeDtypeStruct(y1.shape, F32),
        compiler_params=_cp(32, 2),
    )(y1, hp, wo)


def _acc_init(b, *refs):
    @pl.when(b == 0)
    def _():
        for r in refs:
            r[...] = jnp.zeros_like(r)


def s5_out_bwd(dyg, ug, hp, wo, name):
    bsz, _, nc, _ = dyg.shape

    def body(dy_ref, u_ref, hp_ref, wo_ref, dhp_ref, dwo_ref, dkt_ref):
        _acc_init(pl.program_id(1), dwo_ref, dkt_ref)
        dyb = dy_ref[...]
        for d in range(2):
            dhp_ref[d] = _dot_nt(dyb, wo_ref[d])
            dwo_ref[d] += _dot_tn(hp_ref[d].astype(BF16), dyb)
        dkt_ref[...] += _dot_tn(u_ref[...], dyb)

    return pl.pallas_call(
        body, name=name, grid=(G, bsz),
        in_specs=[_gb((nc, QC)), _gb((nc, QC)), _gs(nc), _gw((P2, QC))],
        out_specs=[_gs(nc), _gw((P2, QC)), pl.BlockSpec((None, QC, QC), lambda g, b: (g, 0, 0))],
        out_shape=[jax.ShapeDtypeStruct(hp.shape, F32), jax.ShapeDtypeStruct((2, G, P2, QC), F32),
                   jax.ShapeDtypeStruct((G, QC, QC), F32)],
        compiler_params=_cp(32, 2),
    )(dyg, ug, hp, wo)


def s5_chunk_bwd(dyg, ug, ds, kt, ws, name):
    bsz, _, nc, _ = dyg.shape

    def body(dy_ref, u_ref, ds_ref, kt_ref, ws_ref, du_ref, dws_ref):
        _acc_init(pl.program_id(1), dws_ref)
        dyb = dy_ref[...]
        du = _dot_nt(dyb, kt_ref[0]) + _dot_nt(dyb, kt_ref[1])
        for d in range(2):
            dsb = ds_ref[d].astype(BF16)
            du += _dot_nt(dsb, ws_ref[d])
            dws_ref[d] += _dot_tn(u_ref[...], dsb)
        du_ref[...] = du

    return pl.pallas_call(
        body, name=name, grid=(G, bsz),
        in_specs=[_gb((nc, QC)), _gb((nc, QC)), _gs(nc), _gw((QC, QC)), _gw((QC, P2))],
        out_specs=[_gb((nc, QC)), _gw((QC, P2))],
        out_shape=[jax.ShapeDtypeStruct(dyg.shape, F32), jax.ShapeDtypeStruct((2, G, QC, P2), F32)],
        compiler_params=_cp(32, 2),
    )(dyg, ug, ds, kt, ws)


def local_step(x, ctx, target, mods, norm_g, final_g, even, odd, late=None):
    bsz, seq, _ = x.shape
    lc = ctx.shape[1]
    t_len = lc + seq
    ncc = lc // Q
    cos, sin = rope_tables(lc, seq)
    h = jnp.concatenate([ctx, x], axis=1)
    ssm_stacked = [jnp.stack([even[0]["ssm"][k], even[1]["ssm"][k]]) for k in range(7)]
    mats_all, mats_vjp = jax.vjp(jax.vmap(s5_mats), *ssm_stacked)
    d_mats = [None, None]
    saved = []
    for i in range(DEPTH):
        j = i // 2
        g = norm_g[i].reshape(1, D)
        if i % 2 == 0:
            w = even[j]
            a, q, kv, g_attn, u, g_ssm = norm_in(h, g, mods[i], w["w_in"], EVEN_SPLITS, f"even_in{j}")
            if i == 0 and late is not None:
                o_attn, lse, *gathered = attn_fwd(q, kv, cos, sin, w["sink"], lc, f"attn_fwd{j}", carry=late[0])
                late[1](gathered)
            else:
                o_attn, lse = attn_fwd(q, kv, cos, sin, w["sink"], lc, f"attn_fwd{j}")
            kt, ws, wo, a1, a2 = (m[j] for m in mats_all)
            kt, ws, wo = kt.astype(BF16), ws.astype(BF16), wo.astype(BF16)
            d_skip = w["ssm"][7].reshape(1, SSM_W)
            ug = to_groups(u, f"u_to_groups{j}")
            y1, s = s5_chunk_fwd(ug, kt, ws, f"s5_chunk_fwd{j}")
            hp = s5_scan(s, a1, a2, ncc, False, f"s5_scan_fwd{j}")
            y_scan = from_groups(s5_out_fwd(y1, hp, wo, f"s5_out_fwd{j}"), f"y_from_groups{j}")
            h_new, mix, yout = even_out(h, mods[i], o_attn, g_attn, y_scan, u, d_skip, g_ssm, w["glu_w"], w["glu_b"],
                                        w["w_out"], f"even_out{j}")
            saved.append(dict(h=h, a=a, q=q, kv=kv, g_attn=g_attn, g_ssm=g_ssm, o_attn=o_attn, lse=lse, ug=ug, u=u,
                              hp=hp, y_scan=y_scan, mix=mix, yout=yout, mats=(kt, ws, wo, a1, a2), d_skip=d_skip))
        else:
            w = odd[j]
            a, u, gate = norm_in(h, g, mods[i], w["w_in"], ODD_SPLITS, f"odd_in{j}")
            pm = pool_band(u, lc, False, f"pool_band_fwd{j}")
            h_new, mix, yout = pool_out(h, mods[i], pm, gate, w["pool_w"], w["pool_scale"], w["w_out"], f"pool_out{j}")
            saved.append(dict(h=h, a=a, pm=pm, gate=gate, mix=mix, yout=yout))
        h = h_new

    dh, loss_acc, dfg = loss_head(h, final_g.reshape(1, D), target, "loss_head")
    grads = dict(final_g=dfg[0], norm_g=[None] * DEPTH, even=[None, None], odd=[None, None])
    dmods = [None] * DEPTH
    rows = bsz * t_len
    flat = lambda v: v.reshape(rows, v.shape[-1])
    for i in reversed(range(DEPTH)):
        j = i // 2
        sv = saved[i]
        g = norm_g[i].reshape(1, D)
        if i % 2 == 0:
            w = even[j]
            kt, ws, wo, a1, a2 = sv["mats"]
            (d_oattn, d_gattn, d_gssm, d_yssm, dyout, zz, dsg, dgate, dglu_b) = even_out_bwd(
                dh, mods[i], sv["o_attn"], sv["g_attn"], sv["y_scan"], sv["u"], sv["d_skip"], sv["g_ssm"], w["glu_w"],
                w["glu_b"], w["w_out"], sv["yout"], f"even_out_bwd{j}")
            g_w_out = matmul_tn(flat(sv["mix"]), flat(dyout), D, D, f"even_w_out_grad{j}")
            g_glu_w = matmul_tn(flat(zz), flat(dsg), SSM_W, SSM_W, f"glu_w_grad{j}")
            dq, dkv, dsink = attn_bwd(sv["q"], sv["kv"], sv["o_attn"], sv["lse"], d_oattn, cos, sin, w["sink"], lc,
                                      f"attn_bwd{j}")
            dyg = to_groups(d_yssm, f"dy_to_groups{j}")
            dhp, dwo, dkt = s5_out_bwd(dyg, sv["ug"], sv["hp"], wo, f"s5_out_bwd{j}")
            ds, da1, da2 = s5_scan(dhp, a1, -a2, ncc, True, f"s5_scan_bwd{j}", hp=sv["hp"])
            dug, dws = s5_chunk_bwd(dyg, sv["ug"], ds, kt, ws, f"s5_chunk_bwd{j}")
            dkt2 = jnp.stack([dkt, dkt])
            da1 = da1.sum(axis=1).reshape(2, 1, G * P2)
            da2 = da2.sum(axis=1).reshape(2, 1, G * P2)
            d_mats[j] = (dkt2, dws, dwo, da1, da2)
            dparts = [dq, dkv, d_gattn, from_groups(dug, f"du_from_groups{j}"), d_gssm]
            dh, dz, dmod, dg = norm_in_bwd(dparts, dh, sv["h"], g, mods[i], w["w_in"], f"even_in_bwd{j}",
                                           skip=(3, d_yssm, sv["d_skip"]))
            g_w_in = matmul_tn(flat(sv["a"]), flat(dz), D, dz.shape[-1], f"even_w_in_grad{j}")
            grads["even"][j] = dict(w_in=g_w_in, w_out=g_w_out, sink=dsink[0], d_skip=dglu_b[1], glu_w=g_glu_w,
                                    glu_b=dglu_b[0])
        else:
            w = odd[j]
            dpm, dgt, dyout, dpp, dgate, dps = pool_out_bwd(dh, mods[i], sv["pm"], sv["gate"], w["pool_w"],
                                                            w["pool_scale"], w["w_out"], sv["yout"],
                                                            f"pool_out_bwd{j}")
            g_w_out = matmul_tn(flat(sv["mix"]), flat(dyout), D, D, f"odd_w_out_grad{j}")
            g_pool_w = jnp.stack([matmul_tn(flat(sv["pm"]), flat(dpp), POOL_G, POOL_G, f"pool_w_grad{j}_{gi}",
                                            a_col=gi, b_col=gi) for gi in range(4)])
            du = pool_band(dpm, lc, True, f"pool_band_bwd{j}")
            dh, dz, dmod, dg = norm_in_bwd([du, dgt], dh, sv["h"], g, mods[i], w["w_in"], f"odd_in_bwd{j}")
            g_w_in = matmul_tn(flat(sv["a"]), flat(dz), D, dz.shape[-1], f"odd_w_in_grad{j}")
            grads["odd"][j] = dict(w_in=g_w_in, w_out=g_w_out, pool_w=g_pool_w, pool_scale=dps[0])
        grads["norm_g"][i] = dg[0]
        dmods[i] = jnp.concatenate([dmod[:, :, 0:2, :], dgate[:, :, 0:1, :]], axis=2)
    g_ssm = mats_vjp(tuple(jnp.stack([d_mats[0][k], d_mats[1][k]]) for k in range(5)))
    for j in range(2):
        grads["even"][j]["ssm"] = tuple(gk[j] for gk in g_ssm) + (grads["even"][j].pop("d_skip"),)
    return loss_acc[0, 0], dh[:, lc:, :], dmods, grads


N_DEV = 8
HBM_SPEC = pl.BlockSpec(memory_space=pltpu.HBM)


def allgather8(x_shard, name):
    m_per, n = x_shard.shape

    def body(x_ref, out_ref, send_sems, recv_sems, local_sem):
        x, y, c = lax.axis_index("x"), lax.axis_index("y"), lax.axis_index("c")
        me, sibling = (x, y, c), (x, y, 1 - c)
        chips = [(1 - x, y), (x, 1 - y), (1 - x, 1 - y)]

        def rows(px, py, pc):
            return out_ref.at[pl.ds((4 * px + 2 * py + pc) * m_per, m_per), :]

        def copy(k, block, to, src=None):
            return pltpu.make_async_remote_copy(
                src_ref=rows(*block) if src is None else src, dst_ref=rows(*block),
                send_sem=send_sems.at[k], recv_sem=recv_sems.at[k], device_id=to, device_id_type=MESH)

        mine = pltpu.make_async_copy(x_ref, rows(*me), local_sem)
        mine.start()
        first = [copy(0, me, sibling, src=x_ref)]
        first += [copy(1 + j, me, (*chip, c), src=x_ref) for j, chip in enumerate(chips)]
        for cp in first:
            cp.start()
        passed = [copy(4 + j, (*chip, c), sibling) for j, chip in enumerate(chips)]
        for j, chip in enumerate(chips):
            copy(1 + j, (*chip, c), me).wait_recv()
            passed[j].start()
        copy(0, sibling, me).wait_recv()
        for j, chip in enumerate(chips):
            copy(4 + j, (*chip, 1 - c), me).wait_recv()
        for cp in first + passed:
            cp.wait_send()
        mine.wait()

    return pl.pallas_call(
        body, name=name,
        out_shape=jax.ShapeDtypeStruct((N_DEV * m_per, n), x_shard.dtype),
        in_specs=[pl.BlockSpec(memory_space=pltpu.VMEM)],
        out_specs=pl.BlockSpec(memory_space=pltpu.VMEM),
        scratch_shapes=[pltpu.SemaphoreType.DMA((7,)), pltpu.SemaphoreType.DMA((7,)), pltpu.SemaphoreType.DMA],
        compiler_params=_cp(56),
    )(x_shard)


def xy_exchange(srcs, scatter, name):
    n = len(srcs)

    def body(*refs):
        start, wait = _xy_copies(refs[:n], refs[n:2 * n], *refs[2 * n:], scatter)
        start()
        wait()

    return pl.pallas_call(
        body, name=name, out_shape=_xy_out_shapes(srcs, scatter),
        in_specs=[HBM_SPEC] * n, out_specs=[HBM_SPEC] * n, scratch_shapes=_xy_sems(n),
    )(*srcs)


def _xy_out_shapes(srcs, scatter):
    return [jax.ShapeDtypeStruct((4,) + (tuple(s.shape[1:]) if scatter else tuple(s.shape)), s.dtype) for s in srcs]


def _xy_sems(n):
    return [pltpu.SemaphoreType.DMA((3 * n,)), pltpu.SemaphoreType.DMA((3 * n,)), pltpu.SemaphoreType.DMA((n,))]


def _xy_copies(src_refs, out_refs, send_sems, recv_sems, local_sems, scatter):
    n = len(src_refs)

    def parts():
        x, y, c = lax.axis_index("x"), lax.axis_index("y"), lax.axis_index("c")
        my = 2 * x + y
        peers = [(1 - x, y), (x, 1 - y), (1 - x, 1 - y)]

        def piece(i, pos):
            return src_refs[i].at[pos] if scatter else src_refs[i]

        def copy(i, k, src_pos, dst_pos):
            px, py = peers[k]
            return pltpu.make_async_remote_copy(
                src_ref=piece(i, src_pos), dst_ref=out_refs[i].at[dst_pos], send_sem=send_sems.at[3 * i + k],
                recv_sem=recv_sems.at[3 * i + k], device_id=(px, py, c), device_id_type=MESH)

        local = [pltpu.make_async_copy(piece(i, my), out_refs[i].at[my], local_sems.at[i]) for i in range(n)]
        sends = [copy(i, k, 2 * px + py, my) for i in range(n) for k, (px, py) in enumerate(peers)]
        lands = [copy(i, k, my, 2 * px + py) for i in range(n) for k, (px, py) in enumerate(peers)]
        return local, sends, lands

    def start():
        local, sends, _ = parts()
        for cp in local + sends:
            cp.start()

    def wait():
        local, sends, lands = parts()
        for cp in lands:
            cp.wait_recv()
        for cp in sends:
            cp.wait_send()
        for cp in local:
            cp.wait()

    return start, wait


def sibling_exchange(srcs, name):
    n = len(srcs)

    def body(*refs):
        src_refs, out_refs = refs[:n], refs[n:2 * n]
        send_sems, recv_sems = refs[2 * n:]
        peer = (lax.axis_index("x"), lax.axis_index("y"), 1 - lax.axis_index("c"))
        cps = [pltpu.make_async_remote_copy(src_ref=src_refs[i], dst_ref=out_refs[i], send_sem=send_sems.at[i],
                                            recv_sem=recv_sems.at[i], device_id=peer, device_id_type=MESH)
               for i in range(n)]
        for cp in cps:
            cp.start()
        for cp in cps:
            cp.wait()

    return pl.pallas_call(
        body, name=name, out_shape=[jax.ShapeDtypeStruct(s.shape, s.dtype) for s in srcs],
        in_specs=[HBM_SPEC] * n, out_specs=[HBM_SPEC] * n,
        scratch_shapes=[pltpu.SemaphoreType.DMA((n,)), pltpu.SemaphoreType.DMA((n,))],
    )(*srcs)


def _row_tile(rows, bytes_per_row, limit):
    best = None
    for tr in range(8, rows + 1, 8):
        if rows % tr == 0 and tr * bytes_per_row <= limit:
            best = tr
    return best if best is not None else rows


def sum_slots(x, name):
    n, rows, cols = x.shape
    tr = _row_tile(rows, n * cols * 4, 4 * MB)

    def body(x_ref, o_ref):
        acc = x_ref[0].astype(F32)
        for k in range(1, n):
            acc = acc + x_ref[k].astype(F32)
        o_ref[...] = acc

    return pl.pallas_call(
        body, name=name, grid=(rows // tr,),
        in_specs=[pl.BlockSpec((n, tr, cols), lambda r: (0, r, 0))],
        out_specs=pl.BlockSpec((tr, cols), lambda r: (r, 0)),
        out_shape=jax.ShapeDtypeStruct((rows, cols), F32),
        compiler_params=_cp(32, 1),
    )(x)


ADA_COLS = 3 * D // 4
C_ROWS = 8


def ada_fwd(c_all, ada_w, ada_b_cols, name):
    nrow = c_all.shape[0]

    def body(c_ref, w_ref, b_ref, o_ref):
        s, _ = _silu_and_grad(c_ref[...])
        o_ref[...] = _dot(s.astype(BF16), w_ref[...].astype(BF16)) + b_ref[...]

    return pl.pallas_call(
        body, name=name, grid=(DEPTH,),
        in_specs=[pl.BlockSpec((nrow, D), lambda i: (0, 0)), pl.BlockSpec((None, D, ADA_COLS), lambda i: (i, 0, 0)),
                  pl.BlockSpec((None, 1, ADA_COLS), lambda i: (i, 0, 0))],
        out_specs=pl.BlockSpec((None, nrow, ADA_COLS), lambda i: (i, 0, 0)),
        out_shape=jax.ShapeDtypeStruct((DEPTH, nrow, ADA_COLS), F32),
        compiler_params=_cp(32, 1),
    )(c_all, ada_w, ada_b_cols)


def ada_bwd(c_all, d_cols, ada_w, name):
    nrow = c_all.shape[0]

    def body(c_ref, d_ref, w_ref, gw_ref, ds_ref):
        @pl.when(pl.program_id(0) == 0)
        def _():
            ds_ref[...] = jnp.zeros_like(ds_ref)
        s, _ = _silu_and_grad(c_ref[...])
        dl = d_ref[...]
        gw_ref[...] = _dot_tn(s.astype(BF16), dl.astype(BF16))
        rid = lax.broadcasted_iota(jnp.int32, (nrow, 1), 0) % C_ROWS
        dctx = jnp.where((rid == 2) | (rid == 3), dl, 0.0).astype(BF16)
        ds_ref[0:1, :] += _rowsum(_dot_nt(dctx, w_ref[...].astype(BF16)))

    return pl.pallas_call(
        body, name=name, grid=(DEPTH,),
        in_specs=[pl.BlockSpec((nrow, D), lambda i: (0, 0)), pl.BlockSpec((None, nrow, ADA_COLS), lambda i: (i, 0, 0)),
                  pl.BlockSpec((None, D, ADA_COLS), lambda i: (i, 0, 0))],
        out_specs=[pl.BlockSpec((None, D, ADA_COLS), lambda i: (i, 0, 0)), pl.BlockSpec((8, D), lambda i: (0, 0))],
        out_shape=[jax.ShapeDtypeStruct((DEPTH, D, ADA_COLS), F32), jax.ShapeDtypeStruct((8, D), F32)],
        compiler_params=_cp(32, 1),
    )(c_all, d_cols, ada_w)


def ada_bias_grad(d_all, name):
    nrow = d_all.shape[1]

    def body(d_ref, o_ref):
        o_ref[...] = jnp.broadcast_to(_rowsum(d_ref[...]), o_ref.shape)

    return pl.pallas_call(
        body, name=name, grid=(DEPTH,),
        in_specs=[pl.BlockSpec((None, nrow, 3 * D), lambda i: (i, 0, 0))],
        out_specs=pl.BlockSpec((None, 8, 3 * D), lambda i: (i, 0, 0)),
        out_shape=jax.ShapeDtypeStruct((DEPTH, 8, 3 * D), F32),
        compiler_params=_cp(32, 1),
    )(d_all)


def silu_chain(ds, c, name):
    def body(ds_ref, c_ref, o_ref):
        _, dsl = _silu_and_grad(c_ref[...])
        o_ref[...] = ds_ref[...] * dsl

    return pl.pallas_call(body, name=name, out_shape=jax.ShapeDtypeStruct(ds.shape, F32))(ds, c)


def _flat_cols(shape):
    size = int(np.prod(shape))
    if shape[-1] >= 128:
        return shape[-1]
    for cols in (1024, 128):
        if size % cols == 0:
            return cols
    return shape[-1]


def adamw(w, m, v, grads, name):
    shape = w.shape
    cols = _flat_cols(shape)
    as2d = lambda a: a.reshape(-1, cols)
    rows = w.size // cols
    tr = _row_tile(rows, cols * 4, MB)
    k = len(grads)

    def body(*refs):
        w_ref, m_ref, v_ref = refs[:3]
        g_refs = refs[3:3 + k]
        g_out, d_out, m_out, v_out = refs[3 + k:]
        g = g_refs[0][...]
        for r in g_refs[1:]:
            g = g + r[...]
        g_out[...] = g
        mn = ADAM_B1 * m_ref[...] + (1.0 - ADAM_B1) * g
        vn = ADAM_B2 * v_ref[...] + (1.0 - ADAM_B2) * (g * g)
        m_out[...] = mn
        v_out[...] = vn
        m_hat = mn / (1.0 - ADAM_B1 ** ADAM_STEP)
        v_hat = vn / (1.0 - ADAM_B2 ** ADAM_STEP)
        d_out[...] = -ADAM_LR * (m_hat / (jnp.sqrt(v_hat) + ADAM_EPS) + ADAM_WD * w_ref[...])

    spec = pl.BlockSpec((tr, cols), lambda r: (r, 0))
    outs = pl.pallas_call(
        body, name=name, grid=(rows // tr,),
        in_specs=[spec] * (3 + k), out_specs=[spec] * 4,
        out_shape=[jax.ShapeDtypeStruct((rows, cols), F32)] * 4,
        compiler_params=_cp(32, 1),
    )(as2d(w), as2d(m), as2d(v), *[as2d(g) for g in grads])
    return tuple(o.reshape(shape) for o in outs)


BIG = (("even_w_in", (2, D, 576), 2), ("even_w_out", (2, 256, D), 1), ("glu_w", (2, 128, SSM_W), 1),
       ("odd_w_in", (2, D, 512), 2), ("odd_w_out", (2, 256, D), 1), ("pool_w", (2, 4, 64, POOL_G), 2))


def _full_shape(shard, axis):
    return tuple(4 * s if a == axis else s for a, s in enumerate(shard))


def _to_shards(full, shard, axis):
    return jnp.moveaxis(full.reshape(shard[:axis] + (4,) + shard[axis:]), axis, 0)


def _from_shards(stacked, shard, axis):
    return jnp.moveaxis(stacked, 0, axis).reshape(_full_shape(shard, axis))


SMALL = (("ds_ctx", (D,)), ("norm_g", (DEPTH, D)), ("final_g", (D,)), ("attn_sink", (2, N_HEADS)),
         ("ssm_a_re", (2, 2, G, P)), ("ssm_a_im", (2, 2, G, P)), ("ssm_log_dt", (2, 2, G)),
         ("ssm_b_re", (2, 2, G, P, C)), ("ssm_b_im", (2, 2, G, P, C)), ("ssm_c_re", (2, 2, G, C, P)),
         ("ssm_c_im", (2, 2, G, C, P)), ("ssm_d", (2, SSM_W)), ("glu_b", (2, SSM_W)), ("pool_scale", (2, D)))
SMALL_PAD = 8 * 128


def pack_small(vals):
    flat = jnp.concatenate([vals[n].reshape(-1) for n, _ in SMALL])
    pad = (-flat.shape[0]) % SMALL_PAD
    return jnp.pad(flat, (0, pad)).reshape(-1, 128)


def unpack_small(packed):
    flat, out, off = packed.reshape(-1), {}, 0
    for n, shape in SMALL:
        size = int(np.prod(shape))
        out[n] = flat[off:off + size].reshape(shape)
        off += size
    return out


WEIGHT_NAMES = ('c_ctx', 'ada_w', 'ada_b', 'norm_g', 'even_w_in', 'even_w_out', 'attn_sink', 'ssm_a_re', 'ssm_a_im',
                'ssm_log_dt', 'ssm_b_re', 'ssm_b_im', 'ssm_c_re', 'ssm_c_im', 'ssm_d', 'glu_w', 'glu_b', 'odd_w_in',
                'odd_w_out', 'pool_w', 'pool_scale', 'final_g')
SSM_NAMES = ('ssm_a_re', 'ssm_a_im', 'ssm_log_dt', 'ssm_b_re', 'ssm_b_im', 'ssm_c_re', 'ssm_c_im', 'ssm_d')


def kernel(x, c, ctx, c_ctx, ada_w, ada_b, norm_g, even_w_in, even_w_out, attn_sink, ssm_a_re, ssm_a_im, ssm_log_dt, ssm_b_re, ssm_b_im, ssm_c_re, ssm_c_im, ssm_d, glu_w, glu_b, odd_w_in, odd_w_out, pool_w, pool_scale, final_g, loss_target, m_c_ctx, m_ada_w, m_ada_b, m_norm_g, m_even_w_in, m_even_w_out, m_attn_sink, m_ssm_a_re, m_ssm_a_im, m_ssm_log_dt, m_ssm_b_re, m_ssm_b_im, m_ssm_c_re, m_ssm_c_im, m_ssm_d, m_glu_w, m_glu_b, m_odd_w_in, m_odd_w_out, m_pool_w, m_pool_scale, m_final_g, v_c_ctx, v_ada_w, v_ada_b, v_norm_g, v_even_w_in, v_even_w_out, v_attn_sink, v_ssm_a_re, v_ssm_a_im, v_ssm_log_dt, v_ssm_b_re, v_ssm_b_im, v_ssm_c_re, v_ssm_c_im, v_ssm_d, v_glu_w, v_glu_b, v_odd_w_in, v_odd_w_out, v_pool_w, v_pool_scale, v_final_g):
    env = dict(locals())
    weights = {n: env[n] for n in WEIGHT_NAMES}
    bsz = x.shape[0]
    ax, ay, ac = lax.axis_index("x"), lax.axis_index("y"), lax.axis_index("c")
    pos = 2 * ax + ay
    dev = 2 * pos + ac

    c_rows = jnp.concatenate([c, c_ctx.reshape(1, D), c_ctx.reshape(1, D), jnp.zeros((C_ROWS - bsz - 2, D), F32)])
    c_all = allgather8(c_rows, "gather_c")
    ada_b_cols = lax.dynamic_slice(ada_b, (0, pos * ADA_COLS), (DEPTH, ADA_COLS)).reshape(DEPTH, 1, ADA_COLS)
    mod_cols = ada_fwd(c_all, ada_w, ada_b_cols, "ada_fwd")
    nrow = N_DEV * C_ROWS
    misc = jnp.concatenate([mod_cols.reshape(DEPTH * nrow, ADA_COLS),
                            jnp.pad(pool_scale, ((0, 6), (0, ADA_COLS - pool_scale.shape[1])))])
    misc_all = allgather8(misc, "gather_mod").reshape(4, 2, DEPTH * nrow + 8, ADA_COLS)[:, 0]
    mod_full = misc_all[:, :DEPTH * nrow].reshape(4, DEPTH, nrow, ADA_COLS).transpose(1, 2, 0, 3)
    mod_mine = lax.dynamic_slice(mod_full.reshape(DEPTH, nrow, 3 * D), (0, dev * C_ROWS, 0), (DEPTH, C_ROWS, 3 * D))
    mods = []
    for i in range(DEPTH):
        lat = mod_mine[i, :bsz].reshape(bsz, 1, 3, D)
        con = jnp.broadcast_to(mod_mine[i, bsz].reshape(1, 1, 3, D), (bsz, 1, 3, D))
        mods.append(jnp.pad(jnp.concatenate([con, lat], axis=1), ((0, 0), (0, 0), (0, 5), (0, 0))))
    pool_scale_full = misc_all[:, DEPTH * nrow:DEPTH * nrow + 2, :pool_scale.shape[1]].transpose(1, 0, 2).reshape(2, D)

    first, shard, axis = BIG[0]
    w_in_full = _from_shards(xy_exchange([weights[first].astype(BF16)], False, "gather_w_in")[0], shard, axis)
    even = [dict(w_in=w_in_full[j], sink=attn_sink[j], ssm=tuple(weights[n][j] for n in SSM_NAMES),
                 glu_b=glu_b[j].reshape(1, SSM_W)) for j in range(2)]
    odd = [dict(pool_scale=pool_scale_full[j].reshape(1, D)) for j in range(2)]

    def fill(gathered):
        full = {n: _from_shards(g, shard, axis) for (n, shard, axis), g in zip(BIG[1:], gathered)}
        for j in range(2):
            even[j].update(w_out=full["even_w_out"][j], glu_w=full["glu_w"][j])
            odd[j].update(w_in=full["odd_w_in"][j], w_out=full["odd_w_out"][j], pool_w=full["pool_w"][j])

    late = ([weights[n].astype(BF16) for n, _, _ in BIG[1:]], fill)
    loss_local, grad_x, dmods, grads = local_step(x, ctx, loss_target, mods, norm_g, final_g, even, odd, late)
    loss = lax.psum(loss_local, ("x", "y", "c"))

    d_rows = jnp.stack([jnp.concatenate([dm[:, 1].reshape(bsz, 3 * D), dm[:, 0].reshape(bsz, 3 * D),
                                         jnp.zeros((C_ROWS - 2 * bsz, 3 * D), F32)]) for dm in dmods])
    d_all = allgather8(d_rows.reshape(DEPTH * C_ROWS, 3 * D), "gather_dmod")
    d_all = d_all.reshape(N_DEV, DEPTH, C_ROWS, 3 * D).transpose(1, 0, 2, 3).reshape(DEPTH, nrow, 3 * D)
    d_cols = lax.dynamic_slice(d_all, (0, 0, pos * ADA_COLS), (DEPTH, nrow, ADA_COLS))
    g_ada_w, ds_ctx = ada_bwd(c_all, d_cols, ada_w, "ada_bwd")
    g_ada_b = ada_bias_grad(d_all, "ada_bias_grad")[:, 0]

    small = dict(ds_ctx=ds_ctx[0] * (ac == 0).astype(F32), norm_g=jnp.stack(grads["norm_g"]), final_g=grads["final_g"],
                 attn_sink=jnp.stack([grads["even"][j]["sink"] for j in range(2)]),
                 glu_b=jnp.stack([grads["even"][j]["glu_b"] for j in range(2)]),
                 pool_scale=jnp.stack([grads["odd"][j]["pool_scale"] for j in range(2)]))
    for k, n in enumerate(SSM_NAMES):
        small[n] = jnp.stack([grads["even"][j]["ssm"][k] for j in range(2)])
    packed = pack_small(small)
    small_sum = sum_slots(allgather8(packed, "gather_small").reshape(N_DEV, packed.shape[0], 128), "sum_small")
    g_small = unpack_small(small_sum)
    g_small["c_ctx"] = silu_chain(g_small.pop("ds_ctx").reshape(1, D), c_ctx.reshape(1, D), "c_ctx_grad").reshape(D)
    g_small["ada_b"] = g_ada_b
    g_small["pool_scale"] = lax.dynamic_slice(g_small["pool_scale"], (0, pos * 256), (2, 256))

    big_full = dict(even_w_in=jnp.stack([grads["even"][j]["w_in"] for j in range(2)]),
                    even_w_out=jnp.stack([grads["even"][j]["w_out"] for j in range(2)]),
                    glu_w=jnp.stack([grads["even"][j]["glu_w"] for j in range(2)]),
                    odd_w_in=jnp.stack([grads["odd"][j]["w_in"] for j in range(2)]),
                    odd_w_out=jnp.stack([grads["odd"][j]["w_out"] for j in range(2)]),
                    pool_w=jnp.stack([grads["odd"][j]["pool_w"] for j in range(2)]))
    landed = xy_exchange([_to_shards(big_full[n], shard, axis).astype(BF16) for n, shard, axis in BIG], True,
                         "scatter_grads")
    mine4 = [sum_slots(r.reshape(4, -1, r.shape[-1]), "sum_positions_" + n).reshape(shard)
             for (n, shard, _), r in zip(BIG, landed)]
    other4 = sibling_exchange(mine4, "swap_cores")
    g_mine = dict(zip([n for n, _, _ in BIG], mine4))
    g_other = dict(zip([n for n, _, _ in BIG], other4))

    results = {}
    for n in WEIGHT_NAMES:
        if n in g_mine:
            gs = [g_mine[n], g_other[n]]
        elif n == "ada_w":
            gs = [g_ada_w]
        else:
            gs = [g_small[n]]
        results[n] = adamw(weights[n], env["m_" + n], env["v_" + n], gs, "adamw_" + n)
    outs = [loss, grad_x]
    for k in range(4):
        outs += [results[n][k] for n in WEIGHT_NAMES]
    return tuple(outs)
```

```python
import functools

import numpy as np
import jax
import jax.numpy as jnp
from jax import lax
from jax.experimental import pallas as pl
from jax.experimental.pallas import tpu as pltpu

F32 = jnp.float32
BF16 = jnp.bfloat16
MESH = pl.DeviceIdType.MESH

D = 1024
DEPTH = 4
EPS = 1e-6
NEG_INF = -1e30
GRID_W = 64
ROPE_BASE = 10000.0
ROPE_FREQS = 16
HEAD_DIM = 64
N_HEADS = 8
N_KV = 2
GROUP = 4
ATTN_W = N_HEADS * HEAD_DIM
KV_W = N_KV * HEAD_DIM
WINDOW = 128
AB = 128
SSM_W = 512
G = 32
C = 16
P = 64
Q = 16
QC = Q * C
P2 = 2 * P
SCAN_G = 16
POOL_R = (1, 2, 4, 8)
POOL_G = 256
HALO = 8
TM = 256
EVEN_SPLITS = (512, 256, 512, 512, 512)
ODD_SPLITS = (1024, 1024)

ADAM_LR = 0.001
ADAM_B1 = 0.9
ADAM_B2 = 0.999
ADAM_EPS = 1e-08
ADAM_WD = 0.01
ADAM_STEP = 10

MB = 1024 * 1024


def _cp(vmem_mb=48, n_axes=0):
    kw = dict(vmem_limit_bytes=vmem_mb * MB)
    if n_axes:
        kw["dimension_semantics"] = ("arbitrary",) * n_axes
    return pltpu.CompilerParams(**kw)


def _sig(x):
    return 1.0 / (1.0 + jnp.exp(-x))


def _silu_and_grad(x):
    s = _sig(x)
    return x * s, s * (1.0 + x * (1.0 - s))


_GELU_C = 0.7978845608028654
_GELU_A = 0.044715


def _gelu_and_grad(x):
    th = jnp.tanh(_GELU_C * (x + _GELU_A * x * x * x))
    val = 0.5 * x * (1.0 + th)
    grad = 0.5 * (1.0 + th) + 0.5 * x * (1.0 - th * th) * _GELU_C * (1.0 + 3.0 * _GELU_A * x * x)
    return val, grad


def _rms(h):
    r = lax.rsqrt(jnp.mean(h * h, axis=-1, keepdims=True) + EPS)
    return h * r, r


def _dot(a, b):
    return jnp.dot(a, b, preferred_element_type=F32)


def _dot_nt(a, b):
    return lax.dot_general(a, b, (((1,), (1,)), ((), ())), preferred_element_type=F32)


def _dot_tn(a, b):
    return lax.dot_general(a, b, (((0,), (0,)), ((), ())), preferred_element_type=F32)


def _rowsum(x):
    return jnp.sum(x, axis=0, keepdims=True)


def _seg(t):
    return jnp.minimum(t, 1)


def _row_spec(n):
    return pl.BlockSpec((None, TM, n), lambda b, t: (b, t, 0))


def _const_spec(shape):
    nd = len(shape)
    return pl.BlockSpec(shape, lambda b, t: (0,) * nd)


def _mod_spec():
    return pl.BlockSpec((None, None, 8, D), lambda b, t: (b, _seg(t), 0, 0))


def norm_in(h, g, mod, w, splits, name):
    bsz, t_len, _ = h.shape
    n = w.shape[1]
    offs = [int(v) for v in np.cumsum((0,) + tuple(splits))]

    def body(h_ref, g_ref, mod_ref, w_ref, a_ref, *outs):
        xh, _ = _rms(h_ref[...])
        a = xh * g_ref[...] * (1.0 + mod_ref[1:2, :]) + mod_ref[0:1, :]
        ab = a.astype(BF16)
        a_ref[...] = ab
        z = _dot(ab, w_ref[...])
        for o, lo, hi in zip(outs, offs[:-1], offs[1:]):
            o[...] = z[:, lo:hi]

    return pl.pallas_call(
        body, name=name, grid=(bsz, t_len // TM),
        in_specs=[_row_spec(D), _const_spec((1, D)), _mod_spec(), _const_spec((D, n))],
        out_specs=[_row_spec(D)] + [_row_spec(s) for s in splits],
        out_shape=[jax.ShapeDtypeStruct((bsz, t_len, D), BF16)]
        + [jax.ShapeDtypeStruct((bsz, t_len, s), F32) for s in splits],
        compiler_params=_cp(48, 2),
    )(h, g, mod, w)


def norm_in_bwd(dparts, dh_in, h, g, mod, w, name, skip=None):
    bsz, t_len, _ = h.shape
    n = w.shape[1]
    k = len(dparts)
    extra = [] if skip is None else [skip[1], skip[2]]

    def body(*refs):
        parts = [r[...] for r in refs[:k]]
        if skip is not None:
            parts[skip[0]] = parts[skip[0]] + refs[k][...] * refs[k + 1][...]
        dh_in_ref, h_ref, g_ref, mod_ref, w_ref, dh_ref, dz_ref, dmod_ref, dg_ref = refs[k + len(extra):]
        b, t = pl.program_id(0), pl.program_id(1)
        dz = jnp.concatenate(parts, axis=1).astype(BF16)
        dz_ref[...] = dz
        da = _dot_nt(dz, w_ref[...])
        xh, r = _rms(h_ref[...])
        gg = g_ref[...]
        sc1 = 1.0 + mod_ref[1:2, :]

        @pl.when(t <= 1)
        def _():
            dmod_ref[...] = jnp.zeros_like(dmod_ref)

        @pl.when((b == 0) & (t == 0))
        def _():
            dg_ref[...] = jnp.zeros_like(dg_ref)

        dmod_ref[0:1, :] += _rowsum(da)
        dmod_ref[1:2, :] += _rowsum(da * (xh * gg))
        dg_ref[0:1, :] += _rowsum(da * sc1 * xh)
        dxh = da * gg * sc1
        dh_ref[...] = dh_in_ref[...] + r * (dxh - xh * jnp.mean(dxh * xh, axis=-1, keepdims=True))

    return pl.pallas_call(
        body, name=name, grid=(bsz, t_len // TM),
        in_specs=[_row_spec(p.shape[-1]) for p in dparts]
        + ([_row_spec(extra[0].shape[-1]), _const_spec(extra[1].shape)] if extra else [])
        + [_row_spec(D), _row_spec(D), _const_spec((1, D)), _mod_spec(), _const_spec((D, n))],
        out_specs=[_row_spec(D), _row_spec(n), _mod_spec(), _const_spec((8, D))],
        out_shape=[jax.ShapeDtypeStruct((bsz, t_len, D), F32), jax.ShapeDtypeStruct((bsz, t_len, n), BF16),
                   jax.ShapeDtypeStruct((bsz, 2, 8, D), F32), jax.ShapeDtypeStruct((8, D), F32)],
        compiler_params=_cp(56, 2),
    )(*dparts, *extra, dh_in, h, g, mod, w)


def matmul_tn(a, b, m, n, name, a_col=0, b_col=0):
    rows = a.shape[0]
    tr = 512 if rows % 512 == 0 else rows
    tn = n
    for cand in (1024, 768, 512, 256, 128):
        if n > 1024 and n % cand == 0:
            tn = cand
            break
    nb = n // tn

    def body(a_ref, b_ref, o_ref):
        @pl.when(pl.program_id(1) == 0)
        def _():
            o_ref[...] = jnp.zeros_like(o_ref)
        o_ref[...] += _dot_tn(a_ref[...].astype(BF16), b_ref[...].astype(BF16))

    return pl.pallas_call(
        body, name=name, grid=(nb, rows // tr),
        in_specs=[pl.BlockSpec((tr, m), lambda j, r: (r, a_col)),
                  pl.BlockSpec((tr, tn), lambda j, r: (r, b_col * nb + j))],
        out_specs=pl.BlockSpec((m, tn), lambda j, r: (0, j)),
        out_shape=jax.ShapeDtypeStruct((m, n), F32),
        compiler_params=_cp(48, 2),
    )(a, b)


def even_out(h, mod, o_attn, g_attn, y_scan, u, d_skip, g_ssm, glu_w, glu_b, w_out, name):
    bsz, t_len, _ = h.shape

    def body(h_ref, mod_ref, oa_ref, ga_ref, ys_ref, u_ref, dk_ref, gs_ref, gw_ref, gb_ref, wo_ref,
             hn_ref, mix_ref, yo_ref):
        zz, _ = _gelu_and_grad(ys_ref[...] + u_ref[...] * dk_ref[...])
        s = _dot(zz.astype(BF16), gw_ref[...]) + gb_ref[...]
        o_ssm = zz * _sig(s)
        sa, _ = _silu_and_grad(ga_ref[...])
        ss, _ = _silu_and_grad(gs_ref[...])
        mb = jnp.concatenate([oa_ref[...] * sa, o_ssm * ss], axis=1).astype(BF16)
        mix_ref[...] = mb
        yo = _dot(mb, wo_ref[...])
        yo_ref[...] = yo
        hn_ref[...] = h_ref[...] + mod_ref[2:3, :] * yo

    return pl.pallas_call(
        body, name=name, grid=(bsz, t_len // TM),
        in_specs=[_row_spec(D), _mod_spec(), _row_spec(512), _row_spec(512), _row_spec(512), _row_spec(512),
                  _const_spec((1, 512)), _row_spec(512), _const_spec((512, 512)), _const_spec((1, 512)),
                  _const_spec((D, D))],
        out_specs=[_row_spec(D), _row_spec(D), _row_spec(D)],
        out_shape=[jax.ShapeDtypeStruct((bsz, t_len, D), F32), jax.ShapeDtypeStruct((bsz, t_len, D), BF16),
                   jax.ShapeDtypeStruct((bsz, t_len, D), F32)],
        compiler_params=_cp(48, 2),
    )(h, mod, o_attn, g_attn, y_scan, u, d_skip, g_ssm, glu_w, glu_b, w_out)


def even_out_bwd(dh, mod, o_attn, g_attn, y_scan, u, d_skip, g_ssm, glu_w, glu_b, w_out, yout, name):
    bsz, t_len, _ = dh.shape

    def body(dh_ref, mod_ref, oa_ref, ga_ref, ys_ref, u_ref, dk_ref, gs_ref, gw_ref, gb_ref, wo_ref, yo_ref,
             doa_ref, dga_ref, dgs_ref, dys_ref, dyo_ref, zz_ref, ds_ref, dgate_ref, dgb_ref):
        b, t = pl.program_id(0), pl.program_id(1)
        dhv = dh_ref[...]

        @pl.when(t <= 1)
        def _():
            dgate_ref[...] = jnp.zeros_like(dgate_ref)

        @pl.when((b == 0) & (t == 0))
        def _():
            dgb_ref[...] = jnp.zeros_like(dgb_ref)

        dgate_ref[0:1, :] += _rowsum(dhv * yo_ref[...])
        dyb = (mod_ref[2:3, :] * dhv).astype(BF16)
        dyo_ref[...] = dyb
        dmix = _dot_nt(dyb, wo_ref[...])
        sa, dsa = _silu_and_grad(ga_ref[...])
        doa_ref[...] = dmix[:, :512] * sa
        dga_ref[...] = dmix[:, :512] * oa_ref[...] * dsa
        uv = u_ref[...]
        zz, dzz_dy = _gelu_and_grad(ys_ref[...] + uv * dk_ref[...])
        zb = zz.astype(BF16)
        zz_ref[...] = zb
        sg = _sig(_dot(zb, gw_ref[...]) + gb_ref[...])
        ss, dss = _silu_and_grad(gs_ref[...])
        dm = dmix[:, 512:]
        dgs_ref[...] = dm * (zz * sg) * dss
        do = dm * ss
        ds = do * zz * sg * (1.0 - sg)
        dsb = ds.astype(BF16)
        ds_ref[...] = dsb
        dgb_ref[0:1, :] += _rowsum(ds)
        dys = (do * sg + _dot_nt(dsb, gw_ref[...])) * dzz_dy
        dys_ref[...] = dys
        dgb_ref[1:2, :] += _rowsum(dys * uv)

    r512 = jax.ShapeDtypeStruct((bsz, t_len, 512), F32)
    return pl.pallas_call(
        body, name=name, grid=(bsz, t_len // TM),
        in_specs=[_row_spec(D), _mod_spec(), _row_spec(512), _row_spec(512), _row_spec(512), _row_spec(512),
                  _const_spec((1, 512)), _row_spec(512), _const_spec((512, 512)), _const_spec((1, 512)),
                  _const_spec((D, D)), _row_spec(D)],
        out_specs=[_row_spec(512)] * 4 + [_row_spec(D), _row_spec(512), _row_spec(512), _mod_spec(),
                                           _const_spec((8, 512))],
        out_shape=[r512, r512, r512, r512, jax.ShapeDtypeStruct((bsz, t_len, D), BF16),
                   jax.ShapeDtypeStruct((bsz, t_len, 512), BF16), jax.ShapeDtypeStruct((bsz, t_len, 512), BF16),
                   jax.ShapeDtypeStruct((bsz, 2, 8, D), F32), jax.ShapeDtypeStruct((8, 512), F32)],
        compiler_params=_cp(48, 2),
    )(dh, mod, o_attn, g_attn, y_scan, u, d_skip, g_ssm, glu_w, glu_b, w_out, yout)


def _split3_dot(band, x):
    x1 = x.astype(BF16)
    r1 = x - x1.astype(F32)
    x2 = r1.astype(BF16)
    x3 = (r1 - x2.astype(F32)).astype(BF16)
    return _dot(band, x3) + _dot(band, x2) + _dot(band, x1)


def pool_band(x, lc, transpose, name):
    bsz, t_len, _ = x.shape
    assert lc == TM
    hb = TM // HALO

    def body(xp_ref, xc_ref, xn_ref, o_ref):
        t = pl.program_id(1)
        seg_lo = jnp.where(t == 0, 0, lc)
        seg_hi = jnp.where(t == 0, lc, t_len)
        cur = xc_ref[...]
        xh = jnp.concatenate([xp_ref[...], cur, xn_ref[...]], axis=0)
        row_t = t * TM + lax.broadcasted_iota(jnp.int32, (TM, 1), 0)
        col_s = t * TM - HALO + lax.broadcasted_iota(jnp.int32, (1, TM + 2 * HALO), 1)
        row_s = t * TM - HALO + lax.broadcasted_iota(jnp.int32, (TM + 2 * HALO, 1), 0)
        s_ok = (col_s >= seg_lo) & (col_s < seg_hi)
        outs = []
        for gi, r in enumerate(POOL_R):
            band = ((jnp.abs(row_t - col_s) <= r) & s_ok).astype(BF16)
            xg = xh[:, gi * POOL_G:(gi + 1) * POOL_G]
            if transpose:
                cnt_s = jnp.minimum(row_s + r, seg_hi - 1) - jnp.maximum(row_s - r, seg_lo) + 1
                xg = xg * (1.0 / jnp.maximum(cnt_s, 1).astype(F32))
            acc = _split3_dot(band, xg)
            if not transpose:
                cnt_t = jnp.minimum(row_t + r, seg_hi - 1) - jnp.maximum(row_t - r, seg_lo) + 1
                acc = acc * (1.0 / cnt_t.astype(F32))
            outs.append(acc - cur[:, gi * POOL_G:(gi + 1) * POOL_G])
        o_ref[...] = jnp.concatenate(outs, axis=1)

    return pl.pallas_call(
        body, name=name, grid=(bsz, t_len // TM),
        in_specs=[pl.BlockSpec((None, HALO, D), lambda b, t: (b, jnp.maximum(t * hb - 1, 0), 0)),
                  _row_spec(D),
                  pl.BlockSpec((None, HALO, D), lambda b, t: (b, jnp.minimum((t + 1) * hb, t_len // HALO - 1), 0))],
        out_specs=_row_spec(D),
        out_shape=jax.ShapeDtypeStruct((bsz, t_len, D), F32),
        compiler_params=_cp(48, 2),
    )(x, x, x)


def pool_out(h, mod, pm, gate, pool_w, pool_scale, w_out, name):
    bsz, t_len, _ = h.shape

    def body(h_ref, mod_ref, pm_ref, gt_ref, pw_ref, ps_ref, wo_ref, hn_ref, mix_ref, yo_ref):
        pmv = pm_ref[...]
        ppre = jnp.concatenate([_dot(pmv[:, g * POOL_G:(g + 1) * POOL_G].astype(BF16), pw_ref[g])
                                for g in range(4)], axis=1)
        sl, _ = _silu_and_grad(gt_ref[...])
        mb = (ppre * ps_ref[...] * sl).astype(BF16)
        mix_ref[...] = mb
        yo = _dot(mb, wo_ref[...])
        yo_ref[...] = yo
        hn_ref[...] = h_ref[...] + mod_ref[2:3, :] * yo

    return pl.pallas_call(
        body, name=name, grid=(bsz, t_len // TM),
        in_specs=[_row_spec(D), _mod_spec(), _row_spec(D), _row_spec(D), _const_spec((4, POOL_G, POOL_G)),
                  _const_spec((1, D)), _const_spec((D, D))],
        out_specs=[_row_spec(D), _row_spec(D), _row_spec(D)],
        out_shape=[jax.ShapeDtypeStruct((bsz, t_len, D), F32), jax.ShapeDtypeStruct((bsz, t_len, D), BF16),
                   jax.ShapeDtypeStruct((bsz, t_len, D), F32)],
        compiler_params=_cp(48, 2),
    )(h, mod, pm, gate, pool_w, pool_scale, w_out)


def pool_out_bwd(dh, mod, pm, gate, pool_w, pool_scale, w_out, yout, name):
    bsz, t_len, _ = dh.shape

    def body(dh_ref, mod_ref, pm_ref, gt_ref, pw_ref, ps_ref, wo_ref, yo_ref,
             dpm_ref, dgt_ref, dyo_ref, dpp_ref, dgate_ref, dps_ref):
        b, t = pl.program_id(0), pl.program_id(1)
        dhv = dh_ref[...]

        @pl.when(t <= 1)
        def _():
            dgate_ref[...] = jnp.zeros_like(dgate_ref)

        @pl.when((b == 0) & (t == 0))
        def _():
            dps_ref[...] = jnp.zeros_like(dps_ref)

        dgate_ref[0:1, :] += _rowsum(dhv * yo_ref[...])
        dyb = (mod_ref[2:3, :] * dhv).astype(BF16)
        dyo_ref[...] = dyb
        dmix = _dot_nt(dyb, wo_ref[...])
        pmv = pm_ref[...]
        ppre = jnp.concatenate([_dot(pmv[:, g * POOL_G:(g + 1) * POOL_G].astype(BF16), pw_ref[g])
                                for g in range(4)], axis=1)
        ps = ps_ref[...]
        sl, dsl = _silu_and_grad(gt_ref[...])
        dp = dmix * sl
        dgt_ref[...] = dmix * (ppre * ps) * dsl
        dps_ref[0:1, :] += _rowsum(dp * ppre)
        dppb = (dp * ps).astype(BF16)
        dpp_ref[...] = dppb
        dpm_ref[...] = jnp.concatenate([_dot_nt(dppb[:, g * POOL_G:(g + 1) * POOL_G], pw_ref[g])
                                        for g in range(4)], axis=1)

    return pl.pallas_call(
        body, name=name, grid=(bsz, t_len // TM),
        in_specs=[_row_spec(D), _mod_spec(), _row_spec(D), _row_spec(D), _const_spec((4, POOL_G, POOL_G)),
                  _const_spec((1, D)), _const_spec((D, D)), _row_spec(D)],
        out_specs=[_row_spec(D), _row_spec(D), _row_spec(D), _row_spec(D), _mod_spec(), _const_spec((8, D))],
        out_shape=[jax.ShapeDtypeStruct((bsz, t_len, D), F32), jax.ShapeDtypeStruct((bsz, t_len, D), F32),
                   jax.ShapeDtypeStruct((bsz, t_len, D), BF16), jax.ShapeDtypeStruct((bsz, t_len, D), BF16),
                   jax.ShapeDtypeStruct((bsz, 2, 8, D), F32), jax.ShapeDtypeStruct((8, D), F32)],
        compiler_params=_cp(48, 2),
    )(dh, mod, pm, gate, pool_w, pool_scale, w_out, yout)


def loss_head(h, final_g, target, name):
    bsz, t_len, _ = h.shape

    def body(h_ref, g_ref, tg_ref, dh_ref, loss_ref, dg_ref):
        b, t = pl.program_id(0), pl.program_id(1)

        @pl.when((b == 0) & (t == 0))
        def _():
            loss_ref[...] = jnp.zeros_like(loss_ref)
            dg_ref[...] = jnp.zeros_like(dg_ref)

        lat = (t > 0).astype(F32)
        xh, r = _rms(h_ref[...])
        gg = g_ref[...]
        err = (xh * gg - tg_ref[...]) * lat
        loss_ref[...] += 0.5 * jnp.sum(jnp.mean(err * err, axis=-1, keepdims=True))
        dy = err * (1.0 / D)
        dg_ref[0:1, :] += _rowsum(dy * xh)
        dxh = dy * gg
        dh_ref[...] = r * (dxh - xh * jnp.mean(dxh * xh, axis=-1, keepdims=True))

    return pl.pallas_call(
        body, name=name, grid=(bsz, t_len // TM),
        in_specs=[_row_spec(D), _const_spec((1, D)),
                  pl.BlockSpec((None, TM, D), lambda b, t: (b, jnp.maximum(t - 1, 0), 0))],
        out_specs=[_row_spec(D), _const_spec((8, 128)), _const_spec((8, D))],
        out_shape=[jax.ShapeDtypeStruct((bsz, t_len, D), F32), jax.ShapeDtypeStruct((8, 128), F32),
                   jax.ShapeDtypeStruct((8, D), F32)],
        compiler_params=_cp(48, 2),
    )(h, final_g, target)


def _swap16(x):
    n = x.shape[-1]
    ax = x.ndim - 1
    lane = lax.broadcasted_iota(jnp.int32, x.shape, ax)
    return jnp.where((lane % 32) < 16, pltpu.roll(x, n - 16, ax), pltpu.roll(x, 16, ax))


def _rope(x, cos, sin):
    return x * cos + _swap16(x) * sin


def _rope_t(dy, cos, sin):
    return dy * cos + _swap16(dy * sin)


def rope_tables(lc, seq):
    rows = seq // GRID_W
    row = jnp.repeat(jnp.arange(rows, dtype=F32), GRID_W)
    col = jnp.tile(jnp.arange(GRID_W, dtype=F32), rows)
    inv_freq = ROPE_BASE ** (-jnp.arange(ROPE_FREQS, dtype=F32) / ROPE_FREQS)
    ar, ac = row[:, None] * inv_freq, col[:, None] * inv_freq
    cos = jnp.concatenate([jnp.cos(ar), jnp.cos(ar), jnp.cos(ac), jnp.cos(ac)], axis=1)
    sin = jnp.concatenate([-jnp.sin(ar), jnp.sin(ar), -jnp.sin(ac), jnp.sin(ac)], axis=1)
    cos = jnp.concatenate([jnp.ones((lc, HEAD_DIM), F32), cos], axis=0)
    sin = jnp.concatenate([jnp.zeros((lc, HEAD_DIM), F32), sin], axis=0)
    return jnp.tile(cos, (1, 2)), jnp.tile(sin, (1, 2))


def _attn_mask(i, lc, t_len):
    qrow = i * AB + lax.broadcasted_iota(jnp.int32, (AB, 1), 0)
    kloc = (i - 1) * AB + lax.broadcasted_iota(jnp.int32, (1, 3 * AB), 1)
    valid = (qrow >= lc) & (kloc >= lc) & (kloc < t_len) & (jnp.abs(qrow - kloc) <= WINDOW)
    mask = jnp.concatenate([valid, jnp.ones((AB, lc), jnp.bool_)], axis=1)
    return jnp.concatenate([mask] * GROUP, axis=0)


def _attn_specs(t_len, lc):
    nb = t_len // AB
    prev = lambda b, i: (b, jnp.maximum(i - 1, 0), 0)
    cur = lambda b, i: (b, i, 0)
    nxt = lambda b, i: (b, jnp.minimum(i + 1, nb - 1), 0)
    kv = [pl.BlockSpec((None, AB, 2 * KV_W), f) for f in (prev, cur, nxt)]
    kv.append(pl.BlockSpec((None, lc, 2 * KV_W), lambda b, i: (b, 0, 0)))
    tab = [pl.BlockSpec((AB, 128), lambda b, i, f=f: f(b, i)[1:]) for f in (prev, cur, nxt)]
    return kv, tab


def _attn_keys(kvp, kvc, kvn, kvx, cp, cc, cn, sp, sc, sn):
    kk = jnp.concatenate([_rope(kvp[:, :KV_W], cp, sp), _rope(kvc[:, :KV_W], cc, sc),
                          _rope(kvn[:, :KV_W], cn, sn), kvx[:, :KV_W]], axis=0)
    vv = jnp.concatenate([kvp[:, KV_W:], kvc[:, KV_W:], kvn[:, KV_W:], kvx[:, KV_W:]], axis=0)
    return kk, vv


def _stack_heads(x, hk):
    return jnp.concatenate([x[:, (GROUP * hk + g) * HEAD_DIM:(GROUP * hk + g + 1) * HEAD_DIM]
                            for g in range(GROUP)], axis=0)


def _sink_col(sink_ref, hk):
    return jnp.concatenate([jnp.full((AB, 1), sink_ref[GROUP * hk + g], F32) for g in range(GROUP)], axis=0)


def attn_fwd(q, kv, cos, sin, sink, lc, name, carry=()):
    bsz, t_len, _ = q.shape
    nb = t_len // AB
    kv_specs, tab_specs = _attn_specs(t_len, lc)
    scale = HEAD_DIM ** -0.5
    nc = len(carry)

    def body(sink_ref, q_ref, kvp_ref, kvc_ref, kvn_ref, kvx_ref, cp, cc, cn, sp, sc, sn, *rest):
        o_ref, lse_ref = rest[nc:nc + 2]
        b, i = pl.program_id(0), pl.program_id(1)
        if nc:
            start, wait = _xy_copies(rest[:nc], rest[nc + 2:2 * nc + 2], *rest[2 * nc + 2:], False)
            pl.when((b == 0) & (i == 0))(start)
        mask = _attn_mask(i, lc, t_len)
        qr = _rope(q_ref[...], jnp.tile(cc[...], (1, 4)), jnp.tile(sc[...], (1, 4)))
        kk, vv = _attn_keys(kvp_ref[...], kvc_ref[...], kvn_ref[...], kvx_ref[...],
                            cp[...], cc[...], cn[...], sp[...], sc[...], sn[...])
        outs, lses = [], []
        for hk in range(N_KV):
            kh = kk[:, hk * HEAD_DIM:(hk + 1) * HEAD_DIM].astype(BF16)
            vh = vv[:, hk * HEAD_DIM:(hk + 1) * HEAD_DIM].astype(BF16)
            q4 = _stack_heads(qr, hk).astype(BF16)
            s = jnp.where(mask, _dot_nt(q4, kh) * scale, NEG_INF)
            sk = _sink_col(sink_ref, hk)
            m = jnp.maximum(jnp.max(s, axis=-1, keepdims=True), sk)
            p = jnp.exp(s - m)
            l = jnp.sum(p, axis=-1, keepdims=True) + jnp.exp(sk - m)
            o = _dot(p.astype(BF16), vh) / l
            lse = m + jnp.log(l)
            for g in range(GROUP):
                outs.append(o[g * AB:(g + 1) * AB])
                lses.append(lse[g * AB:(g + 1) * AB])
        o_ref[...] = jnp.concatenate(outs, axis=1)
        lse_ref[...] = jnp.concatenate(lses, axis=1)
        if nc:
            pl.when((b == bsz - 1) & (i == nb - 1))(wait)

    return pl.pallas_call(
        body, name=name, grid=(bsz, nb),
        in_specs=[pl.BlockSpec(memory_space=pltpu.SMEM),
                  pl.BlockSpec((None, AB, ATTN_W), lambda b, i: (b, i, 0))] + kv_specs + tab_specs + tab_specs
        + [HBM_SPEC] * nc,
        out_specs=[pl.BlockSpec((None, AB, ATTN_W), lambda b, i: (b, i, 0)),
                   pl.BlockSpec((None, AB, N_HEADS), lambda b, i: (b, i, 0))] + [HBM_SPEC] * nc,
        out_shape=[jax.ShapeDtypeStruct((bsz, t_len, ATTN_W), F32), jax.ShapeDtypeStruct((bsz, t_len, N_HEADS), F32)]
        + _xy_out_shapes(carry, False),
        scratch_shapes=_xy_sems(nc) if nc else [],
        compiler_params=_cp(48, 2),
    )(sink, q, kv, kv, kv, kv, cos, cos, cos, sin, sin, sin, *carry)


def attn_bwd(q, kv, o, lse, do, cos, sin, sink, lc, name, carry=()):
    bsz, t_len, _ = q.shape
    nb = t_len // AB
    kv_specs, tab_specs = _attn_specs(t_len, lc)
    scale = HEAD_DIM ** -0.5
    blk = lambda w: pl.BlockSpec((None, AB, w), lambda b, i: (b, i, 0))
    full_tab = pl.BlockSpec((t_len, 128), lambda b, i: (0, 0))
    nc = len(carry)

    def body(sink_ref, q_ref, kvp_ref, kvc_ref, kvn_ref, kvx_ref, cp, cc, cn, sp, sc, sn, cf, sf,
             o_ref, lse_ref, do_ref, *rest):
        dq_ref, dkv_ref, dsink_ref = rest[nc:nc + 3]
        b, i = pl.program_id(0), pl.program_id(1)
        if nc:
            start, wait = _xy_copies(rest[:nc], rest[nc + 3:2 * nc + 3], *rest[2 * nc + 3:], True)
            pl.when((b == 0) & (i == 0))(start)

        @pl.when(i == 0)
        def _():
            dkv_ref[...] = jnp.zeros_like(dkv_ref)

        @pl.when((b == 0) & (i == 0))
        def _():
            dsink_ref[...] = jnp.zeros_like(dsink_ref)

        mask = _attn_mask(i, lc, t_len)
        cq, sq = jnp.tile(cc[...], (1, 4)), jnp.tile(sc[...], (1, 4))
        qr = _rope(q_ref[...], cq, sq)
        kk, vv = _attn_keys(kvp_ref[...], kvc_ref[...], kvn_ref[...], kvx_ref[...],
                            cp[...], cc[...], cn[...], sp[...], sc[...], sn[...])
        dov, ov, lsev = do_ref[...], o_ref[...], lse_ref[...]
        dqs, dks, dvs, dsk = [], [], [], []
        for hk in range(N_KV):
            kh = kk[:, hk * HEAD_DIM:(hk + 1) * HEAD_DIM].astype(BF16)
            vh = vv[:, hk * HEAD_DIM:(hk + 1) * HEAD_DIM].astype(BF16)
            q4 = _stack_heads(qr, hk).astype(BF16)
            do4 = _stack_heads(dov, hk)
            o4 = _stack_heads(ov, hk)
            lse4 = jnp.concatenate([lsev[:, GROUP * hk + g:GROUP * hk + g + 1] for g in range(GROUP)], axis=0)
            delta = jnp.sum(do4 * o4, axis=-1, keepdims=True)
            s = jnp.where(mask, _dot_nt(q4, kh) * scale, NEG_INF)
            p = jnp.exp(s - lse4)
            do4b = do4.astype(BF16)
            dp = _dot_nt(do4b, vh)
            ds = (p * (dp - delta) * scale).astype(BF16)
            dq4 = _dot(ds, kh)
            dks.append(_dot_tn(ds, q4))
            dvs.append(_dot_tn(p.astype(BF16), do4b))
            pd = jnp.exp(_sink_col(sink_ref, hk) - lse4) * delta
            for g in range(GROUP):
                dqs.append(dq4[g * AB:(g + 1) * AB])
                dsk.append(-jnp.sum(pd[g * AB:(g + 1) * AB], axis=0, keepdims=True))
        dq_ref[...] = _rope_t(jnp.concatenate(dqs, axis=1), cq, sq)
        dsink_ref[0:1, :] += jnp.concatenate(dsk, axis=1)
        dkv = jnp.concatenate(dks + dvs, axis=1)
        starts = (jnp.maximum(i - 1, 0), i, jnp.minimum(i + 1, nb - 1))
        for j, st in enumerate(starts):
            rows = pl.ds(pl.multiple_of(st * AB, AB), AB)
            dkv_ref[rows, :] += dkv[j * AB:(j + 1) * AB]
        dkv_ref[0:lc, :] += dkv[3 * AB:]

        @pl.when(i == nb - 1)
        def _():
            def unrotate(j, carry):
                rows = pl.ds(pl.multiple_of(j * AB, AB), AB)
                dkv_ref[rows, 0:KV_W] = _rope_t(dkv_ref[rows, 0:KV_W], cf[rows, :], sf[rows, :])
                return carry
            lax.fori_loop(0, nb, unrotate, 0)

        if nc:
            pl.when((b == bsz - 1) & (i == nb - 1))(wait)

    return pl.pallas_call(
        body, name=name, grid=(bsz, nb),
        in_specs=[pl.BlockSpec(memory_space=pltpu.SMEM), blk(ATTN_W)] + kv_specs + tab_specs + tab_specs
        + [full_tab, full_tab, blk(ATTN_W), blk(N_HEADS), blk(ATTN_W)] + [HBM_SPEC] * nc,
        out_specs=[blk(ATTN_W), pl.BlockSpec((None, t_len, 2 * KV_W), lambda b, i: (b, 0, 0)),
                   pl.BlockSpec((8, N_HEADS), lambda b, i: (0, 0))] + [HBM_SPEC] * nc,
        out_shape=[jax.ShapeDtypeStruct((bsz, t_len, ATTN_W), F32), jax.ShapeDtypeStruct((bsz, t_len, 2 * KV_W), F32),
                   jax.ShapeDtypeStruct((8, N_HEADS), F32)] + _xy_out_shapes(carry, True),
        scratch_shapes=_xy_sems(nc) if nc else [],
        compiler_params=_cp(56, 2),
    )(sink, q, kv, kv, kv, kv, cos, cos, cos, sin, sin, sin, cos, sin, o, lse, do, *carry)


def _s5_mats_dir(a_re, a_im, log_dt, b_re, b_im, c_re, c_im, flip):
    hp = lax.Precision.HIGHEST
    lam = lax.complex(a_re, a_im)
    ldt = lam * jnp.exp(log_dt)[:, None]
    a_bar = jnp.exp(ldt)
    b_bar = ((a_bar - 1.0) / lam)[..., None] * lax.complex(b_re, b_im)
    cm = lax.complex(c_re, c_im)
    tt = np.arange(Q)
    powers = lambda e: jnp.exp(ldt[..., None] * jnp.asarray(e, F32))
    ca = cm[:, :, :, None] * powers(Q - 1 - tt if flip else tt)[:, None, :, :]
    ca = jnp.concatenate([jnp.real(ca), -jnp.imag(ca)], axis=2)
    bb = jnp.concatenate([jnp.real(b_bar), jnp.imag(b_bar)], axis=1)
    k = jnp.einsum('gpk,gcpt->gktc', bb, ca, precision=hp).reshape(G, C, QC)
    slabs = []
    for t1 in range(Q):
        if flip:
            sh = (Q - 1 - t1) * C
            slabs.append(jnp.pad(k, ((0, 0), (0, 0), (0, sh)))[..., sh:])
        else:
            slabs.append(jnp.pad(k, ((0, 0), (0, 0), (t1 * C, 0)))[..., :QC])
    kt = jnp.stack(slabs, axis=1).reshape(G, QC, QC)
    ws = powers(tt if flip else Q - 1 - tt)[:, :, :, None] * b_bar[:, :, None, :]
    ws = ws.transpose(0, 2, 3, 1)
    wo = cm[:, :, :, None] * powers(Q - tt if flip else tt + 1)[:, None, :, :]
    wo = wo.transpose(0, 2, 3, 1)
    ws = jnp.concatenate([jnp.real(ws), jnp.imag(ws)], axis=-1).reshape(G, QC, P2)
    wo = jnp.concatenate([jnp.real(wo), -jnp.imag(wo)], axis=1).reshape(G, P2, QC)
    a1, a2 = _pair_forms(powers([Q]))
    return kt, ws, wo, a1, a2


def _pair_forms(z):
    re, im = jnp.real(z), jnp.imag(z)
    k = z.shape[-1]
    a1 = jnp.concatenate([re, re], axis=1).transpose(2, 0, 1).reshape(k, G * P2)
    a2 = jnp.concatenate([-im, im], axis=1).transpose(2, 0, 1).reshape(k, G * P2)
    return a1, a2


def s5_mats(a_re, a_im, log_dt, b_re, b_im, c_re, c_im):
    per_dir = [_s5_mats_dir(a_re[d], a_im[d], log_dt[d], b_re[d], b_im[d], c_re[d], c_im[d], d == 1)
               for d in range(2)]
    return tuple(jnp.stack([m[i] for m in per_dir]) for i in range(5))


GH = G // 8
RT = 16 * Q


def _perm_consts():
    r = np.arange(RT)
    rows = np.zeros((RT, RT), np.float32)
    rows[(r % Q) * 16 + r // Q, r] = 1.0
    q = np.arange(8 * 128)
    lanes = np.zeros((8 * 128, 8 * 128), np.float32)
    lanes[q, ((q % 128) // C) * 128 + (q // 128) * C + q % C] = 1.0
    return jnp.asarray(rows, BF16), jnp.asarray(lanes, BF16)


def to_groups(x, name):
    bsz, t_len, _ = x.shape
    nc = t_len // Q
    rows, lanes = _perm_consts()

    def body(x_ref, r_ref, p_ref, o_ref, w_ref):
        for j in range(t_len // RT):
            pt = _dot(r_ref[...], x_ref[j * RT:(j + 1) * RT, :].astype(BF16)).astype(BF16)
            for t in range(Q):
                w_ref[j * 16:(j + 1) * 16, t * SSM_W:(t + 1) * SSM_W] = pt[t * 16:(t + 1) * 16, :]
        for gh in range(GH):
            for th in range(2):
                inp = jnp.concatenate([w_ref[:, (th * 8 + tl) * SSM_W + gh * 128:(th * 8 + tl) * SSM_W + (gh + 1) * 128]
                                       for tl in range(8)], axis=1)
                out = _dot(inp, p_ref[...]).astype(BF16)
                for gl in range(8):
                    o_ref[gh * 8 + gl, :, th * 128:(th + 1) * 128] = out[:, gl * 128:(gl + 1) * 128]

    return pl.pallas_call(
        body, name=name, grid=(bsz,),
        in_specs=[pl.BlockSpec((None, t_len, SSM_W), lambda b: (b, 0, 0)), pl.BlockSpec((RT, RT), lambda b: (0, 0)),
                  pl.BlockSpec((1024, 1024), lambda b: (0, 0))],
        out_specs=pl.BlockSpec((None, G, nc, QC), lambda b: (b, 0, 0, 0)),
        out_shape=jax.ShapeDtypeStruct((bsz, G, nc, QC), BF16),
        scratch_shapes=[pltpu.VMEM((nc, Q * SSM_W), BF16)],
        compiler_params=_cp(56, 1),
    )(x, rows, lanes)


def from_groups(xg, name):
    bsz, _, nc, _ = xg.shape
    t_len = nc * Q
    rows, lanes = _perm_consts()

    def body(x_ref, r_ref, p_ref, o_ref, whi_ref, wlo_ref):
        gh = pl.program_id(1)
        for th in range(2):
            inp = jnp.concatenate([x_ref[gl, :, th * 128:(th + 1) * 128] for gl in range(8)], axis=1)
            hi = inp.astype(BF16)
            lo = (inp - hi.astype(F32)).astype(BF16)
            whi_ref[gh, :, th * 1024:(th + 1) * 1024] = _dot(hi, p_ref[...]).astype(BF16)
            wlo_ref[gh, :, th * 1024:(th + 1) * 1024] = _dot(lo, p_ref[...]).astype(BF16)

        @pl.when(gh == GH - 1)
        def _():
            for j in range(t_len // RT):
                def tile(w_ref):
                    return jnp.concatenate(
                        [jnp.concatenate([w_ref[k, j * 16:(j + 1) * 16, t * 128:(t + 1) * 128] for k in range(GH)],
                                         axis=1) for t in range(Q)], axis=0)
                o_ref[j * RT:(j + 1) * RT, :] = _dot(r_ref[...], tile(whi_ref)) + _dot(r_ref[...], tile(wlo_ref))

    return pl.pallas_call(
        body, name=name, grid=(bsz, GH),
        in_specs=[pl.BlockSpec((None, 8, nc, QC), lambda b, k: (b, k, 0, 0)),
                  pl.BlockSpec((RT, RT), lambda b, k: (0, 0)), pl.BlockSpec((1024, 1024), lambda b, k: (0, 0))],
        out_specs=pl.BlockSpec((None, t_len, SSM_W), lambda b, k: (b, 0, 0)),
        out_shape=jax.ShapeDtypeStruct((bsz, t_len, SSM_W), F32),
        scratch_shapes=[pltpu.VMEM((GH, nc, Q * 128), BF16), pltpu.VMEM((GH, nc, Q * 128), BF16)],
        compiler_params=_cp(56, 2),
    )(xg, rows, lanes)


def _gb(shape):
    return pl.BlockSpec((None, None) + shape, lambda g, b: (b, g, 0, 0))


def _gw(shape):
    return pl.BlockSpec((2, None) + shape, lambda g, b: (0, g, 0, 0))


def _gs(nc):
    return pl.BlockSpec((2, None, nc, P2), lambda g, b: (0, b, 0, g))


def s5_chunk_fwd(ug, kt, ws, name):
    bsz, _, nc, _ = ug.shape

    def body(u_ref, kt_ref, ws_ref, y_ref, s_ref):
        ub = u_ref[...]
        y_ref[...] = _dot(ub, kt_ref[0]) + _dot(ub, kt_ref[1])
        s_ref[0] = _dot(ub, ws_ref[0])
        s_ref[1] = _dot(ub, ws_ref[1])

    return pl.pallas_call(
        body, name=name, grid=(G, bsz),
        in_specs=[_gb((nc, QC)), _gw((QC, QC)), _gw((QC, P2))],
        out_specs=[_gb((nc, QC)), _gs(nc)],
        out_shape=[jax.ShapeDtypeStruct((bsz, G, nc, QC), F32), jax.ShapeDtypeStruct((2, bsz, nc, G * P2), F32)],
        compiler_params=_cp(32, 2),
    )(ug, kt, ws)


def s5_scan(s, a1, a2, ncc, reverse, name, hp=None):
    _, bsz, nc, gw = s.shape
    as_rows = lambda v: v.reshape(v.shape[:-1] + (G, P2))
    st = pl.BlockSpec((2, None, nc, SCAN_G, P2), lambda b, w: (0, b, 0, w, 0))
    av = pl.BlockSpec((2, SCAN_G, P2), lambda b, w: (0, w, 0))
    acc = pl.BlockSpec((2, None, SCAN_G, P2), lambda b, w: (0, b, w, 0))
    with_da = hp is not None

    def body(*refs):
        if with_da:
            s_ref, a1_ref, a2_ref, hp_ref, out_ref, da1_ref, da2_ref = refs
        else:
            s_ref, a1_ref, a2_ref, out_ref = refs
        a1v = (a1_ref[0], a1_ref[1])
        a2v = (a2_ref[0], a2_ref[1])
        swap = lambda h: pltpu.roll(h, P, 1)

        def step(j, carry):
            i = nc - 1 - j if reverse else j
            order = (i, jnp.where(i < ncc, ncc - 1 - i, nc - 1 - (i - ncc)))
            hs, da1, da2 = carry
            nh, n1, n2 = [], [], []
            for d, n in enumerate(order):
                h, hw = hs[d]
                out_ref[d, n] = h
                sv = s_ref[d, n]
                nh.append((a1v[d] * h + a2v[d] * hw + sv, a1v[d] * hw - a2v[d] * h + swap(sv)))
                if with_da:
                    hv = hp_ref[d, n]
                    n1.append(da1[d] + h * hv)
                    n2.append(da2[d] + h * swap(hv))
            return tuple(nh), tuple(n1), tuple(n2)

        z = jnp.zeros((SCAN_G, P2), F32)
        zz = (z, z) if with_da else ()
        _, da1, da2 = lax.fori_loop(0, nc, step, (((z, z), (z, z)), zz, zz), unroll=2)
        if with_da:
            for d in range(2):
                da1_ref[d] = da1[d]
                da2_ref[d] = da2[d]

    out_shape = [jax.ShapeDtypeStruct((2, bsz, nc, G, P2), F32)]
    out_specs = [st]
    ins = [as_rows(s), as_rows(a1[:, 0]), as_rows(a2[:, 0])]
    in_specs = [st, av, av]
    if with_da:
        ins.append(as_rows(hp))
        in_specs.append(st)
        out_shape += [jax.ShapeDtypeStruct((2, bsz, G, P2), F32)] * 2
        out_specs += [acc, acc]
    res = pl.pallas_call(
        body, name=name, grid=(bsz, G // SCAN_G), in_specs=in_specs, out_specs=out_specs, out_shape=out_shape,
        compiler_params=_cp(48, 2),
    )(*ins)
    out = res[0].reshape(s.shape)
    return (out, res[1].reshape(2, bsz, gw), res[2].reshape(2, bsz, gw)) if with_da else out


def s5_out_fwd(y1, hp, wo, name):
    bsz, _, nc, _ = y1.shape

    def body(y1_ref, hp_ref, wo_ref, y_ref):
        y_ref[...] = (y1_ref[...] + _dot(hp_ref[0].astype(BF16), wo_ref[0])
                      + _dot(hp_ref[1].astype(BF16), wo_ref[1]))

    return pl.pallas_call(
        body, name=name, grid=(G, bsz),
        in_specs=[_gb((nc, QC)), _gs(nc), _gw((P2, QC))],
        out_specs=_gb((nc, QC)),
        out_shape=jax.ShapeDtypeStruct(y1.shape, F32),
        compiler_params=_cp(32, 2),
    )(y1, hp, wo)


def _acc_init(b, *refs):
    @pl.when(b == 0)
    def _():
        for r in refs:
            r[...] = jnp.zeros_like(r)


def s5_out_bwd(dyg, ug, hp, wo, name):
    bsz, _, nc, _ = dyg.shape

    def body(dy_ref, u_ref, hp_ref, wo_ref, dhp_ref, dwo_ref, dkt_ref):
        _acc_init(pl.program_id(1), dwo_ref, dkt_ref)
        dyb = dy_ref[...]
        for d in range(2):
            dhp_ref[d] = _dot_nt(dyb, wo_ref[d])
            dwo_ref[d] += _dot_tn(hp_ref[d].astype(BF16), dyb)
        dkt_ref[...] += _dot_tn(u_ref[...], dyb)

    return pl.pallas_call(
        body, name=name, grid=(G, bsz),
        in_specs=[_gb((nc, QC)), _gb((nc, QC)), _gs(nc), _gw((P2, QC))],
        out_specs=[_gs(nc), _gw((P2, QC)), pl.BlockSpec((None, QC, QC), lambda g, b: (g, 0, 0))],
        out_shape=[jax.ShapeDtypeStruct(hp.shape, F32), jax.ShapeDtypeStruct((2, G, P2, QC), F32),
                   jax.ShapeDtypeStruct((G, QC, QC), F32)],
        compiler_params=_cp(32, 2),
    )(dyg, ug, hp, wo)


def s5_chunk_bwd(dyg, ug, ds, kt, ws, name):
    bsz, _, nc, _ = dyg.shape

    def body(dy_ref, u_ref, ds_ref, kt_ref, ws_ref, du_ref, dws_ref):
        _acc_init(pl.program_id(1), dws_ref)
        dyb = dy_ref[...]
        du = _dot_nt(dyb, kt_ref[0]) + _dot_nt(dyb, kt_ref[1])
        for d in range(2):
            dsb = ds_ref[d].astype(BF16)
            du += _dot_nt(dsb, ws_ref[d])
            dws_ref[d] += _dot_tn(u_ref[...], dsb)
        du_ref[...] = du

    return pl.pallas_call(
        body, name=name, grid=(G, bsz),
        in_specs=[_gb((nc, QC)), _gb((nc, QC)), _gs(nc), _gw((QC, QC)), _gw((QC, P2))],
        out_specs=[_gb((nc, QC)), _gw((QC, P2))],
        out_shape=[jax.ShapeDtypeStruct(dyg.shape, F32), jax.ShapeDtypeStruct((2, G, QC, P2), F32)],
        compiler_params=_cp(32, 2),
    )(dyg, ug, ds, kt, ws)


def local_step(x, ctx, target, mods, norm_g, final_g, even, odd, late=None, scatter=None):
    bsz, seq, _ = x.shape
    lc = ctx.shape[1]
    t_len = lc + seq
    ncc = lc // Q
    cos, sin = rope_tables(lc, seq)
    h = jnp.concatenate([ctx, x], axis=1)
    ssm_stacked = [jnp.stack([even[0]["ssm"][k], even[1]["ssm"][k]]) for k in range(7)]
    mats_all, mats_vjp = jax.vjp(jax.vmap(s5_mats), *ssm_stacked)
    d_mats = [None, None]
    saved = []
    for i in range(DEPTH):
        j = i // 2
        g = norm_g[i].reshape(1, D)
        if i % 2 == 0:
            w = even[j]
            a, q, kv, g_attn, u, g_ssm = norm_in(h, g, mods[i], w["w_in"], EVEN_SPLITS, f"even_in{j}")
            if i == 0 and late is not None:
                o_attn, lse, *gathered = attn_fwd(q, kv, cos, sin, w["sink"], lc, f"attn_fwd{j}", carry=late[0])
                late[1](gathered)
            else:
                o_attn, lse = attn_fwd(q, kv, cos, sin, w["sink"], lc, f"attn_fwd{j}")
            kt, ws, wo, a1, a2 = (m[j] for m in mats_all)
            kt, ws, wo = kt.astype(BF16), ws.astype(BF16), wo.astype(BF16)
            d_skip = w["ssm"][7].reshape(1, SSM_W)
            ug = to_groups(u, f"u_to_groups{j}")
            y1, s = s5_chunk_fwd(ug, kt, ws, f"s5_chunk_fwd{j}")
            hp = s5_scan(s, a1, a2, ncc, False, f"s5_scan_fwd{j}")
            y_scan = from_groups(s5_out_fwd(y1, hp, wo, f"s5_out_fwd{j}"), f"y_from_groups{j}")
            h_new, mix, yout = even_out(h, mods[i], o_attn, g_attn, y_scan, u, d_skip, g_ssm, w["glu_w"], w["glu_b"],
                                        w["w_out"], f"even_out{j}")
            saved.append(dict(h=h, a=a, q=q, kv=kv, g_attn=g_attn, g_ssm=g_ssm, o_attn=o_attn, lse=lse, ug=ug, u=u,
                              hp=hp, y_scan=y_scan, mix=mix, yout=yout, mats=(kt, ws, wo, a1, a2), d_skip=d_skip))
        else:
            w = odd[j]
            a, u, gate = norm_in(h, g, mods[i], w["w_in"], ODD_SPLITS, f"odd_in{j}")
            pm = pool_band(u, lc, False, f"pool_band_fwd{j}")
            h_new, mix, yout = pool_out(h, mods[i], pm, gate, w["pool_w"], w["pool_scale"], w["w_out"], f"pool_out{j}")
            saved.append(dict(h=h, a=a, pm=pm, gate=gate, mix=mix, yout=yout))
        h = h_new

    dh, loss_acc, dfg = loss_head(h, final_g.reshape(1, D), target, "loss_head")
    grads = dict(final_g=dfg[0], norm_g=[None] * DEPTH, even=[None, None], odd=[None, None])
    dmods = [None] * DEPTH
    rows = bsz * t_len
    flat = lambda v: v.reshape(rows, v.shape[-1])
    for i in reversed(range(DEPTH)):
        j = i // 2
        sv = saved[i]
        g = norm_g[i].reshape(1, D)
        if i % 2 == 0:
            w = even[j]
            kt, ws, wo, a1, a2 = sv["mats"]
            (d_oattn, d_gattn, d_gssm, d_yssm, dyout, zz, dsg, dgate, dglu_b) = even_out_bwd(
                dh, mods[i], sv["o_attn"], sv["g_attn"], sv["y_scan"], sv["u"], sv["d_skip"], sv["g_ssm"], w["glu_w"],
                w["glu_b"], w["w_out"], sv["yout"], f"even_out_bwd{j}")
            g_w_out = matmul_tn(flat(sv["mix"]), flat(dyout), D, D, f"even_w_out_grad{j}")
            g_glu_w = matmul_tn(flat(zz), flat(dsg), SSM_W, SSM_W, f"glu_w_grad{j}")
            carry = scatter(grads) if i == 0 and scatter is not None else ()
            dq, dkv, dsink, *grads["landed"] = attn_bwd(sv["q"], sv["kv"], sv["o_attn"], sv["lse"], d_oattn, cos, sin,
                                                        w["sink"], lc, f"attn_bwd{j}", carry=carry)
            dyg = to_groups(d_yssm, f"dy_to_groups{j}")
            dhp, dwo, dkt = s5_out_bwd(dyg, sv["ug"], sv["hp"], wo, f"s5_out_bwd{j}")
            ds, da1, da2 = s5_scan(dhp, a1, -a2, ncc, True, f"s5_scan_bwd{j}", hp=sv["hp"])
            dug, dws = s5_chunk_bwd(dyg, sv["ug"], ds, kt, ws, f"s5_chunk_bwd{j}")
            dkt2 = jnp.stack([dkt, dkt])
            da1 = da1.sum(axis=1).reshape(2, 1, G * P2)
            da2 = da2.sum(axis=1).reshape(2, 1, G * P2)
            d_mats[j] = (dkt2, dws, dwo, da1, da2)
            dparts = [dq, dkv, d_gattn, from_groups(dug, f"du_from_groups{j}"), d_gssm]
            dh, dz, dmod, dg = norm_in_bwd(dparts, dh, sv["h"], g, mods[i], w["w_in"], f"even_in_bwd{j}",
                                           skip=(3, d_yssm, sv["d_skip"]))
            g_w_in = matmul_tn(flat(sv["a"]), flat(dz), D, dz.shape[-1], f"even_w_in_grad{j}")
            grads["even"][j] = dict(w_in=g_w_in, w_out=g_w_out, sink=dsink[0], d_skip=dglu_b[1], glu_w=g_glu_w,
                                    glu_b=dglu_b[0])
        else:
            w = odd[j]
            dpm, dgt, dyout, dpp, dgate, dps = pool_out_bwd(dh, mods[i], sv["pm"], sv["gate"], w["pool_w"],
                                                            w["pool_scale"], w["w_out"], sv["yout"],
                                                            f"pool_out_bwd{j}")
            g_w_out = matmul_tn(flat(sv["mix"]), flat(dyout), D, D, f"odd_w_out_grad{j}")
            g_pool_w = jnp.stack([matmul_tn(flat(sv["pm"]), flat(dpp), POOL_G, POOL_G, f"pool_w_grad{j}_{gi}",
                                            a_col=gi, b_col=gi) for gi in range(4)])
            du = pool_band(dpm, lc, True, f"pool_band_bwd{j}")
            dh, dz, dmod, dg = norm_in_bwd([du, dgt], dh, sv["h"], g, mods[i], w["w_in"], f"odd_in_bwd{j}")
            g_w_in = matmul_tn(flat(sv["a"]), flat(dz), D, dz.shape[-1], f"odd_w_in_grad{j}")
            grads["odd"][j] = dict(w_in=g_w_in, w_out=g_w_out, pool_w=g_pool_w, pool_scale=dps[0])
        grads["norm_g"][i] = dg[0]
        dmods[i] = jnp.concatenate([dmod[:, :, 0:2, :], dgate[:, :, 0:1, :]], axis=2)
    g_ssm = mats_vjp(tuple(jnp.stack([d_mats[0][k], d_mats[1][k]]) for k in range(5)))
    for j in range(2):
        grads["even"][j]["ssm"] = tuple(gk[j] for gk in g_ssm) + (grads["even"][j].pop("d_skip"),)
    return loss_acc[0, 0], dh[:, lc:, :], dmods, grads


N_DEV = 8
HBM_SPEC = pl.BlockSpec(memory_space=pltpu.HBM)


def allgather8(x_shard, name):
    m_per, n = x_shard.shape

    def body(x_ref, out_ref, send_sems, recv_sems, local_sem):
        x, y, c = lax.axis_index("x"), lax.axis_index("y"), lax.axis_index("c")
        me, sibling = (x, y, c), (x, y, 1 - c)
        chips = [(1 - x, y), (x, 1 - y), (1 - x, 1 - y)]

        def rows(px, py, pc):
            return out_ref.at[pl.ds((4 * px + 2 * py + pc) * m_per, m_per), :]

        def copy(k, block, to, src=None):
            return pltpu.make_async_remote_copy(
                src_ref=rows(*block) if src is None else src, dst_ref=rows(*block),
                send_sem=send_sems.at[k], recv_sem=recv_sems.at[k], device_id=to, device_id_type=MESH)

        mine = pltpu.make_async_copy(x_ref, rows(*me), local_sem)
        mine.start()
        first = [copy(0, me, sibling, src=x_ref)]
        first += [copy(1 + j, me, (*chip, c), src=x_ref) for j, chip in enumerate(chips)]
        for cp in first:
            cp.start()
        passed = [copy(4 + j, (*chip, c), sibling) for j, chip in enumerate(chips)]
        for j, chip in enumerate(chips):
            copy(1 + j, (*chip, c), me).wait_recv()
            passed[j].start()
        copy(0, sibling, me).wait_recv()
        for j, chip in enumerate(chips):
            copy(4 + j, (*chip, 1 - c), me).wait_recv()
        for cp in first + passed:
            cp.wait_send()
        mine.wait()

    return pl.pallas_call(
        body, name=name,
        out_shape=jax.ShapeDtypeStruct((N_DEV * m_per, n), x_shard.dtype),
        in_specs=[pl.BlockSpec(memory_space=pltpu.VMEM)],
        out_specs=pl.BlockSpec(memory_space=pltpu.VMEM),
        scratch_shapes=[pltpu.SemaphoreType.DMA((7,)), pltpu.SemaphoreType.DMA((7,)), pltpu.SemaphoreType.DMA],
        compiler_params=_cp(56),
    )(x_shard)


def xy_exchange(srcs, scatter, name):
    n = len(srcs)

    def body(*refs):
        start, wait = _xy_copies(refs[:n], refs[n:2 * n], *refs[2 * n:], scatter)
        start()
        wait()

    return pl.pallas_call(
        body, name=name, out_shape=_xy_out_shapes(srcs, scatter),
        in_specs=[HBM_SPEC] * n, out_specs=[HBM_SPEC] * n, scratch_shapes=_xy_sems(n),
    )(*srcs)


def _xy_out_shapes(srcs, scatter):
    return [jax.ShapeDtypeStruct((4,) + (tuple(s.shape[1:]) if scatter else tuple(s.shape)), s.dtype) for s in srcs]


def _xy_sems(n):
    return [pltpu.SemaphoreType.DMA((3 * n,)), pltpu.SemaphoreType.DMA((3 * n,)), pltpu.SemaphoreType.DMA((n,))]


def _xy_copies(src_refs, out_refs, send_sems, recv_sems, local_sems, scatter):
    n = len(src_refs)

    def parts():
        x, y, c = lax.axis_index("x"), lax.axis_index("y"), lax.axis_index("c")
        my = 2 * x + y
        peers = [(1 - x, y), (x, 1 - y), (1 - x, 1 - y)]

        def piece(i, pos):
            return src_refs[i].at[pos] if scatter else src_refs[i]

        def copy(i, k, src_pos, dst_pos):
            px, py = peers[k]
            return pltpu.make_async_remote_copy(
                src_ref=piece(i, src_pos), dst_ref=out_refs[i].at[dst_pos], send_sem=send_sems.at[3 * i + k],
                recv_sem=recv_sems.at[3 * i + k], device_id=(px, py, c), device_id_type=MESH)

        local = [pltpu.make_async_copy(piece(i, my), out_refs[i].at[my], local_sems.at[i]) for i in range(n)]
        sends = [copy(i, k, 2 * px + py, my) for i in range(n) for k, (px, py) in enumerate(peers)]
        lands = [copy(i, k, my, 2 * px + py) for i in range(n) for k, (px, py) in enumerate(peers)]
        return local, sends, lands

    def start():
        local, sends, _ = parts()
        for cp in local + sends:
            cp.start()

    def wait():
        local, sends, lands = parts()
        for cp in lands:
            cp.wait_recv()
        for cp in sends:
            cp.wait_send()
        for cp in local:
            cp.wait()

    return start, wait


def sibling_exchange(srcs, name):
    n = len(srcs)

    def body(*refs):
        src_refs, out_refs = refs[:n], refs[n:2 * n]
        send_sems, recv_sems = refs[2 * n:]
        peer = (lax.axis_index("x"), lax.axis_index("y"), 1 - lax.axis_index("c"))
        cps = [pltpu.make_async_remote_copy(src_ref=src_refs[i], dst_ref=out_refs[i], send_sem=send_sems.at[i],
                                            recv_sem=recv_sems.at[i], device_id=peer, device_id_type=MESH)
               for i in range(n)]
        for cp in cps:
            cp.start()
        for cp in cps:
            cp.wait()

    return pl.pallas_call(
        body, name=name, out_shape=[jax.ShapeDtypeStruct(s.shape, s.dtype) for s in srcs],
        in_specs=[HBM_SPEC] * n, out_specs=[HBM_SPEC] * n,
        scratch_shapes=[pltpu.SemaphoreType.DMA((n,)), pltpu.SemaphoreType.DMA((n,))],
    )(*srcs)


def _row_tile(rows, bytes_per_row, limit):
    best = None
    for tr in range(8, rows + 1, 8):
        if rows % tr == 0 and tr * bytes_per_row <= limit:
            best = tr
    return best if best is not None else rows


def sum_slots(x, name):
    n, rows, cols = x.shape
    tr = _row_tile(rows, n * cols * 4, 4 * MB)

    def body(x_ref, o_ref):
        acc = x_ref[0].astype(F32)
        for k in range(1, n):
            acc = acc + x_ref[k].astype(F32)
        o_ref[...] = acc

    return pl.pallas_call(
        body, name=name, grid=(rows // tr,),
        in_specs=[pl.BlockSpec((n, tr, cols), lambda r: (0, r, 0))],
        out_specs=pl.BlockSpec((tr, cols), lambda r: (r, 0)),
        out_shape=jax.ShapeDtypeStruct((rows, cols), F32),
        compiler_params=_cp(32, 1),
    )(x)


ADA_COLS = 3 * D // 4
C_ROWS = 8


def ada_fwd(c_all, ada_w, ada_b_cols, name):
    nrow = c_all.shape[0]

    def body(c_ref, w_ref, b_ref, o_ref):
        s, _ = _silu_and_grad(c_ref[...])
        o_ref[...] = _dot(s.astype(BF16), w_ref[...].astype(BF16)) + b_ref[...]

    return pl.pallas_call(
        body, name=name, grid=(DEPTH,),
        in_specs=[pl.BlockSpec((nrow, D), lambda i: (0, 0)), pl.BlockSpec((None, D, ADA_COLS), lambda i: (i, 0, 0)),
                  pl.BlockSpec((None, 1, ADA_COLS), lambda i: (i, 0, 0))],
        out_specs=pl.BlockSpec((None, nrow, ADA_COLS), lambda i: (i, 0, 0)),
        out_shape=jax.ShapeDtypeStruct((DEPTH, nrow, ADA_COLS), F32),
        compiler_params=_cp(32, 1),
    )(c_all, ada_w, ada_b_cols)


def ada_bwd(c_all, d_cols, ada_w, name):
    nrow = c_all.shape[0]

    def body(c_ref, d_ref, w_ref, gw_ref, ds_ref):
        @pl.when(pl.program_id(0) == 0)
        def _():
            ds_ref[...] = jnp.zeros_like(ds_ref)
        s, _ = _silu_and_grad(c_ref[...])
        dl = d_ref[...]
        gw_ref[...] = _dot_tn(s.astype(BF16), dl.astype(BF16))
        rid = lax.broadcasted_iota(jnp.int32, (nrow, 1), 0) % C_ROWS
        dctx = jnp.where((rid == 2) | (rid == 3), dl, 0.0).astype(BF16)
        ds_ref[0:1, :] += _rowsum(_dot_nt(dctx, w_ref[...].astype(BF16)))

    return pl.pallas_call(
        body, name=name, grid=(DEPTH,),
        in_specs=[pl.BlockSpec((nrow, D), lambda i: (0, 0)), pl.BlockSpec((None, nrow, ADA_COLS), lambda i: (i, 0, 0)),
                  pl.BlockSpec((None, D, ADA_COLS), lambda i: (i, 0, 0))],
        out_specs=[pl.BlockSpec((None, D, ADA_COLS), lambda i: (i, 0, 0)), pl.BlockSpec((8, D), lambda i: (0, 0))],
        out_shape=[jax.ShapeDtypeStruct((DEPTH, D, ADA_COLS), F32), jax.ShapeDtypeStruct((8, D), F32)],
        compiler_params=_cp(32, 1),
    )(c_all, d_cols, ada_w)


def ada_bias_grad(d_all, name):
    nrow = d_all.shape[1]

    def body(d_ref, o_ref):
        o_ref[...] = jnp.broadcast_to(_rowsum(d_ref[...]), o_ref.shape)

    return pl.pallas_call(
        body, name=name, grid=(DEPTH,),
        in_specs=[pl.BlockSpec((None, nrow, 3 * D), lambda i: (i, 0, 0))],
        out_specs=pl.BlockSpec((None, 8, 3 * D), lambda i: (i, 0, 0)),
        out_shape=jax.ShapeDtypeStruct((DEPTH, 8, 3 * D), F32),
        compiler_params=_cp(32, 1),
    )(d_all)


def silu_chain(ds, c, name):
    def body(ds_ref, c_ref, o_ref):
        _, dsl = _silu_and_grad(c_ref[...])
        o_ref[...] = ds_ref[...] * dsl

    return pl.pallas_call(body, name=name, out_shape=jax.ShapeDtypeStruct(ds.shape, F32))(ds, c)


def _flat_cols(shape):
    size = int(np.prod(shape))
    if shape[-1] >= 128:
        return shape[-1]
    for cols in (1024, 128):
        if size % cols == 0:
            return cols
    return shape[-1]


def adamw(w, m, v, grads, name):
    shape = w.shape
    cols = _flat_cols(shape)
    as2d = lambda a: a.reshape(-1, cols)
    rows = w.size // cols
    tr = _row_tile(rows, cols * 4, MB)
    k = len(grads)

    def body(*refs):
        w_ref, m_ref, v_ref = refs[:3]
        g_refs = refs[3:3 + k]
        g_out, d_out, m_out, v_out = refs[3 + k:]
        g = g_refs[0][...]
        for r in g_refs[1:]:
            g = g + r[...]
        g_out[...] = g
        mn = ADAM_B1 * m_ref[...] + (1.0 - ADAM_B1) * g
        vn = ADAM_B2 * v_ref[...] + (1.0 - ADAM_B2) * (g * g)
        m_out[...] = mn
        v_out[...] = vn
        m_hat = mn / (1.0 - ADAM_B1 ** ADAM_STEP)
        v_hat = vn / (1.0 - ADAM_B2 ** ADAM_STEP)
        d_out[...] = -ADAM_LR * (m_hat / (jnp.sqrt(v_hat) + ADAM_EPS) + ADAM_WD * w_ref[...])

    spec = pl.BlockSpec((tr, cols), lambda r: (r, 0))
    outs = pl.pallas_call(
        body, name=name, grid=(rows // tr,),
        in_specs=[spec] * (3 + k), out_specs=[spec] * 4,
        out_shape=[jax.ShapeDtypeStruct((rows, cols), F32)] * 4,
        compiler_params=_cp(32, 1),
    )(as2d(w), as2d(m), as2d(v), *[as2d(g) for g in grads])
    return tuple(o.reshape(shape) for o in outs)


BIG = (("even_w_in", (2, D, 576), 2), ("even_w_out", (2, 256, D), 1), ("glu_w", (2, 128, SSM_W), 1),
       ("odd_w_in", (2, D, 512), 2), ("odd_w_out", (2, 256, D), 1), ("pool_w", (2, 4, 64, POOL_G), 2))


def _full_shape(shard, axis):
    return tuple(4 * s if a == axis else s for a, s in enumerate(shard))


def _to_shards(full, shard, axis):
    return jnp.moveaxis(full.reshape(shard[:axis] + (4,) + shard[axis:]), axis, 0)


def _from_shards(stacked, shard, axis):
    return jnp.moveaxis(stacked, 0, axis).reshape(_full_shape(shard, axis))


GRAD_KEY = dict(even_w_in=("even", "w_in"), even_w_out=("even", "w_out"), glu_w=("even", "glu_w"),
                odd_w_in=("odd", "w_in"), odd_w_out=("odd", "w_out"), pool_w=("odd", "pool_w"))
RIDE = tuple((n, 1) for n, _, _ in BIG[:3]) + tuple((n, j) for n, _, _ in BIG[3:] for j in range(2))
LAST = tuple((n, 0) for n, _, _ in BIG[:3])


def _grad_piece(grads, name, j):
    kind, key = GRAD_KEY[name]
    shard, axis = next((s, a) for n, s, a in BIG if n == name)
    return _to_shards(grads[kind][j][key], shard[1:], axis - 1).astype(BF16)


SMALL = (("ds_ctx", (D,)), ("norm_g", (DEPTH, D)), ("final_g", (D,)), ("attn_sink", (2, N_HEADS)),
         ("ssm_a_re", (2, 2, G, P)), ("ssm_a_im", (2, 2, G, P)), ("ssm_log_dt", (2, 2, G)),
         ("ssm_b_re", (2, 2, G, P, C)), ("ssm_b_im", (2, 2, G, P, C)), ("ssm_c_re", (2, 2, G, C, P)),
         ("ssm_c_im", (2, 2, G, C, P)), ("ssm_d", (2, SSM_W)), ("glu_b", (2, SSM_W)), ("pool_scale", (2, D)))
SMALL_PAD = 8 * 128


def pack_small(vals):
    flat = jnp.concatenate([vals[n].reshape(-1) for n, _ in SMALL])
    pad = (-flat.shape[0]) % SMALL_PAD
    return jnp.pad(flat, (0, pad)).reshape(-1, 128)


def unpack_small(packed):
    flat, out, off = packed.reshape(-1), {}, 0
    for n, shape in SMALL:
        size = int(np.prod(shape))
        out[n] = flat[off:off + size].reshape(shape)
        off += size
    return out


WEIGHT_NAMES = ('c_ctx', 'ada_w', 'ada_b', 'norm_g', 'even_w_in', 'even_w_out', 'attn_sink', 'ssm_a_re', 'ssm_a_im',
                'ssm_log_dt', 'ssm_b_re', 'ssm_b_im', 'ssm_c_re', 'ssm_c_im', 'ssm_d', 'glu_w', 'glu_b', 'odd_w_in',
                'odd_w_out', 'pool_w', 'pool_scale', 'final_g')
SSM_NAMES = ('ssm_a_re', 'ssm_a_im', 'ssm_log_dt', 'ssm_b_re', 'ssm_b_im', 'ssm_c_re', 'ssm_c_im', 'ssm_d')


def kernel(x, c, ctx, c_ctx, ada_w, ada_b, norm_g, even_w_in, even_w_out, attn_sink, ssm_a_re, ssm_a_im, ssm_log_dt, ssm_b_re, ssm_b_im, ssm_c_re, ssm_c_im, ssm_d, glu_w, glu_b, odd_w_in, odd_w_out, pool_w, pool_scale, final_g, loss_target, m_c_ctx, m_ada_w, m_ada_b, m_norm_g, m_even_w_in, m_even_w_out, m_attn_sink, m_ssm_a_re, m_ssm_a_im, m_ssm_log_dt, m_ssm_b_re, m_ssm_b_im, m_ssm_c_re, m_ssm_c_im, m_ssm_d, m_glu_w, m_glu_b, m_odd_w_in, m_odd_w_out, m_pool_w, m_pool_scale, m_final_g, v_c_ctx, v_ada_w, v_ada_b, v_norm_g, v_even_w_in, v_even_w_out, v_attn_sink, v_ssm_a_re, v_ssm_a_im, v_ssm_log_dt, v_ssm_b_re, v_ssm_b_im, v_ssm_c_re, v_ssm_c_im, v_ssm_d, v_glu_w, v_glu_b, v_odd_w_in, v_odd_w_out, v_pool_w, v_pool_scale, v_final_g):
    env = dict(locals())
    weights = {n: env[n] for n in WEIGHT_NAMES}
    bsz = x.shape[0]
    ax, ay, ac = lax.axis_index("x"), lax.axis_index("y"), lax.axis_index("c")
    pos = 2 * ax + ay
    dev = 2 * pos + ac

    c_rows = jnp.concatenate([c, c_ctx.reshape(1, D), c_ctx.reshape(1, D), jnp.zeros((C_ROWS - bsz - 2, D), F32)])
    c_all = allgather8(c_rows, "gather_c")
    ada_b_cols = lax.dynamic_slice(ada_b, (0, pos * ADA_COLS), (DEPTH, ADA_COLS)).reshape(DEPTH, 1, ADA_COLS)
    mod_cols = ada_fwd(c_all, ada_w, ada_b_cols, "ada_fwd")
    nrow = N_DEV * C_ROWS
    misc = jnp.concatenate([mod_cols.reshape(DEPTH * nrow, ADA_COLS),
                            jnp.pad(pool_scale, ((0, 6), (0, ADA_COLS - pool_scale.shape[1])))])
    misc_all = allgather8(misc, "gather_mod").reshape(4, 2, DEPTH * nrow + 8, ADA_COLS)[:, 0]
    mod_full = misc_all[:, :DEPTH * nrow].reshape(4, DEPTH, nrow, ADA_COLS).transpose(1, 2, 0, 3)
    mod_mine = lax.dynamic_slice(mod_full.reshape(DEPTH, nrow, 3 * D), (0, dev * C_ROWS, 0), (DEPTH, C_ROWS, 3 * D))
    mods = []
    for i in range(DEPTH):
        lat = mod_mine[i, :bsz].reshape(bsz, 1, 3, D)
        con = jnp.broadcast_to(mod_mine[i, bsz].reshape(1, 1, 3, D), (bsz, 1, 3, D))
        mods.append(jnp.pad(jnp.concatenate([con, lat], axis=1), ((0, 0), (0, 0), (0, 5), (0, 0))))
    pool_scale_full = misc_all[:, DEPTH * nrow:DEPTH * nrow + 2, :pool_scale.shape[1]].transpose(1, 0, 2).reshape(2, D)

    first, shard, axis = BIG[0]
    w_in_full = _from_shards(xy_exchange([weights[first].astype(BF16)], False, "gather_w_in")[0], shard, axis)
    even = [dict(w_in=w_in_full[j], sink=attn_sink[j], ssm=tuple(weights[n][j] for n in SSM_NAMES),
                 glu_b=glu_b[j].reshape(1, SSM_W)) for j in range(2)]
    odd = [dict(pool_scale=pool_scale_full[j].reshape(1, D)) for j in range(2)]

    def fill(gathered):
        full = {n: _from_shards(g, shard, axis) for (n, shard, axis), g in zip(BIG[1:], gathered)}
        for j in range(2):
            even[j].update(w_out=full["even_w_out"][j], glu_w=full["glu_w"][j])
            odd[j].update(w_in=full["odd_w_in"][j], w_out=full["odd_w_out"][j], pool_w=full["pool_w"][j])

    late = ([weights[n].astype(BF16) for n, _, _ in BIG[1:]], fill)
    loss_local, grad_x, dmods, grads = local_step(
        x, ctx, loss_target, mods, norm_g, final_g, even, odd, late,
        scatter=lambda grads: [_grad_piece(grads, n, j) for n, j in RIDE])
    loss = lax.psum(loss_local, ("x", "y", "c"))

    d_rows = jnp.stack([jnp.concatenate([dm[:, 1].reshape(bsz, 3 * D), dm[:, 0].reshape(bsz, 3 * D),
                                         jnp.zeros((C_ROWS - 2 * bsz, 3 * D), F32)]) for dm in dmods])
    d_all = allgather8(d_rows.reshape(DEPTH * C_ROWS, 3 * D), "gather_dmod")
    d_all = d_all.reshape(N_DEV, DEPTH, C_ROWS, 3 * D).transpose(1, 0, 2, 3).reshape(DEPTH, nrow, 3 * D)
    d_cols = lax.dynamic_slice(d_all, (0, 0, pos * ADA_COLS), (DEPTH, nrow, ADA_COLS))
    g_ada_w, ds_ctx = ada_bwd(c_all, d_cols, ada_w, "ada_bwd")
    g_ada_b = ada_bias_grad(d_all, "ada_bias_grad")[:, 0]

    small = dict(ds_ctx=ds_ctx[0] * (ac == 0).astype(F32), norm_g=jnp.stack(grads["norm_g"]), final_g=grads["final_g"],
                 attn_sink=jnp.stack([grads["even"][j]["sink"] for j in range(2)]),
                 glu_b=jnp.stack([grads["even"][j]["glu_b"] for j in range(2)]),
                 pool_scale=jnp.stack([grads["odd"][j]["pool_scale"] for j in range(2)]))
    for k, n in enumerate(SSM_NAMES):
        small[n] = jnp.stack([grads["even"][j]["ssm"][k] for j in range(2)])
    packed = pack_small(small)
    small_sum = sum_slots(allgather8(packed, "gather_small").reshape(N_DEV, packed.shape[0], 128), "sum_small")
    g_small = unpack_small(small_sum)
    g_small["c_ctx"] = silu_chain(g_small.pop("ds_ctx").reshape(1, D), c_ctx.reshape(1, D), "c_ctx_grad").reshape(D)
    g_small["ada_b"] = g_ada_b
    g_small["pool_scale"] = lax.dynamic_slice(g_small["pool_scale"], (0, pos * 256), (2, 256))

    landed = dict(zip(RIDE, grads["landed"]))
    landed.update(zip(LAST, xy_exchange([_grad_piece(grads, n, j) for n, j in LAST], True, "scatter_grads")))
    mine4 = [jnp.stack([sum_slots(landed[n, j].reshape(4, -1, shard[-1]), f"sum_positions_{n}{j}").reshape(shard[1:])
                        for j in range(2)]) for n, shard, _ in BIG]
    other4 = sibling_exchange(mine4, "swap_cores")
    g_mine = dict(zip([n for n, _, _ in BIG], mine4))
    g_other = dict(zip([n for n, _, _ in BIG], other4))

    results = {}
    for n in WEIGHT_NAMES:
        if n in g_mine:
            gs = [g_mine[n], g_other[n]]
        elif n == "ada_w":
            gs = [g_ada_w]
        else:
            gs = [g_small[n]]
        results[n] = adamw(weights[n], env["m_" + n], env["v_" + n], gs, "adamw_" + n)
    outs = [loss, grad_x]
    for k in range(4):
        outs += [results[n][k] for n in WEIGHT_NAMES]
    return tuple(outs)
```

```python
import functools

import numpy as np
import jax
import jax.numpy as jnp
from jax import lax
from jax.experimental import pallas as pl
from jax.experimental.pallas import tpu as pltpu

F32 = jnp.float32
BF16 = jnp.bfloat16
MESH = pl.DeviceIdType.MESH

D = 1024
DEPTH = 4
EPS = 1e-6
NEG_INF = -1e30
GRID_W = 64
ROPE_BASE = 10000.0
ROPE_FREQS = 16
HEAD_DIM = 64
N_HEADS = 8
N_KV = 2
GROUP = 4
ATTN_W = N_HEADS * HEAD_DIM
KV_W = N_KV * HEAD_DIM
WINDOW = 128
AB = 128
SSM_W = 512
G = 32
C = 16
P = 64
Q = 16
QC = Q * C
P2 = 2 * P
SCAN_G = 16
POOL_R = (1, 2, 4, 8)
POOL_G = 256
HALO = 8
TM = 256
EVEN_SPLITS = (512, 256, 512, 512, 512)
ODD_SPLITS = (1024, 1024)

ADAM_LR = 0.001
ADAM_B1 = 0.9
ADAM_B2 = 0.999
ADAM_EPS = 1e-08
ADAM_WD = 0.01
ADAM_STEP = 10

MB = 1024 * 1024


def _cp(vmem_mb=48, n_axes=0):
    kw = dict(vmem_limit_bytes=vmem_mb * MB)
    if n_axes:
        kw["dimension_semantics"] = ("arbitrary",) * n_axes
    return pltpu.CompilerParams(**kw)


def _sig(x):
    return 1.0 / (1.0 + jnp.exp(-x))


def _silu_and_grad(x):
    s = _sig(x)
    return x * s, s * (1.0 + x * (1.0 - s))


_GELU_C = 0.7978845608028654
_GELU_A = 0.044715


def _gelu_and_grad(x):
    th = jnp.tanh(_GELU_C * (x + _GELU_A * x * x * x))
    val = 0.5 * x * (1.0 + th)
    grad = 0.5 * (1.0 + th) + 0.5 * x * (1.0 - th * th) * _GELU_C * (1.0 + 3.0 * _GELU_A * x * x)
    return val, grad


def _rms(h):
    r = lax.rsqrt(jnp.mean(h * h, axis=-1, keepdims=True) + EPS)
    return h * r, r


def _dot(a, b):
    return jnp.dot(a, b, preferred_element_type=F32)


def _dot_nt(a, b):
    return lax.dot_general(a, b, (((1,), (1,)), ((), ())), preferred_element_type=F32)


def _dot_tn(a, b):
    return lax.dot_general(a, b, (((0,), (0,)), ((), ())), preferred_element_type=F32)


def _rowsum(x):
    return jnp.sum(x, axis=0, keepdims=True)


def _seg(t):
    return jnp.minimum(t, 1)


def _row_spec(n):
    return pl.BlockSpec((None, TM, n), lambda b, t: (b, t, 0))


def _const_spec(shape):
    nd = len(shape)
    return pl.BlockSpec(shape, lambda b, t: (0,) * nd)


def _mod_spec():
    return pl.BlockSpec((None, None, 8, D), lambda b, t: (b, _seg(t), 0, 0))


def norm_in(h, g, mod, w, splits, name):
    bsz, t_len, _ = h.shape
    n = w.shape[1]
    offs = [int(v) for v in np.cumsum((0,) + tuple(splits))]

    def body(h_ref, g_ref, mod_ref, w_ref, a_ref, *outs):
        xh, _ = _rms(h_ref[...])
        a = xh * g_ref[...] * (1.0 + mod_ref[1:2, :]) + mod_ref[0:1, :]
        ab = a.astype(BF16)
        a_ref[...] = ab
        z = _dot(ab, w_ref[...])
        for o, lo, hi in zip(outs, offs[:-1], offs[1:]):
            o[...] = z[:, lo:hi]

    return pl.pallas_call(
        body, name=name, grid=(bsz, t_len // TM),
        in_specs=[_row_spec(D), _const_spec((1, D)), _mod_spec(), _const_spec((D, n))],
        out_specs=[_row_spec(D)] + [_row_spec(s) for s in splits],
        out_shape=[jax.ShapeDtypeStruct((bsz, t_len, D), BF16)]
        + [jax.ShapeDtypeStruct((bsz, t_len, s), F32) for s in splits],
        compiler_params=_cp(48, 2),
    )(h, g, mod, w)


def norm_in_bwd(dparts, dh_in, h, g, mod, w, name, skip=None):
    bsz, t_len, _ = h.shape
    n = w.shape[1]
    k = len(dparts)
    extra = [] if skip is None else [skip[1], skip[2]]

    def body(*refs):
        parts = [r[...] for r in refs[:k]]
        if skip is not None:
            parts[skip[0]] = parts[skip[0]] + refs[k][...] * refs[k + 1][...]
        dh_in_ref, h_ref, g_ref, mod_ref, w_ref, dh_ref, dz_ref, dmod_ref, dg_ref = refs[k + len(extra):]
        b, t = pl.program_id(0), pl.program_id(1)
        dz = jnp.concatenate(parts, axis=1).astype(BF16)
        dz_ref[...] = dz
        da = _dot_nt(dz, w_ref[...])
        xh, r = _rms(h_ref[...])
        gg = g_ref[...]
        sc1 = 1.0 + mod_ref[1:2, :]

        @pl.when(t <= 1)
        def _():
            dmod_ref[...] = jnp.zeros_like(dmod_ref)

        @pl.when((b == 0) & (t == 0))
        def _():
            dg_ref[...] = jnp.zeros_like(dg_ref)

        dmod_ref[0:1, :] += _rowsum(da)
        dmod_ref[1:2, :] += _rowsum(da * (xh * gg))
        dg_ref[0:1, :] += _rowsum(da * sc1 * xh)
        dxh = da * gg * sc1
        dh_ref[...] = dh_in_ref[...] + r * (dxh - xh * jnp.mean(dxh * xh, axis=-1, keepdims=True))

    return pl.pallas_call(
        body, name=name, grid=(bsz, t_len // TM),
        in_specs=[_row_spec(p.shape[-1]) for p in dparts]
        + ([_row_spec(extra[0].shape[-1]), _const_spec(extra[1].shape)] if extra else [])
        + [_row_spec(D), _row_spec(D), _const_spec((1, D)), _mod_spec(), _const_spec((D, n))],
        out_specs=[_row_spec(D), _row_spec(n), _mod_spec(), _const_spec((8, D))],
        out_shape=[jax.ShapeDtypeStruct((bsz, t_len, D), F32), jax.ShapeDtypeStruct((bsz, t_len, n), BF16),
                   jax.ShapeDtypeStruct((bsz, 2, 8, D), F32), jax.ShapeDtypeStruct((8, D), F32)],
        compiler_params=_cp(56, 2),
    )(*dparts, *extra, dh_in, h, g, mod, w)


def matmul_tn(a, b, m, n, name, a_col=0, b_col=0):
    rows = a.shape[0]
    tr = 512 if rows % 512 == 0 else rows
    tn = n
    for cand in (1024, 768, 512, 256, 128):
        if n > 1024 and n % cand == 0:
            tn = cand
            break
    nb = n // tn

    def body(a_ref, b_ref, o_ref):
        @pl.when(pl.program_id(1) == 0)
        def _():
            o_ref[...] = jnp.zeros_like(o_ref)
        o_ref[...] += _dot_tn(a_ref[...].astype(BF16), b_ref[...].astype(BF16))

    return pl.pallas_call(
        body, name=name, grid=(nb, rows // tr),
        in_specs=[pl.BlockSpec((tr, m), lambda j, r: (r, a_col)),
                  pl.BlockSpec((tr, tn), lambda j, r: (r, b_col * nb + j))],
        out_specs=pl.BlockSpec((m, tn), lambda j, r: (0, j)),
        out_shape=jax.ShapeDtypeStruct((m, n), F32),
        compiler_params=_cp(48, 2),
    )(a, b)


def even_out(h, mod, o_attn, g_attn, y_scan, u, d_skip, g_ssm, glu_w, glu_b, w_out, name):
    bsz, t_len, _ = h.shape

    def body(h_ref, mod_ref, oa_ref, ga_ref, ys_ref, u_ref, dk_ref, gs_ref, gw_ref, gb_ref, wo_ref,
             hn_ref, mix_ref, yo_ref):
        zz, _ = _gelu_and_grad(ys_ref[...] + u_ref[...] * dk_ref[...])
        s = _dot(zz.astype(BF16), gw_ref[...]) + gb_ref[...]
        o_ssm = zz * _sig(s)
        sa, _ = _silu_and_grad(ga_ref[...])
        ss, _ = _silu_and_grad(gs_ref[...])
        mb = jnp.concatenate([oa_ref[...] * sa, o_ssm * ss], axis=1).astype(BF16)
        mix_ref[...] = mb
        yo = _dot(mb, wo_ref[...])
        yo_ref[...] = yo
        hn_ref[...] = h_ref[...] + mod_ref[2:3, :] * yo

    return pl.pallas_call(
        body, name=name, grid=(bsz, t_len // TM),
        in_specs=[_row_spec(D), _mod_spec(), _row_spec(512), _row_spec(512), _row_spec(512), _row_spec(512),
                  _const_spec((1, 512)), _row_spec(512), _const_spec((512, 512)), _const_spec((1, 512)),
                  _const_spec((D, D))],
        out_specs=[_row_spec(D), _row_spec(D), _row_spec(D)],
        out_shape=[jax.ShapeDtypeStruct((bsz, t_len, D), F32), jax.ShapeDtypeStruct((bsz, t_len, D), BF16),
                   jax.ShapeDtypeStruct((bsz, t_len, D), F32)],
        compiler_params=_cp(48, 2),
    )(h, mod, o_attn, g_attn, y_scan, u, d_skip, g_ssm, glu_w, glu_b, w_out)


def even_out_bwd(dh, mod, o_attn, g_attn, y_scan, u, d_skip, g_ssm, glu_w, glu_b, w_out, yout, name):
    bsz, t_len, _ = dh.shape

    def body(dh_ref, mod_ref, oa_ref, ga_ref, ys_ref, u_ref, dk_ref, gs_ref, gw_ref, gb_ref, wo_ref, yo_ref,
             doa_ref, dga_ref, dgs_ref, dys_ref, dyo_ref, zz_ref, ds_ref, dgate_ref, dgb_ref):
        b, t = pl.program_id(0), pl.program_id(1)
        dhv = dh_ref[...]

        @pl.when(t <= 1)
        def _():
            dgate_ref[...] = jnp.zeros_like(dgate_ref)

        @pl.when((b == 0) & (t == 0))
        def _():
            dgb_ref[...] = jnp.zeros_like(dgb_ref)

        dgate_ref[0:1, :] += _rowsum(dhv * yo_ref[...])
        dyb = (mod_ref[2:3, :] * dhv).astype(BF16)
        dyo_ref[...] = dyb
        dmix = _dot_nt(dyb, wo_ref[...])
        sa, dsa = _silu_and_grad(ga_ref[...])
        doa_ref[...] = dmix[:, :512] * sa
        dga_ref[...] = dmix[:, :512] * oa_ref[...] * dsa
        uv = u_ref[...]
        zz, dzz_dy = _gelu_and_grad(ys_ref[...] + uv * dk_ref[...])
        zb = zz.astype(BF16)
        zz_ref[...] = zb
        sg = _sig(_dot(zb, gw_ref[...]) + gb_ref[...])
        ss, dss = _silu_and_grad(gs_ref[...])
        dm = dmix[:, 512:]
        dgs_ref[...] = dm * (zz * sg) * dss
        do = dm * ss
        ds = do * zz * sg * (1.0 - sg)
        dsb = ds.astype(BF16)
        ds_ref[...] = dsb
        dgb_ref[0:1, :] += _rowsum(ds)
        dys = (do * sg + _dot_nt(dsb, gw_ref[...])) * dzz_dy
        dys_ref[...] = dys
        dgb_ref[1:2, :] += _rowsum(dys * uv)

    r512 = jax.ShapeDtypeStruct((bsz, t_len, 512), F32)
    return pl.pallas_call(
        body, name=name, grid=(bsz, t_len // TM),
        in_specs=[_row_spec(D), _mod_spec(), _row_spec(512), _row_spec(512), _row_spec(512), _row_spec(512),
                  _const_spec((1, 512)), _row_spec(512), _const_spec((512, 512)), _const_spec((1, 512)),
                  _const_spec((D, D)), _row_spec(D)],
        out_specs=[_row_spec(512)] * 4 + [_row_spec(D), _row_spec(512), _row_spec(512), _mod_spec(),
                                           _const_spec((8, 512))],
        out_shape=[r512, r512, r512, r512, jax.ShapeDtypeStruct((bsz, t_len, D), BF16),
                   jax.ShapeDtypeStruct((bsz, t_len, 512), BF16), jax.ShapeDtypeStruct((bsz, t_len, 512), BF16),
                   jax.ShapeDtypeStruct((bsz, 2, 8, D), F32), jax.ShapeDtypeStruct((8, 512), F32)],
        compiler_params=_cp(48, 2),
    )(dh, mod, o_attn, g_attn, y_scan, u, d_skip, g_ssm, glu_w, glu_b, w_out, yout)


def _split3_dot(band, x):
    x1 = x.astype(BF16)
    r1 = x - x1.astype(F32)
    x2 = r1.astype(BF16)
    x3 = (r1 - x2.astype(F32)).astype(BF16)
    return _dot(band, x3) + _dot(band, x2) + _dot(band, x1)


def pool_band(x, lc, transpose, name):
    bsz, t_len, _ = x.shape
    assert lc == TM
    hb = TM // HALO

    def body(xp_ref, xc_ref, xn_ref, o_ref):
        t = pl.program_id(1)
        seg_lo = jnp.where(t == 0, 0, lc)
        seg_hi = jnp.where(t == 0, lc, t_len)
        cur = xc_ref[...]
        xh = jnp.concatenate([xp_ref[...], cur, xn_ref[...]], axis=0)
        row_t = t * TM + lax.broadcasted_iota(jnp.int32, (TM, 1), 0)
        col_s = t * TM - HALO + lax.broadcasted_iota(jnp.int32, (1, TM + 2 * HALO), 1)
        row_s = t * TM - HALO + lax.broadcasted_iota(jnp.int32, (TM + 2 * HALO, 1), 0)
        s_ok = (col_s >= seg_lo) & (col_s < seg_hi)
        outs = []
        for gi, r in enumerate(POOL_R):
            band = ((jnp.abs(row_t - col_s) <= r) & s_ok).astype(BF16)
            xg = xh[:, gi * POOL_G:(gi + 1) * POOL_G]
            if transpose:
                cnt_s = jnp.minimum(row_s + r, seg_hi - 1) - jnp.maximum(row_s - r, seg_lo) + 1
                xg = xg * (1.0 / jnp.maximum(cnt_s, 1).astype(F32))
            acc = _split3_dot(band, xg)
            if not transpose:
                cnt_t = jnp.minimum(row_t + r, seg_hi - 1) - jnp.maximum(row_t - r, seg_lo) + 1
                acc = acc * (1.0 / cnt_t.astype(F32))
            outs.append(acc - cur[:, gi * POOL_G:(gi + 1) * POOL_G])
        o_ref[...] = jnp.concatenate(outs, axis=1)

    return pl.pallas_call(
        body, name=name, grid=(bsz, t_len // TM),
        in_specs=[pl.BlockSpec((None, HALO, D), lambda b, t: (b, jnp.maximum(t * hb - 1, 0), 0)),
                  _row_spec(D),
                  pl.BlockSpec((None, HALO, D), lambda b, t: (b, jnp.minimum((t + 1) * hb, t_len // HALO - 1), 0))],
        out_specs=_row_spec(D),
        out_shape=jax.ShapeDtypeStruct((bsz, t_len, D), F32),
        compiler_params=_cp(48, 2),
    )(x, x, x)


def pool_out(h, mod, pm, gate, pool_w, pool_scale, w_out, name):
    bsz, t_len, _ = h.shape

    def body(h_ref, mod_ref, pm_ref, gt_ref, pw_ref, ps_ref, wo_ref, hn_ref, mix_ref, yo_ref):
        pmv = pm_ref[...]
        ppre = jnp.concatenate([_dot(pmv[:, g * POOL_G:(g + 1) * POOL_G].astype(BF16), pw_ref[g])
                                for g in range(4)], axis=1)
        sl, _ = _silu_and_grad(gt_ref[...])
        mb = (ppre * ps_ref[...] * sl).astype(BF16)
        mix_ref[...] = mb
        yo = _dot(mb, wo_ref[...])
        yo_ref[...] = yo
        hn_ref[...] = h_ref[...] + mod_ref[2:3, :] * yo

    return pl.pallas_call(
        body, name=name, grid=(bsz, t_len // TM),
        in_specs=[_row_spec(D), _mod_spec(), _row_spec(D), _row_spec(D), _const_spec((4, POOL_G, POOL_G)),
                  _const_spec((1, D)), _const_spec((D, D))],
        out_specs=[_row_spec(D), _row_spec(D), _row_spec(D)],
        out_shape=[jax.ShapeDtypeStruct((bsz, t_len, D), F32), jax.ShapeDtypeStruct((bsz, t_len, D), BF16),
                   jax.ShapeDtypeStruct((bsz, t_len, D), F32)],
        compiler_params=_cp(48, 2),
    )(h, mod, pm, gate, pool_w, pool_scale, w_out)


def pool_out_bwd(dh, mod, pm, gate, pool_w, pool_scale, w_out, yout, name):
    bsz, t_len, _ = dh.shape

    def body(dh_ref, mod_ref, pm_ref, gt_ref, pw_ref, ps_ref, wo_ref, yo_ref,
             dpm_ref, dgt_ref, dyo_ref, dpp_ref, dgate_ref, dps_ref):
        b, t = pl.program_id(0), pl.program_id(1)
        dhv = dh_ref[...]

        @pl.when(t <= 1)
        def _():
            dgate_ref[...] = jnp.zeros_like(dgate_ref)

        @pl.when((b == 0) & (t == 0))
        def _():
            dps_ref[...] = jnp.zeros_like(dps_ref)

        dgate_ref[0:1, :] += _rowsum(dhv * yo_ref[...])
        dyb = (mod_ref[2:3, :] * dhv).astype(BF16)
        dyo_ref[...] = dyb
        dmix = _dot_nt(dyb, wo_ref[...])
        pmv = pm_ref[...]
        ppre = jnp.concatenate([_dot(pmv[:, g * POOL_G:(g + 1) * POOL_G].astype(BF16), pw_ref[g])
                                for g in range(4)], axis=1)
        ps = ps_ref[...]
        sl, dsl = _silu_and_grad(gt_ref[...])
        dp = dmix * sl
        dgt_ref[...] = dmix * (ppre * ps) * dsl
        dps_ref[0:1, :] += _rowsum(dp * ppre)
        dppb = (dp * ps).astype(BF16)
        dpp_ref[...] = dppb
        dpm_ref[...] = jnp.concatenate([_dot_nt(dppb[:, g * POOL_G:(g + 1) * POOL_G], pw_ref[g])
                                        for g in range(4)], axis=1)

    return pl.pallas_call(
        body, name=name, grid=(bsz, t_len // TM),
        in_specs=[_row_spec(D), _mod_spec(), _row_spec(D), _row_spec(D), _const_spec((4, POOL_G, POOL_G)),
                  _const_spec((1, D)), _const_spec((D, D)), _row_spec(D)],
        out_specs=[_row_spec(D), _row_spec(D), _row_spec(D), _row_spec(D), _mod_spec(), _const_spec((8, D))],
        out_shape=[jax.ShapeDtypeStruct((bsz, t_len, D), F32), jax.ShapeDtypeStruct((bsz, t_len, D), F32),
                   jax.ShapeDtypeStruct((bsz, t_len, D), BF16), jax.ShapeDtypeStruct((bsz, t_len, D), BF16),
                   jax.ShapeDtypeStruct((bsz, 2, 8, D), F32), jax.ShapeDtypeStruct((8, D), F32)],
        compiler_params=_cp(48, 2),
    )(dh, mod, pm, gate, pool_w, pool_scale, w_out, yout)


def loss_head(h, final_g, target, name):
    bsz, t_len, _ = h.shape

    def body(h_ref, g_ref, tg_ref, dh_ref, loss_ref, dg_ref):
        b, t = pl.program_id(0), pl.program_id(1)

        @pl.when((b == 0) & (t == 0))
        def _():
            loss_ref[...] = jnp.zeros_like(loss_ref)
            dg_ref[...] = jnp.zeros_like(dg_ref)

        lat = (t > 0).astype(F32)
        xh, r = _rms(h_ref[...])
        gg = g_ref[...]
        err = (xh * gg - tg_ref[...]) * lat
        loss_ref[...] += 0.5 * jnp.sum(jnp.mean(err * err, axis=-1, keepdims=True))
        dy = err * (1.0 / D)
        dg_ref[0:1, :] += _rowsum(dy * xh)
        dxh = dy * gg
        dh_ref[...] = r * (dxh - xh * jnp.mean(dxh * xh, axis=-1, keepdims=True))

    return pl.pallas_call(
        body, name=name, grid=(bsz, t_len // TM),
        in_specs=[_row_spec(D), _const_spec((1, D)),
                  pl.BlockSpec((None, TM, D), lambda b, t: (b, jnp.maximum(t - 1, 0), 0))],
        out_specs=[_row_spec(D), _const_spec((8, 128)), _const_spec((8, D))],
        out_shape=[jax.ShapeDtypeStruct((bsz, t_len, D), F32), jax.ShapeDtypeStruct((8, 128), F32),
                   jax.ShapeDtypeStruct((8, D), F32)],
        compiler_params=_cp(48, 2),
    )(h, final_g, target)


def _swap16(x):
    n = x.shape[-1]
    ax = x.ndim - 1
    lane = lax.broadcasted_iota(jnp.int32, x.shape, ax)
    return jnp.where((lane % 32) < 16, pltpu.roll(x, n - 16, ax), pltpu.roll(x, 16, ax))


def _rope(x, cos, sin):
    return x * cos + _swap16(x) * sin


def _rope_t(dy, cos, sin):
    return dy * cos + _swap16(dy * sin)


def rope_tables(lc, seq):
    rows = seq // GRID_W
    row = jnp.repeat(jnp.arange(rows, dtype=F32), GRID_W)
    col = jnp.tile(jnp.arange(GRID_W, dtype=F32), rows)
    inv_freq = ROPE_BASE ** (-jnp.arange(ROPE_FREQS, dtype=F32) / ROPE_FREQS)
    ar, ac = row[:, None] * inv_freq, col[:, None] * inv_freq
    cos = jnp.concatenate([jnp.cos(ar), jnp.cos(ar), jnp.cos(ac), jnp.cos(ac)], axis=1)
    sin = jnp.concatenate([-jnp.sin(ar), jnp.sin(ar), -jnp.sin(ac), jnp.sin(ac)], axis=1)
    cos = jnp.concatenate([jnp.ones((lc, HEAD_DIM), F32), cos], axis=0)
    sin = jnp.concatenate([jnp.zeros((lc, HEAD_DIM), F32), sin], axis=0)
    return jnp.tile(cos, (1, 2)), jnp.tile(sin, (1, 2))


def _attn_mask(i, lc, t_len):
    qrow = i * AB + lax.broadcasted_iota(jnp.int32, (AB, 1), 0)
    kloc = (i - 1) * AB + lax.broadcasted_iota(jnp.int32, (1, 3 * AB), 1)
    valid = (qrow >= lc) & (kloc >= lc) & (kloc < t_len) & (jnp.abs(qrow - kloc) <= WINDOW)
    mask = jnp.concatenate([valid, jnp.ones((AB, lc), jnp.bool_)], axis=1)
    return jnp.concatenate([mask] * GROUP, axis=0)


def _attn_specs(t_len, lc):
    nb = t_len // AB
    prev = lambda b, i: (b, jnp.maximum(i - 1, 0), 0)
    cur = lambda b, i: (b, i, 0)
    nxt = lambda b, i: (b, jnp.minimum(i + 1, nb - 1), 0)
    kv = [pl.BlockSpec((None, AB, 2 * KV_W), f) for f in (prev, cur, nxt)]
    kv.append(pl.BlockSpec((None, lc, 2 * KV_W), lambda b, i: (b, 0, 0)))
    tab = [pl.BlockSpec((AB, 128), lambda b, i, f=f: f(b, i)[1:]) for f in (prev, cur, nxt)]
    return kv, tab


def _attn_keys(kvp, kvc, kvn, kvx, cp, cc, cn, sp, sc, sn):
    kk = jnp.concatenate([_rope(kvp[:, :KV_W], cp, sp), _rope(kvc[:, :KV_W], cc, sc),
                          _rope(kvn[:, :KV_W], cn, sn), kvx[:, :KV_W]], axis=0)
    vv = jnp.concatenate([kvp[:, KV_W:], kvc[:, KV_W:], kvn[:, KV_W:], kvx[:, KV_W:]], axis=0)
    return kk, vv


def _stack_heads(x, hk):
    return jnp.concatenate([x[:, (GROUP * hk + g) * HEAD_DIM:(GROUP * hk + g + 1) * HEAD_DIM]
                            for g in range(GROUP)], axis=0)


def _sink_col(sink_ref, hk):
    return jnp.concatenate([jnp.full((AB, 1), sink_ref[GROUP * hk + g], F32) for g in range(GROUP)], axis=0)


def attn_fwd(q, kv, cos, sin, sink, lc, name, carry=()):
    bsz, t_len, _ = q.shape
    nb = t_len // AB
    kv_specs, tab_specs = _attn_specs(t_len, lc)
    scale = HEAD_DIM ** -0.5
    nc = len(carry)

    def body(sink_ref, q_ref, kvp_ref, kvc_ref, kvn_ref, kvx_ref, cp, cc, cn, sp, sc, sn, *rest):
        o_ref, lse_ref = rest[nc:nc + 2]
        b, i = pl.program_id(0), pl.program_id(1)
        if nc:
            start, wait = _xy_copies(rest[:nc], rest[nc + 2:2 * nc + 2], *rest[2 * nc + 2:], False)
            pl.when((b == 0) & (i == 0))(start)
        mask = _attn_mask(i, lc, t_len)
        qr = _rope(q_ref[...], jnp.tile(cc[...], (1, 4)), jnp.tile(sc[...], (1, 4)))
        kk, vv = _attn_keys(kvp_ref[...], kvc_ref[...], kvn_ref[...], kvx_ref[...],
                            cp[...], cc[...], cn[...], sp[...], sc[...], sn[...])
        outs, lses = [], []
        for hk in range(N_KV):
            kh = kk[:, hk * HEAD_DIM:(hk + 1) * HEAD_DIM].astype(BF16)
            vh = vv[:, hk * HEAD_DIM:(hk + 1) * HEAD_DIM].astype(BF16)
            q4 = _stack_heads(qr, hk).astype(BF16)
            s = jnp.where(mask, _dot_nt(q4, kh) * scale, NEG_INF)
            sk = _sink_col(sink_ref, hk)
            m = jnp.maximum(jnp.max(s, axis=-1, keepdims=True), sk)
            p = jnp.exp(s - m)
            l = jnp.sum(p, axis=-1, keepdims=True) + jnp.exp(sk - m)
            o = _dot(p.astype(BF16), vh) / l
            lse = m + jnp.log(l)
            for g in range(GROUP):
                outs.append(o[g * AB:(g + 1) * AB])
                lses.append(lse[g * AB:(g + 1) * AB])
        o_ref[...] = jnp.concatenate(outs, axis=1)
        lse_ref[...] = jnp.concatenate(lses, axis=1)
        if nc:
            pl.when((b == bsz - 1) & (i == nb - 1))(wait)

    return pl.pallas_call(
        body, name=name, grid=(bsz, nb),
        in_specs=[pl.BlockSpec(memory_space=pltpu.SMEM),
                  pl.BlockSpec((None, AB, ATTN_W), lambda b, i: (b, i, 0))] + kv_specs + tab_specs + tab_specs
        + [HBM_SPEC] * nc,
        out_specs=[pl.BlockSpec((None, AB, ATTN_W), lambda b, i: (b, i, 0)),
                   pl.BlockSpec((None, AB, N_HEADS), lambda b, i: (b, i, 0))] + [HBM_SPEC] * nc,
        out_shape=[jax.ShapeDtypeStruct((bsz, t_len, ATTN_W), F32), jax.ShapeDtypeStruct((bsz, t_len, N_HEADS), F32)]
        + _xy_out_shapes(carry, False),
        scratch_shapes=_xy_sems(nc) if nc else [],
        compiler_params=_cp(48, 2),
    )(sink, q, kv, kv, kv, kv, cos, cos, cos, sin, sin, sin, *carry)


def attn_bwd(q, kv, o, lse, do, cos, sin, sink, lc, name, carry=()):
    bsz, t_len, _ = q.shape
    nb = t_len // AB
    kv_specs, tab_specs = _attn_specs(t_len, lc)
    scale = HEAD_DIM ** -0.5
    blk = lambda w: pl.BlockSpec((None, AB, w), lambda b, i: (b, i, 0))
    full_tab = pl.BlockSpec((t_len, 128), lambda b, i: (0, 0))
    nc = len(carry)

    def body(sink_ref, q_ref, kvp_ref, kvc_ref, kvn_ref, kvx_ref, cp, cc, cn, sp, sc, sn, cf, sf,
             o_ref, lse_ref, do_ref, *rest):
        dq_ref, dkv_ref, dsink_ref = rest[nc:nc + 3]
        b, i = pl.program_id(0), pl.program_id(1)
        if nc:
            start, wait = _xy_copies(rest[:nc], rest[nc + 3:2 * nc + 3], *rest[2 * nc + 3:], True)
            pl.when((b == 0) & (i == 0))(start)

        @pl.when(i == 0)
        def _():
            dkv_ref[...] = jnp.zeros_like(dkv_ref)

        @pl.when((b == 0) & (i == 0))
        def _():
            dsink_ref[...] = jnp.zeros_like(dsink_ref)

        mask = _attn_mask(i, lc, t_len)
        cq, sq = jnp.tile(cc[...], (1, 4)), jnp.tile(sc[...], (1, 4))
        qr = _rope(q_ref[...], cq, sq)
        kk, vv = _attn_keys(kvp_ref[...], kvc_ref[...], kvn_ref[...], kvx_ref[...],
                            cp[...], cc[...], cn[...], sp[...], sc[...], sn[...])
        dov, ov, lsev = do_ref[...], o_ref[...], lse_ref[...]
        dqs, dks, dvs, dsk = [], [], [], []
        for hk in range(N_KV):
            kh = kk[:, hk * HEAD_DIM:(hk + 1) * HEAD_DIM].astype(BF16)
            vh = vv[:, hk * HEAD_DIM:(hk + 1) * HEAD_DIM].astype(BF16)
            q4 = _stack_heads(qr, hk).astype(BF16)
            do4 = _stack_heads(dov, hk)
            o4 = _stack_heads(ov, hk)
            lse4 = jnp.concatenate([lsev[:, GROUP * hk + g:GROUP * hk + g + 1] for g in range(GROUP)], axis=0)
            delta = jnp.sum(do4 * o4, axis=-1, keepdims=True)
            s = jnp.where(mask, _dot_nt(q4, kh) * scale, NEG_INF)
            p = jnp.exp(s - lse4)
            do4b = do4.astype(BF16)
            dp = _dot_nt(do4b, vh)
            ds = (p * (dp - delta) * scale).astype(BF16)
            dq4 = _dot(ds, kh)
            dks.append(_dot_tn(ds, q4))
            dvs.append(_dot_tn(p.astype(BF16), do4b))
            pd = jnp.exp(_sink_col(sink_ref, hk) - lse4) * delta
            for g in range(GROUP):
                dqs.append(dq4[g * AB:(g + 1) * AB])
                dsk.append(-jnp.sum(pd[g * AB:(g + 1) * AB], axis=0, keepdims=True))
        dq_ref[...] = _rope_t(jnp.concatenate(dqs, axis=1), cq, sq)
        dsink_ref[0:1, :] += jnp.concatenate(dsk, axis=1)
        dkv = jnp.concatenate(dks + dvs, axis=1)
        starts = (jnp.maximum(i - 1, 0), i, jnp.minimum(i + 1, nb - 1))
        for j, st in enumerate(starts):
            rows = pl.ds(pl.multiple_of(st * AB, AB), AB)
            dkv_ref[rows, :] += dkv[j * AB:(j + 1) * AB]
        dkv_ref[0:lc, :] += dkv[3 * AB:]

        @pl.when(i == nb - 1)
        def _():
            def unrotate(j, carry):
                rows = pl.ds(pl.multiple_of(j * AB, AB), AB)
                dkv_ref[rows, 0:KV_W] = _rope_t(dkv_ref[rows, 0:KV_W], cf[rows, :], sf[rows, :])
                return carry
            lax.fori_loop(0, nb, unrotate, 0)

        if nc:
            pl.when((b == bsz - 1) & (i == nb - 1))(wait)

    return pl.pallas_call(
        body, name=name, grid=(bsz, nb),
        in_specs=[pl.BlockSpec(memory_space=pltpu.SMEM), blk(ATTN_W)] + kv_specs + tab_specs + tab_specs
        + [full_tab, full_tab, blk(ATTN_W), blk(N_HEADS), blk(ATTN_W)] + [HBM_SPEC] * nc,
        out_specs=[blk(ATTN_W), pl.BlockSpec((None, t_len, 2 * KV_W), lambda b, i: (b, 0, 0)),
                   pl.BlockSpec((8, N_HEADS), lambda b, i: (0, 0))] + [HBM_SPEC] * nc,
        out_shape=[jax.ShapeDtypeStruct((bsz, t_len, ATTN_W), F32), jax.ShapeDtypeStruct((bsz, t_len, 2 * KV_W), F32),
                   jax.ShapeDtypeStruct((8, N_HEADS), F32)] + _xy_out_shapes(carry, True),
        scratch_shapes=_xy_sems(nc) if nc else [],
        compiler_params=_cp(56, 2),
    )(sink, q, kv, kv, kv, kv, cos, cos, cos, sin, sin, sin, cos, sin, o, lse, do, *carry)


def _s5_mats_dir(a_re, a_im, log_dt, b_re, b_im, c_re, c_im, flip):
    hp = lax.Precision.HIGHEST
    lam = lax.complex(a_re, a_im)
    ldt = lam * jnp.exp(log_dt)[:, None]
    a_bar = jnp.exp(ldt)
    b_bar = ((a_bar - 1.0) / lam)[..., None] * lax.complex(b_re, b_im)
    cm = lax.complex(c_re, c_im)
    tt = np.arange(Q)
    powers = lambda e: jnp.exp(ldt[..., None] * jnp.asarray(e, F32))
    ca = cm[:, :, :, None] * powers(Q - 1 - tt if flip else tt)[:, None, :, :]
    ca = jnp.concatenate([jnp.real(ca), -jnp.imag(ca)], axis=2)
    bb = jnp.concatenate([jnp.real(b_bar), jnp.imag(b_bar)], axis=1)
    k = jnp.einsum('gpk,gcpt->gktc', bb, ca, precision=hp).reshape(G, C, QC)
    slabs = []
    for t1 in range(Q):
        if flip:
            sh = (Q - 1 - t1) * C
            slabs.append(jnp.pad(k, ((0, 0), (0, 0), (0, sh)))[..., sh:])
        else:
            slabs.append(jnp.pad(k, ((0, 0), (0, 0), (t1 * C, 0)))[..., :QC])
    kt = jnp.stack(slabs, axis=1).reshape(G, QC, QC)
    ws = powers(tt if flip else Q - 1 - tt)[:, :, :, None] * b_bar[:, :, None, :]
    ws = ws.transpose(0, 2, 3, 1)
    wo = cm[:, :, :, None] * powers(Q - tt if flip else tt + 1)[:, None, :, :]
    wo = wo.transpose(0, 2, 3, 1)
    ws = jnp.concatenate([jnp.real(ws), jnp.imag(ws)], axis=-1).reshape(G, QC, P2)
    wo = jnp.concatenate([jnp.real(wo), -jnp.imag(wo)], axis=1).reshape(G, P2, QC)
    a1, a2 = _pair_forms(powers([Q]))
    return kt, ws, wo, a1, a2


def _pair_forms(z):
    re, im = jnp.real(z), jnp.imag(z)
    k = z.shape[-1]
    a1 = jnp.concatenate([re, re], axis=1).transpose(2, 0, 1).reshape(k, G * P2)
    a2 = jnp.concatenate([-im, im], axis=1).transpose(2, 0, 1).reshape(k, G * P2)
    return a1, a2


def s5_mats(a_re, a_im, log_dt, b_re, b_im, c_re, c_im):
    per_dir = [_s5_mats_dir(a_re[d], a_im[d], log_dt[d], b_re[d], b_im[d], c_re[d], c_im[d], d == 1)
               for d in range(2)]
    return tuple(jnp.stack([m[i] for m in per_dir]) for i in range(5))


GH = G // 8
RT = 16 * Q


def _perm_consts():
    r = np.arange(RT)
    rows = np.zeros((RT, RT), np.float32)
    rows[(r % Q) * 16 + r // Q, r] = 1.0
    q = np.arange(8 * 128)
    lanes = np.zeros((8 * 128, 8 * 128), np.float32)
    lanes[q, ((q % 128) // C) * 128 + (q // 128) * C + q % C] = 1.0
    return jnp.asarray(rows, BF16), jnp.asarray(lanes, BF16)


def to_groups(x, name):
    bsz, t_len, _ = x.shape
    nc = t_len // Q
    rows, lanes = _perm_consts()

    def body(x_ref, r_ref, p_ref, o_ref, w_ref):
        for j in range(t_len // RT):
            pt = _dot(r_ref[...], x_ref[j * RT:(j + 1) * RT, :].astype(BF16)).astype(BF16)
            for t in range(Q):
                w_ref[j * 16:(j + 1) * 16, t * SSM_W:(t + 1) * SSM_W] = pt[t * 16:(t + 1) * 16, :]
        for gh in range(GH):
            for th in range(2):
                inp = jnp.concatenate([w_ref[:, (th * 8 + tl) * SSM_W + gh * 128:(th * 8 + tl) * SSM_W + (gh + 1) * 128]
                                       for tl in range(8)], axis=1)
                out = _dot(inp, p_ref[...]).astype(BF16)
                for gl in range(8):
                    o_ref[gh * 8 + gl, :, th * 128:(th + 1) * 128] = out[:, gl * 128:(gl + 1) * 128]

    return pl.pallas_call(
        body, name=name, grid=(bsz,),
        in_specs=[pl.BlockSpec((None, t_len, SSM_W), lambda b: (b, 0, 0)), pl.BlockSpec((RT, RT), lambda b: (0, 0)),
                  pl.BlockSpec((1024, 1024), lambda b: (0, 0))],
        out_specs=pl.BlockSpec((None, G, nc, QC), lambda b: (b, 0, 0, 0)),
        out_shape=jax.ShapeDtypeStruct((bsz, G, nc, QC), BF16),
        scratch_shapes=[pltpu.VMEM((nc, Q * SSM_W), BF16)],
        compiler_params=_cp(56, 1),
    )(x, rows, lanes)


def from_groups(xg, name):
    bsz, _, nc, _ = xg.shape
    t_len = nc * Q
    rows, lanes = _perm_consts()

    def body(x_ref, r_ref, p_ref, o_ref, whi_ref, wlo_ref):
        gh = pl.program_id(1)
        for th in range(2):
            inp = jnp.concatenate([x_ref[gl, :, th * 128:(th + 1) * 128] for gl in range(8)], axis=1)
            hi = inp.astype(BF16)
            lo = (inp - hi.astype(F32)).astype(BF16)
            whi_ref[gh, :, th * 1024:(th + 1) * 1024] = _dot(hi, p_ref[...]).astype(BF16)
            wlo_ref[gh, :, th * 1024:(th + 1) * 1024] = _dot(lo, p_ref[...]).astype(BF16)

        @pl.when(gh == GH - 1)
        def _():
            for j in range(t_len // RT):
                def tile(w_ref):
                    return jnp.concatenate(
                        [jnp.concatenate([w_ref[k, j * 16:(j + 1) * 16, t * 128:(t + 1) * 128] for k in range(GH)],
                                         axis=1) for t in range(Q)], axis=0)
                o_ref[j * RT:(j + 1) * RT, :] = _dot(r_ref[...], tile(whi_ref)) + _dot(r_ref[...], tile(wlo_ref))

    return pl.pallas_call(
        body, name=name, grid=(bsz, GH),
        in_specs=[pl.BlockSpec((None, 8, nc, QC), lambda b, k: (b, k, 0, 0)),
                  pl.BlockSpec((RT, RT), lambda b, k: (0, 0)), pl.BlockSpec((1024, 1024), lambda b, k: (0, 0))],
        out_specs=pl.BlockSpec((None, t_len, SSM_W), lambda b, k: (b, 0, 0)),
        out_shape=jax.ShapeDtypeStruct((bsz, t_len, SSM_W), F32),
        scratch_shapes=[pltpu.VMEM((GH, nc, Q * 128), BF16), pltpu.VMEM((GH, nc, Q * 128), BF16)],
        compiler_params=_cp(56, 2),
    )(xg, rows, lanes)


def _gb(shape):
    return pl.BlockSpec((None, None) + shape, lambda g, b: (b, g, 0, 0))


def _gw(shape):
    return pl.BlockSpec((2, None) + shape, lambda g, b: (0, g, 0, 0))


def _gs(nc):
    return pl.BlockSpec((2, None, nc, P2), lambda g, b: (0, b, 0, g))


def s5_chunk_fwd(ug, kt, ws, name):
    bsz, _, nc, _ = ug.shape

    def body(u_ref, kt_ref, ws_ref, y_ref, s_ref):
        ub = u_ref[...]
        y_ref[...] = _dot(ub, kt_ref[0]) + _dot(ub, kt_ref[1])
        s_ref[0] = _dot(ub, ws_ref[0])
        s_ref[1] = _dot(ub, ws_ref[1])

    return pl.pallas_call(
        body, name=name, grid=(G, bsz),
        in_specs=[_gb((nc, QC)), _gw((QC, QC)), _gw((QC, P2))],
        out_specs=[_gb((nc, QC)), _gs(nc)],
        out_shape=[jax.ShapeDtypeStruct((bsz, G, nc, QC), F32), jax.ShapeDtypeStruct((2, bsz, nc, G * P2), F32)],
        compiler_params=_cp(32, 2),
    )(ug, kt, ws)


def s5_scan(s, a1, a2, ncc, reverse, name, hp=None):
    _, bsz, nc, gw = s.shape
    as_rows = lambda v: v.reshape(v.shape[:-1] + (G, P2))
    st = pl.BlockSpec((2, None, nc, SCAN_G, P2), lambda b, w: (0, b, 0, w, 0))
    av = pl.BlockSpec((2, SCAN_G, P2), lambda b, w: (0, w, 0))
    acc = pl.BlockSpec((2, None, SCAN_G, P2), lambda b, w: (0, b, w, 0))
    with_da = hp is not None

    def body(*refs):
        if with_da:
            s_ref, a1_ref, a2_ref, hp_ref, out_ref, da1_ref, da2_ref = refs
        else:
            s_ref, a1_ref, a2_ref, out_ref = refs
        a1v = (a1_ref[0], a1_ref[1])
        a2v = (a2_ref[0], a2_ref[1])
        swap = lambda h: pltpu.roll(h, P, 1)

        def block(jb, carry):
            pb = nblk - 1 - jb if reverse else jb
            base = (pb * ncc, jnp.where(pb == 0, ncc - 1, nc - 1 - (pb - 1) * ncc))
            hs, da1, da2 = carry
            for kk in range(ncc):
                k = ncc - 1 - kk if reverse else kk
                nh, n1, n2 = [], [], []
                for d, n in enumerate((base[0] + k, base[1] - k)):
                    h, hw = hs[d]
                    out_ref[d, n] = h
                    sv = s_ref[d, n]
                    nh.append((a1v[d] * h + a2v[d] * hw + sv, a1v[d] * hw - a2v[d] * h + swap(sv)))
                    if with_da:
                        hv = hp_ref[d, n]
                        n1.append(da1[d] + h * hv)
                        n2.append(da2[d] + h * swap(hv))
                hs, da1, da2 = tuple(nh), tuple(n1), tuple(n2)
            return hs, da1, da2

        assert nc % ncc == 0
        nblk = nc // ncc
        z = jnp.zeros((SCAN_G, P2), F32)
        zz = (z, z) if with_da else ()
        _, da1, da2 = lax.fori_loop(0, nblk, block, (((z, z), (z, z)), zz, zz))
        if with_da:
            for d in range(2):
                da1_ref[d] = da1[d]
                da2_ref[d] = da2[d]

    out_shape = [jax.ShapeDtypeStruct((2, bsz, nc, G, P2), F32)]
    out_specs = [st]
    ins = [as_rows(s), as_rows(a1[:, 0]), as_rows(a2[:, 0])]
    in_specs = [st, av, av]
    if with_da:
        ins.append(as_rows(hp))
        in_specs.append(st)
        out_shape += [jax.ShapeDtypeStruct((2, bsz, G, P2), F32)] * 2
        out_specs += [acc, acc]
    res = pl.pallas_call(
        body, name=name, grid=(bsz, G // SCAN_G), in_specs=in_specs, out_specs=out_specs, out_shape=out_shape,
        compiler_params=_cp(48, 2),
    )(*ins)
    out = res[0].reshape(s.shape)
    return (out, res[1].reshape(2, bsz, gw), res[2].reshape(2, bsz, gw)) if with_da else out


def s5_out_fwd(y1, hp, wo, name):
    bsz, _, nc, _ = y1.shape

    def body(y1_ref, hp_ref, wo_ref, y_ref):
        y_ref[...] = (y1_ref[...] + _dot(hp_ref[0].astype(BF16), wo_ref[0])
                      + _dot(hp_ref[1].astype(BF16), wo_ref[1]))

    return pl.pallas_call(
        body, name=name, grid=(G, bsz),
        in_specs=[_gb((nc, QC)), _gs(nc), _gw((P2, QC))],
        out_specs=_gb((nc, QC)),
        out_shape=jax.ShapeDtypeStruct(y1.shape, F32),
        compiler_params=_cp(32, 2),
    )(y1, hp, wo)


def _acc_init(b, *refs):
    @pl.when(b == 0)
    def _():
        for r in refs:
            r[...] = jnp.zeros_like(r)


def s5_out_bwd(dyg, ug, hp, wo, name):
    bsz, _, nc, _ = dyg.shape

    def body(dy_ref, u_ref, hp_ref, wo_ref, dhp_ref, dwo_ref, dkt_ref):
        _acc_init(pl.program_id(1), dwo_ref, dkt_ref)
        dyb = dy_ref[...]
        for d in range(2):
            dhp_ref[d] = _dot_nt(dyb, wo_ref[d])
            dwo_ref[d] += _dot_tn(hp_ref[d].astype(BF16), dyb)
        dkt_ref[...] += _dot_tn(u_ref[...], dyb)

    return pl.pallas_call(
        body, name=name, grid=(G, bsz),
        in_specs=[_gb((nc, QC)), _gb((nc, QC)), _gs(nc), _gw((P2, QC))],
        out_specs=[_gs(nc), _gw((P2, QC)), pl.BlockSpec((None, QC, QC), lambda g, b: (g, 0, 0))],
        out_shape=[jax.ShapeDtypeStruct(hp.shape, F32), jax.ShapeDtypeStruct((2, G, P2, QC), F32),
                   jax.ShapeDtypeStruct((G, QC, QC), F32)],
        compiler_params=_cp(32, 2),
    )(dyg, ug, hp, wo)


def s5_chunk_bwd(dyg, ug, ds, kt, ws, name):
    bsz, _, nc, _ = dyg.shape

    def body(dy_ref, u_ref, ds_ref, kt_ref, ws_ref, du_ref, dws_ref):
        _acc_init(pl.program_id(1), dws_ref)
        dyb = dy_ref[...]
        du = _dot_nt(dyb, kt_ref[0]) + _dot_nt(dyb, kt_ref[1])
        for d in range(2):
            dsb = ds_ref[d].astype(BF16)
            du += _dot_nt(dsb, ws_ref[d])
            dws_ref[d] += _dot_tn(u_ref[...], dsb)
        du_ref[...] = du

    return pl.pallas_call(
        body, name=name, grid=(G, bsz),
        in_specs=[_gb((nc, QC)), _gb((nc, QC)), _gs(nc), _gw((QC, QC)), _gw((QC, P2))],
        out_specs=[_gb((nc, QC)), _gw((QC, P2))],
        out_shape=[jax.ShapeDtypeStruct(dyg.shape, F32), jax.ShapeDtypeStruct((2, G, QC, P2), F32)],
        compiler_params=_cp(32, 2),
    )(dyg, ug, ds, kt, ws)


def local_step(x, ctx, target, mods, norm_g, final_g, even, odd, late=None, scatter=None):
    bsz, seq, _ = x.shape
    lc = ctx.shape[1]
    t_len = lc + seq
    ncc = lc // Q
    cos, sin = rope_tables(lc, seq)
    h = jnp.concatenate([ctx, x], axis=1)
    ssm_stacked = [jnp.stack([even[0]["ssm"][k], even[1]["ssm"][k]]) for k in range(7)]
    mats_all, mats_vjp = jax.vjp(jax.vmap(s5_mats), *ssm_stacked)
    d_mats = [None, None]
    saved = []
    for i in range(DEPTH):
        j = i // 2
        g = norm_g[i].reshape(1, D)
        if i % 2 == 0:
            w = even[j]
            a, q, kv, g_attn, u, g_ssm = norm_in(h, g, mods[i], w["w_in"], EVEN_SPLITS, f"even_in{j}")
            if i == 0 and late is not None:
                o_attn, lse, *gathered = attn_fwd(q, kv, cos, sin, w["sink"], lc, f"attn_fwd{j}", carry=late[0])
                late[1](gathered)
            else:
                o_attn, lse = attn_fwd(q, kv, cos, sin, w["sink"], lc, f"attn_fwd{j}")
            kt, ws, wo, a1, a2 = (m[j] for m in mats_all)
            kt, ws, wo = kt.astype(BF16), ws.astype(BF16), wo.astype(BF16)
            d_skip = w["ssm"][7].reshape(1, SSM_W)
            ug = to_groups(u, f"u_to_groups{j}")
            y1, s = s5_chunk_fwd(ug, kt, ws, f"s5_chunk_fwd{j}")
            hp = s5_scan(s, a1, a2, ncc, False, f"s5_scan_fwd{j}")
            y_scan = from_groups(s5_out_fwd(y1, hp, wo, f"s5_out_fwd{j}"), f"y_from_groups{j}")
            h_new, mix, yout = even_out(h, mods[i], o_attn, g_attn, y_scan, u, d_skip, g_ssm, w["glu_w"], w["glu_b"],
                                        w["w_out"], f"even_out{j}")
            saved.append(dict(h=h, a=a, q=q, kv=kv, g_attn=g_attn, g_ssm=g_ssm, o_attn=o_attn, lse=lse, ug=ug, u=u,
                              hp=hp, y_scan=y_scan, mix=mix, yout=yout, mats=(kt, ws, wo, a1, a2), d_skip=d_skip))
        else:
            w = odd[j]
            a, u, gate = norm_in(h, g, mods[i], w["w_in"], ODD_SPLITS, f"odd_in{j}")
            pm = pool_band(u, lc, False, f"pool_band_fwd{j}")
            h_new, mix, yout = pool_out(h, mods[i], pm, gate, w["pool_w"], w["pool_scale"], w["w_out"], f"pool_out{j}")
            saved.append(dict(h=h, a=a, pm=pm, gate=gate, mix=mix, yout=yout))
        h = h_new

    dh, loss_acc, dfg = loss_head(h, final_g.reshape(1, D), target, "loss_head")
    grads = dict(final_g=dfg[0], norm_g=[None] * DEPTH, even=[None, None], odd=[None, None])
    dmods = [None] * DEPTH
    rows = bsz * t_len
    flat = lambda v: v.reshape(rows, v.shape[-1])
    for i in reversed(range(DEPTH)):
        j = i // 2
        sv = saved[i]
        g = norm_g[i].reshape(1, D)
        if i % 2 == 0:
            w = even[j]
            kt, ws, wo, a1, a2 = sv["mats"]
            (d_oattn, d_gattn, d_gssm, d_yssm, dyout, zz, dsg, dgate, dglu_b) = even_out_bwd(
                dh, mods[i], sv["o_attn"], sv["g_attn"], sv["y_scan"], sv["u"], sv["d_skip"], sv["g_ssm"], w["glu_w"],
                w["glu_b"], w["w_out"], sv["yout"], f"even_out_bwd{j}")
            g_w_out = matmul_tn(flat(sv["mix"]), flat(dyout), D, D, f"even_w_out_grad{j}")
            g_glu_w = matmul_tn(flat(zz), flat(dsg), SSM_W, SSM_W, f"glu_w_grad{j}")
            carry = scatter(grads) if i == 0 and scatter is not None else ()
            dq, dkv, dsink, *grads["landed"] = attn_bwd(sv["q"], sv["kv"], sv["o_attn"], sv["lse"], d_oattn, cos, sin,
                                                        w["sink"], lc, f"attn_bwd{j}", carry=carry)
            dyg = to_groups(d_yssm, f"dy_to_groups{j}")
            dhp, dwo, dkt = s5_out_bwd(dyg, sv["ug"], sv["hp"], wo, f"s5_out_bwd{j}")
            ds, da1, da2 = s5_scan(dhp, a1, -a2, ncc, True, f"s5_scan_bwd{j}", hp=sv["hp"])
            dug, dws = s5_chunk_bwd(dyg, sv["ug"], ds, kt, ws, f"s5_chunk_bwd{j}")
            dkt2 = jnp.stack([dkt, dkt])
            da1 = da1.sum(axis=1).reshape(2, 1, G * P2)
            da2 = da2.sum(axis=1).reshape(2, 1, G * P2)
            d_mats[j] = (dkt2, dws, dwo, da1, da2)
            dparts = [dq, dkv, d_gattn, from_groups(dug, f"du_from_groups{j}"), d_gssm]
            dh, dz, dmod, dg = norm_in_bwd(dparts, dh, sv["h"], g, mods[i], w["w_in"], f"even_in_bwd{j}",
                                           skip=(3, d_yssm, sv["d_skip"]))
            g_w_in = matmul_tn(flat(sv["a"]), flat(dz), D, dz.shape[-1], f"even_w_in_grad{j}")
            grads["even"][j] = dict(w_in=g_w_in, w_out=g_w_out, sink=dsink[0], d_skip=dglu_b[1], glu_w=g_glu_w,
                                    glu_b=dglu_b[0])
        else:
            w = odd[j]
            dpm, dgt, dyout, dpp, dgate, dps = pool_out_bwd(dh, mods[i], sv["pm"], sv["gate"], w["pool_w"],
                                                            w["pool_scale"], w["w_out"], sv["yout"],
                                                            f"pool_out_bwd{j}")
            g_w_out = matmul_tn(flat(sv["mix"]), flat(dyout), D, D, f"odd_w_out_grad{j}")
            g_pool_w = jnp.stack([matmul_tn(flat(sv["pm"]), flat(dpp), POOL_G, POOL_G, f"pool_w_grad{j}_{gi}",
                                            a_col=gi, b_col=gi) for gi in range(4)])
            du = pool_band(dpm, lc, True, f"pool_band_bwd{j}")
            dh, dz, dmod, dg = norm_in_bwd([du, dgt], dh, sv["h"], g, mods[i], w["w_in"], f"odd_in_bwd{j}")
            g_w_in = matmul_tn(flat(sv["a"]), flat(dz), D, dz.shape[-1], f"odd_w_in_grad{j}")
            grads["odd"][j] = dict(w_in=g_w_in, w_out=g_w_out, pool_w=g_pool_w, pool_scale=dps[0])
        grads["norm_g"][i] = dg[0]
        dmods[i] = jnp.concatenate([dmod[:, :, 0:2, :], dgate[:, :, 0:1, :]], axis=2)
    g_ssm = mats_vjp(tuple(jnp.stack([d_mats[0][k], d_mats[1][k]]) for k in range(5)))
    for j in range(2):
        grads["even"][j]["ssm"] = tuple(gk[j] for gk in g_ssm) + (grads["even"][j].pop("d_skip"),)
    return loss_acc[0, 0], dh[:, lc:, :], dmods, grads


N_DEV = 8
HBM_SPEC = pl.BlockSpec(memory_space=pltpu.HBM)


def allgather8(x_shard, name):
    m_per, n = x_shard.shape

    def body(x_ref, out_ref, send_sems, recv_sems, local_sem):
        x, y, c = lax.axis_index("x"), lax.axis_index("y"), lax.axis_index("c")
        me, sibling = (x, y, c), (x, y, 1 - c)
        chips = [(1 - x, y), (x, 1 - y), (1 - x, 1 - y)]

        def rows(px, py, pc):
            return out_ref.at[pl.ds((4 * px + 2 * py + pc) * m_per, m_per), :]

        def copy(k, block, to, src=None):
            return pltpu.make_async_remote_copy(
                src_ref=rows(*block) if src is None else src, dst_ref=rows(*block),
                send_sem=send_sems.at[k], recv_sem=recv_sems.at[k], device_id=to, device_id_type=MESH)

        mine = pltpu.make_async_copy(x_ref, rows(*me), local_sem)
        mine.start()
        first = [copy(0, me, sibling, src=x_ref)]
        first += [copy(1 + j, me, (*chip, c), src=x_ref) for j, chip in enumerate(chips)]
        for cp in first:
            cp.start()
        passed = [copy(4 + j, (*chip, c), sibling) for j, chip in enumerate(chips)]
        for j, chip in enumerate(chips):
            copy(1 + j, (*chip, c), me).wait_recv()
            passed[j].start()
        copy(0, sibling, me).wait_recv()
        for j, chip in enumerate(chips):
            copy(4 + j, (*chip, 1 - c), me).wait_recv()
        for cp in first + passed:
            cp.wait_send()
        mine.wait()

    return pl.pallas_call(
        body, name=name,
        out_shape=jax.ShapeDtypeStruct((N_DEV * m_per, n), x_shard.dtype),
        in_specs=[pl.BlockSpec(memory_space=pltpu.VMEM)],
        out_specs=pl.BlockSpec(memory_space=pltpu.VMEM),
        scratch_shapes=[pltpu.SemaphoreType.DMA((7,)), pltpu.SemaphoreType.DMA((7,)), pltpu.SemaphoreType.DMA],
        compiler_params=_cp(56),
    )(x_shard)


def xy_exchange(srcs, scatter, name):
    n = len(srcs)

    def body(*refs):
        start, wait = _xy_copies(refs[:n], refs[n:2 * n], *refs[2 * n:], scatter)
        start()
        wait()

    return pl.pallas_call(
        body, name=name, out_shape=_xy_out_shapes(srcs, scatter),
        in_specs=[HBM_SPEC] * n, out_specs=[HBM_SPEC] * n, scratch_shapes=_xy_sems(n),
    )(*srcs)


def _xy_out_shapes(srcs, scatter):
    return [jax.ShapeDtypeStruct((4,) + (tuple(s.shape[1:]) if scatter else tuple(s.shape)), s.dtype) for s in srcs]


def _xy_sems(n):
    return [pltpu.SemaphoreType.DMA((3 * n,)), pltpu.SemaphoreType.DMA((3 * n,)), pltpu.SemaphoreType.DMA((n,))]


def _xy_copies(src_refs, out_refs, send_sems, recv_sems, local_sems, scatter):
    n = len(src_refs)

    def parts():
        x, y, c = lax.axis_index("x"), lax.axis_index("y"), lax.axis_index("c")
        my = 2 * x + y
        peers = [(1 - x, y), (x, 1 - y), (1 - x, 1 - y)]

        def piece(i, pos):
            return src_refs[i].at[pos] if scatter else src_refs[i]

        def copy(i, k, src_pos, dst_pos):
            px, py = peers[k]
            return pltpu.make_async_remote_copy(
                src_ref=piece(i, src_pos), dst_ref=out_refs[i].at[dst_pos], send_sem=send_sems.at[3 * i + k],
                recv_sem=recv_sems.at[3 * i + k], device_id=(px, py, c), device_id_type=MESH)

        local = [pltpu.make_async_copy(piece(i, my), out_refs[i].at[my], local_sems.at[i]) for i in range(n)]
        sends = [copy(i, k, 2 * px + py, my) for i in range(n) for k, (px, py) in enumerate(peers)]
        lands = [copy(i, k, my, 2 * px + py) for i in range(n) for k, (px, py) in enumerate(peers)]
        return local, sends, lands

    def start():
        local, sends, _ = parts()
        for cp in local + sends:
            cp.start()

    def wait():
        local, sends, lands = parts()
        for cp in lands:
            cp.wait_recv()
        for cp in sends:
            cp.wait_send()
        for cp in local:
            cp.wait()

    return start, wait


def sibling_exchange(srcs, name):
    n = len(srcs)

    def body(*refs):
        src_refs, out_refs = refs[:n], refs[n:2 * n]
        send_sems, recv_sems = refs[2 * n:]
        peer = (lax.axis_index("x"), lax.axis_index("y"), 1 - lax.axis_index("c"))
        cps = [pltpu.make_async_remote_copy(src_ref=src_refs[i], dst_ref=out_refs[i], send_sem=send_sems.at[i],
                                            recv_sem=recv_sems.at[i], device_id=peer, device_id_type=MESH)
               for i in range(n)]
        for cp in cps:
            cp.start()
        for cp in cps:
            cp.wait()

    return pl.pallas_call(
        body, name=name, out_shape=[jax.ShapeDtypeStruct(s.shape, s.dtype) for s in srcs],
        in_specs=[HBM_SPEC] * n, out_specs=[HBM_SPEC] * n,
        scratch_shapes=[pltpu.SemaphoreType.DMA((n,)), pltpu.SemaphoreType.DMA((n,))],
    )(*srcs)


def _row_tile(rows, bytes_per_row, limit):
    best = None
    for tr in range(8, rows + 1, 8):
        if rows % tr == 0 and tr * bytes_per_row <= limit:
            best = tr
    return best if best is not None else rows


def sum_slots(x, name):
    n, rows, cols = x.shape
    tr = _row_tile(rows, n * cols * 4, 4 * MB)

    def body(x_ref, o_ref):
        acc = x_ref[0].astype(F32)
        for k in range(1, n):
            acc = acc + x_ref[k].astype(F32)
        o_ref[...] = acc

    return pl.pallas_call(
        body, name=name, grid=(rows // tr,),
        in_specs=[pl.BlockSpec((n, tr, cols), lambda r: (0, r, 0))],
        out_specs=pl.BlockSpec((tr, cols), lambda r: (r, 0)),
        out_shape=jax.ShapeDtypeStruct((rows, cols), F32),
        compiler_params=_cp(32, 1),
    )(x)


ADA_COLS = 3 * D // 4
C_ROWS = 8


def ada_fwd(c_all, ada_w, ada_b_cols, name):
    nrow = c_all.shape[0]

    def body(c_ref, w_ref, b_ref, o_ref):
        s, _ = _silu_and_grad(c_ref[...])
        o_ref[...] = _dot(s.astype(BF16), w_ref[...].astype(BF16)) + b_ref[...]

    return pl.pallas_call(
        body, name=name, grid=(DEPTH,),
        in_specs=[pl.BlockSpec((nrow, D), lambda i: (0, 0)), pl.BlockSpec((None, D, ADA_COLS), lambda i: (i, 0, 0)),
                  pl.BlockSpec((None, 1, ADA_COLS), lambda i: (i, 0, 0))],
        out_specs=pl.BlockSpec((None, nrow, ADA_COLS), lambda i: (i, 0, 0)),
        out_shape=jax.ShapeDtypeStruct((DEPTH, nrow, ADA_COLS), F32),
        compiler_params=_cp(32, 1),
    )(c_all, ada_w, ada_b_cols)


def ada_bwd(c_all, d_cols, ada_w, name):
    nrow = c_all.shape[0]

    def body(c_ref, d_ref, w_ref, gw_ref, ds_ref):
        @pl.when(pl.program_id(0) == 0)
        def _():
            ds_ref[...] = jnp.zeros_like(ds_ref)
        s, _ = _silu_and_grad(c_ref[...])
        dl = d_ref[...]
        gw_ref[...] = _dot_tn(s.astype(BF16), dl.astype(BF16))
        rid = lax.broadcasted_iota(jnp.int32, (nrow, 1), 0) % C_ROWS
        dctx = jnp.where((rid == 2) | (rid == 3), dl, 0.0).astype(BF16)
        ds_ref[0:1, :] += _rowsum(_dot_nt(dctx, w_ref[...].astype(BF16)))

    return pl.pallas_call(
        body, name=name, grid=(DEPTH,),
        in_specs=[pl.BlockSpec((nrow, D), lambda i: (0, 0)), pl.BlockSpec((None, nrow, ADA_COLS), lambda i: (i, 0, 0)),
                  pl.BlockSpec((None, D, ADA_COLS), lambda i: (i, 0, 0))],
        out_specs=[pl.BlockSpec((None, D, ADA_COLS), lambda i: (i, 0, 0)), pl.BlockSpec((8, D), lambda i: (0, 0))],
        out_shape=[jax.ShapeDtypeStruct((DEPTH, D, ADA_COLS), F32), jax.ShapeDtypeStruct((8, D), F32)],
        compiler_params=_cp(32, 1),
    )(c_all, d_cols, ada_w)


def ada_bias_grad(d_all, name):
    nrow = d_all.shape[1]

    def body(d_ref, o_ref):
        o_ref[...] = jnp.broadcast_to(_rowsum(d_ref[...]), o_ref.shape)

    return pl.pallas_call(
        body, name=name, grid=(DEPTH,),
        in_specs=[pl.BlockSpec((None, nrow, 3 * D), lambda i: (i, 0, 0))],
        out_specs=pl.BlockSpec((None, 8, 3 * D), lambda i: (i, 0, 0)),
        out_shape=jax.ShapeDtypeStruct((DEPTH, 8, 3 * D), F32),
        compiler_params=_cp(32, 1),
    )(d_all)


def silu_chain(ds, c, name):
    def body(ds_ref, c_ref, o_ref):
        _, dsl = _silu_and_grad(c_ref[...])
        o_ref[...] = ds_ref[...] * dsl

    return pl.pallas_call(body, name=name, out_shape=jax.ShapeDtypeStruct(ds.shape, F32))(ds, c)


def _flat_cols(shape):
    size = int(np.prod(shape))
    if shape[-1] >= 128:
        return shape[-1]
    for cols in (1024, 128):
        if size % cols == 0:
            return cols
    return shape[-1]


def adamw(w, m, v, grads, name):
    shape = w.shape
    cols = _flat_cols(shape)
    as2d = lambda a: a.reshape(-1, cols)
    rows = w.size // cols
    tr = _row_tile(rows, cols * 4, MB)
    k = len(grads)

    def body(*refs):
        w_ref, m_ref, v_ref = refs[:3]
        g_refs = refs[3:3 + k]
        g_out, d_out, m_out, v_out = refs[3 + k:]
        g = g_refs[0][...]
        for r in g_refs[1:]:
            g = g + r[...]
        g_out[...] = g
        mn = ADAM_B1 * m_ref[...] + (1.0 - ADAM_B1) * g
        vn = ADAM_B2 * v_ref[...] + (1.0 - ADAM_B2) * (g * g)
        m_out[...] = mn
        v_out[...] = vn
        m_hat = mn / (1.0 - ADAM_B1 ** ADAM_STEP)
        v_hat = vn / (1.0 - ADAM_B2 ** ADAM_STEP)
        d_out[...] = -ADAM_LR * (m_hat / (jnp.sqrt(v_hat) + ADAM_EPS) + ADAM_WD * w_ref[...])

    spec = pl.BlockSpec((tr, cols), lambda r: (r, 0))
    outs = pl.pallas_call(
        body, name=name, grid=(rows // tr,),
        in_specs=[spec] * (3 + k), out_specs=[spec] * 4,
        out_shape=[jax.ShapeDtypeStruct((rows, cols), F32)] * 4,
        compiler_params=_cp(32, 1),
    )(as2d(w), as2d(m), as2d(v), *[as2d(g) for g in grads])
    return tuple(o.reshape(shape) for o in outs)


BIG = (("even_w_in", (2, D, 576), 2), ("even_w_out", (2, 256, D), 1), ("glu_w", (2, 128, SSM_W), 1),
       ("odd_w_in", (2, D, 512), 2), ("odd_w_out", (2, 256, D), 1), ("pool_w", (2, 4, 64, POOL_G), 2))


def _full_shape(shard, axis):
    return tuple(4 * s if a == axis else s for a, s in enumerate(shard))


def _to_shards(full, shard, axis):
    return jnp.moveaxis(full.reshape(shard[:axis] + (4,) + shard[axis:]), axis, 0)


def _from_shards(stacked, shard, axis):
    return jnp.moveaxis(stacked, 0, axis).reshape(_full_shape(shard, axis))


GRAD_KEY = dict(even_w_in=("even", "w_in"), even_w_out=("even", "w_out"), glu_w=("even", "glu_w"),
                odd_w_in=("odd", "w_in"), odd_w_out=("odd", "w_out"), pool_w=("odd", "pool_w"))
RIDE = tuple((n, 1) for n, _, _ in BIG[:3]) + tuple((n, j) for n, _, _ in BIG[3:] for j in range(2))
LAST = tuple((n, 0) for n, _, _ in BIG[:3])


def _grad_piece(grads, name, j):
    kind, key = GRAD_KEY[name]
    shard, axis = next((s, a) for n, s, a in BIG if n == name)
    return _to_shards(grads[kind][j][key], shard[1:], axis - 1).astype(BF16)


SMALL = (("ds_ctx", (D,)), ("norm_g", (DEPTH, D)), ("final_g", (D,)), ("attn_sink", (2, N_HEADS)),
         ("ssm_a_re", (2, 2, G, P)), ("ssm_a_im", (2, 2, G, P)), ("ssm_log_dt", (2, 2, G)),
         ("ssm_b_re", (2, 2, G, P, C)), ("ssm_b_im", (2, 2, G, P, C)), ("ssm_c_re", (2, 2, G, C, P)),
         ("ssm_c_im", (2, 2, G, C, P)), ("ssm_d", (2, SSM_W)), ("glu_b", (2, SSM_W)), ("pool_scale", (2, D)))
SMALL_PAD = 8 * 128


def pack_small(vals):
    flat = jnp.concatenate([vals[n].reshape(-1) for n, _ in SMALL])
    pad = (-flat.shape[0]) % SMALL_PAD
    return jnp.pad(flat, (0, pad)).reshape(-1, 128)


def unpack_small(packed):
    flat, out, off = packed.reshape(-1), {}, 0
    for n, shape in SMALL:
        size = int(np.prod(shape))
        out[n] = flat[off:off + size].reshape(shape)
        off += size
    return out


WEIGHT_NAMES = ('c_ctx', 'ada_w', 'ada_b', 'norm_g', 'even_w_in', 'even_w_out', 'attn_sink', 'ssm_a_re', 'ssm_a_im',
                'ssm_log_dt', 'ssm_b_re', 'ssm_b_im', 'ssm_c_re', 'ssm_c_im', 'ssm_d', 'glu_w', 'glu_b', 'odd_w_in',
                'odd_w_out', 'pool_w', 'pool_scale', 'final_g')
SSM_NAMES = ('ssm_a_re', 'ssm_a_im', 'ssm_log_dt', 'ssm_b_re', 'ssm_b_im', 'ssm_c_re', 'ssm_c_im', 'ssm_d')


def kernel(x, c, ctx, c_ctx, ada_w, ada_b, norm_g, even_w_in, even_w_out, attn_sink, ssm_a_re, ssm_a_im, ssm_log_dt, ssm_b_re, ssm_b_im, ssm_c_re, ssm_c_im, ssm_d, glu_w, glu_b, odd_w_in, odd_w_out, pool_w, pool_scale, final_g, loss_target, m_c_ctx, m_ada_w, m_ada_b, m_norm_g, m_even_w_in, m_even_w_out, m_attn_sink, m_ssm_a_re, m_ssm_a_im, m_ssm_log_dt, m_ssm_b_re, m_ssm_b_im, m_ssm_c_re, m_ssm_c_im, m_ssm_d, m_glu_w, m_glu_b, m_odd_w_in, m_odd_w_out, m_pool_w, m_pool_scale, m_final_g, v_c_ctx, v_ada_w, v_ada_b, v_norm_g, v_even_w_in, v_even_w_out, v_attn_sink, v_ssm_a_re, v_ssm_a_im, v_ssm_log_dt, v_ssm_b_re, v_ssm_b_im, v_ssm_c_re, v_ssm_c_im, v_ssm_d, v_glu_w, v_glu_b, v_odd_w_in, v_odd_w_out, v_pool_w, v_pool_scale, v_final_g):
    env = dict(locals())
    weights = {n: env[n] for n in WEIGHT_NAMES}
    bsz = x.shape[0]
    ax, ay, ac = lax.axis_index("x"), lax.axis_index("y"), lax.axis_index("c")
    pos = 2 * ax + ay
    dev = 2 * pos + ac

    c_rows = jnp.concatenate([c, c_ctx.reshape(1, D), c_ctx.reshape(1, D), jnp.zeros((C_ROWS - bsz - 2, D), F32)])
    c_all = allgather8(c_rows, "gather_c")
    ada_b_cols = lax.dynamic_slice(ada_b, (0, pos * ADA_COLS), (DEPTH, ADA_COLS)).reshape(DEPTH, 1, ADA_COLS)
    mod_cols = ada_fwd(c_all, ada_w, ada_b_cols, "ada_fwd")
    nrow = N_DEV * C_ROWS
    misc = jnp.concatenate([mod_cols.reshape(DEPTH * nrow, ADA_COLS),
                            jnp.pad(pool_scale, ((0, 6), (0, ADA_COLS - pool_scale.shape[1])))])
    misc_all = allgather8(misc, "gather_mod").reshape(4, 2, DEPTH * nrow + 8, ADA_COLS)[:, 0]
    mod_full = misc_all[:, :DEPTH * nrow].reshape(4, DEPTH, nrow, ADA_COLS).transpose(1, 2, 0, 3)
    mod_mine = lax.dynamic_slice(mod_full.reshape(DEPTH, nrow, 3 * D), (0, dev * C_ROWS, 0), (DEPTH, C_ROWS, 3 * D))
    mods = []
    for i in range(DEPTH):
        lat = mod_mine[i, :bsz].reshape(bsz, 1, 3, D)
        con = jnp.broadcast_to(mod_mine[i, bsz].reshape(1, 1, 3, D), (bsz, 1, 3, D))
        mods.append(jnp.pad(jnp.concatenate([con, lat], axis=1), ((0, 0), (0, 0), (0, 5), (0, 0))))
    pool_scale_full = misc_all[:, DEPTH * nrow:DEPTH * nrow + 2, :pool_scale.shape[1]].transpose(1, 0, 2).reshape(2, D)

    first, shard, axis = BIG[0]
    w_in_full = _from_shards(xy_exchange([weights[first].astype(BF16)], False, "gather_w_in")[0], shard, axis)
    even = [dict(w_in=w_in_full[j], sink=attn_sink[j], ssm=tuple(weights[n][j] for n in SSM_NAMES),
                 glu_b=glu_b[j].reshape(1, SSM_W)) for j in range(2)]
    odd = [dict(pool_scale=pool_scale_full[j].reshape(1, D)) for j in range(2)]

    def fill(gathered):
        full = {n: _from_shards(g, shard, axis) for (n, shard, axis), g in zip(BIG[1:], gathered)}
        for j in range(2):
            even[j].update(w_out=full["even_w_out"][j], glu_w=full["glu_w"][j])
            odd[j].update(w_in=full["odd_w_in"][j], w_out=full["odd_w_out"][j], pool_w=full["pool_w"][j])

    late = ([weights[n].astype(BF16) for n, _, _ in BIG[1:]], fill)
    loss_local, grad_x, dmods, grads = local_step(
        x, ctx, loss_target, mods, norm_g, final_g, even, odd, late,
        scatter=lambda grads: [_grad_piece(grads, n, j) for n, j in RIDE])
    loss = lax.psum(loss_local, ("x", "y", "c"))

    d_rows = jnp.stack([jnp.concatenate([dm[:, 1].reshape(bsz, 3 * D), dm[:, 0].reshape(bsz, 3 * D),
                                         jnp.zeros((C_ROWS - 2 * bsz, 3 * D), F32)]) for dm in dmods])
    d_all = allgather8(d_rows.reshape(DEPTH * C_ROWS, 3 * D), "gather_dmod")
    d_all = d_all.reshape(N_DEV, DEPTH, C_ROWS, 3 * D).transpose(1, 0, 2, 3).reshape(DEPTH, nrow, 3 * D)
    d_cols = lax.dynamic_slice(d_all, (0, 0, pos * ADA_COLS), (DEPTH, nrow, ADA_COLS))
    g_ada_w, ds_ctx = ada_bwd(c_all, d_cols, ada_w, "ada_bwd")
    g_ada_b = ada_bias_grad(d_all, "ada_bias_grad")[:, 0]

    small = dict(ds_ctx=ds_ctx[0] * (ac == 0).astype(F32), norm_g=jnp.stack(grads["norm_g"]), final_g=grads["final_g"],
                 attn_sink=jnp.stack([grads["even"][j]["sink"] for j in range(2)]),
                 glu_b=jnp.stack([grads["even"][j]["glu_b"] for j in range(2)]),
                 pool_scale=jnp.stack([grads["odd"][j]["pool_scale"] for j in range(2)]))
    for k, n in enumerate(SSM_NAMES):
        small[n] = jnp.stack([grads["even"][j]["ssm"][k] for j in range(2)])
    packed = pack_small(small)
    small_sum = sum_slots(allgather8(packed, "gather_small").reshape(N_DEV, packed.shape[0], 128), "sum_small")
    g_small = unpack_small(small_sum)
    g_small["c_ctx"] = silu_chain(g_small.pop("ds_ctx").reshape(1, D), c_ctx.reshape(1, D), "c_ctx_grad").reshape(D)
    g_small["ada_b"] = g_ada_b
    g_small["pool_scale"] = lax.dynamic_slice(g_small["pool_scale"], (0, pos * 256), (2, 256))

    landed = dict(zip(RIDE, grads["landed"]))
    landed.update(zip(LAST, xy_exchange([_grad_piece(grads, n, j) for n, j in LAST], True, "scatter_grads")))
    mine4 = [jnp.stack([sum_slots(landed[n, j].reshape(4, -1, shard[-1]), f"sum_positions_{n}{j}").reshape(shard[1:])
                        for j in range(2)]) for n, shard, _ in BIG]
    other4 = sibling_exchange(mine4, "swap_cores")
    g_mine = dict(zip([n for n, _, _ in BIG], mine4))
    g_other = dict(zip([n for n, _, _ in BIG], other4))

    results = {}
    for n in WEIGHT_NAMES:
        if n in g_mine:
            gs = [g_mine[n], g_other[n]]
        elif n == "ada_w":
            gs = [g_ada_w]
        else:
            gs = [g_small[n]]
        results[n] = adamw(weights[n], env["m_" + n], env["v_" + n], gs, "adamw_" + n)
    outs = [loss, grad_x]
    for k in range(4):
        outs += [results[n][k] for n in WEIGHT_NAMES]
    return tuple(outs)
```

```python
import functools

import numpy as np
import jax
import jax.numpy as jnp
from jax import lax
from jax.experimental import pallas as pl
from jax.experimental.pallas import tpu as pltpu

F32 = jnp.float32
BF16 = jnp.bfloat16
MESH = pl.DeviceIdType.MESH

D = 1024
DEPTH = 4
EPS = 1e-6
NEG_INF = -1e30
GRID_W = 64
ROPE_BASE = 10000.0
ROPE_FREQS = 16
HEAD_DIM = 64
N_HEADS = 8
N_KV = 2
GROUP = 4
ATTN_W = N_HEADS * HEAD_DIM
KV_W = N_KV * HEAD_DIM
WINDOW = 128
AB = 128
SSM_W = 512
G = 32
C = 16
P = 64
Q = 16
QC = Q * C
P2 = 2 * P
SCAN_G = 16
POOL_R = (1, 2, 4, 8)
POOL_G = 256
HALO = 8
TM = 256
EVEN_SPLITS = (512, 256, 512, 512, 512)
ODD_SPLITS = (1024, 1024)

ADAM_LR = 0.001
ADAM_B1 = 0.9
ADAM_B2 = 0.999
ADAM_EPS = 1e-08
ADAM_WD = 0.01
ADAM_STEP = 10

MB = 1024 * 1024


def _cp(vmem_mb=48, n_axes=0):
    kw = dict(vmem_limit_bytes=vmem_mb * MB)
    if n_axes:
        kw["dimension_semantics"] = ("arbitrary",) * n_axes
    return pltpu.CompilerParams(**kw)


def _sig(x):
    return 1.0 / (1.0 + jnp.exp(-x))


def _silu_and_grad(x):
    s = _sig(x)
    return x * s, s * (1.0 + x * (1.0 - s))


_GELU_C = 0.7978845608028654
_GELU_A = 0.044715


def _gelu_and_grad(x):
    th = jnp.tanh(_GELU_C * (x + _GELU_A * x * x * x))
    val = 0.5 * x * (1.0 + th)
    grad = 0.5 * (1.0 + th) + 0.5 * x * (1.0 - th * th) * _GELU_C * (1.0 + 3.0 * _GELU_A * x * x)
    return val, grad


def _rms(h):
    r = lax.rsqrt(jnp.mean(h * h, axis=-1, keepdims=True) + EPS)
    return h * r, r


def _dot(a, b):
    return jnp.dot(a, b, preferred_element_type=F32)


def _dot_nt(a, b):
    return lax.dot_general(a, b, (((1,), (1,)), ((), ())), preferred_element_type=F32)


def _dot_tn(a, b):
    return lax.dot_general(a, b, (((0,), (0,)), ((), ())), preferred_element_type=F32)


def _rowsum(x):
    return jnp.sum(x, axis=0, keepdims=True)


def _seg(t):
    return jnp.minimum(t, 1)


def _row_spec(n):
    return pl.BlockSpec((None, TM, n), lambda b, t: (b, t, 0))


def _const_spec(shape):
    nd = len(shape)
    return pl.BlockSpec(shape, lambda b, t: (0,) * nd)


def _mod_spec():
    return pl.BlockSpec((None, None, 8, D), lambda b, t: (b, _seg(t), 0, 0))


def norm_in(h, g, mod, w, splits, gates, name):
    bsz, t_len, _ = h.shape
    n = w.shape[1]
    offs = [int(v) for v in np.cumsum((0,) + tuple(splits))]

    def body(h_ref, g_ref, mod_ref, w_ref, a_ref, *outs):
        xh, _ = _rms(h_ref[...])
        a = xh * g_ref[...] * (1.0 + mod_ref[1:2, :]) + mod_ref[0:1, :]
        ab = a.astype(BF16)
        a_ref[...] = ab
        z = _dot(ab, w_ref[...])
        for o, lo, hi in zip(outs, offs[:-1], offs[1:]):
            o[...] = z[:, lo:hi].astype(o.dtype)

    return pl.pallas_call(
        body, name=name, grid=(bsz, t_len // TM),
        in_specs=[_row_spec(D), _const_spec((1, D)), _mod_spec(), _const_spec((D, n))],
        out_specs=[_row_spec(D)] + [_row_spec(s) for s in splits],
        out_shape=[jax.ShapeDtypeStruct((bsz, t_len, D), BF16)]
        + [jax.ShapeDtypeStruct((bsz, t_len, s), BF16 if k in gates else F32) for k, s in enumerate(splits)],
        compiler_params=_cp(48, 2),
    )(h, g, mod, w)


def norm_in_bwd(dparts, dh_in, h, g, mod, w, name, skip=None):
    bsz, t_len, _ = h.shape
    n = w.shape[1]
    k = len(dparts)
    extra = [] if skip is None else [skip[1], skip[2]]

    def body(*refs):
        parts = [r[...] for r in refs[:k]]
        if skip is not None:
            parts[skip[0]] = parts[skip[0]] + refs[k][...] * refs[k + 1][...]
        parts = [p.astype(BF16) for p in parts]
        dh_in_ref, h_ref, g_ref, mod_ref, w_ref, dh_ref, dz_ref, dmod_ref, dg_ref = refs[k + len(extra):]
        b, t = pl.program_id(0), pl.program_id(1)
        dz = jnp.concatenate(parts, axis=1)
        dz_ref[...] = dz
        da = _dot_nt(dz, w_ref[...])
        xh, r = _rms(h_ref[...])
        gg = g_ref[...]
        sc1 = 1.0 + mod_ref[1:2, :]

        @pl.when(t <= 1)
        def _():
            dmod_ref[...] = jnp.zeros_like(dmod_ref)

        @pl.when((b == 0) & (t == 0))
        def _():
            dg_ref[...] = jnp.zeros_like(dg_ref)

        dmod_ref[0:1, :] += _rowsum(da)
        dmod_ref[1:2, :] += _rowsum(da * (xh * gg))
        dg_ref[0:1, :] += _rowsum(da * sc1 * xh)
        dxh = da * gg * sc1
        dh_ref[...] = dh_in_ref[...] + r * (dxh - xh * jnp.mean(dxh * xh, axis=-1, keepdims=True))

    return pl.pallas_call(
        body, name=name, grid=(bsz, t_len // TM),
        in_specs=[_row_spec(p.shape[-1]) for p in dparts]
        + ([_row_spec(extra[0].shape[-1]), _const_spec(extra[1].shape)] if extra else [])
        + [_row_spec(D), _row_spec(D), _const_spec((1, D)), _mod_spec(), _const_spec((D, n))],
        out_specs=[_row_spec(D), _row_spec(n), _mod_spec(), _const_spec((8, D))],
        out_shape=[jax.ShapeDtypeStruct((bsz, t_len, D), F32), jax.ShapeDtypeStruct((bsz, t_len, n), BF16),
                   jax.ShapeDtypeStruct((bsz, 2, 8, D), F32), jax.ShapeDtypeStruct((8, D), F32)],
        compiler_params=_cp(56, 2),
    )(*dparts, *extra, dh_in, h, g, mod, w)


def matmul_tn(a, b, m, n, name, a_col=0, b_col=0):
    rows = a.shape[0]
    tr = 512 if rows % 512 == 0 else rows
    tn = n
    for cand in (1024, 768, 512, 256, 128):
        if n > 1024 and n % cand == 0:
            tn = cand
            break
    nb = n // tn

    def body(a_ref, b_ref, o_ref):
        @pl.when(pl.program_id(1) == 0)
        def _():
            o_ref[...] = jnp.zeros_like(o_ref)
        o_ref[...] += _dot_tn(a_ref[...].astype(BF16), b_ref[...].astype(BF16))

    return pl.pallas_call(
        body, name=name, grid=(nb, rows // tr),
        in_specs=[pl.BlockSpec((tr, m), lambda j, r: (r, a_col)),
                  pl.BlockSpec((tr, tn), lambda j, r: (r, b_col * nb + j))],
        out_specs=pl.BlockSpec((m, tn), lambda j, r: (0, j)),
        out_shape=jax.ShapeDtypeStruct((m, n), F32),
        compiler_params=_cp(48, 2),
    )(a, b)


def even_out(h, mod, o_attn, g_attn, y_scan, u, d_skip, g_ssm, glu_w, glu_b, w_out, name):
    bsz, t_len, _ = h.shape

    def body(h_ref, mod_ref, oa_ref, ga_ref, ys_ref, u_ref, dk_ref, gs_ref, gw_ref, gb_ref, wo_ref,
             hn_ref, mix_ref, yo_ref):
        zz, _ = _gelu_and_grad(ys_ref[...] + u_ref[...] * dk_ref[...])
        s = _dot(zz.astype(BF16), gw_ref[...]) + gb_ref[...]
        o_ssm = zz * _sig(s)
        sa, _ = _silu_and_grad(ga_ref[...].astype(F32))
        ss, _ = _silu_and_grad(gs_ref[...].astype(F32))
        mb = jnp.concatenate([oa_ref[...] * sa, o_ssm * ss], axis=1).astype(BF16)
        mix_ref[...] = mb
        yo = _dot(mb, wo_ref[...])
        yo_ref[...] = yo.astype(BF16)
        hn_ref[...] = h_ref[...] + mod_ref[2:3, :] * yo

    return pl.pallas_call(
        body, name=name, grid=(bsz, t_len // TM),
        in_specs=[_row_spec(D), _mod_spec(), _row_spec(512), _row_spec(512), _row_spec(512), _row_spec(512),
                  _const_spec((1, 512)), _row_spec(512), _const_spec((512, 512)), _const_spec((1, 512)),
                  _const_spec((D, D))],
        out_specs=[_row_spec(D), _row_spec(D), _row_spec(D)],
        out_shape=[jax.ShapeDtypeStruct((bsz, t_len, D), F32), jax.ShapeDtypeStruct((bsz, t_len, D), BF16),
                   jax.ShapeDtypeStruct((bsz, t_len, D), BF16)],
        compiler_params=_cp(48, 2),
    )(h, mod, o_attn, g_attn, y_scan, u, d_skip, g_ssm, glu_w, glu_b, w_out)


def even_out_bwd(dh, mod, o_attn, g_attn, y_scan, u, d_skip, g_ssm, glu_w, glu_b, w_out, yout, name):
    bsz, t_len, _ = dh.shape

    def body(dh_ref, mod_ref, oa_ref, ga_ref, ys_ref, u_ref, dk_ref, gs_ref, gw_ref, gb_ref, wo_ref, yo_ref,
             doa_ref, dga_ref, dgs_ref, dys_ref, dyo_ref, zz_ref, ds_ref, dgate_ref, dgb_ref):
        b, t = pl.program_id(0), pl.program_id(1)
        dhv = dh_ref[...]

        @pl.when(t <= 1)
        def _():
            dgate_ref[...] = jnp.zeros_like(dgate_ref)

        @pl.when((b == 0) & (t == 0))
        def _():
            dgb_ref[...] = jnp.zeros_like(dgb_ref)

        dgate_ref[0:1, :] += _rowsum(dhv * yo_ref[...].astype(F32))
        dyb = (mod_ref[2:3, :] * dhv).astype(BF16)
        dyo_ref[...] = dyb
        dmix = _dot_nt(dyb, wo_ref[...])
        sa, dsa = _silu_and_grad(ga_ref[...].astype(F32))
        doa_ref[...] = dmix[:, :512] * sa
        dga_ref[...] = (dmix[:, :512] * oa_ref[...] * dsa).astype(BF16)
        uv = u_ref[...]
        zz, dzz_dy = _gelu_and_grad(ys_ref[...] + uv * dk_ref[...])
        zb = zz.astype(BF16)
        zz_ref[...] = zb
        sg = _sig(_dot(zb, gw_ref[...]) + gb_ref[...])
        ss, dss = _silu_and_grad(gs_ref[...].astype(F32))
        dm = dmix[:, 512:]
        dgs_ref[...] = (dm * (zz * sg) * dss).astype(BF16)
        do = dm * ss
        ds = do * zz * sg * (1.0 - sg)
        dsb = ds.astype(BF16)
        ds_ref[...] = dsb
        dgb_ref[0:1, :] += _rowsum(ds)
        dys = (do * sg + _dot_nt(dsb, gw_ref[...])) * dzz_dy
        dys_ref[...] = dys
        dgb_ref[1:2, :] += _rowsum(dys * uv)

    r512 = jax.ShapeDtypeStruct((bsz, t_len, 512), F32)
    b512 = jax.ShapeDtypeStruct((bsz, t_len, 512), BF16)
    return pl.pallas_call(
        body, name=name, grid=(bsz, t_len // TM),
        in_specs=[_row_spec(D), _mod_spec(), _row_spec(512), _row_spec(512), _row_spec(512), _row_spec(512),
                  _const_spec((1, 512)), _row_spec(512), _const_spec((512, 512)), _const_spec((1, 512)),
                  _const_spec((D, D)), _row_spec(D)],
        out_specs=[_row_spec(512)] * 4 + [_row_spec(D), _row_spec(512), _row_spec(512), _mod_spec(),
                                           _const_spec((8, 512))],
        out_shape=[r512, b512, b512, r512, jax.ShapeDtypeStruct((bsz, t_len, D), BF16),
                   jax.ShapeDtypeStruct((bsz, t_len, 512), BF16), jax.ShapeDtypeStruct((bsz, t_len, 512), BF16),
                   jax.ShapeDtypeStruct((bsz, 2, 8, D), F32), jax.ShapeDtypeStruct((8, 512), F32)],
        compiler_params=_cp(48, 2),
    )(dh, mod, o_attn, g_attn, y_scan, u, d_skip, g_ssm, glu_w, glu_b, w_out, yout)


def _split3_dot(band, x):
    x1 = x.astype(BF16)
    r1 = x - x1.astype(F32)
    x2 = r1.astype(BF16)
    x3 = (r1 - x2.astype(F32)).astype(BF16)
    return _dot(band, x3) + _dot(band, x2) + _dot(band, x1)


def pool_band(x, lc, transpose, name):
    bsz, t_len, _ = x.shape
    assert lc == TM
    hb = TM // HALO

    def body(xp_ref, xc_ref, xn_ref, o_ref):
        t = pl.program_id(1)
        seg_lo = jnp.where(t == 0, 0, lc)
        seg_hi = jnp.where(t == 0, lc, t_len)
        cur = xc_ref[...]
        xh = jnp.concatenate([xp_ref[...], cur, xn_ref[...]], axis=0)
        row_t = t * TM + lax.broadcasted_iota(jnp.int32, (TM, 1), 0)
        col_s = t * TM - HALO + lax.broadcasted_iota(jnp.int32, (1, TM + 2 * HALO), 1)
        row_s = t * TM - HALO + lax.broadcasted_iota(jnp.int32, (TM + 2 * HALO, 1), 0)
        s_ok = (col_s >= seg_lo) & (col_s < seg_hi)
        outs = []
        for gi, r in enumerate(POOL_R):
            band = ((jnp.abs(row_t - col_s) <= r) & s_ok).astype(BF16)
            xg = xh[:, gi * POOL_G:(gi + 1) * POOL_G]
            if transpose:
                cnt_s = jnp.minimum(row_s + r, seg_hi - 1) - jnp.maximum(row_s - r, seg_lo) + 1
                xg = xg * (1.0 / jnp.maximum(cnt_s, 1).astype(F32))
            acc = _split3_dot(band, xg)
            if not transpose:
                cnt_t = jnp.minimum(row_t + r, seg_hi - 1) - jnp.maximum(row_t - r, seg_lo) + 1
                acc = acc * (1.0 / cnt_t.astype(F32))
            outs.append(acc - cur[:, gi * POOL_G:(gi + 1) * POOL_G])
        o_ref[...] = jnp.concatenate(outs, axis=1)

    return pl.pallas_call(
        body, name=name, grid=(bsz, t_len // TM),
        in_specs=[pl.BlockSpec((None, HALO, D), lambda b, t: (b, jnp.maximum(t * hb - 1, 0), 0)),
                  _row_spec(D),
                  pl.BlockSpec((None, HALO, D), lambda b, t: (b, jnp.minimum((t + 1) * hb, t_len // HALO - 1), 0))],
        out_specs=_row_spec(D),
        out_shape=jax.ShapeDtypeStruct((bsz, t_len, D), F32),
        compiler_params=_cp(48, 2),
    )(x, x, x)


def pool_out(h, mod, pm, gate, pool_w, pool_scale, w_out, name):
    bsz, t_len, _ = h.shape

    def body(h_ref, mod_ref, pm_ref, gt_ref, pw_ref, ps_ref, wo_ref, hn_ref, mix_ref, yo_ref):
        pmv = pm_ref[...]
        ppre = jnp.concatenate([_dot(pmv[:, g * POOL_G:(g + 1) * POOL_G].astype(BF16), pw_ref[g])
                                for g in range(4)], axis=1)
        sl, _ = _silu_and_grad(gt_ref[...].astype(F32))
        mb = (ppre * ps_ref[...] * sl).astype(BF16)
        mix_ref[...] = mb
        yo = _dot(mb, wo_ref[...])
        yo_ref[...] = yo.astype(BF16)
        hn_ref[...] = h_ref[...] + mod_ref[2:3, :] * yo

    return pl.pallas_call(
        body, name=name, grid=(bsz, t_len // TM),
        in_specs=[_row_spec(D), _mod_spec(), _row_spec(D), _row_spec(D), _const_spec((4, POOL_G, POOL_G)),
                  _const_spec((1, D)), _const_spec((D, D))],
        out_specs=[_row_spec(D), _row_spec(D), _row_spec(D)],
        out_shape=[jax.ShapeDtypeStruct((bsz, t_len, D), F32), jax.ShapeDtypeStruct((bsz, t_len, D), BF16),
                   jax.ShapeDtypeStruct((bsz, t_len, D), BF16)],
        compiler_params=_cp(48, 2),
    )(h, mod, pm, gate, pool_w, pool_scale, w_out)


def pool_out_bwd(dh, mod, pm, gate, pool_w, pool_scale, w_out, yout, name):
    bsz, t_len, _ = dh.shape

    def body(dh_ref, mod_ref, pm_ref, gt_ref, pw_ref, ps_ref, wo_ref, yo_ref,
             dpm_ref, dgt_ref, dyo_ref, dpp_ref, dgate_ref, dps_ref):
        b, t = pl.program_id(0), pl.program_id(1)
        dhv = dh_ref[...]

        @pl.when(t <= 1)
        def _():
            dgate_ref[...] = jnp.zeros_like(dgate_ref)

        @pl.when((b == 0) & (t == 0))
        def _():
            dps_ref[...] = jnp.zeros_like(dps_ref)

        dgate_ref[0:1, :] += _rowsum(dhv * yo_ref[...].astype(F32))
        dyb = (mod_ref[2:3, :] * dhv).astype(BF16)
        dyo_ref[...] = dyb
        dmix = _dot_nt(dyb, wo_ref[...])
        pmv = pm_ref[...]
        ppre = jnp.concatenate([_dot(pmv[:, g * POOL_G:(g + 1) * POOL_G].astype(BF16), pw_ref[g])
                                for g in range(4)], axis=1)
        ps = ps_ref[...]
        sl, dsl = _silu_and_grad(gt_ref[...].astype(F32))
        dp = dmix * sl
        dgt_ref[...] = (dmix * (ppre * ps) * dsl).astype(BF16)
        dps_ref[0:1, :] += _rowsum(dp * ppre)
        dppb = (dp * ps).astype(BF16)
        dpp_ref[...] = dppb
        dpm_ref[...] = jnp.concatenate([_dot_nt(dppb[:, g * POOL_G:(g + 1) * POOL_G], pw_ref[g])
                                        for g in range(4)], axis=1)

    return pl.pallas_call(
        body, name=name, grid=(bsz, t_len // TM),
        in_specs=[_row_spec(D), _mod_spec(), _row_spec(D), _row_spec(D), _const_spec((4, POOL_G, POOL_G)),
                  _const_spec((1, D)), _const_spec((D, D)), _row_spec(D)],
        out_specs=[_row_spec(D), _row_spec(D), _row_spec(D), _row_spec(D), _mod_spec(), _const_spec((8, D))],
        out_shape=[jax.ShapeDtypeStruct((bsz, t_len, D), F32), jax.ShapeDtypeStruct((bsz, t_len, D), BF16),
                   jax.ShapeDtypeStruct((bsz, t_len, D), BF16), jax.ShapeDtypeStruct((bsz, t_len, D), BF16),
                   jax.ShapeDtypeStruct((bsz, 2, 8, D), F32), jax.ShapeDtypeStruct((8, D), F32)],
        compiler_params=_cp(48, 2),
    )(dh, mod, pm, gate, pool_w, pool_scale, w_out, yout)


def loss_head(h, final_g, target, name):
    bsz, t_len, _ = h.shape

    def body(h_ref, g_ref, tg_ref, dh_ref, loss_ref, dg_ref):
        b, t = pl.program_id(0), pl.program_id(1)

        @pl.when((b == 0) & (t == 0))
        def _():
            loss_ref[...] = jnp.zeros_like(loss_ref)
            dg_ref[...] = jnp.zeros_like(dg_ref)

        lat = (t > 0).astype(F32)
        xh, r = _rms(h_ref[...])
        gg = g_ref[...]
        err = (xh * gg - tg_ref[...]) * lat
        loss_ref[...] += 0.5 * jnp.sum(jnp.mean(err * err, axis=-1, keepdims=True))
        dy = err * (1.0 / D)
        dg_ref[0:1, :] += _rowsum(dy * xh)
        dxh = dy * gg
        dh_ref[...] = r * (dxh - xh * jnp.mean(dxh * xh, axis=-1, keepdims=True))

    return pl.pallas_call(
        body, name=name, grid=(bsz, t_len // TM),
        in_specs=[_row_spec(D), _const_spec((1, D)),
                  pl.BlockSpec((None, TM, D), lambda b, t: (b, jnp.maximum(t - 1, 0), 0))],
        out_specs=[_row_spec(D), _const_spec((8, 128)), _const_spec((8, D))],
        out_shape=[jax.ShapeDtypeStruct((bsz, t_len, D), F32), jax.ShapeDtypeStruct((8, 128), F32),
                   jax.ShapeDtypeStruct((8, D), F32)],
        compiler_params=_cp(48, 2),
    )(h, final_g, target)


def _swap16(x):
    n = x.shape[-1]
    ax = x.ndim - 1
    lane = lax.broadcasted_iota(jnp.int32, x.shape, ax)
    return jnp.where((lane % 32) < 16, pltpu.roll(x, n - 16, ax), pltpu.roll(x, 16, ax))


def _rope(x, cos, sin):
    return x * cos + _swap16(x) * sin


def _rope_t(dy, cos, sin):
    return dy * cos + _swap16(dy * sin)


def rope_tables(lc, seq):
    rows = seq // GRID_W
    row = jnp.repeat(jnp.arange(rows, dtype=F32), GRID_W)
    col = jnp.tile(jnp.arange(GRID_W, dtype=F32), rows)
    inv_freq = ROPE_BASE ** (-jnp.arange(ROPE_FREQS, dtype=F32) / ROPE_FREQS)
    ar, ac = row[:, None] * inv_freq, col[:, None] * inv_freq
    cos = jnp.concatenate([jnp.cos(ar), jnp.cos(ar), jnp.cos(ac), jnp.cos(ac)], axis=1)
    sin = jnp.concatenate([-jnp.sin(ar), jnp.sin(ar), -jnp.sin(ac), jnp.sin(ac)], axis=1)
    cos = jnp.concatenate([jnp.ones((lc, HEAD_DIM), F32), cos], axis=0)
    sin = jnp.concatenate([jnp.zeros((lc, HEAD_DIM), F32), sin], axis=0)
    return jnp.tile(cos, (1, 2)), jnp.tile(sin, (1, 2))


def _attn_mask(i, lc, t_len):
    qrow = i * AB + lax.broadcasted_iota(jnp.int32, (AB, 1), 0)
    kloc = (i - 1) * AB + lax.broadcasted_iota(jnp.int32, (1, 3 * AB), 1)
    valid = (qrow >= lc) & (kloc >= lc) & (kloc < t_len) & (jnp.abs(qrow - kloc) <= WINDOW)
    mask = jnp.concatenate([valid, jnp.ones((AB, lc), jnp.bool_)], axis=1)
    return jnp.concatenate([mask] * GROUP, axis=0)


def _attn_specs(t_len, lc):
    nb = t_len // AB
    prev = lambda b, i: (b, jnp.maximum(i - 1, 0), 0)
    cur = lambda b, i: (b, i, 0)
    nxt = lambda b, i: (b, jnp.minimum(i + 1, nb - 1), 0)
    kv = [pl.BlockSpec((None, AB, 2 * KV_W), f) for f in (prev, cur, nxt)]
    kv.append(pl.BlockSpec((None, lc, 2 * KV_W), lambda b, i: (b, 0, 0)))
    tab = [pl.BlockSpec((AB, 128), lambda b, i, f=f: f(b, i)[1:]) for f in (prev, cur, nxt)]
    return kv, tab


def _attn_keys(kvp, kvc, kvn, kvx, cp, cc, cn, sp, sc, sn):
    kk = jnp.concatenate([_rope(kvp[:, :KV_W], cp, sp), _rope(kvc[:, :KV_W], cc, sc),
                          _rope(kvn[:, :KV_W], cn, sn), kvx[:, :KV_W]], axis=0)
    vv = jnp.concatenate([kvp[:, KV_W:], kvc[:, KV_W:], kvn[:, KV_W:], kvx[:, KV_W:]], axis=0)
    return kk, vv


def _stack_heads(x, hk):
    return jnp.concatenate([x[:, (GROUP * hk + g) * HEAD_DIM:(GROUP * hk + g + 1) * HEAD_DIM]
                            for g in range(GROUP)], axis=0)


def _sink_col(sink_ref, hk):
    return jnp.concatenate([jnp.full((AB, 1), sink_ref[GROUP * hk + g], F32) for g in range(GROUP)], axis=0)


def attn_fwd(q, kv, cos, sin, sink, lc, name, carry=()):
    bsz, t_len, _ = q.shape
    nb = t_len // AB
    kv_specs, tab_specs = _attn_specs(t_len, lc)
    scale = HEAD_DIM ** -0.5
    nc = len(carry)

    def body(sink_ref, q_ref, kvp_ref, kvc_ref, kvn_ref, kvx_ref, cp, cc, cn, sp, sc, sn, *rest):
        o_ref, lse_ref = rest[nc:nc + 2]
        b, i = pl.program_id(0), pl.program_id(1)
        if nc:
            start, wait = _xy_copies(rest[:nc], rest[nc + 2:2 * nc + 2], *rest[2 * nc + 2:], False)
            pl.when((b == 0) & (i == 0))(start)
        mask = _attn_mask(i, lc, t_len)
        qr = _rope(q_ref[...], jnp.tile(cc[...], (1, 4)), jnp.tile(sc[...], (1, 4)))
        kk, vv = _attn_keys(kvp_ref[...], kvc_ref[...], kvn_ref[...], kvx_ref[...],
                            cp[...], cc[...], cn[...], sp[...], sc[...], sn[...])
        outs, lses = [], []
        for hk in range(N_KV):
            kh = kk[:, hk * HEAD_DIM:(hk + 1) * HEAD_DIM].astype(BF16)
            vh = vv[:, hk * HEAD_DIM:(hk + 1) * HEAD_DIM].astype(BF16)
            q4 = _stack_heads(qr, hk).astype(BF16)
            s = jnp.where(mask, _dot_nt(q4, kh) * scale, NEG_INF)
            sk = _sink_col(sink_ref, hk)
            m = jnp.maximum(jnp.max(s, axis=-1, keepdims=True), sk)
            p = jnp.exp(s - m)
            l = jnp.sum(p, axis=-1, keepdims=True) + jnp.exp(sk - m)
            o = _dot(p.astype(BF16), vh) / l
            lse = m + jnp.log(l)
            for g in range(GROUP):
                outs.append(o[g * AB:(g + 1) * AB])
                lses.append(lse[g * AB:(g + 1) * AB])
        o_ref[...] = jnp.concatenate(outs, axis=1)
        lse_ref[...] = jnp.concatenate(lses, axis=1)
        if nc:
            pl.when((b == bsz - 1) & (i == nb - 1))(wait)

    return pl.pallas_call(
        body, name=name, grid=(bsz, nb),
        in_specs=[pl.BlockSpec(memory_space=pltpu.SMEM),
                  pl.BlockSpec((None, AB, ATTN_W), lambda b, i: (b, i, 0))] + kv_specs + tab_specs + tab_specs
        + [HBM_SPEC] * nc,
        out_specs=[pl.BlockSpec((None, AB, ATTN_W), lambda b, i: (b, i, 0)),
                   pl.BlockSpec((None, AB, N_HEADS), lambda b, i: (b, i, 0))] + [HBM_SPEC] * nc,
        out_shape=[jax.ShapeDtypeStruct((bsz, t_len, ATTN_W), F32), jax.ShapeDtypeStruct((bsz, t_len, N_HEADS), F32)]
        + _xy_out_shapes(carry, False),
        scratch_shapes=_xy_sems(nc) if nc else [],
        compiler_params=_cp(48, 2),
    )(sink, q, kv, kv, kv, kv, cos, cos, cos, sin, sin, sin, *carry)


def attn_bwd(q, kv, o, lse, do, cos, sin, sink, lc, name, carry=()):
    bsz, t_len, _ = q.shape
    nb = t_len // AB
    kv_specs, tab_specs = _attn_specs(t_len, lc)
    scale = HEAD_DIM ** -0.5
    blk = lambda w: pl.BlockSpec((None, AB, w), lambda b, i: (b, i, 0))
    full_tab = pl.BlockSpec((t_len, 128), lambda b, i: (0, 0))
    nc = len(carry)

    def body(sink_ref, q_ref, kvp_ref, kvc_ref, kvn_ref, kvx_ref, cp, cc, cn, sp, sc, sn, cf, sf,
             o_ref, lse_ref, do_ref, *rest):
        dq_ref, dkv_ref, dsink_ref = rest[nc:nc + 3]
        b, i = pl.program_id(0), pl.program_id(1)
        if nc:
            start, wait = _xy_copies(rest[:nc], rest[nc + 3:2 * nc + 3], *rest[2 * nc + 3:], True)
            pl.when((b == 0) & (i == 0))(start)

        @pl.when(i == 0)
        def _():
            dkv_ref[...] = jnp.zeros_like(dkv_ref)

        @pl.when((b == 0) & (i == 0))
        def _():
            dsink_ref[...] = jnp.zeros_like(dsink_ref)

        mask = _attn_mask(i, lc, t_len)
        cq, sq = jnp.tile(cc[...], (1, 4)), jnp.tile(sc[...], (1, 4))
        qr = _rope(q_ref[...], cq, sq)
        kk, vv = _attn_keys(kvp_ref[...], kvc_ref[...], kvn_ref[...], kvx_ref[...],
                            cp[...], cc[...], cn[...], sp[...], sc[...], sn[...])
        dov, ov, lsev = do_ref[...], o_ref[...], lse_ref[...]
        dqs, dks, dvs, dsk = [], [], [], []
        for hk in range(N_KV):
            kh = kk[:, hk * HEAD_DIM:(hk + 1) * HEAD_DIM].astype(BF16)
            vh = vv[:, hk * HEAD_DIM:(hk + 1) * HEAD_DIM].astype(BF16)
            q4 = _stack_heads(qr, hk).astype(BF16)
            do4 = _stack_heads(dov, hk)
            o4 = _stack_heads(ov, hk)
            lse4 = jnp.concatenate([lsev[:, GROUP * hk + g:GROUP * hk + g + 1] for g in range(GROUP)], axis=0)
            delta = jnp.sum(do4 * o4, axis=-1, keepdims=True)
            s = jnp.where(mask, _dot_nt(q4, kh) * scale, NEG_INF)
            p = jnp.exp(s - lse4)
            do4b = do4.astype(BF16)
            dp = _dot_nt(do4b, vh)
            ds = (p * (dp - delta) * scale).astype(BF16)
            dq4 = _dot(ds, kh)
            dks.append(_dot_tn(ds, q4))
            dvs.append(_dot_tn(p.astype(BF16), do4b))
            pd = jnp.exp(_sink_col(sink_ref, hk) - lse4) * delta
            for g in range(GROUP):
                dqs.append(dq4[g * AB:(g + 1) * AB])
                dsk.append(-jnp.sum(pd[g * AB:(g + 1) * AB], axis=0, keepdims=True))
        dq_ref[...] = _rope_t(jnp.concatenate(dqs, axis=1), cq, sq).astype(BF16)
        dsink_ref[0:1, :] += jnp.concatenate(dsk, axis=1)
        dkv = jnp.concatenate(dks + dvs, axis=1)
        starts = (jnp.maximum(i - 1, 0), i, jnp.minimum(i + 1, nb - 1))
        for j, st in enumerate(starts):
            rows = pl.ds(pl.multiple_of(st * AB, AB), AB)
            dkv_ref[rows, :] += dkv[j * AB:(j + 1) * AB]
        dkv_ref[0:lc, :] += dkv[3 * AB:]

        @pl.when(i == nb - 1)
        def _():
            def unrotate(j, carry):
                rows = pl.ds(pl.multiple_of(j * AB, AB), AB)
                dkv_ref[rows, 0:KV_W] = _rope_t(dkv_ref[rows, 0:KV_W], cf[rows, :], sf[rows, :])
                return carry
            lax.fori_loop(0, nb, unrotate, 0)

        if nc:
            pl.when((b == bsz - 1) & (i == nb - 1))(wait)

    return pl.pallas_call(
        body, name=name, grid=(bsz, nb),
        in_specs=[pl.BlockSpec(memory_space=pltpu.SMEM), blk(ATTN_W)] + kv_specs + tab_specs + tab_specs
        + [full_tab, full_tab, blk(ATTN_W), blk(N_HEADS), blk(ATTN_W)] + [HBM_SPEC] * nc,
        out_specs=[blk(ATTN_W), pl.BlockSpec((None, t_len, 2 * KV_W), lambda b, i: (b, 0, 0)),
                   pl.BlockSpec((8, N_HEADS), lambda b, i: (0, 0))] + [HBM_SPEC] * nc,
        out_shape=[jax.ShapeDtypeStruct((bsz, t_len, ATTN_W), BF16), jax.ShapeDtypeStruct((bsz, t_len, 2 * KV_W), F32),
                   jax.ShapeDtypeStruct((8, N_HEADS), F32)] + _xy_out_shapes(carry, True),
        scratch_shapes=_xy_sems(nc) if nc else [],
        compiler_params=_cp(56, 2),
    )(sink, q, kv, kv, kv, kv, cos, cos, cos, sin, sin, sin, cos, sin, o, lse, do, *carry)


def _s5_mats_dir(a_re, a_im, log_dt, b_re, b_im, c_re, c_im, flip):
    hp = lax.Precision.HIGHEST
    lam = lax.complex(a_re, a_im)
    ldt = lam * jnp.exp(log_dt)[:, None]
    a_bar = jnp.exp(ldt)
    b_bar = ((a_bar - 1.0) / lam)[..., None] * lax.complex(b_re, b_im)
    cm = lax.complex(c_re, c_im)
    tt = np.arange(Q)
    powers = lambda e: jnp.exp(ldt[..., None] * jnp.asarray(e, F32))
    ca = cm[:, :, :, None] * powers(Q - 1 - tt if flip else tt)[:, None, :, :]
    ca = jnp.concatenate([jnp.real(ca), -jnp.imag(ca)], axis=2)
    bb = jnp.concatenate([jnp.real(b_bar), jnp.imag(b_bar)], axis=1)
    k = jnp.einsum('gpk,gcpt->gktc', bb, ca, precision=hp).reshape(G, C, QC)
    slabs = []
    for t1 in range(Q):
        if flip:
            sh = (Q - 1 - t1) * C
            slabs.append(jnp.pad(k, ((0, 0), (0, 0), (0, sh)))[..., sh:])
        else:
            slabs.append(jnp.pad(k, ((0, 0), (0, 0), (t1 * C, 0)))[..., :QC])
    kt = jnp.stack(slabs, axis=1).reshape(G, QC, QC)
    ws = powers(tt if flip else Q - 1 - tt)[:, :, :, None] * b_bar[:, :, None, :]
    ws = ws.transpose(0, 2, 3, 1)
    wo = cm[:, :, :, None] * powers(Q - tt if flip else tt + 1)[:, None, :, :]
    wo = wo.transpose(0, 2, 3, 1)
    ws = jnp.concatenate([jnp.real(ws), jnp.imag(ws)], axis=-1).reshape(G, QC, P2)
    wo = jnp.concatenate([jnp.real(wo), -jnp.imag(wo)], axis=1).reshape(G, P2, QC)
    a1, a2 = _pair_forms(powers([Q]))
    return kt, ws, wo, a1, a2


def _pair_forms(z):
    re, im = jnp.real(z), jnp.imag(z)
    k = z.shape[-1]
    a1 = jnp.concatenate([re, re], axis=1).transpose(2, 0, 1).reshape(k, G * P2)
    a2 = jnp.concatenate([-im, im], axis=1).transpose(2, 0, 1).reshape(k, G * P2)
    return a1, a2


def s5_mats(a_re, a_im, log_dt, b_re, b_im, c_re, c_im):
    per_dir = [_s5_mats_dir(a_re[d], a_im[d], log_dt[d], b_re[d], b_im[d], c_re[d], c_im[d], d == 1)
               for d in range(2)]
    return tuple(jnp.stack([m[i] for m in per_dir]) for i in range(5))


GH = G // 8
RT = 16 * Q


def _perm_consts():
    r = np.arange(RT)
    rows = np.zeros((RT, RT), np.float32)
    rows[(r % Q) * 16 + r // Q, r] = 1.0
    q = np.arange(8 * 128)
    lanes = np.zeros((8 * 128, 8 * 128), np.float32)
    lanes[q, ((q % 128) // C) * 128 + (q // 128) * C + q % C] = 1.0
    return jnp.asarray(rows, BF16), jnp.asarray(lanes, BF16)


def to_groups(x, name):
    bsz, t_len, _ = x.shape
    nc = t_len // Q
    rows, lanes = _perm_consts()

    def body(x_ref, r_ref, p_ref, o_ref, w_ref):
        for j in range(t_len // RT):
            pt = _dot(r_ref[...], x_ref[j * RT:(j + 1) * RT, :].astype(BF16)).astype(BF16)
            for t in range(Q):
                w_ref[j * 16:(j + 1) * 16, t * SSM_W:(t + 1) * SSM_W] = pt[t * 16:(t + 1) * 16, :]
        for gh in range(GH):
            for th in range(2):
                inp = jnp.concatenate([w_ref[:, (th * 8 + tl) * SSM_W + gh * 128:(th * 8 + tl) * SSM_W + (gh + 1) * 128]
                                       for tl in range(8)], axis=1)
                out = _dot(inp, p_ref[...]).astype(BF16)
                for gl in range(8):
                    o_ref[gh * 8 + gl, :, th * 128:(th + 1) * 128] = out[:, gl * 128:(gl + 1) * 128]

    return pl.pallas_call(
        body, name=name, grid=(bsz,),
        in_specs=[pl.BlockSpec((None, t_len, SSM_W), lambda b: (b, 0, 0)), pl.BlockSpec((RT, RT), lambda b: (0, 0)),
                  pl.BlockSpec((1024, 1024), lambda b: (0, 0))],
        out_specs=pl.BlockSpec((None, G, nc, QC), lambda b: (b, 0, 0, 0)),
        out_shape=jax.ShapeDtypeStruct((bsz, G, nc, QC), BF16),
        scratch_shapes=[pltpu.VMEM((nc, Q * SSM_W), BF16)],
        compiler_params=_cp(56, 1),
    )(x, rows, lanes)


def from_groups(xg, name):
    bsz, _, nc, _ = xg.shape
    t_len = nc * Q
    rows, lanes = _perm_consts()

    def body(x_ref, r_ref, p_ref, o_ref, whi_ref, wlo_ref):
        gh = pl.program_id(1)
        for th in range(2):
            inp = jnp.concatenate([x_ref[gl, :, th * 128:(th + 1) * 128] for gl in range(8)], axis=1)
            hi = inp.astype(BF16)
            lo = (inp - hi.astype(F32)).astype(BF16)
            whi_ref[gh, :, th * 1024:(th + 1) * 1024] = _dot(hi, p_ref[...]).astype(BF16)
            wlo_ref[gh, :, th * 1024:(th + 1) * 1024] = _dot(lo, p_ref[...]).astype(BF16)

        @pl.when(gh == GH - 1)
        def _():
            for j in range(t_len // RT):
                def tile(w_ref):
                    return jnp.concatenate(
                        [jnp.concatenate([w_ref[k, j * 16:(j + 1) * 16, t * 128:(t + 1) * 128] for k in range(GH)],
                                         axis=1) for t in range(Q)], axis=0)
                o_ref[j * RT:(j + 1) * RT, :] = _dot(r_ref[...], tile(whi_ref)) + _dot(r_ref[...], tile(wlo_ref))

    return pl.pallas_call(
        body, name=name, grid=(bsz, GH),
        in_specs=[pl.BlockSpec((None, 8, nc, QC), lambda b, k: (b, k, 0, 0)),
                  pl.BlockSpec((RT, RT), lambda b, k: (0, 0)), pl.BlockSpec((1024, 1024), lambda b, k: (0, 0))],
        out_specs=pl.BlockSpec((None, t_len, SSM_W), lambda b, k: (b, 0, 0)),
        out_shape=jax.ShapeDtypeStruct((bsz, t_len, SSM_W), F32),
        scratch_shapes=[pltpu.VMEM((GH, nc, Q * 128), BF16), pltpu.VMEM((GH, nc, Q * 128), BF16)],
        compiler_params=_cp(56, 2),
    )(xg, rows, lanes)


def _gb(shape):
    return pl.BlockSpec((None, None) + shape, lambda g, b: (b, g, 0, 0))


def _gw(shape):
    return pl.BlockSpec((2, None) + shape, lambda g, b: (0, g, 0, 0))


def _gs(nc):
    return pl.BlockSpec((2, None, nc, P2), lambda g, b: (0, b, 0, g))


def s5_chunk_fwd(ug, kt, ws, name):
    bsz, _, nc, _ = ug.shape

    def body(u_ref, kt_ref, ws_ref, y_ref, s_ref):
        ub = u_ref[...]
        y_ref[...] = _dot(ub, kt_ref[0]) + _dot(ub, kt_ref[1])
        s_ref[0] = _dot(ub, ws_ref[0])
        s_ref[1] = _dot(ub, ws_ref[1])

    return pl.pallas_call(
        body, name=name, grid=(G, bsz),
        in_specs=[_gb((nc, QC)), _gw((QC, QC)), _gw((QC, P2))],
        out_specs=[_gb((nc, QC)), _gs(nc)],
        out_shape=[jax.ShapeDtypeStruct((bsz, G, nc, QC), F32), jax.ShapeDtypeStruct((2, bsz, nc, G * P2), F32)],
        compiler_params=_cp(32, 2),
    )(ug, kt, ws)


def s5_scan(s, a1, a2, ncc, reverse, name, hp=None):
    _, bsz, nc, gw = s.shape
    as_rows = lambda v: v.reshape(v.shape[:-1] + (G, P2))
    st = pl.BlockSpec((2, None, nc, SCAN_G, P2), lambda b, w: (0, b, 0, w, 0))
    av = pl.BlockSpec((2, SCAN_G, P2), lambda b, w: (0, w, 0))
    acc = pl.BlockSpec((2, None, SCAN_G, P2), lambda b, w: (0, b, w, 0))
    with_da = hp is not None

    def body(*refs):
        if with_da:
            s_ref, a1_ref, a2_ref, hp_ref, out_ref, da1_ref, da2_ref = refs
        else:
            s_ref, a1_ref, a2_ref, out_ref = refs
        a1v = (a1_ref[0], a1_ref[1])
        a2v = (a2_ref[0], a2_ref[1])
        swap = lambda h: pltpu.roll(h, P, 1)

        def block(jb, carry):
            pb = nblk - 1 - jb if reverse else jb
            base = (pb * ncc, jnp.where(pb == 0, ncc - 1, nc - 1 - (pb - 1) * ncc))
            hs, da1, da2 = carry
            for kk in range(ncc):
                k = ncc - 1 - kk if reverse else kk
                nh, n1, n2 = [], [], []
                for d, n in enumerate((base[0] + k, base[1] - k)):
                    h, hw = hs[d]
                    out_ref[d, n] = h
                    sv = s_ref[d, n]
                    nh.append((a1v[d] * h + a2v[d] * hw + sv, a1v[d] * hw - a2v[d] * h + swap(sv)))
                    if with_da:
                        hv = hp_ref[d, n]
                        n1.append(da1[d] + h * hv)
                        n2.append(da2[d] + h * swap(hv))
                hs, da1, da2 = tuple(nh), tuple(n1), tuple(n2)
            return hs, da1, da2

        assert nc % ncc == 0
        nblk = nc // ncc
        z = jnp.zeros((SCAN_G, P2), F32)
        zz = (z, z) if with_da else ()
        _, da1, da2 = lax.fori_loop(0, nblk, block, (((z, z), (z, z)), zz, zz))
        if with_da:
            for d in range(2):
                da1_ref[d] = da1[d]
                da2_ref[d] = da2[d]

    out_shape = [jax.ShapeDtypeStruct((2, bsz, nc, G, P2), F32)]
    out_specs = [st]
    ins = [as_rows(s), as_rows(a1[:, 0]), as_rows(a2[:, 0])]
    in_specs = [st, av, av]
    if with_da:
        ins.append(as_rows(hp))
        in_specs.append(st)
        out_shape += [jax.ShapeDtypeStruct((2, bsz, G, P2), F32)] * 2
        out_specs += [acc, acc]
    res = pl.pallas_call(
        body, name=name, grid=(bsz, G // SCAN_G), in_specs=in_specs, out_specs=out_specs, out_shape=out_shape,
        compiler_params=_cp(48, 2),
    )(*ins)
    out = res[0].reshape(s.shape)
    return (out, res[1].reshape(2, bsz, gw), res[2].reshape(2, bsz, gw)) if with_da else out


def s5_out_fwd(y1, hp, wo, name):
    bsz, _, nc, _ = y1.shape

    def body(y1_ref, hp_ref, wo_ref, y_ref):
        y_ref[...] = (y1_ref[...] + _dot(hp_ref[0].astype(BF16), wo_ref[0])
                      + _dot(hp_ref[1].astype(BF16), wo_ref[1]))

    return pl.pallas_call(
        body, name=name, grid=(G, bsz),
        in_specs=[_gb((nc, QC)), _gs(nc), _gw((P2, QC))],
        out_specs=_gb((nc, QC)),
        out_shape=jax.ShapeDtypeStruct(y1.shape, F32),
        compiler_params=_cp(32, 2),
    )(y1, hp, wo)


def _acc_init(b, *refs):
    @pl.when(b == 0)
    def _():
        for r in refs:
            r[...] = jnp.zeros_like(r)


def s5_out_bwd(dyg, ug, hp, wo, name):
    bsz, _, nc, _ = dyg.shape

    def body(dy_ref, u_ref, hp_ref, wo_ref, dhp_ref, dwo_ref, dkt_ref):
        _acc_init(pl.program_id(1), dwo_ref, dkt_ref)
        dyb = dy_ref[...]
        for d in range(2):
            dhp_ref[d] = _dot_nt(dyb, wo_ref[d])
            dwo_ref[d] += _dot_tn(hp_ref[d].astype(BF16), dyb)
        dkt_ref[...] += _dot_tn(u_ref[...], dyb)

    return pl.pallas_call(
        body, name=name, grid=(G, bsz),
        in_specs=[_gb((nc, QC)), _gb((nc, QC)), _gs(nc), _gw((P2, QC))],
        out_specs=[_gs(nc), _gw((P2, QC)), pl.BlockSpec((None, QC, QC), lambda g, b: (g, 0, 0))],
        out_shape=[jax.ShapeDtypeStruct(hp.shape, F32), jax.ShapeDtypeStruct((2, G, P2, QC), F32),
                   jax.ShapeDtypeStruct((G, QC, QC), F32)],
        compiler_params=_cp(32, 2),
    )(dyg, ug, hp, wo)


def s5_chunk_bwd(dyg, ug, ds, kt, ws, name):
    bsz, _, nc, _ = dyg.shape

    def body(dy_ref, u_ref, ds_ref, kt_ref, ws_ref, du_ref, dws_ref):
        _acc_init(pl.program_id(1), dws_ref)
        dyb = dy_ref[...]
        du = _dot_nt(dyb, kt_ref[0]) + _dot_nt(dyb, kt_ref[1])
        for d in range(2):
            dsb = ds_ref[d].astype(BF16)
            du += _dot_nt(dsb, ws_ref[d])
            dws_ref[d] += _dot_tn(u_ref[...], dsb)
        du_ref[...] = du

    return pl.pallas_call(
        body, name=name, grid=(G, bsz),
        in_specs=[_gb((nc, QC)), _gb((nc, QC)), _gs(nc), _gw((QC, QC)), _gw((QC, P2))],
        out_specs=[_gb((nc, QC)), _gw((QC, P2))],
        out_shape=[jax.ShapeDtypeStruct(dyg.shape, F32), jax.ShapeDtypeStruct((2, G, QC, P2), F32)],
        compiler_params=_cp(32, 2),
    )(dyg, ug, ds, kt, ws)


def local_step(x, ctx, target, mods, norm_g, final_g, even, odd, late=None, scatter=None):
    bsz, seq, _ = x.shape
    lc = ctx.shape[1]
    t_len = lc + seq
    ncc = lc // Q
    cos, sin = rope_tables(lc, seq)
    h = jnp.concatenate([ctx, x], axis=1)
    ssm_stacked = [jnp.stack([even[0]["ssm"][k], even[1]["ssm"][k]]) for k in range(7)]
    mats_all, mats_vjp = jax.vjp(jax.vmap(s5_mats), *ssm_stacked)
    d_mats = [None, None]
    saved = []
    for i in range(DEPTH):
        j = i // 2
        g = norm_g[i].reshape(1, D)
        if i % 2 == 0:
            w = even[j]
            a, q, kv, g_attn, u, g_ssm = norm_in(h, g, mods[i], w["w_in"], EVEN_SPLITS, (2, 4), f"even_in{j}")
            if i == 0 and late is not None:
                o_attn, lse, *gathered = attn_fwd(q, kv, cos, sin, w["sink"], lc, f"attn_fwd{j}", carry=late[0])
                late[1](gathered)
            else:
                o_attn, lse = attn_fwd(q, kv, cos, sin, w["sink"], lc, f"attn_fwd{j}")
            kt, ws, wo, a1, a2 = (m[j] for m in mats_all)
            kt, ws, wo = kt.astype(BF16), ws.astype(BF16), wo.astype(BF16)
            d_skip = w["ssm"][7].reshape(1, SSM_W)
            ug = to_groups(u, f"u_to_groups{j}")
            y1, s = s5_chunk_fwd(ug, kt, ws, f"s5_chunk_fwd{j}")
            hp = s5_scan(s, a1, a2, ncc, False, f"s5_scan_fwd{j}")
            y_scan = from_groups(s5_out_fwd(y1, hp, wo, f"s5_out_fwd{j}"), f"y_from_groups{j}")
            h_new, mix, yout = even_out(h, mods[i], o_attn, g_attn, y_scan, u, d_skip, g_ssm, w["glu_w"], w["glu_b"],
                                        w["w_out"], f"even_out{j}")
            saved.append(dict(h=h, a=a, q=q, kv=kv, g_attn=g_attn, g_ssm=g_ssm, o_attn=o_attn, lse=lse, ug=ug, u=u,
                              hp=hp, y_scan=y_scan, mix=mix, yout=yout, mats=(kt, ws, wo, a1, a2), d_skip=d_skip))
        else:
            w = odd[j]
            a, u, gate = norm_in(h, g, mods[i], w["w_in"], ODD_SPLITS, (1,), f"odd_in{j}")
            pm = pool_band(u, lc, False, f"pool_band_fwd{j}")
            h_new, mix, yout = pool_out(h, mods[i], pm, gate, w["pool_w"], w["pool_scale"], w["w_out"], f"pool_out{j}")
            saved.append(dict(h=h, a=a, pm=pm, gate=gate, mix=mix, yout=yout))
        h = h_new

    dh, loss_acc, dfg = loss_head(h, final_g.reshape(1, D), target, "loss_head")
    grads = dict(final_g=dfg[0], norm_g=[None] * DEPTH, even=[None, None], odd=[None, None])
    dmods = [None] * DEPTH
    rows = bsz * t_len
    flat = lambda v: v.reshape(rows, v.shape[-1])
    for i in reversed(range(DEPTH)):
        j = i // 2
        sv = saved[i]
        g = norm_g[i].reshape(1, D)
        if i % 2 == 0:
            w = even[j]
            kt, ws, wo, a1, a2 = sv["mats"]
            (d_oattn, d_gattn, d_gssm, d_yssm, dyout, zz, dsg, dgate, dglu_b) = even_out_bwd(
                dh, mods[i], sv["o_attn"], sv["g_attn"], sv["y_scan"], sv["u"], sv["d_skip"], sv["g_ssm"], w["glu_w"],
                w["glu_b"], w["w_out"], sv["yout"], f"even_out_bwd{j}")
            g_w_out = matmul_tn(flat(sv["mix"]), flat(dyout), D, D, f"even_w_out_grad{j}")
            g_glu_w = matmul_tn(flat(zz), flat(dsg), SSM_W, SSM_W, f"glu_w_grad{j}")
            carry = scatter(grads) if i == 0 and scatter is not None else ()
            dq, dkv, dsink, *grads["landed"] = attn_bwd(sv["q"], sv["kv"], sv["o_attn"], sv["lse"], d_oattn, cos, sin,
                                                        w["sink"], lc, f"attn_bwd{j}", carry=carry)
            dyg = to_groups(d_yssm, f"dy_to_groups{j}")
            dhp, dwo, dkt = s5_out_bwd(dyg, sv["ug"], sv["hp"], wo, f"s5_out_bwd{j}")
            ds, da1, da2 = s5_scan(dhp, a1, -a2, ncc, True, f"s5_scan_bwd{j}", hp=sv["hp"])
            dug, dws = s5_chunk_bwd(dyg, sv["ug"], ds, kt, ws, f"s5_chunk_bwd{j}")
            dkt2 = jnp.stack([dkt, dkt])
            da1 = da1.sum(axis=1).reshape(2, 1, G * P2)
            da2 = da2.sum(axis=1).reshape(2, 1, G * P2)
            d_mats[j] = (dkt2, dws, dwo, da1, da2)
            dparts = [dq, dkv, d_gattn, from_groups(dug, f"du_from_groups{j}"), d_gssm]
            dh, dz, dmod, dg = norm_in_bwd(dparts, dh, sv["h"], g, mods[i], w["w_in"], f"even_in_bwd{j}",
                                           skip=(3, d_yssm, sv["d_skip"]))
            g_w_in = matmul_tn(flat(sv["a"]), flat(dz), D, dz.shape[-1], f"even_w_in_grad{j}")
            grads["even"][j] = dict(w_in=g_w_in, w_out=g_w_out, sink=dsink[0], d_skip=dglu_b[1], glu_w=g_glu_w,
                                    glu_b=dglu_b[0])
        else:
            w = odd[j]
            dpm, dgt, dyout, dpp, dgate, dps = pool_out_bwd(dh, mods[i], sv["pm"], sv["gate"], w["pool_w"],
                                                            w["pool_scale"], w["w_out"], sv["yout"],
                                                            f"pool_out_bwd{j}")
            g_w_out = matmul_tn(flat(sv["mix"]), flat(dyout), D, D, f"odd_w_out_grad{j}")
            g_pool_w = jnp.stack([matmul_tn(flat(sv["pm"]), flat(dpp), POOL_G, POOL_G, f"pool_w_grad{j}_{gi}",
                                            a_col=gi, b_col=gi) for gi in range(4)])
            du = pool_band(dpm, lc, True, f"pool_band_bwd{j}")
            dh, dz, dmod, dg = norm_in_bwd([du, dgt], dh, sv["h"], g, mods[i], w["w_in"], f"odd_in_bwd{j}")
            g_w_in = matmul_tn(flat(sv["a"]), flat(dz), D, dz.shape[-1], f"odd_w_in_grad{j}")
            grads["odd"][j] = dict(w_in=g_w_in, w_out=g_w_out, pool_w=g_pool_w, pool_scale=dps[0])
        grads["norm_g"][i] = dg[0]
        dmods[i] = jnp.concatenate([dmod[:, :, 0:2, :], dgate[:, :, 0:1, :]], axis=2)
    g_ssm = mats_vjp(tuple(jnp.stack([d_mats[0][k], d_mats[1][k]]) for k in range(5)))
    for j in range(2):
        grads["even"][j]["ssm"] = tuple(gk[j] for gk in g_ssm) + (grads["even"][j].pop("d_skip"),)
    return loss_acc[0, 0], dh[:, lc:, :], dmods, grads


N_DEV = 8
HBM_SPEC = pl.BlockSpec(memory_space=pltpu.HBM)


def allgather8(x_shard, name):
    m_per, n = x_shard.shape

    def body(x_ref, out_ref, send_sems, recv_sems, local_sem):
        x, y, c = lax.axis_index("x"), lax.axis_index("y"), lax.axis_index("c")
        me, sibling = (x, y, c), (x, y, 1 - c)
        chips = [(1 - x, y), (x, 1 - y), (1 - x, 1 - y)]

        def rows(px, py, pc):
            return out_ref.at[pl.ds((4 * px + 2 * py + pc) * m_per, m_per), :]

        def copy(k, block, to, src=None):
            return pltpu.make_async_remote_copy(
                src_ref=rows(*block) if src is None else src, dst_ref=rows(*block),
                send_sem=send_sems.at[k], recv_sem=recv_sems.at[k], device_id=to, device_id_type=MESH)

        mine = pltpu.make_async_copy(x_ref, rows(*me), local_sem)
        mine.start()
        first = [copy(0, me, sibling, src=x_ref)]
        first += [copy(1 + j, me, (*chip, c), src=x_ref) for j, chip in enumerate(chips)]
        for cp in first:
            cp.start()
        passed = [copy(4 + j, (*chip, c), sibling) for j, chip in enumerate(chips)]
        for j, chip in enumerate(chips):
            copy(1 + j, (*chip, c), me).wait_recv()
            passed[j].start()
        copy(0, sibling, me).wait_recv()
        for j, chip in enumerate(chips):
            copy(4 + j, (*chip, 1 - c), me).wait_recv()
        for cp in first + passed:
            cp.wait_send()
        mine.wait()

    return pl.pallas_call(
        body, name=name,
        out_shape=jax.ShapeDtypeStruct((N_DEV * m_per, n), x_shard.dtype),
        in_specs=[pl.BlockSpec(memory_space=pltpu.VMEM)],
        out_specs=pl.BlockSpec(memory_space=pltpu.VMEM),
        scratch_shapes=[pltpu.SemaphoreType.DMA((7,)), pltpu.SemaphoreType.DMA((7,)), pltpu.SemaphoreType.DMA],
        compiler_params=_cp(56),
    )(x_shard)


def xy_exchange(srcs, scatter, name):
    n = len(srcs)

    def body(*refs):
        start, wait = _xy_copies(refs[:n], refs[n:2 * n], *refs[2 * n:], scatter)
        start()
        wait()

    return pl.pallas_call(
        body, name=name, out_shape=_xy_out_shapes(srcs, scatter),
        in_specs=[HBM_SPEC] * n, out_specs=[HBM_SPEC] * n, scratch_shapes=_xy_sems(n),
    )(*srcs)


def _xy_out_shapes(srcs, scatter):
    return [jax.ShapeDtypeStruct((4,) + (tuple(s.shape[1:]) if scatter else tuple(s.shape)), s.dtype) for s in srcs]


def _xy_sems(n):
    return [pltpu.SemaphoreType.DMA((3 * n,)), pltpu.SemaphoreType.DMA((3 * n,)), pltpu.SemaphoreType.DMA((n,))]


def _xy_copies(src_refs, out_refs, send_sems, recv_sems, local_sems, scatter):
    n = len(src_refs)

    def parts():
        x, y, c = lax.axis_index("x"), lax.axis_index("y"), lax.axis_index("c")
        my = 2 * x + y
        peers = [(1 - x, y), (x, 1 - y), (1 - x, 1 - y)]

        def piece(i, pos):
            return src_refs[i].at[pos] if scatter else src_refs[i]

        def copy(i, k, src_pos, dst_pos):
            px, py = peers[k]
            return pltpu.make_async_remote_copy(
                src_ref=piece(i, src_pos), dst_ref=out_refs[i].at[dst_pos], send_sem=send_sems.at[3 * i + k],
                recv_sem=recv_sems.at[3 * i + k], device_id=(px, py, c), device_id_type=MESH)

        local = [pltpu.make_async_copy(piece(i, my), out_refs[i].at[my], local_sems.at[i]) for i in range(n)]
        sends = [copy(i, k, 2 * px + py, my) for i in range(n) for k, (px, py) in enumerate(peers)]
        lands = [copy(i, k, my, 2 * px + py) for i in range(n) for k, (px, py) in enumerate(peers)]
        return local, sends, lands

    def start():
        local, sends, _ = parts()
        for cp in local + sends:
            cp.start()

    def wait():
        local, sends, lands = parts()
        for cp in lands:
            cp.wait_recv()
        for cp in sends:
            cp.wait_send()
        for cp in local:
            cp.wait()

    return start, wait


def sibling_exchange(srcs, name):
    n = len(srcs)

    def body(*refs):
        src_refs, out_refs = refs[:n], refs[n:2 * n]
        send_sems, recv_sems = refs[2 * n:]
        peer = (lax.axis_index("x"), lax.axis_index("y"), 1 - lax.axis_index("c"))
        cps = [pltpu.make_async_remote_copy(src_ref=src_refs[i], dst_ref=out_refs[i], send_sem=send_sems.at[i],
                                            recv_sem=recv_sems.at[i], device_id=peer, device_id_type=MESH)
               for i in range(n)]
        for cp in cps:
            cp.start()
        for cp in cps:
            cp.wait()

    return pl.pallas_call(
        body, name=name, out_shape=[jax.ShapeDtypeStruct(s.shape, s.dtype) for s in srcs],
        in_specs=[HBM_SPEC] * n, out_specs=[HBM_SPEC] * n,
        scratch_shapes=[pltpu.SemaphoreType.DMA((n,)), pltpu.SemaphoreType.DMA((n,))],
    )(*srcs)


def _row_tile(rows, bytes_per_row, limit):
    best = None
    for tr in range(8, rows + 1, 8):
        if rows % tr == 0 and tr * bytes_per_row <= limit:
            best = tr
    return best if best is not None else rows


def sum_slots(x, name):
    n, rows, cols = x.shape
    tr = _row_tile(rows, n * cols * 4, 4 * MB)

    def body(x_ref, o_ref):
        acc = x_ref[0].astype(F32)
        for k in range(1, n):
            acc = acc + x_ref[k].astype(F32)
        o_ref[...] = acc

    return pl.pallas_call(
        body, name=name, grid=(rows // tr,),
        in_specs=[pl.BlockSpec((n, tr, cols), lambda r: (0, r, 0))],
        out_specs=pl.BlockSpec((tr, cols), lambda r: (r, 0)),
        out_shape=jax.ShapeDtypeStruct((rows, cols), F32),
        compiler_params=_cp(32, 1),
    )(x)


ADA_COLS = 3 * D // 4
C_ROWS = 8


def ada_fwd(c_all, ada_w, ada_b_cols, name):
    nrow = c_all.shape[0]

    def body(c_ref, w_ref, b_ref, o_ref):
        s, _ = _silu_and_grad(c_ref[...])
        o_ref[...] = _dot(s.astype(BF16), w_ref[...].astype(BF16)) + b_ref[...]

    return pl.pallas_call(
        body, name=name, grid=(DEPTH,),
        in_specs=[pl.BlockSpec((nrow, D), lambda i: (0, 0)), pl.BlockSpec((None, D, ADA_COLS), lambda i: (i, 0, 0)),
                  pl.BlockSpec((None, 1, ADA_COLS), lambda i: (i, 0, 0))],
        out_specs=pl.BlockSpec((None, nrow, ADA_COLS), lambda i: (i, 0, 0)),
        out_shape=jax.ShapeDtypeStruct((DEPTH, nrow, ADA_COLS), F32),
        compiler_params=_cp(32, 1),
    )(c_all, ada_w, ada_b_cols)


def ada_bwd(c_all, d_cols, ada_w, name):
    nrow = c_all.shape[0]

    def body(c_ref, d_ref, w_ref, gw_ref, ds_ref):
        @pl.when(pl.program_id(0) == 0)
        def _():
            ds_ref[...] = jnp.zeros_like(ds_ref)
        s, _ = _silu_and_grad(c_ref[...])
        dl = d_ref[...]
        gw_ref[...] = _dot_tn(s.astype(BF16), dl.astype(BF16))
        rid = lax.broadcasted_iota(jnp.int32, (nrow, 1), 0) % C_ROWS
        dctx = jnp.where((rid == 2) | (rid == 3), dl, 0.0).astype(BF16)
        ds_ref[0:1, :] += _rowsum(_dot_nt(dctx, w_ref[...].astype(BF16)))

    return pl.pallas_call(
        body, name=name, grid=(DEPTH,),
        in_specs=[pl.BlockSpec((nrow, D), lambda i: (0, 0)), pl.BlockSpec((None, nrow, ADA_COLS), lambda i: (i, 0, 0)),
                  pl.BlockSpec((None, D, ADA_COLS), lambda i: (i, 0, 0))],
        out_specs=[pl.BlockSpec((None, D, ADA_COLS), lambda i: (i, 0, 0)), pl.BlockSpec((8, D), lambda i: (0, 0))],
        out_shape=[jax.ShapeDtypeStruct((DEPTH, D, ADA_COLS), F32), jax.ShapeDtypeStruct((8, D), F32)],
        compiler_params=_cp(32, 1),
    )(c_all, d_cols, ada_w)


def ada_bias_grad(d_all, name):
    nrow = d_all.shape[1]

    def body(d_ref, o_ref):
        o_ref[...] = jnp.broadcast_to(_rowsum(d_ref[...]), o_ref.shape)

    return pl.pallas_call(
        body, name=name, grid=(DEPTH,),
        in_specs=[pl.BlockSpec((None, nrow, 3 * D), lambda i: (i, 0, 0))],
        out_specs=pl.BlockSpec((None, 8, 3 * D), lambda i: (i, 0, 0)),
        out_shape=jax.ShapeDtypeStruct((DEPTH, 8, 3 * D), F32),
        compiler_params=_cp(32, 1),
    )(d_all)


def silu_chain(ds, c, name):
    def body(ds_ref, c_ref, o_ref):
        _, dsl = _silu_and_grad(c_ref[...])
        o_ref[...] = ds_ref[...] * dsl

    return pl.pallas_call(body, name=name, out_shape=jax.ShapeDtypeStruct(ds.shape, F32))(ds, c)


def _flat_cols(shape):
    size = int(np.prod(shape))
    if shape[-1] >= 128:
        return shape[-1]
    for cols in (1024, 128):
        if size % cols == 0:
            return cols
    return shape[-1]


def adamw(w, m, v, grads, name):
    shape = w.shape
    cols = _flat_cols(shape)
    as2d = lambda a: a.reshape(-1, cols)
    rows = w.size // cols
    tr = _row_tile(rows, cols * 4, MB)
    k = len(grads)

    def body(*refs):
        w_ref, m_ref, v_ref = refs[:3]
        g_refs = refs[3:3 + k]
        g_out, d_out, m_out, v_out = refs[3 + k:]
        g = g_refs[0][...]
        for r in g_refs[1:]:
            g = g + r[...]
        g_out[...] = g
        mn = ADAM_B1 * m_ref[...] + (1.0 - ADAM_B1) * g
        vn = ADAM_B2 * v_ref[...] + (1.0 - ADAM_B2) * (g * g)
        m_out[...] = mn
        v_out[...] = vn
        m_hat = mn / (1.0 - ADAM_B1 ** ADAM_STEP)
        v_hat = vn / (1.0 - ADAM_B2 ** ADAM_STEP)
        d_out[...] = -ADAM_LR * (m_hat / (jnp.sqrt(v_hat) + ADAM_EPS) + ADAM_WD * w_ref[...])

    spec = pl.BlockSpec((tr, cols), lambda r: (r, 0))
    outs = pl.pallas_call(
        body, name=name, grid=(rows // tr,),
        in_specs=[spec] * (3 + k), out_specs=[spec] * 4,
        out_shape=[jax.ShapeDtypeStruct((rows, cols), F32)] * 4,
        compiler_params=_cp(32, 1),
    )(as2d(w), as2d(m), as2d(v), *[as2d(g) for g in grads])
    return tuple(o.reshape(shape) for o in outs)


BIG = (("even_w_in", (2, D, 576), 2), ("even_w_out", (2, 256, D), 1), ("glu_w", (2, 128, SSM_W), 1),
       ("odd_w_in", (2, D, 512), 2), ("odd_w_out", (2, 256, D), 1), ("pool_w", (2, 4, 64, POOL_G), 2))


def _full_shape(shard, axis):
    return tuple(4 * s if a == axis else s for a, s in enumerate(shard))


def _to_shards(full, shard, axis):
    return jnp.moveaxis(full.reshape(shard[:axis] + (4,) + shard[axis:]), axis, 0)


def _from_shards(stacked, shard, axis):
    return jnp.moveaxis(stacked, 0, axis).reshape(_full_shape(shard, axis))


GRAD_KEY = dict(even_w_in=("even", "w_in"), even_w_out=("even", "w_out"), glu_w=("even", "glu_w"),
                odd_w_in=("odd", "w_in"), odd_w_out=("odd", "w_out"), pool_w=("odd", "pool_w"))
RIDE = tuple((n, 1) for n, _, _ in BIG[:3]) + tuple((n, j) for n, _, _ in BIG[3:] for j in range(2))
LAST = tuple((n, 0) for n, _, _ in BIG[:3])


def _grad_piece(grads, name, j):
    kind, key = GRAD_KEY[name]
    shard, axis = next((s, a) for n, s, a in BIG if n == name)
    return _to_shards(grads[kind][j][key], shard[1:], axis - 1).astype(BF16)


SMALL = (("ds_ctx", (D,)), ("norm_g", (DEPTH, D)), ("final_g", (D,)), ("attn_sink", (2, N_HEADS)),
         ("ssm_a_re", (2, 2, G, P)), ("ssm_a_im", (2, 2, G, P)), ("ssm_log_dt", (2, 2, G)),
         ("ssm_b_re", (2, 2, G, P, C)), ("ssm_b_im", (2, 2, G, P, C)), ("ssm_c_re", (2, 2, G, C, P)),
         ("ssm_c_im", (2, 2, G, C, P)), ("ssm_d", (2, SSM_W)), ("glu_b", (2, SSM_W)), ("pool_scale", (2, D)))
SMALL_PAD = 8 * 128


def pack_small(vals):
    flat = jnp.concatenate([vals[n].reshape(-1) for n, _ in SMALL])
    pad = (-flat.shape[0]) % SMALL_PAD
    return jnp.pad(flat, (0, pad)).reshape(-1, 128)


def unpack_small(packed):
    flat, out, off = packed.reshape(-1), {}, 0
    for n, shape in SMALL:
        size = int(np.prod(shape))
        out[n] = flat[off:off + size].reshape(shape)
        off += size
    return out


WEIGHT_NAMES = ('c_ctx', 'ada_w', 'ada_b', 'norm_g', 'even_w_in', 'even_w_out', 'attn_sink', 'ssm_a_re', 'ssm_a_im',
                'ssm_log_dt', 'ssm_b_re', 'ssm_b_im', 'ssm_c_re', 'ssm_c_im', 'ssm_d', 'glu_w', 'glu_b', 'odd_w_in',
                'odd_w_out', 'pool_w', 'pool_scale', 'final_g')
SSM_NAMES = ('ssm_a_re', 'ssm_a_im', 'ssm_log_dt', 'ssm_b_re', 'ssm_b_im', 'ssm_c_re', 'ssm_c_im', 'ssm_d')


def kernel(x, c, ctx, c_ctx, ada_w, ada_b, norm_g, even_w_in, even_w_out, attn_sink, ssm_a_re, ssm_a_im, ssm_log_dt, ssm_b_re, ssm_b_im, ssm_c_re, ssm_c_im, ssm_d, glu_w, glu_b, odd_w_in, odd_w_out, pool_w, pool_scale, final_g, loss_target, m_c_ctx, m_ada_w, m_ada_b, m_norm_g, m_even_w_in, m_even_w_out, m_attn_sink, m_ssm_a_re, m_ssm_a_im, m_ssm_log_dt, m_ssm_b_re, m_ssm_b_im, m_ssm_c_re, m_ssm_c_im, m_ssm_d, m_glu_w, m_glu_b, m_odd_w_in, m_odd_w_out, m_pool_w, m_pool_scale, m_final_g, v_c_ctx, v_ada_w, v_ada_b, v_norm_g, v_even_w_in, v_even_w_out, v_attn_sink, v_ssm_a_re, v_ssm_a_im, v_ssm_log_dt, v_ssm_b_re, v_ssm_b_im, v_ssm_c_re, v_ssm_c_im, v_ssm_d, v_glu_w, v_glu_b, v_odd_w_in, v_odd_w_out, v_pool_w, v_pool_scale, v_final_g):
    env = dict(locals())
    weights = {n: env[n] for n in WEIGHT_NAMES}
    bsz = x.shape[0]
    ax, ay, ac = lax.axis_index("x"), lax.axis_index("y"), lax.axis_index("c")
    pos = 2 * ax + ay
    dev = 2 * pos + ac

    c_rows = jnp.concatenate([c, c_ctx.reshape(1, D), c_ctx.reshape(1, D), jnp.zeros((C_ROWS - bsz - 2, D), F32)])
    c_all = allgather8(c_rows, "gather_c")
    ada_b_cols = lax.dynamic_slice(ada_b, (0, pos * ADA_COLS), (DEPTH, ADA_COLS)).reshape(DEPTH, 1, ADA_COLS)
    mod_cols = ada_fwd(c_all, ada_w, ada_b_cols, "ada_fwd")
    nrow = N_DEV * C_ROWS
    misc = jnp.concatenate([mod_cols.reshape(DEPTH * nrow, ADA_COLS),
                            jnp.pad(pool_scale, ((0, 6), (0, ADA_COLS - pool_scale.shape[1])))])
    misc_all = allgather8(misc, "gather_mod").reshape(4, 2, DEPTH * nrow + 8, ADA_COLS)[:, 0]
    mod_full = misc_all[:, :DEPTH * nrow].reshape(4, DEPTH, nrow, ADA_COLS).transpose(1, 2, 0, 3)
    mod_mine = lax.dynamic_slice(mod_full.reshape(DEPTH, nrow, 3 * D), (0, dev * C_ROWS, 0), (DEPTH, C_ROWS, 3 * D))
    mods = []
    for i in range(DEPTH):
        lat = mod_mine[i, :bsz].reshape(bsz, 1, 3, D)
        con = jnp.broadcast_to(mod_mine[i, bsz].reshape(1, 1, 3, D), (bsz, 1, 3, D))
        mods.append(jnp.pad(jnp.concatenate([con, lat], axis=1), ((0, 0), (0, 0), (0, 5), (0, 0))))
    pool_scale_full = misc_all[:, DEPTH * nrow:DEPTH * nrow + 2, :pool_scale.shape[1]].transpose(1, 0, 2).reshape(2, D)

    first, shard, axis = BIG[0]
    w_in_full = _from_shards(xy_exchange([weights[first].astype(BF16)], False, "gather_w_in")[0], shard, axis)
    even = [dict(w_in=w_in_full[j], sink=attn_sink[j], ssm=tuple(weights[n][j] for n in SSM_NAMES),
                 glu_b=glu_b[j].reshape(1, SSM_W)) for j in range(2)]
    odd = [dict(pool_scale=pool_scale_full[j].reshape(1, D)) for j in range(2)]

    def fill(gathered):
        full = {n: _from_shards(g, shard, axis) for (n, shard, axis), g in zip(BIG[1:], gathered)}
        for j in range(2):
            even[j].update(w_out=full["even_w_out"][j], glu_w=full["glu_w"][j])
            odd[j].update(w_in=full["odd_w_in"][j], w_out=full["odd_w_out"][j], pool_w=full["pool_w"][j])

    late = ([weights[n].astype(BF16) for n, _, _ in BIG[1:]], fill)
    loss_local, grad_x, dmods, grads = local_step(
        x, ctx, loss_target, mods, norm_g, final_g, even, odd, late,
        scatter=lambda grads: [_grad_piece(grads, n, j) for n, j in RIDE])
    loss = lax.psum(loss_local, ("x", "y", "c"))

    d_rows = jnp.stack([jnp.concatenate([dm[:, 1].reshape(bsz, 3 * D), dm[:, 0].reshape(bsz, 3 * D),
                                         jnp.zeros((C_ROWS - 2 * bsz, 3 * D), F32)]) for dm in dmods])
    d_all = allgather8(d_rows.reshape(DEPTH * C_ROWS, 3 * D), "gather_dmod")
    d_all = d_all.reshape(N_DEV, DEPTH, C_ROWS, 3 * D).transpose(1, 0, 2, 3).reshape(DEPTH, nrow, 3 * D)
    d_cols = lax.dynamic_slice(d_all, (0, 0, pos * ADA_COLS), (DEPTH, nrow, ADA_COLS))
    g_ada_w, ds_ctx = ada_bwd(c_all, d_cols, ada_w, "ada_bwd")
    g_ada_b = ada_bias_grad(d_all, "ada_bias_grad")[:, 0]

    small = dict(ds_ctx=ds_ctx[0] * (ac == 0).astype(F32), norm_g=jnp.stack(grads["norm_g"]), final_g=grads["final_g"],
                 attn_sink=jnp.stack([grads["even"][j]["sink"] for j in range(2)]),
                 glu_b=jnp.stack([grads["even"][j]["glu_b"] for j in range(2)]),
                 pool_scale=jnp.stack([grads["odd"][j]["pool_scale"] for j in range(2)]))
    for k, n in enumerate(SSM_NAMES):
        small[n] = jnp.stack([grads["even"][j]["ssm"][k] for j in range(2)])
    packed = pack_small(small)
    small_sum = sum_slots(allgather8(packed, "gather_small").reshape(N_DEV, packed.shape[0], 128), "sum_small")
    g_small = unpack_small(small_sum)
    g_small["c_ctx"] = silu_chain(g_small.pop("ds_ctx").reshape(1, D), c_ctx.reshape(1, D), "c_ctx_grad").reshape(D)
    g_small["ada_b"] = g_ada_b
    g_small["pool_scale"] = lax.dynamic_slice(g_small["pool_scale"], (0, pos * 256), (2, 256))

    landed = dict(zip(RIDE, grads["landed"]))
    landed.update(zip(LAST, xy_exchange([_grad_piece(grads, n, j) for n, j in LAST], True, "scatter_grads")))
    mine4 = [jnp.stack([sum_slots(landed[n, j].reshape(4, -1, shard[-1]), f"sum_positions_{n}{j}").reshape(shard[1:])
                        for j in range(2)]) for n, shard, _ in BIG]
    other4 = sibling_exchange(mine4, "swap_cores")
    g_mine = dict(zip([n for n, _, _ in BIG], mine4))
    g_other = dict(zip([n for n, _, _ in BIG], other4))

    results = {}
    for n in WEIGHT_NAMES:
        if n in g_mine:
            gs = [g_mine[n], g_other[n]]
        elif n == "ada_w":
            gs = [g_ada_w]
        else:
            gs = [g_small[n]]
        results[n] = adamw(weights[n], env["m_" + n], env["v_" + n], gs, "adamw_" + n)
    outs = [loss, grad_x]
    for k in range(4):
        outs += [results[n][k] for n in WEIGHT_NAMES]
    return tuple(outs)
```

```python
import functools

import numpy as np
import jax
import jax.numpy as jnp
from jax import lax
from jax.experimental import pallas as pl
from jax.experimental.pallas import tpu as pltpu

F32 = jnp.float32
BF16 = jnp.bfloat16
MESH = pl.DeviceIdType.MESH

D = 1024
DEPTH = 4
EPS = 1e-6
NEG_INF = -1e30
GRID_W = 64
ROPE_BASE = 10000.0
ROPE_FREQS = 16
HEAD_DIM = 64
N_HEADS = 8
N_KV = 2
GROUP = 4
ATTN_W = N_HEADS * HEAD_DIM
KV_W = N_KV * HEAD_DIM
WINDOW = 128
AB = 128
SSM_W = 512
G = 32
C = 16
P = 64
Q = 16
QC = Q * C
P2 = 2 * P
SCAN_G = 16
POOL_R = (1, 2, 4, 8)
POOL_G = 256
HALO = 8
TM = 256
EVEN_SPLITS = (512, 256, 512, 512, 512)
ODD_SPLITS = (1024, 1024)

ADAM_LR = 0.001
ADAM_B1 = 0.9
ADAM_B2 = 0.999
ADAM_EPS = 1e-08
ADAM_WD = 0.01
ADAM_STEP = 10

MB = 1024 * 1024


def _cp(vmem_mb=48, n_axes=0):
    kw = dict(vmem_limit_bytes=vmem_mb * MB)
    if n_axes:
        kw["dimension_semantics"] = ("arbitrary",) * n_axes
    return pltpu.CompilerParams(**kw)


def _sig(x):
    return 1.0 / (1.0 + jnp.exp(-x))


def _silu_and_grad(x):
    s = _sig(x)
    return x * s, s * (1.0 + x * (1.0 - s))


_GELU_C = 0.7978845608028654
_GELU_A = 0.044715


def _gelu_and_grad(x):
    th = jnp.tanh(_GELU_C * (x + _GELU_A * x * x * x))
    val = 0.5 * x * (1.0 + th)
    grad = 0.5 * (1.0 + th) + 0.5 * x * (1.0 - th * th) * _GELU_C * (1.0 + 3.0 * _GELU_A * x * x)
    return val, grad


def _rms(h):
    r = lax.rsqrt(jnp.mean(h * h, axis=-1, keepdims=True) + EPS)
    return h * r, r


def _dot(a, b):
    return jnp.dot(a, b, preferred_element_type=F32)


def _dot_nt(a, b):
    return lax.dot_general(a, b, (((1,), (1,)), ((), ())), preferred_element_type=F32)


def _dot_tn(a, b):
    return lax.dot_general(a, b, (((0,), (0,)), ((), ())), preferred_element_type=F32)


def _rowsum(x):
    return jnp.sum(x, axis=0, keepdims=True)


def _seg(t):
    return jnp.minimum(t, 1)


def _row_spec(n):
    return pl.BlockSpec((None, TM, n), lambda b, t: (b, t, 0))


def _const_spec(shape):
    nd = len(shape)
    return pl.BlockSpec(shape, lambda b, t: (0,) * nd)


def _mod_spec():
    return pl.BlockSpec((None, None, 8, D), lambda b, t: (b, _seg(t), 0, 0))


def norm_in(h, g, mod, w, splits, gates, name):
    bsz, t_len, _ = h.shape
    n = w.shape[1]
    offs = [int(v) for v in np.cumsum((0,) + tuple(splits))]

    def body(h_ref, g_ref, mod_ref, w_ref, a_ref, *outs):
        xh, _ = _rms(h_ref[...])
        a = xh * g_ref[...] * (1.0 + mod_ref[1:2, :]) + mod_ref[0:1, :]
        ab = a.astype(BF16)
        a_ref[...] = ab
        z = _dot(ab, w_ref[...])
        for o, lo, hi in zip(outs, offs[:-1], offs[1:]):
            o[...] = z[:, lo:hi].astype(o.dtype)

    return pl.pallas_call(
        body, name=name, grid=(bsz, t_len // TM),
        in_specs=[_row_spec(D), _const_spec((1, D)), _mod_spec(), _const_spec((D, n))],
        out_specs=[_row_spec(D)] + [_row_spec(s) for s in splits],
        out_shape=[jax.ShapeDtypeStruct((bsz, t_len, D), BF16)]
        + [jax.ShapeDtypeStruct((bsz, t_len, s), BF16 if k in gates else F32) for k, s in enumerate(splits)],
        compiler_params=_cp(48, 2),
    )(h, g, mod, w)


def norm_in_bwd(dparts, dh_in, h, g, mod, w, name, skip=None):
    bsz, t_len, _ = h.shape
    n = w.shape[1]
    k = len(dparts)
    extra = [] if skip is None else [skip[1], skip[2]]

    def body(*refs):
        parts = [r[...] for r in refs[:k]]
        if skip is not None:
            parts[skip[0]] = parts[skip[0]] + refs[k][...] * refs[k + 1][...]
        parts = [p.astype(BF16) for p in parts]
        dh_in_ref, h_ref, g_ref, mod_ref, w_ref, dh_ref, dz_ref, dmod_ref, dg_ref = refs[k + len(extra):]
        b, t = pl.program_id(0), pl.program_id(1)
        dz = jnp.concatenate(parts, axis=1)
        dz_ref[...] = dz
        da = _dot_nt(dz, w_ref[...])
        xh, r = _rms(h_ref[...])
        gg = g_ref[...]
        sc1 = 1.0 + mod_ref[1:2, :]

        @pl.when(t <= 1)
        def _():
            dmod_ref[...] = jnp.zeros_like(dmod_ref)

        @pl.when((b == 0) & (t == 0))
        def _():
            dg_ref[...] = jnp.zeros_like(dg_ref)

        dmod_ref[0:1, :] += _rowsum(da)
        dmod_ref[1:2, :] += _rowsum(da * (xh * gg))
        dg_ref[0:1, :] += _rowsum(da * sc1 * xh)
        dxh = da * gg * sc1
        dh_ref[...] = dh_in_ref[...] + r * (dxh - xh * jnp.mean(dxh * xh, axis=-1, keepdims=True))

    return pl.pallas_call(
        body, name=name, grid=(bsz, t_len // TM),
        in_specs=[_row_spec(p.shape[-1]) for p in dparts]
        + ([_row_spec(extra[0].shape[-1]), _const_spec(extra[1].shape)] if extra else [])
        + [_row_spec(D), _row_spec(D), _const_spec((1, D)), _mod_spec(), _const_spec((D, n))],
        out_specs=[_row_spec(D), _row_spec(n), _mod_spec(), _const_spec((8, D))],
        out_shape=[jax.ShapeDtypeStruct((bsz, t_len, D), F32), jax.ShapeDtypeStruct((bsz, t_len, n), BF16),
                   jax.ShapeDtypeStruct((bsz, 2, 8, D), F32), jax.ShapeDtypeStruct((8, D), F32)],
        compiler_params=_cp(56, 2),
    )(*dparts, *extra, dh_in, h, g, mod, w)


def matmul_tn(a, b, m, n, name, a_col=0, b_col=0):
    rows = a.shape[0]
    tr = 512 if rows % 512 == 0 else rows
    tn = n
    for cand in (1024, 768, 512, 256, 128):
        if n > 1024 and n % cand == 0:
            tn = cand
            break
    nb = n // tn

    def body(a_ref, b_ref, o_ref):
        @pl.when(pl.program_id(1) == 0)
        def _():
            o_ref[...] = jnp.zeros_like(o_ref)
        o_ref[...] += _dot_tn(a_ref[...].astype(BF16), b_ref[...].astype(BF16))

    return pl.pallas_call(
        body, name=name, grid=(nb, rows // tr),
        in_specs=[pl.BlockSpec((tr, m), lambda j, r: (r, a_col)),
                  pl.BlockSpec((tr, tn), lambda j, r: (r, b_col * nb + j))],
        out_specs=pl.BlockSpec((m, tn), lambda j, r: (0, j)),
        out_shape=jax.ShapeDtypeStruct((m, n), F32),
        compiler_params=_cp(48, 2),
    )(a, b)


def even_out(h, mod, o_attn, g_attn, y_scan, u, d_skip, g_ssm, glu_w, glu_b, w_out, name):
    bsz, t_len, _ = h.shape

    def body(h_ref, mod_ref, oa_ref, ga_ref, ys_ref, u_ref, dk_ref, gs_ref, gw_ref, gb_ref, wo_ref,
             hn_ref, mix_ref, yo_ref):
        zz, _ = _gelu_and_grad(ys_ref[...] + u_ref[...] * dk_ref[...])
        s = _dot(zz.astype(BF16), gw_ref[...]) + gb_ref[...]
        o_ssm = zz * _sig(s)
        sa, _ = _silu_and_grad(ga_ref[...].astype(F32))
        ss, _ = _silu_and_grad(gs_ref[...].astype(F32))
        mb = jnp.concatenate([oa_ref[...] * sa, o_ssm * ss], axis=1).astype(BF16)
        mix_ref[...] = mb
        yo = _dot(mb, wo_ref[...])
        yo_ref[...] = yo.astype(BF16)
        hn_ref[...] = h_ref[...] + mod_ref[2:3, :] * yo

    return pl.pallas_call(
        body, name=name, grid=(bsz, t_len // TM),
        in_specs=[_row_spec(D), _mod_spec(), _row_spec(512), _row_spec(512), _row_spec(512), _row_spec(512),
                  _const_spec((1, 512)), _row_spec(512), _const_spec((512, 512)), _const_spec((1, 512)),
                  _const_spec((D, D))],
        out_specs=[_row_spec(D), _row_spec(D), _row_spec(D)],
        out_shape=[jax.ShapeDtypeStruct((bsz, t_len, D), F32), jax.ShapeDtypeStruct((bsz, t_len, D), BF16),
                   jax.ShapeDtypeStruct((bsz, t_len, D), BF16)],
        compiler_params=_cp(48, 2),
    )(h, mod, o_attn, g_attn, y_scan, u, d_skip, g_ssm, glu_w, glu_b, w_out)


def even_out_bwd(dh, mod, o_attn, g_attn, y_scan, u, d_skip, g_ssm, glu_w, glu_b, w_out, yout, name):
    bsz, t_len, _ = dh.shape

    def body(dh_ref, mod_ref, oa_ref, ga_ref, ys_ref, u_ref, dk_ref, gs_ref, gw_ref, gb_ref, wo_ref, yo_ref,
             doa_ref, dga_ref, dgs_ref, dys_ref, dyo_ref, zz_ref, ds_ref, dgate_ref, dgb_ref):
        b, t = pl.program_id(0), pl.program_id(1)
        dhv = dh_ref[...]

        @pl.when(t <= 1)
        def _():
            dgate_ref[...] = jnp.zeros_like(dgate_ref)

        @pl.when((b == 0) & (t == 0))
        def _():
            dgb_ref[...] = jnp.zeros_like(dgb_ref)

        dgate_ref[0:1, :] += _rowsum(dhv * yo_ref[...].astype(F32))
        dyb = (mod_ref[2:3, :] * dhv).astype(BF16)
        dyo_ref[...] = dyb
        dmix = _dot_nt(dyb, wo_ref[...])
        sa, dsa = _silu_and_grad(ga_ref[...].astype(F32))
        doa_ref[...] = dmix[:, :512] * sa
        dga_ref[...] = (dmix[:, :512] * oa_ref[...] * dsa).astype(BF16)
        uv = u_ref[...]
        zz, dzz_dy = _gelu_and_grad(ys_ref[...] + uv * dk_ref[...])
        zb = zz.astype(BF16)
        zz_ref[...] = zb
        sg = _sig(_dot(zb, gw_ref[...]) + gb_ref[...])
        ss, dss = _silu_and_grad(gs_ref[...].astype(F32))
        dm = dmix[:, 512:]
        dgs_ref[...] = (dm * (zz * sg) * dss).astype(BF16)
        do = dm * ss
        ds = do * zz * sg * (1.0 - sg)
        dsb = ds.astype(BF16)
        ds_ref[...] = dsb
        dgb_ref[0:1, :] += _rowsum(ds)
        dys = (do * sg + _dot_nt(dsb, gw_ref[...])) * dzz_dy
        dys_ref[...] = dys
        dgb_ref[1:2, :] += _rowsum(dys * uv)

    r512 = jax.ShapeDtypeStruct((bsz, t_len, 512), F32)
    b512 = jax.ShapeDtypeStruct((bsz, t_len, 512), BF16)
    return pl.pallas_call(
        body, name=name, grid=(bsz, t_len // TM),
        in_specs=[_row_spec(D), _mod_spec(), _row_spec(512), _row_spec(512), _row_spec(512), _row_spec(512),
                  _const_spec((1, 512)), _row_spec(512), _const_spec((512, 512)), _const_spec((1, 512)),
                  _const_spec((D, D)), _row_spec(D)],
        out_specs=[_row_spec(512)] * 4 + [_row_spec(D), _row_spec(512), _row_spec(512), _mod_spec(),
                                           _const_spec((8, 512))],
        out_shape=[r512, b512, b512, r512, jax.ShapeDtypeStruct((bsz, t_len, D), BF16),
                   jax.ShapeDtypeStruct((bsz, t_len, 512), BF16), jax.ShapeDtypeStruct((bsz, t_len, 512), BF16),
                   jax.ShapeDtypeStruct((bsz, 2, 8, D), F32), jax.ShapeDtypeStruct((8, 512), F32)],
        compiler_params=_cp(48, 2),
    )(dh, mod, o_attn, g_attn, y_scan, u, d_skip, g_ssm, glu_w, glu_b, w_out, yout)


def _split3_dot(band, x):
    x1 = x.astype(BF16)
    r1 = x - x1.astype(F32)
    x2 = r1.astype(BF16)
    x3 = (r1 - x2.astype(F32)).astype(BF16)
    return _dot(band, x3) + _dot(band, x2) + _dot(band, x1)


def pool_band(x, lc, transpose, name):
    bsz, t_len, _ = x.shape
    assert lc == TM
    hb = TM // HALO

    def body(xp_ref, xc_ref, xn_ref, o_ref):
        t = pl.program_id(1)
        seg_lo = jnp.where(t == 0, 0, lc)
        seg_hi = jnp.where(t == 0, lc, t_len)
        cur = xc_ref[...]
        xh = jnp.concatenate([xp_ref[...], cur, xn_ref[...]], axis=0)
        row_t = t * TM + lax.broadcasted_iota(jnp.int32, (TM, 1), 0)
        col_s = t * TM - HALO + lax.broadcasted_iota(jnp.int32, (1, TM + 2 * HALO), 1)
        row_s = t * TM - HALO + lax.broadcasted_iota(jnp.int32, (TM + 2 * HALO, 1), 0)
        s_ok = (col_s >= seg_lo) & (col_s < seg_hi)
        outs = []
        for gi, r in enumerate(POOL_R):
            band = ((jnp.abs(row_t - col_s) <= r) & s_ok).astype(BF16)
            xg = xh[:, gi * POOL_G:(gi + 1) * POOL_G]
            if transpose:
                cnt_s = jnp.minimum(row_s + r, seg_hi - 1) - jnp.maximum(row_s - r, seg_lo) + 1
                xg = xg * (1.0 / jnp.maximum(cnt_s, 1).astype(F32))
            acc = _split3_dot(band, xg)
            if not transpose:
                cnt_t = jnp.minimum(row_t + r, seg_hi - 1) - jnp.maximum(row_t - r, seg_lo) + 1
                acc = acc * (1.0 / cnt_t.astype(F32))
            outs.append(acc - cur[:, gi * POOL_G:(gi + 1) * POOL_G])
        o_ref[...] = jnp.concatenate(outs, axis=1).astype(BF16)

    return pl.pallas_call(
        body, name=name, grid=(bsz, t_len // TM),
        in_specs=[pl.BlockSpec((None, HALO, D), lambda b, t: (b, jnp.maximum(t * hb - 1, 0), 0)),
                  _row_spec(D),
                  pl.BlockSpec((None, HALO, D), lambda b, t: (b, jnp.minimum((t + 1) * hb, t_len // HALO - 1), 0))],
        out_specs=_row_spec(D),
        out_shape=jax.ShapeDtypeStruct((bsz, t_len, D), BF16),
        compiler_params=_cp(48, 2),
    )(x, x, x)


def pool_out(h, mod, pm, gate, pool_w, pool_scale, w_out, name):
    bsz, t_len, _ = h.shape

    def body(h_ref, mod_ref, pm_ref, gt_ref, pw_ref, ps_ref, wo_ref, hn_ref, mix_ref, yo_ref):
        pmv = pm_ref[...]
        ppre = jnp.concatenate([_dot(pmv[:, g * POOL_G:(g + 1) * POOL_G].astype(BF16), pw_ref[g])
                                for g in range(4)], axis=1)
        sl, _ = _silu_and_grad(gt_ref[...].astype(F32))
        mb = (ppre * ps_ref[...] * sl).astype(BF16)
        mix_ref[...] = mb
        yo = _dot(mb, wo_ref[...])
        yo_ref[...] = yo.astype(BF16)
        hn_ref[...] = h_ref[...] + mod_ref[2:3, :] * yo

    return pl.pallas_call(
        body, name=name, grid=(bsz, t_len // TM),
        in_specs=[_row_spec(D), _mod_spec(), _row_spec(D), _row_spec(D), _const_spec((4, POOL_G, POOL_G)),
                  _const_spec((1, D)), _const_spec((D, D))],
        out_specs=[_row_spec(D), _row_spec(D), _row_spec(D)],
        out_shape=[jax.ShapeDtypeStruct((bsz, t_len, D), F32), jax.ShapeDtypeStruct((bsz, t_len, D), BF16),
                   jax.ShapeDtypeStruct((bsz, t_len, D), BF16)],
        compiler_params=_cp(48, 2),
    )(h, mod, pm, gate, pool_w, pool_scale, w_out)


def pool_out_bwd(dh, mod, pm, gate, pool_w, pool_scale, w_out, yout, name):
    bsz, t_len, _ = dh.shape

    def body(dh_ref, mod_ref, pm_ref, gt_ref, pw_ref, ps_ref, wo_ref, yo_ref,
             dpm_ref, dgt_ref, dyo_ref, dpp_ref, dgate_ref, dps_ref):
        b, t = pl.program_id(0), pl.program_id(1)
        dhv = dh_ref[...]

        @pl.when(t <= 1)
        def _():
            dgate_ref[...] = jnp.zeros_like(dgate_ref)

        @pl.when((b == 0) & (t == 0))
        def _():
            dps_ref[...] = jnp.zeros_like(dps_ref)

        dgate_ref[0:1, :] += _rowsum(dhv * yo_ref[...].astype(F32))
        dyb = (mod_ref[2:3, :] * dhv).astype(BF16)
        dyo_ref[...] = dyb
        dmix = _dot_nt(dyb, wo_ref[...])
        pmv = pm_ref[...]
        ppre = jnp.concatenate([_dot(pmv[:, g * POOL_G:(g + 1) * POOL_G].astype(BF16), pw_ref[g])
                                for g in range(4)], axis=1)
        ps = ps_ref[...]
        sl, dsl = _silu_and_grad(gt_ref[...].astype(F32))
        dp = dmix * sl
        dgt_ref[...] = (dmix * (ppre * ps) * dsl).astype(BF16)
        dps_ref[0:1, :] += _rowsum(dp * ppre)
        dppb = (dp * ps).astype(BF16)
        dpp_ref[...] = dppb
        dpm_ref[...] = jnp.concatenate([_dot_nt(dppb[:, g * POOL_G:(g + 1) * POOL_G], pw_ref[g])
                                        for g in range(4)], axis=1)

    return pl.pallas_call(
        body, name=name, grid=(bsz, t_len // TM),
        in_specs=[_row_spec(D), _mod_spec(), _row_spec(D), _row_spec(D), _const_spec((4, POOL_G, POOL_G)),
                  _const_spec((1, D)), _const_spec((D, D)), _row_spec(D)],
        out_specs=[_row_spec(D), _row_spec(D), _row_spec(D), _row_spec(D), _mod_spec(), _const_spec((8, D))],
        out_shape=[jax.ShapeDtypeStruct((bsz, t_len, D), F32), jax.ShapeDtypeStruct((bsz, t_len, D), BF16),
                   jax.ShapeDtypeStruct((bsz, t_len, D), BF16), jax.ShapeDtypeStruct((bsz, t_len, D), BF16),
                   jax.ShapeDtypeStruct((bsz, 2, 8, D), F32), jax.ShapeDtypeStruct((8, D), F32)],
        compiler_params=_cp(48, 2),
    )(dh, mod, pm, gate, pool_w, pool_scale, w_out, yout)


def loss_head(h, final_g, target, name):
    bsz, t_len, _ = h.shape

    def body(h_ref, g_ref, tg_ref, dh_ref, loss_ref, dg_ref):
        b, t = pl.program_id(0), pl.program_id(1)

        @pl.when((b == 0) & (t == 0))
        def _():
            loss_ref[...] = jnp.zeros_like(loss_ref)
            dg_ref[...] = jnp.zeros_like(dg_ref)

        lat = (t > 0).astype(F32)
        xh, r = _rms(h_ref[...])
        gg = g_ref[...]
        err = (xh * gg - tg_ref[...]) * lat
        loss_ref[...] += 0.5 * jnp.sum(jnp.mean(err * err, axis=-1, keepdims=True))
        dy = err * (1.0 / D)
        dg_ref[0:1, :] += _rowsum(dy * xh)
        dxh = dy * gg
        dh_ref[...] = r * (dxh - xh * jnp.mean(dxh * xh, axis=-1, keepdims=True))

    return pl.pallas_call(
        body, name=name, grid=(bsz, t_len // TM),
        in_specs=[_row_spec(D), _const_spec((1, D)),
                  pl.BlockSpec((None, TM, D), lambda b, t: (b, jnp.maximum(t - 1, 0), 0))],
        out_specs=[_row_spec(D), _const_spec((8, 128)), _const_spec((8, D))],
        out_shape=[jax.ShapeDtypeStruct((bsz, t_len, D), F32), jax.ShapeDtypeStruct((8, 128), F32),
                   jax.ShapeDtypeStruct((8, D), F32)],
        compiler_params=_cp(48, 2),
    )(h, final_g, target)


def _swap16(x):
    n = x.shape[-1]
    ax = x.ndim - 1
    lane = lax.broadcasted_iota(jnp.int32, x.shape, ax)
    return jnp.where((lane % 32) < 16, pltpu.roll(x, n - 16, ax), pltpu.roll(x, 16, ax))


def _rope(x, cos, sin):
    return x * cos + _swap16(x) * sin


def _rope_t(dy, cos, sin):
    return dy * cos + _swap16(dy * sin)


def rope_tables(lc, seq):
    rows = seq // GRID_W
    row = jnp.repeat(jnp.arange(rows, dtype=F32), GRID_W)
    col = jnp.tile(jnp.arange(GRID_W, dtype=F32), rows)
    inv_freq = ROPE_BASE ** (-jnp.arange(ROPE_FREQS, dtype=F32) / ROPE_FREQS)
    ar, ac = row[:, None] * inv_freq, col[:, None] * inv_freq
    cos = jnp.concatenate([jnp.cos(ar), jnp.cos(ar), jnp.cos(ac), jnp.cos(ac)], axis=1)
    sin = jnp.concatenate([-jnp.sin(ar), jnp.sin(ar), -jnp.sin(ac), jnp.sin(ac)], axis=1)
    cos = jnp.concatenate([jnp.ones((lc, HEAD_DIM), F32), cos], axis=0)
    sin = jnp.concatenate([jnp.zeros((lc, HEAD_DIM), F32), sin], axis=0)
    return jnp.tile(cos, (1, 2)), jnp.tile(sin, (1, 2))


def _attn_mask(i, lc, t_len):
    qrow = i * AB + lax.broadcasted_iota(jnp.int32, (AB, 1), 0)
    kloc = (i - 1) * AB + lax.broadcasted_iota(jnp.int32, (1, 3 * AB), 1)
    valid = (qrow >= lc) & (kloc >= lc) & (kloc < t_len) & (jnp.abs(qrow - kloc) <= WINDOW)
    mask = jnp.concatenate([valid, jnp.ones((AB, lc), jnp.bool_)], axis=1)
    return jnp.concatenate([mask] * GROUP, axis=0)


def _attn_specs(t_len, lc):
    nb = t_len // AB
    prev = lambda b, i: (b, jnp.maximum(i - 1, 0), 0)
    cur = lambda b, i: (b, i, 0)
    nxt = lambda b, i: (b, jnp.minimum(i + 1, nb - 1), 0)
    kv = [pl.BlockSpec((None, AB, 2 * KV_W), f) for f in (prev, cur, nxt)]
    kv.append(pl.BlockSpec((None, lc, 2 * KV_W), lambda b, i: (b, 0, 0)))
    tab = [pl.BlockSpec((AB, 128), lambda b, i, f=f: f(b, i)[1:]) for f in (prev, cur, nxt)]
    return kv, tab


def _attn_keys(kvp, kvc, kvn, kvx, cp, cc, cn, sp, sc, sn):
    kk = jnp.concatenate([_rope(kvp[:, :KV_W], cp, sp), _rope(kvc[:, :KV_W], cc, sc),
                          _rope(kvn[:, :KV_W], cn, sn), kvx[:, :KV_W]], axis=0)
    vv = jnp.concatenate([kvp[:, KV_W:], kvc[:, KV_W:], kvn[:, KV_W:], kvx[:, KV_W:]], axis=0)
    return kk, vv


def _stack_heads(x, hk):
    return jnp.concatenate([x[:, (GROUP * hk + g) * HEAD_DIM:(GROUP * hk + g + 1) * HEAD_DIM]
                            for g in range(GROUP)], axis=0)


def _sink_col(sink_ref, hk):
    return jnp.concatenate([jnp.full((AB, 1), sink_ref[GROUP * hk + g], F32) for g in range(GROUP)], axis=0)


def attn_fwd(q, kv, cos, sin, sink, lc, name, carry=()):
    bsz, t_len, _ = q.shape
    nb = t_len // AB
    kv_specs, tab_specs = _attn_specs(t_len, lc)
    scale = HEAD_DIM ** -0.5
    nc = len(carry)

    def body(sink_ref, q_ref, kvp_ref, kvc_ref, kvn_ref, kvx_ref, cp, cc, cn, sp, sc, sn, *rest):
        o_ref, lse_ref = rest[nc:nc + 2]
        b, i = pl.program_id(0), pl.program_id(1)
        if nc:
            start, wait = _xy_copies(rest[:nc], rest[nc + 2:2 * nc + 2], *rest[2 * nc + 2:], False)
            pl.when((b == 0) & (i == 0))(start)
        mask = _attn_mask(i, lc, t_len)
        qr = _rope(q_ref[...], jnp.tile(cc[...], (1, 4)), jnp.tile(sc[...], (1, 4)))
        kk, vv = _attn_keys(kvp_ref[...], kvc_ref[...], kvn_ref[...], kvx_ref[...],
                            cp[...], cc[...], cn[...], sp[...], sc[...], sn[...])
        outs, lses = [], []
        for hk in range(N_KV):
            kh = kk[:, hk * HEAD_DIM:(hk + 1) * HEAD_DIM].astype(BF16)
            vh = vv[:, hk * HEAD_DIM:(hk + 1) * HEAD_DIM].astype(BF16)
            q4 = _stack_heads(qr, hk).astype(BF16)
            s = jnp.where(mask, _dot_nt(q4, kh) * scale, NEG_INF)
            sk = _sink_col(sink_ref, hk)
            m = jnp.maximum(jnp.max(s, axis=-1, keepdims=True), sk)
            p = jnp.exp(s - m)
            l = jnp.sum(p, axis=-1, keepdims=True) + jnp.exp(sk - m)
            o = _dot(p.astype(BF16), vh) / l
            lse = m + jnp.log(l)
            for g in range(GROUP):
                outs.append(o[g * AB:(g + 1) * AB])
                lses.append(lse[g * AB:(g + 1) * AB])
        o_ref[...] = jnp.concatenate(outs, axis=1)
        lse_ref[...] = jnp.concatenate(lses, axis=1)
        if nc:
            pl.when((b == bsz - 1) & (i == nb - 1))(wait)

    return pl.pallas_call(
        body, name=name, grid=(bsz, nb),
        in_specs=[pl.BlockSpec(memory_space=pltpu.SMEM),
                  pl.BlockSpec((None, AB, ATTN_W), lambda b, i: (b, i, 0))] + kv_specs + tab_specs + tab_specs
        + [HBM_SPEC] * nc,
        out_specs=[pl.BlockSpec((None, AB, ATTN_W), lambda b, i: (b, i, 0)),
                   pl.BlockSpec((None, AB, N_HEADS), lambda b, i: (b, i, 0))] + [HBM_SPEC] * nc,
        out_shape=[jax.ShapeDtypeStruct((bsz, t_len, ATTN_W), F32), jax.ShapeDtypeStruct((bsz, t_len, N_HEADS), F32)]
        + _xy_out_shapes(carry, False),
        scratch_shapes=_xy_sems(nc) if nc else [],
        compiler_params=_cp(48, 2),
    )(sink, q, kv, kv, kv, kv, cos, cos, cos, sin, sin, sin, *carry)


def attn_bwd(q, kv, o, lse, do, cos, sin, sink, lc, name, carry=()):
    bsz, t_len, _ = q.shape
    nb = t_len // AB
    kv_specs, tab_specs = _attn_specs(t_len, lc)
    scale = HEAD_DIM ** -0.5
    blk = lambda w: pl.BlockSpec((None, AB, w), lambda b, i: (b, i, 0))
    full_tab = pl.BlockSpec((t_len, 128), lambda b, i: (0, 0))
    nc = len(carry)

    def body(sink_ref, q_ref, kvp_ref, kvc_ref, kvn_ref, kvx_ref, cp, cc, cn, sp, sc, sn, cf, sf,
             o_ref, lse_ref, do_ref, *rest):
        dq_ref, dkv_ref, dsink_ref = rest[nc:nc + 3]
        b, i = pl.program_id(0), pl.program_id(1)
        if nc:
            start, wait = _xy_copies(rest[:nc], rest[nc + 3:2 * nc + 3], *rest[2 * nc + 3:], True)
            pl.when((b == 0) & (i == 0))(start)

        @pl.when(i == 0)
        def _():
            dkv_ref[...] = jnp.zeros_like(dkv_ref)

        @pl.when((b == 0) & (i == 0))
        def _():
            dsink_ref[...] = jnp.zeros_like(dsink_ref)

        mask = _attn_mask(i, lc, t_len)
        cq, sq = jnp.tile(cc[...], (1, 4)), jnp.tile(sc[...], (1, 4))
        qr = _rope(q_ref[...], cq, sq)
        kk, vv = _attn_keys(kvp_ref[...], kvc_ref[...], kvn_ref[...], kvx_ref[...],
                            cp[...], cc[...], cn[...], sp[...], sc[...], sn[...])
        dov, ov, lsev = do_ref[...], o_ref[...], lse_ref[...]
        dqs, dks, dvs, dsk = [], [], [], []
        for hk in range(N_KV):
            kh = kk[:, hk * HEAD_DIM:(hk + 1) * HEAD_DIM].astype(BF16)
            vh = vv[:, hk * HEAD_DIM:(hk + 1) * HEAD_DIM].astype(BF16)
            q4 = _stack_heads(qr, hk).astype(BF16)
            do4 = _stack_heads(dov, hk)
            o4 = _stack_heads(ov, hk)
            lse4 = jnp.concatenate([lsev[:, GROUP * hk + g:GROUP * hk + g + 1] for g in range(GROUP)], axis=0)
            delta = jnp.sum(do4 * o4, axis=-1, keepdims=True)
            s = jnp.where(mask, _dot_nt(q4, kh) * scale, NEG_INF)
            p = jnp.exp(s - lse4)
            do4b = do4.astype(BF16)
            dp = _dot_nt(do4b, vh)
            ds = (p * (dp - delta) * scale).astype(BF16)
            dq4 = _dot(ds, kh)
            dks.append(_dot_tn(ds, q4))
            dvs.append(_dot_tn(p.astype(BF16), do4b))
            pd = jnp.exp(_sink_col(sink_ref, hk) - lse4) * delta
            for g in range(GROUP):
                dqs.append(dq4[g * AB:(g + 1) * AB])
                dsk.append(-jnp.sum(pd[g * AB:(g + 1) * AB], axis=0, keepdims=True))
        dq_ref[...] = _rope_t(jnp.concatenate(dqs, axis=1), cq, sq).astype(BF16)
        dsink_ref[0:1, :] += jnp.concatenate(dsk, axis=1)
        dkv = jnp.concatenate(dks + dvs, axis=1)
        starts = (jnp.maximum(i - 1, 0), i, jnp.minimum(i + 1, nb - 1))
        for j, st in enumerate(starts):
            rows = pl.ds(pl.multiple_of(st * AB, AB), AB)
            dkv_ref[rows, :] += dkv[j * AB:(j + 1) * AB]
        dkv_ref[0:lc, :] += dkv[3 * AB:]

        @pl.when(i == nb - 1)
        def _():
            def unrotate(j, carry):
                rows = pl.ds(pl.multiple_of(j * AB, AB), AB)
                dkv_ref[rows, 0:KV_W] = _rope_t(dkv_ref[rows, 0:KV_W], cf[rows, :], sf[rows, :])
                return carry
            lax.fori_loop(0, nb, unrotate, 0)

        if nc:
            pl.when((b == bsz - 1) & (i == nb - 1))(wait)

    return pl.pallas_call(
        body, name=name, grid=(bsz, nb),
        in_specs=[pl.BlockSpec(memory_space=pltpu.SMEM), blk(ATTN_W)] + kv_specs + tab_specs + tab_specs
        + [full_tab, full_tab, blk(ATTN_W), blk(N_HEADS), blk(ATTN_W)] + [HBM_SPEC] * nc,
        out_specs=[blk(ATTN_W), pl.BlockSpec((None, t_len, 2 * KV_W), lambda b, i: (b, 0, 0)),
                   pl.BlockSpec((8, N_HEADS), lambda b, i: (0, 0))] + [HBM_SPEC] * nc,
        out_shape=[jax.ShapeDtypeStruct((bsz, t_len, ATTN_W), BF16), jax.ShapeDtypeStruct((bsz, t_len, 2 * KV_W), F32),
                   jax.ShapeDtypeStruct((8, N_HEADS), F32)] + _xy_out_shapes(carry, True),
        scratch_shapes=_xy_sems(nc) if nc else [],
        compiler_params=_cp(56, 2),
    )(sink, q, kv, kv, kv, kv, cos, cos, cos, sin, sin, sin, cos, sin, o, lse, do, *carry)


def _s5_mats_dir(a_re, a_im, log_dt, b_re, b_im, c_re, c_im, flip):
    hp = lax.Precision.HIGHEST
    lam = lax.complex(a_re, a_im)
    ldt = lam * jnp.exp(log_dt)[:, None]
    a_bar = jnp.exp(ldt)
    b_bar = ((a_bar - 1.0) / lam)[..., None] * lax.complex(b_re, b_im)
    cm = lax.complex(c_re, c_im)
    tt = np.arange(Q)
    powers = lambda e: jnp.exp(ldt[..., None] * jnp.asarray(e, F32))
    ca = cm[:, :, :, None] * powers(Q - 1 - tt if flip else tt)[:, None, :, :]
    ca = jnp.concatenate([jnp.real(ca), -jnp.imag(ca)], axis=2)
    bb = jnp.concatenate([jnp.real(b_bar), jnp.imag(b_bar)], axis=1)
    k = jnp.einsum('gpk,gcpt->gktc', bb, ca, precision=hp).reshape(G, C, QC)
    slabs = []
    for t1 in range(Q):
        if flip:
            sh = (Q - 1 - t1) * C
            slabs.append(jnp.pad(k, ((0, 0), (0, 0), (0, sh)))[..., sh:])
        else:
            slabs.append(jnp.pad(k, ((0, 0), (0, 0), (t1 * C, 0)))[..., :QC])
    kt = jnp.stack(slabs, axis=1).reshape(G, QC, QC)
    ws = powers(tt if flip else Q - 1 - tt)[:, :, :, None] * b_bar[:, :, None, :]
    ws = ws.transpose(0, 2, 3, 1)
    wo = cm[:, :, :, None] * powers(Q - tt if flip else tt + 1)[:, None, :, :]
    wo = wo.transpose(0, 2, 3, 1)
    ws = jnp.concatenate([jnp.real(ws), jnp.imag(ws)], axis=-1).reshape(G, QC, P2)
    wo = jnp.concatenate([jnp.real(wo), -jnp.imag(wo)], axis=1).reshape(G, P2, QC)
    a1, a2 = _pair_forms(powers([Q]))
    return kt, ws, wo, a1, a2


def _pair_forms(z):
    re, im = jnp.real(z), jnp.imag(z)
    k = z.shape[-1]
    a1 = jnp.concatenate([re, re], axis=1).transpose(2, 0, 1).reshape(k, G * P2)
    a2 = jnp.concatenate([-im, im], axis=1).transpose(2, 0, 1).reshape(k, G * P2)
    return a1, a2


def s5_mats(a_re, a_im, log_dt, b_re, b_im, c_re, c_im):
    per_dir = [_s5_mats_dir(a_re[d], a_im[d], log_dt[d], b_re[d], b_im[d], c_re[d], c_im[d], d == 1)
               for d in range(2)]
    return tuple(jnp.stack([m[i] for m in per_dir]) for i in range(5))


GH = G // 8
RT = 16 * Q


def _perm_consts():
    r = np.arange(RT)
    rows = np.zeros((RT, RT), np.float32)
    rows[(r % Q) * 16 + r // Q, r] = 1.0
    q = np.arange(8 * 128)
    lanes = np.zeros((8 * 128, 8 * 128), np.float32)
    lanes[q, ((q % 128) // C) * 128 + (q // 128) * C + q % C] = 1.0
    return jnp.asarray(rows, BF16), jnp.asarray(lanes, BF16)


def to_groups(x, name):
    bsz, t_len, _ = x.shape
    nc = t_len // Q
    rows, lanes = _perm_consts()

    def body(x_ref, r_ref, p_ref, o_ref, w_ref):
        for j in range(t_len // RT):
            pt = _dot(r_ref[...], x_ref[j * RT:(j + 1) * RT, :].astype(BF16)).astype(BF16)
            for t in range(Q):
                w_ref[j * 16:(j + 1) * 16, t * SSM_W:(t + 1) * SSM_W] = pt[t * 16:(t + 1) * 16, :]
        for gh in range(GH):
            for th in range(2):
                inp = jnp.concatenate([w_ref[:, (th * 8 + tl) * SSM_W + gh * 128:(th * 8 + tl) * SSM_W + (gh + 1) * 128]
                                       for tl in range(8)], axis=1)
                out = _dot(inp, p_ref[...]).astype(BF16)
                for gl in range(8):
                    o_ref[gh * 8 + gl, :, th * 128:(th + 1) * 128] = out[:, gl * 128:(gl + 1) * 128]

    return pl.pallas_call(
        body, name=name, grid=(bsz,),
        in_specs=[pl.BlockSpec((None, t_len, SSM_W), lambda b: (b, 0, 0)), pl.BlockSpec((RT, RT), lambda b: (0, 0)),
                  pl.BlockSpec((1024, 1024), lambda b: (0, 0))],
        out_specs=pl.BlockSpec((None, G, nc, QC), lambda b: (b, 0, 0, 0)),
        out_shape=jax.ShapeDtypeStruct((bsz, G, nc, QC), BF16),
        scratch_shapes=[pltpu.VMEM((nc, Q * SSM_W), BF16)],
        compiler_params=_cp(56, 1),
    )(x, rows, lanes)


def from_groups(xg, name):
    bsz, _, nc, _ = xg.shape
    t_len = nc * Q
    rows, lanes = _perm_consts()

    def body(x_ref, r_ref, p_ref, o_ref, whi_ref, wlo_ref):
        gh = pl.program_id(1)
        for th in range(2):
            inp = jnp.concatenate([x_ref[gl, :, th * 128:(th + 1) * 128] for gl in range(8)], axis=1)
            hi = inp.astype(BF16)
            lo = (inp - hi.astype(F32)).astype(BF16)
            whi_ref[gh, :, th * 1024:(th + 1) * 1024] = _dot(hi, p_ref[...]).astype(BF16)
            wlo_ref[gh, :, th * 1024:(th + 1) * 1024] = _dot(lo, p_ref[...]).astype(BF16)

        @pl.when(gh == GH - 1)
        def _():
            for j in range(t_len // RT):
                def tile(w_ref):
                    return jnp.concatenate(
                        [jnp.concatenate([w_ref[k, j * 16:(j + 1) * 16, t * 128:(t + 1) * 128] for k in range(GH)],
                                         axis=1) for t in range(Q)], axis=0)
                o_ref[j * RT:(j + 1) * RT, :] = _dot(r_ref[...], tile(whi_ref)) + _dot(r_ref[...], tile(wlo_ref))

    return pl.pallas_call(
        body, name=name, grid=(bsz, GH),
        in_specs=[pl.BlockSpec((None, 8, nc, QC), lambda b, k: (b, k, 0, 0)),
                  pl.BlockSpec((RT, RT), lambda b, k: (0, 0)), pl.BlockSpec((1024, 1024), lambda b, k: (0, 0))],
        out_specs=pl.BlockSpec((None, t_len, SSM_W), lambda b, k: (b, 0, 0)),
        out_shape=jax.ShapeDtypeStruct((bsz, t_len, SSM_W), F32),
        scratch_shapes=[pltpu.VMEM((GH, nc, Q * 128), BF16), pltpu.VMEM((GH, nc, Q * 128), BF16)],
        compiler_params=_cp(56, 2),
    )(xg, rows, lanes)


def _gb(shape):
    return pl.BlockSpec((None, None) + shape, lambda g, b: (b, g, 0, 0))


def _gw(shape):
    return pl.BlockSpec((2, None) + shape, lambda g, b: (0, g, 0, 0))


def _gs(nc):
    return pl.BlockSpec((2, None, nc, P2), lambda g, b: (0, b, 0, g))


def s5_chunk_fwd(ug, kt, ws, name):
    bsz, _, nc, _ = ug.shape

    def body(u_ref, kt_ref, ws_ref, y_ref, s_ref):
        ub = u_ref[...]
        y_ref[...] = _dot(ub, kt_ref[0]) + _dot(ub, kt_ref[1])
        s_ref[0] = _dot(ub, ws_ref[0])
        s_ref[1] = _dot(ub, ws_ref[1])

    return pl.pallas_call(
        body, name=name, grid=(G, bsz),
        in_specs=[_gb((nc, QC)), _gw((QC, QC)), _gw((QC, P2))],
        out_specs=[_gb((nc, QC)), _gs(nc)],
        out_shape=[jax.ShapeDtypeStruct((bsz, G, nc, QC), F32), jax.ShapeDtypeStruct((2, bsz, nc, G * P2), F32)],
        compiler_params=_cp(32, 2),
    )(ug, kt, ws)


def s5_scan(s, a1, a2, ncc, reverse, name, hp=None):
    _, bsz, nc, gw = s.shape
    as_rows = lambda v: v.reshape(v.shape[:-1] + (G, P2))
    st = pl.BlockSpec((2, None, nc, SCAN_G, P2), lambda b, w: (0, b, 0, w, 0))
    av = pl.BlockSpec((2, SCAN_G, P2), lambda b, w: (0, w, 0))
    acc = pl.BlockSpec((2, None, SCAN_G, P2), lambda b, w: (0, b, w, 0))
    with_da = hp is not None

    def body(*refs):
        if with_da:
            s_ref, a1_ref, a2_ref, hp_ref, out_ref, da1_ref, da2_ref = refs
        else:
            s_ref, a1_ref, a2_ref, out_ref = refs
        a1v = (a1_ref[0], a1_ref[1])
        a2v = (a2_ref[0], a2_ref[1])
        swap = lambda h: pltpu.roll(h, P, 1)

        def block(jb, carry):
            pb = nblk - 1 - jb if reverse else jb
            base = (pb * ncc, jnp.where(pb == 0, ncc - 1, nc - 1 - (pb - 1) * ncc))
            hs, da1, da2 = carry
            for kk in range(ncc):
                k = ncc - 1 - kk if reverse else kk
                nh, n1, n2 = [], [], []
                for d, n in enumerate((base[0] + k, base[1] - k)):
                    h, hw = hs[d]
                    out_ref[d, n] = h
                    sv = s_ref[d, n]
                    nh.append((a1v[d] * h + a2v[d] * hw + sv, a1v[d] * hw - a2v[d] * h + swap(sv)))
                    if with_da:
                        hv = hp_ref[d, n]
                        n1.append(da1[d] + h * hv)
                        n2.append(da2[d] + h * swap(hv))
                hs, da1, da2 = tuple(nh), tuple(n1), tuple(n2)
            return hs, da1, da2

        assert nc % ncc == 0
        nblk = nc // ncc
        z = jnp.zeros((SCAN_G, P2), F32)
        zz = (z, z) if with_da else ()
        _, da1, da2 = lax.fori_loop(0, nblk, block, (((z, z), (z, z)), zz, zz))
        if with_da:
            for d in range(2):
                da1_ref[d] = da1[d]
                da2_ref[d] = da2[d]

    out_shape = [jax.ShapeDtypeStruct((2, bsz, nc, G, P2), F32)]
    out_specs = [st]
    ins = [as_rows(s), as_rows(a1[:, 0]), as_rows(a2[:, 0])]
    in_specs = [st, av, av]
    if with_da:
        ins.append(as_rows(hp))
        in_specs.append(st)
        out_shape += [jax.ShapeDtypeStruct((2, bsz, G, P2), F32)] * 2
        out_specs += [acc, acc]
    res = pl.pallas_call(
        body, name=name, grid=(bsz, G // SCAN_G), in_specs=in_specs, out_specs=out_specs, out_shape=out_shape,
        compiler_params=_cp(48, 2),
    )(*ins)
    out = res[0].reshape(s.shape)
    return (out, res[1].reshape(2, bsz, gw), res[2].reshape(2, bsz, gw)) if with_da else out


def s5_out_fwd(y1, hp, wo, name):
    bsz, _, nc, _ = y1.shape

    def body(y1_ref, hp_ref, wo_ref, y_ref):
        y_ref[...] = (y1_ref[...] + _dot(hp_ref[0].astype(BF16), wo_ref[0])
                      + _dot(hp_ref[1].astype(BF16), wo_ref[1]))

    return pl.pallas_call(
        body, name=name, grid=(G, bsz),
        in_specs=[_gb((nc, QC)), _gs(nc), _gw((P2, QC))],
        out_specs=_gb((nc, QC)),
        out_shape=jax.ShapeDtypeStruct(y1.shape, F32),
        compiler_params=_cp(32, 2),
    )(y1, hp, wo)


def _acc_init(b, *refs):
    @pl.when(b == 0)
    def _():
        for r in refs:
            r[...] = jnp.zeros_like(r)


def s5_out_bwd(dyg, ug, hp, wo, name):
    bsz, _, nc, _ = dyg.shape

    def body(dy_ref, u_ref, hp_ref, wo_ref, dhp_ref, dwo_ref, dkt_ref):
        _acc_init(pl.program_id(1), dwo_ref, dkt_ref)
        dyb = dy_ref[...]
        for d in range(2):
            dhp_ref[d] = _dot_nt(dyb, wo_ref[d])
            dwo_ref[d] += _dot_tn(hp_ref[d].astype(BF16), dyb)
        dkt_ref[...] += _dot_tn(u_ref[...], dyb)

    return pl.pallas_call(
        body, name=name, grid=(G, bsz),
        in_specs=[_gb((nc, QC)), _gb((nc, QC)), _gs(nc), _gw((P2, QC))],
        out_specs=[_gs(nc), _gw((P2, QC)), pl.BlockSpec((None, QC, QC), lambda g, b: (g, 0, 0))],
        out_shape=[jax.ShapeDtypeStruct(hp.shape, F32), jax.ShapeDtypeStruct((2, G, P2, QC), F32),
                   jax.ShapeDtypeStruct((G, QC, QC), F32)],
        compiler_params=_cp(32, 2),
    )(dyg, ug, hp, wo)


def s5_chunk_bwd(dyg, ug, ds, kt, ws, name):
    bsz, _, nc, _ = dyg.shape

    def body(dy_ref, u_ref, ds_ref, kt_ref, ws_ref, du_ref, dws_ref):
        _acc_init(pl.program_id(1), dws_ref)
        dyb = dy_ref[...]
        du = _dot_nt(dyb, kt_ref[0]) + _dot_nt(dyb, kt_ref[1])
        for d in range(2):
            dsb = ds_ref[d].astype(BF16)
            du += _dot_nt(dsb, ws_ref[d])
            dws_ref[d] += _dot_tn(u_ref[...], dsb)
        du_ref[...] = du

    return pl.pallas_call(
        body, name=name, grid=(G, bsz),
        in_specs=[_gb((nc, QC)), _gb((nc, QC)), _gs(nc), _gw((QC, QC)), _gw((QC, P2))],
        out_specs=[_gb((nc, QC)), _gw((QC, P2))],
        out_shape=[jax.ShapeDtypeStruct(dyg.shape, F32), jax.ShapeDtypeStruct((2, G, QC, P2), F32)],
        compiler_params=_cp(32, 2),
    )(dyg, ug, ds, kt, ws)


def local_step(x, ctx, target, mods, norm_g, final_g, even, odd, late=None, scatter=None):
    bsz, seq, _ = x.shape
    lc = ctx.shape[1]
    t_len = lc + seq
    ncc = lc // Q
    cos, sin = rope_tables(lc, seq)
    h = jnp.concatenate([ctx, x], axis=1)
    ssm_stacked = [jnp.stack([even[0]["ssm"][k], even[1]["ssm"][k]]) for k in range(7)]
    mats_all, mats_vjp = jax.vjp(jax.vmap(s5_mats), *ssm_stacked)
    d_mats = [None, None]
    saved = []
    for i in range(DEPTH):
        j = i // 2
        g = norm_g[i].reshape(1, D)
        if i % 2 == 0:
            w = even[j]
            a, q, kv, g_attn, u, g_ssm = norm_in(h, g, mods[i], w["w_in"], EVEN_SPLITS, (2, 4), f"even_in{j}")
            if i == 0 and late is not None:
                o_attn, lse, *gathered = attn_fwd(q, kv, cos, sin, w["sink"], lc, f"attn_fwd{j}", carry=late[0])
                late[1](gathered)
            else:
                o_attn, lse = attn_fwd(q, kv, cos, sin, w["sink"], lc, f"attn_fwd{j}")
            kt, ws, wo, a1, a2 = (m[j] for m in mats_all)
            kt, ws, wo = kt.astype(BF16), ws.astype(BF16), wo.astype(BF16)
            d_skip = w["ssm"][7].reshape(1, SSM_W)
            ug = to_groups(u, f"u_to_groups{j}")
            y1, s = s5_chunk_fwd(ug, kt, ws, f"s5_chunk_fwd{j}")
            hp = s5_scan(s, a1, a2, ncc, False, f"s5_scan_fwd{j}")
            y_scan = from_groups(s5_out_fwd(y1, hp, wo, f"s5_out_fwd{j}"), f"y_from_groups{j}")
            h_new, mix, yout = even_out(h, mods[i], o_attn, g_attn, y_scan, u, d_skip, g_ssm, w["glu_w"], w["glu_b"],
                                        w["w_out"], f"even_out{j}")
            saved.append(dict(h=h, a=a, q=q, kv=kv, g_attn=g_attn, g_ssm=g_ssm, o_attn=o_attn, lse=lse, ug=ug, u=u,
                              hp=hp, y_scan=y_scan, mix=mix, yout=yout, mats=(kt, ws, wo, a1, a2), d_skip=d_skip))
        else:
            w = odd[j]
            a, u, gate = norm_in(h, g, mods[i], w["w_in"], ODD_SPLITS, (1,), f"odd_in{j}")
            pm = pool_band(u, lc, False, f"pool_band_fwd{j}")
            h_new, mix, yout = pool_out(h, mods[i], pm, gate, w["pool_w"], w["pool_scale"], w["w_out"], f"pool_out{j}")
            saved.append(dict(h=h, a=a, pm=pm, gate=gate, mix=mix, yout=yout))
        h = h_new

    dh, loss_acc, dfg = loss_head(h, final_g.reshape(1, D), target, "loss_head")
    grads = dict(final_g=dfg[0], norm_g=[None] * DEPTH, even=[None, None], odd=[None, None])
    dmods = [None] * DEPTH
    rows = bsz * t_len
    flat = lambda v: v.reshape(rows, v.shape[-1])
    for i in reversed(range(DEPTH)):
        j = i // 2
        sv = saved[i]
        g = norm_g[i].reshape(1, D)
        if i % 2 == 0:
            w = even[j]
            kt, ws, wo, a1, a2 = sv["mats"]
            (d_oattn, d_gattn, d_gssm, d_yssm, dyout, zz, dsg, dgate, dglu_b) = even_out_bwd(
                dh, mods[i], sv["o_attn"], sv["g_attn"], sv["y_scan"], sv["u"], sv["d_skip"], sv["g_ssm"], w["glu_w"],
                w["glu_b"], w["w_out"], sv["yout"], f"even_out_bwd{j}")
            g_w_out = matmul_tn(flat(sv["mix"]), flat(dyout), D, D, f"even_w_out_grad{j}")
            g_glu_w = matmul_tn(flat(zz), flat(dsg), SSM_W, SSM_W, f"glu_w_grad{j}")
            carry = scatter(grads) if i == 0 and scatter is not None else ()
            dq, dkv, dsink, *grads["landed"] = attn_bwd(sv["q"], sv["kv"], sv["o_attn"], sv["lse"], d_oattn, cos, sin,
                                                        w["sink"], lc, f"attn_bwd{j}", carry=carry)
            dyg = to_groups(d_yssm, f"dy_to_groups{j}")
            dhp, dwo, dkt = s5_out_bwd(dyg, sv["ug"], sv["hp"], wo, f"s5_out_bwd{j}")
            ds, da1, da2 = s5_scan(dhp, a1, -a2, ncc, True, f"s5_scan_bwd{j}", hp=sv["hp"])
            dug, dws = s5_chunk_bwd(dyg, sv["ug"], ds, kt, ws, f"s5_chunk_bwd{j}")
            dkt2 = jnp.stack([dkt, dkt])
            da1 = da1.sum(axis=1).reshape(2, 1, G * P2)
            da2 = da2.sum(axis=1).reshape(2, 1, G * P2)
            d_mats[j] = (dkt2, dws, dwo, da1, da2)
            dparts = [dq, dkv, d_gattn, from_groups(dug, f"du_from_groups{j}"), d_gssm]
            dh, dz, dmod, dg = norm_in_bwd(dparts, dh, sv["h"], g, mods[i], w["w_in"], f"even_in_bwd{j}",
                                           skip=(3, d_yssm, sv["d_skip"]))
            g_w_in = matmul_tn(flat(sv["a"]), flat(dz), D, dz.shape[-1], f"even_w_in_grad{j}")
            grads["even"][j] = dict(w_in=g_w_in, w_out=g_w_out, sink=dsink[0], d_skip=dglu_b[1], glu_w=g_glu_w,
                                    glu_b=dglu_b[0])
        else:
            w = odd[j]
            dpm, dgt, dyout, dpp, dgate, dps = pool_out_bwd(dh, mods[i], sv["pm"], sv["gate"], w["pool_w"],
                                                            w["pool_scale"], w["w_out"], sv["yout"],
                                                            f"pool_out_bwd{j}")
            g_w_out = matmul_tn(flat(sv["mix"]), flat(dyout), D, D, f"odd_w_out_grad{j}")
            g_pool_w = jnp.stack([matmul_tn(flat(sv["pm"]), flat(dpp), POOL_G, POOL_G, f"pool_w_grad{j}_{gi}",
                                            a_col=gi, b_col=gi) for gi in range(4)])
            du = pool_band(dpm, lc, True, f"pool_band_bwd{j}")
            dh, dz, dmod, dg = norm_in_bwd([du, dgt], dh, sv["h"], g, mods[i], w["w_in"], f"odd_in_bwd{j}")
            g_w_in = matmul_tn(flat(sv["a"]), flat(dz), D, dz.shape[-1], f"odd_w_in_grad{j}")
            grads["odd"][j] = dict(w_in=g_w_in, w_out=g_w_out, pool_w=g_pool_w, pool_scale=dps[0])
        grads["norm_g"][i] = dg[0]
        dmods[i] = jnp.concatenate([dmod[:, :, 0:2, :], dgate[:, :, 0:1, :]], axis=2)
    g_ssm = mats_vjp(tuple(jnp.stack([d_mats[0][k], d_mats[1][k]]) for k in range(5)))
    for j in range(2):
        grads["even"][j]["ssm"] = tuple(gk[j] for gk in g_ssm) + (grads["even"][j].pop("d_skip"),)
    return loss_acc[0, 0], dh[:, lc:, :], dmods, grads


N_DEV = 8
HBM_SPEC = pl.BlockSpec(memory_space=pltpu.HBM)


def allgather8(x_shard, name):
    m_per, n = x_shard.shape

    def body(x_ref, out_ref, send_sems, recv_sems, local_sem):
        x, y, c = lax.axis_index("x"), lax.axis_index("y"), lax.axis_index("c")
        me, sibling = (x, y, c), (x, y, 1 - c)
        chips = [(1 - x, y), (x, 1 - y), (1 - x, 1 - y)]

        def rows(px, py, pc):
            return out_ref.at[pl.ds((4 * px + 2 * py + pc) * m_per, m_per), :]

        def copy(k, block, to, src=None):
            return pltpu.make_async_remote_copy(
                src_ref=rows(*block) if src is None else src, dst_ref=rows(*block),
                send_sem=send_sems.at[k], recv_sem=recv_sems.at[k], device_id=to, device_id_type=MESH)

        mine = pltpu.make_async_copy(x_ref, rows(*me), local_sem)
        mine.start()
        first = [copy(0, me, sibling, src=x_ref)]
        first += [copy(1 + j, me, (*chip, c), src=x_ref) for j, chip in enumerate(chips)]
        for cp in first:
            cp.start()
        passed = [copy(4 + j, (*chip, c), sibling) for j, chip in enumerate(chips)]
        for j, chip in enumerate(chips):
            copy(1 + j, (*chip, c), me).wait_recv()
            passed[j].start()
        copy(0, sibling, me).wait_recv()
        for j, chip in enumerate(chips):
            copy(4 + j, (*chip, 1 - c), me).wait_recv()
        for cp in first + passed:
            cp.wait_send()
        mine.wait()

    return pl.pallas_call(
        body, name=name,
        out_shape=jax.ShapeDtypeStruct((N_DEV * m_per, n), x_shard.dtype),
        in_specs=[pl.BlockSpec(memory_space=pltpu.VMEM)],
        out_specs=pl.BlockSpec(memory_space=pltpu.VMEM),
        scratch_shapes=[pltpu.SemaphoreType.DMA((7,)), pltpu.SemaphoreType.DMA((7,)), pltpu.SemaphoreType.DMA],
        compiler_params=_cp(56),
    )(x_shard)


def xy_exchange(srcs, scatter, name):
    n = len(srcs)

    def body(*refs):
        start, wait = _xy_copies(refs[:n], refs[n:2 * n], *refs[2 * n:], scatter)
        start()
        wait()

    return pl.pallas_call(
        body, name=name, out_shape=_xy_out_shapes(srcs, scatter),
        in_specs=[HBM_SPEC] * n, out_specs=[HBM_SPEC] * n, scratch_shapes=_xy_sems(n),
    )(*srcs)


def _xy_out_shapes(srcs, scatter):
    return [jax.ShapeDtypeStruct((4,) + (tuple(s.shape[1:]) if scatter else tuple(s.shape)), s.dtype) for s in srcs]


def _xy_sems(n):
    return [pltpu.SemaphoreType.DMA((3 * n,)), pltpu.SemaphoreType.DMA((3 * n,)), pltpu.SemaphoreType.DMA((n,))]


def _xy_copies(src_refs, out_refs, send_sems, recv_sems, local_sems, scatter):
    n = len(src_refs)

    def parts():
        x, y, c = lax.axis_index("x"), lax.axis_index("y"), lax.axis_index("c")
        my = 2 * x + y
        peers = [(1 - x, y), (x, 1 - y), (1 - x, 1 - y)]

        def piece(i, pos):
            return src_refs[i].at[pos] if scatter else src_refs[i]

        def copy(i, k, src_pos, dst_pos):
            px, py = peers[k]
            return pltpu.make_async_remote_copy(
                src_ref=piece(i, src_pos), dst_ref=out_refs[i].at[dst_pos], send_sem=send_sems.at[3 * i + k],
                recv_sem=recv_sems.at[3 * i + k], device_id=(px, py, c), device_id_type=MESH)

        local = [pltpu.make_async_copy(piece(i, my), out_refs[i].at[my], local_sems.at[i]) for i in range(n)]
        sends = [copy(i, k, 2 * px + py, my) for i in range(n) for k, (px, py) in enumerate(peers)]
        lands = [copy(i, k, my, 2 * px + py) for i in range(n) for k, (px, py) in enumerate(peers)]
        return local, sends, lands

    def start():
        local, sends, _ = parts()
        for cp in local + sends:
            cp.start()

    def wait():
        local, sends, lands = parts()
        for cp in lands:
            cp.wait_recv()
        for cp in sends:
            cp.wait_send()
        for cp in local:
            cp.wait()

    return start, wait


def sibling_exchange(srcs, name):
    n = len(srcs)

    def body(*refs):
        src_refs, out_refs = refs[:n], refs[n:2 * n]
        send_sems, recv_sems = refs[2 * n:]
        peer = (lax.axis_index("x"), lax.axis_index("y"), 1 - lax.axis_index("c"))
        cps = [pltpu.make_async_remote_copy(src_ref=src_refs[i], dst_ref=out_refs[i], send_sem=send_sems.at[i],
                                            recv_sem=recv_sems.at[i], device_id=peer, device_id_type=MESH)
               for i in range(n)]
        for cp in cps:
            cp.start()
        for cp in cps:
            cp.wait()

    return pl.pallas_call(
        body, name=name, out_shape=[jax.ShapeDtypeStruct(s.shape, s.dtype) for s in srcs],
        in_specs=[HBM_SPEC] * n, out_specs=[HBM_SPEC] * n,
        scratch_shapes=[pltpu.SemaphoreType.DMA((n,)), pltpu.SemaphoreType.DMA((n,))],
    )(*srcs)


def _row_tile(rows, bytes_per_row, limit):
    best = None
    for tr in range(8, rows + 1, 8):
        if rows % tr == 0 and tr * bytes_per_row <= limit:
            best = tr
    return best if best is not None else rows


def sum_slots(x, name):
    n, rows, cols = x.shape
    tr = _row_tile(rows, n * cols * 4, 4 * MB)

    def body(x_ref, o_ref):
        acc = x_ref[0].astype(F32)
        for k in range(1, n):
            acc = acc + x_ref[k].astype(F32)
        o_ref[...] = acc

    return pl.pallas_call(
        body, name=name, grid=(rows // tr,),
        in_specs=[pl.BlockSpec((n, tr, cols), lambda r: (0, r, 0))],
        out_specs=pl.BlockSpec((tr, cols), lambda r: (r, 0)),
        out_shape=jax.ShapeDtypeStruct((rows, cols), F32),
        compiler_params=_cp(32, 1),
    )(x)


ADA_COLS = 3 * D // 4
C_ROWS = 8


def ada_fwd(c_all, ada_w, ada_b_cols, name):
    nrow = c_all.shape[0]

    def body(c_ref, w_ref, b_ref, o_ref):
        s, _ = _silu_and_grad(c_ref[...])
        o_ref[...] = _dot(s.astype(BF16), w_ref[...].astype(BF16)) + b_ref[...]

    return pl.pallas_call(
        body, name=name, grid=(DEPTH,),
        in_specs=[pl.BlockSpec((nrow, D), lambda i: (0, 0)), pl.BlockSpec((None, D, ADA_COLS), lambda i: (i, 0, 0)),
                  pl.BlockSpec((None, 1, ADA_COLS), lambda i: (i, 0, 0))],
        out_specs=pl.BlockSpec((None, nrow, ADA_COLS), lambda i: (i, 0, 0)),
        out_shape=jax.ShapeDtypeStruct((DEPTH, nrow, ADA_COLS), F32),
        compiler_params=_cp(32, 1),
    )(c_all, ada_w, ada_b_cols)


def ada_bwd(c_all, d_cols, ada_w, name):
    nrow = c_all.shape[0]

    def body(c_ref, d_ref, w_ref, gw_ref, ds_ref):
        @pl.when(pl.program_id(0) == 0)
        def _():
            ds_ref[...] = jnp.zeros_like(ds_ref)
        s, _ = _silu_and_grad(c_ref[...])
        dl = d_ref[...]
        gw_ref[...] = _dot_tn(s.astype(BF16), dl.astype(BF16))
        rid = lax.broadcasted_iota(jnp.int32, (nrow, 1), 0) % C_ROWS
        dctx = jnp.where((rid == 2) | (rid == 3), dl, 0.0).astype(BF16)
        ds_ref[0:1, :] += _rowsum(_dot_nt(dctx, w_ref[...].astype(BF16)))

    return pl.pallas_call(
        body, name=name, grid=(DEPTH,),
        in_specs=[pl.BlockSpec((nrow, D), lambda i: (0, 0)), pl.BlockSpec((None, nrow, ADA_COLS), lambda i: (i, 0, 0)),
                  pl.BlockSpec((None, D, ADA_COLS), lambda i: (i, 0, 0))],
        out_specs=[pl.BlockSpec((None, D, ADA_COLS), lambda i: (i, 0, 0)), pl.BlockSpec((8, D), lambda i: (0, 0))],
        out_shape=[jax.ShapeDtypeStruct((DEPTH, D, ADA_COLS), F32), jax.ShapeDtypeStruct((8, D), F32)],
        compiler_params=_cp(32, 1),
    )(c_all, d_cols, ada_w)


def ada_bias_grad(d_all, name):
    nrow = d_all.shape[1]

    def body(d_ref, o_ref):
        o_ref[...] = jnp.broadcast_to(_rowsum(d_ref[...]), o_ref.shape)

    return pl.pallas_call(
        body, name=name, grid=(DEPTH,),
        in_specs=[pl.BlockSpec((None, nrow, 3 * D), lambda i: (i, 0, 0))],
        out_specs=pl.BlockSpec((None, 8, 3 * D), lambda i: (i, 0, 0)),
        out_shape=jax.ShapeDtypeStruct((DEPTH, 8, 3 * D), F32),
        compiler_params=_cp(32, 1),
    )(d_all)


def silu_chain(ds, c, name):
    def body(ds_ref, c_ref, o_ref):
        _, dsl = _silu_and_grad(c_ref[...])
        o_ref[...] = ds_ref[...] * dsl

    return pl.pallas_call(body, name=name, out_shape=jax.ShapeDtypeStruct(ds.shape, F32))(ds, c)


def _flat_cols(shape):
    size = int(np.prod(shape))
    if shape[-1] >= 128:
        return shape[-1]
    for cols in (1024, 128):
        if size % cols == 0:
            return cols
    return shape[-1]


def adamw(w, m, v, grads, name):
    shape = w.shape
    cols = _flat_cols(shape)
    as2d = lambda a: a.reshape(-1, cols)
    rows = w.size // cols
    tr = _row_tile(rows, cols * 4, MB)
    k = len(grads)

    def body(*refs):
        w_ref, m_ref, v_ref = refs[:3]
        g_refs = refs[3:3 + k]
        g_out, d_out, m_out, v_out = refs[3 + k:]
        g = g_refs[0][...]
        for r in g_refs[1:]:
            g = g + r[...]
        g_out[...] = g
        mn = ADAM_B1 * m_ref[...] + (1.0 - ADAM_B1) * g
        vn = ADAM_B2 * v_ref[...] + (1.0 - ADAM_B2) * (g * g)
        m_out[...] = mn
        v_out[...] = vn
        m_hat = mn / (1.0 - ADAM_B1 ** ADAM_STEP)
        v_hat = vn / (1.0 - ADAM_B2 ** ADAM_STEP)
        d_out[...] = -ADAM_LR * (m_hat / (jnp.sqrt(v_hat) + ADAM_EPS) + ADAM_WD * w_ref[...])

    spec = pl.BlockSpec((tr, cols), lambda r: (r, 0))
    outs = pl.pallas_call(
        body, name=name, grid=(rows // tr,),
        in_specs=[spec] * (3 + k), out_specs=[spec] * 4,
        out_shape=[jax.ShapeDtypeStruct((rows, cols), F32)] * 4,
        compiler_params=_cp(32, 1),
    )(as2d(w), as2d(m), as2d(v), *[as2d(g) for g in grads])
    return tuple(o.reshape(shape) for o in outs)


BIG = (("even_w_in", (2, D, 576), 2), ("even_w_out", (2, 256, D), 1), ("glu_w", (2, 128, SSM_W), 1),
       ("odd_w_in", (2, D, 512), 2), ("odd_w_out", (2, 256, D), 1), ("pool_w", (2, 4, 64, POOL_G), 2))


def _full_shape(shard, axis):
    return tuple(4 * s if a == axis else s for a, s in enumerate(shard))


def _to_shards(full, shard, axis):
    return jnp.moveaxis(full.reshape(shard[:axis] + (4,) + shard[axis:]), axis, 0)


def _from_shards(stacked, shard, axis):
    return jnp.moveaxis(stacked, 0, axis).reshape(_full_shape(shard, axis))


GRAD_KEY = dict(even_w_in=("even", "w_in"), even_w_out=("even", "w_out"), glu_w=("even", "glu_w"),
                odd_w_in=("odd", "w_in"), odd_w_out=("odd", "w_out"), pool_w=("odd", "pool_w"))
RIDE = tuple((n, 1) for n, _, _ in BIG[:3]) + tuple((n, j) for n, _, _ in BIG[3:] for j in range(2))
LAST = tuple((n, 0) for n, _, _ in BIG[:3])


def _grad_piece(grads, name, j):
    kind, key = GRAD_KEY[name]
    shard, axis = next((s, a) for n, s, a in BIG if n == name)
    return _to_shards(grads[kind][j][key], shard[1:], axis - 1).astype(BF16)


SMALL = (("ds_ctx", (D,)), ("norm_g", (DEPTH, D)), ("final_g", (D,)), ("attn_sink", (2, N_HEADS)),
         ("ssm_a_re", (2, 2, G, P)), ("ssm_a_im", (2, 2, G, P)), ("ssm_log_dt", (2, 2, G)),
         ("ssm_b_re", (2, 2, G, P, C)), ("ssm_b_im", (2, 2, G, P, C)), ("ssm_c_re", (2, 2, G, C, P)),
         ("ssm_c_im", (2, 2, G, C, P)), ("ssm_d", (2, SSM_W)), ("glu_b", (2, SSM_W)), ("pool_scale", (2, D)))
SMALL_PAD = 8 * 128


def pack_small(vals):
    flat = jnp.concatenate([vals[n].reshape(-1) for n, _ in SMALL])
    pad = (-flat.shape[0]) % SMALL_PAD
    return jnp.pad(flat, (0, pad)).reshape(-1, 128)


def unpack_small(packed):
    flat, out, off = packed.reshape(-1), {}, 0
    for n, shape in SMALL:
        size = int(np.prod(shape))
        out[n] = flat[off:off + size].reshape(shape)
        off += size
    return out


WEIGHT_NAMES = ('c_ctx', 'ada_w', 'ada_b', 'norm_g', 'even_w_in', 'even_w_out', 'attn_sink', 'ssm_a_re', 'ssm_a_im',
                'ssm_log_dt', 'ssm_b_re', 'ssm_b_im', 'ssm_c_re', 'ssm_c_im', 'ssm_d', 'glu_w', 'glu_b', 'odd_w_in',
                'odd_w_out', 'pool_w', 'pool_scale', 'final_g')
SSM_NAMES = ('ssm_a_re', 'ssm_a_im', 'ssm_log_dt', 'ssm_b_re', 'ssm_b_im', 'ssm_c_re', 'ssm_c_im', 'ssm_d')


def kernel(x, c, ctx, c_ctx, ada_w, ada_b, norm_g, even_w_in, even_w_out, attn_sink, ssm_a_re, ssm_a_im, ssm_log_dt, ssm_b_re, ssm_b_im, ssm_c_re, ssm_c_im, ssm_d, glu_w, glu_b, odd_w_in, odd_w_out, pool_w, pool_scale, final_g, loss_target, m_c_ctx, m_ada_w, m_ada_b, m_norm_g, m_even_w_in, m_even_w_out, m_attn_sink, m_ssm_a_re, m_ssm_a_im, m_ssm_log_dt, m_ssm_b_re, m_ssm_b_im, m_ssm_c_re, m_ssm_c_im, m_ssm_d, m_glu_w, m_glu_b, m_odd_w_in, m_odd_w_out, m_pool_w, m_pool_scale, m_final_g, v_c_ctx, v_ada_w, v_ada_b, v_norm_g, v_even_w_in, v_even_w_out, v_attn_sink, v_ssm_a_re, v_ssm_a_im, v_ssm_log_dt, v_ssm_b_re, v_ssm_b_im, v_ssm_c_re, v_ssm_c_im, v_ssm_d, v_glu_w, v_glu_b, v_odd_w_in, v_odd_w_out, v_pool_w, v_pool_scale, v_final_g):
    env = dict(locals())
    weights = {n: env[n] for n in WEIGHT_NAMES}
    bsz = x.shape[0]
    ax, ay, ac = lax.axis_index("x"), lax.axis_index("y"), lax.axis_index("c")
    pos = 2 * ax + ay
    dev = 2 * pos + ac

    c_rows = jnp.concatenate([c, c_ctx.reshape(1, D), c_ctx.reshape(1, D), jnp.zeros((C_ROWS - bsz - 2, D), F32)])
    c_all = allgather8(c_rows, "gather_c")
    ada_b_cols = lax.dynamic_slice(ada_b, (0, pos * ADA_COLS), (DEPTH, ADA_COLS)).reshape(DEPTH, 1, ADA_COLS)
    mod_cols = ada_fwd(c_all, ada_w, ada_b_cols, "ada_fwd")
    nrow = N_DEV * C_ROWS
    misc = jnp.concatenate([mod_cols.reshape(DEPTH * nrow, ADA_COLS),
                            jnp.pad(pool_scale, ((0, 6), (0, ADA_COLS - pool_scale.shape[1])))])
    misc_all = allgather8(misc, "gather_mod").reshape(4, 2, DEPTH * nrow + 8, ADA_COLS)[:, 0]
    mod_full = misc_all[:, :DEPTH * nrow].reshape(4, DEPTH, nrow, ADA_COLS).transpose(1, 2, 0, 3)
    mod_mine = lax.dynamic_slice(mod_full.reshape(DEPTH, nrow, 3 * D), (0, dev * C_ROWS, 0), (DEPTH, C_ROWS, 3 * D))
    mods = []
    for i in range(DEPTH):
        lat = mod_mine[i, :bsz].reshape(bsz, 1, 3, D)
        con = jnp.broadcast_to(mod_mine[i, bsz].reshape(1, 1, 3, D), (bsz, 1, 3, D))
        mods.append(jnp.pad(jnp.concatenate([con, lat], axis=1), ((0, 0), (0, 0), (0, 5), (0, 0))))
    pool_scale_full = misc_all[:, DEPTH * nrow:DEPTH * nrow + 2, :pool_scale.shape[1]].transpose(1, 0, 2).reshape(2, D)

    first, shard0, axis0 = BIG[0]
    w_in = weights[first].astype(BF16)
    even = [dict(sink=attn_sink[j], ssm=tuple(weights[n][j] for n in SSM_NAMES), glu_b=glu_b[j].reshape(1, SSM_W))
            for j in range(2)]
    even[0]["w_in"] = _from_shards(xy_exchange([w_in[0]], False, "gather_w_in")[0], shard0[1:], axis0 - 1)
    odd = [dict(pool_scale=pool_scale_full[j].reshape(1, D)) for j in range(2)]

    def fill(gathered):
        even[1]["w_in"] = _from_shards(gathered[0], shard0[1:], axis0 - 1)
        full = {n: _from_shards(g, shard, axis) for (n, shard, axis), g in zip(BIG[1:], gathered[1:])}
        for j in range(2):
            even[j].update(w_out=full["even_w_out"][j], glu_w=full["glu_w"][j])
            odd[j].update(w_in=full["odd_w_in"][j], w_out=full["odd_w_out"][j], pool_w=full["pool_w"][j])

    late = ([w_in[1]] + [weights[n].astype(BF16) for n, _, _ in BIG[1:]], fill)
    loss_local, grad_x, dmods, grads = local_step(
        x, ctx, loss_target, mods, norm_g, final_g, even, odd, late,
        scatter=lambda grads: [_grad_piece(grads, n, j) for n, j in RIDE])
    loss = lax.psum(loss_local, ("x", "y", "c"))

    d_rows = jnp.stack([jnp.concatenate([dm[:, 1].reshape(bsz, 3 * D), dm[:, 0].reshape(bsz, 3 * D),
                                         jnp.zeros((C_ROWS - 2 * bsz, 3 * D), F32)]) for dm in dmods])
    d_all = allgather8(d_rows.reshape(DEPTH * C_ROWS, 3 * D), "gather_dmod")
    d_all = d_all.reshape(N_DEV, DEPTH, C_ROWS, 3 * D).transpose(1, 0, 2, 3).reshape(DEPTH, nrow, 3 * D)
    d_cols = lax.dynamic_slice(d_all, (0, 0, pos * ADA_COLS), (DEPTH, nrow, ADA_COLS))
    g_ada_w, ds_ctx = ada_bwd(c_all, d_cols, ada_w, "ada_bwd")
    g_ada_b = ada_bias_grad(d_all, "ada_bias_grad")[:, 0]

    small = dict(ds_ctx=ds_ctx[0] * (ac == 0).astype(F32), norm_g=jnp.stack(grads["norm_g"]), final_g=grads["final_g"],
                 attn_sink=jnp.stack([grads["even"][j]["sink"] for j in range(2)]),
                 glu_b=jnp.stack([grads["even"][j]["glu_b"] for j in range(2)]),
                 pool_scale=jnp.stack([grads["odd"][j]["pool_scale"] for j in range(2)]))
    for k, n in enumerate(SSM_NAMES):
        small[n] = jnp.stack([grads["even"][j]["ssm"][k] for j in range(2)])
    packed = pack_small(small)
    small_sum = sum_slots(allgather8(packed, "gather_small").reshape(N_DEV, packed.shape[0], 128), "sum_small")
    g_small = unpack_small(small_sum)
    g_small["c_ctx"] = silu_chain(g_small.pop("ds_ctx").reshape(1, D), c_ctx.reshape(1, D), "c_ctx_grad").reshape(D)
    g_small["ada_b"] = g_ada_b
    g_small["pool_scale"] = lax.dynamic_slice(g_small["pool_scale"], (0, pos * 256), (2, 256))

    landed = dict(zip(RIDE, grads["landed"]))
    landed.update(zip(LAST, xy_exchange([_grad_piece(grads, n, j) for n, j in LAST], True, "scatter_grads")))
    mine4 = [jnp.stack([sum_slots(landed[n, j].reshape(4, -1, shard[-1]), f"sum_positions_{n}{j}").reshape(shard[1:])
                        for j in range(2)]) for n, shard, _ in BIG]
    other4 = sibling_exchange(mine4, "swap_cores")
    g_mine = dict(zip([n for n, _, _ in BIG], mine4))
    g_other = dict(zip([n for n, _, _ in BIG], other4))

    results = {}
    for n in WEIGHT_NAMES:
        if n in g_mine:
            gs = [g_mine[n], g_other[n]]
        elif n == "ada_w":
            gs = [g_ada_w]
        else:
            gs = [g_small[n]]
        results[n] = adamw(weights[n], env["m_" + n], env["v_" + n], gs, "adamw_" + n)
    outs = [loss, grad_x]
    for k in range(4):
        outs += [results[n][k] for n in WEIGHT_NAMES]
    return tuple(outs)
```

```python
import functools

import numpy as np
import jax
import jax.numpy as jnp
from jax import lax
from jax.experimental import pallas as pl
from jax.experimental.pallas import tpu as pltpu

F32 = jnp.float32
BF16 = jnp.bfloat16
MESH = pl.DeviceIdType.MESH

D = 1024
DEPTH = 4
EPS = 1e-6
NEG_INF = -1e30
GRID_W = 64
ROPE_BASE = 10000.0
ROPE_FREQS = 16
HEAD_DIM = 64
N_HEADS = 8
N_KV = 2
GROUP = 4
ATTN_W = N_HEADS * HEAD_DIM
KV_W = N_KV * HEAD_DIM
WINDOW = 128
AB = 128
SSM_W = 512
G = 32
C = 16
P = 64
Q = 16
QC = Q * C
P2 = 2 * P
SCAN_G = 16
POOL_R = (1, 2, 4, 8)
POOL_G = 256
HALO = 8
TM = 256
EVEN_SPLITS = (512, 256, 512, 512, 512)
ODD_SPLITS = (1024, 1024)

ADAM_LR = 0.001
ADAM_B1 = 0.9
ADAM_B2 = 0.999
ADAM_EPS = 1e-08
ADAM_WD = 0.01
ADAM_STEP = 10

MB = 1024 * 1024


def _cp(vmem_mb=48, n_axes=0):
    kw = dict(vmem_limit_bytes=vmem_mb * MB)
    if n_axes:
        kw["dimension_semantics"] = ("arbitrary",) * n_axes
    return pltpu.CompilerParams(**kw)


def _sig(x):
    return 1.0 / (1.0 + jnp.exp(-x))


def _silu_and_grad(x):
    s = _sig(x)
    return x * s, s * (1.0 + x * (1.0 - s))


_GELU_C = 0.7978845608028654
_GELU_A = 0.044715


def _gelu_and_grad(x):
    th = jnp.tanh(_GELU_C * (x + _GELU_A * x * x * x))
    val = 0.5 * x * (1.0 + th)
    grad = 0.5 * (1.0 + th) + 0.5 * x * (1.0 - th * th) * _GELU_C * (1.0 + 3.0 * _GELU_A * x * x)
    return val, grad


def _rms(h):
    r = lax.rsqrt(jnp.mean(h * h, axis=-1, keepdims=True) + EPS)
    return h * r, r


def _dot(a, b):
    return jnp.dot(a, b, preferred_element_type=F32)


def _dot_nt(a, b):
    return lax.dot_general(a, b, (((1,), (1,)), ((), ())), preferred_element_type=F32)


def _dot_tn(a, b):
    return lax.dot_general(a, b, (((0,), (0,)), ((), ())), preferred_element_type=F32)


def _rowsum(x):
    return jnp.sum(x, axis=0, keepdims=True)


def _seg(t):
    return jnp.minimum(t, 1)


def _row_spec(n):
    return pl.BlockSpec((None, TM, n), lambda b, t: (b, t, 0))


def _const_spec(shape):
    nd = len(shape)
    return pl.BlockSpec(shape, lambda b, t: (0,) * nd)


def _mod_spec():
    return pl.BlockSpec((None, None, 8, D), lambda b, t: (b, _seg(t), 0, 0))


def norm_in(h, g, mod, w, splits, gates, name):
    bsz, t_len, _ = h.shape
    n = w.shape[1]
    offs = [int(v) for v in np.cumsum((0,) + tuple(splits))]

    def body(h_ref, g_ref, mod_ref, w_ref, a_ref, *outs):
        xh, _ = _rms(h_ref[...])
        a = xh * g_ref[...] * (1.0 + mod_ref[1:2, :]) + mod_ref[0:1, :]
        ab = a.astype(BF16)
        a_ref[...] = ab
        z = _dot(ab, w_ref[...])
        for o, lo, hi in zip(outs, offs[:-1], offs[1:]):
            o[...] = z[:, lo:hi].astype(o.dtype)

    return pl.pallas_call(
        body, name=name, grid=(bsz, t_len // TM),
        in_specs=[_row_spec(D), _const_spec((1, D)), _mod_spec(), _const_spec((D, n))],
        out_specs=[_row_spec(D)] + [_row_spec(s) for s in splits],
        out_shape=[jax.ShapeDtypeStruct((bsz, t_len, D), BF16)]
        + [jax.ShapeDtypeStruct((bsz, t_len, s), BF16 if k in gates else F32) for k, s in enumerate(splits)],
        compiler_params=_cp(48, 2),
    )(h, g, mod, w)


def norm_in_bwd(dparts, dh_in, h, g, mod, w, name, skip=None, latent_only=False):
    bsz, t_len, _ = h.shape
    n = w.shape[1]
    k = len(dparts)
    extra = [] if skip is None else [skip[1], skip[2]]

    def body(*refs):
        parts = [r[...] for r in refs[:k]]
        if skip is not None:
            parts[skip[0]] = parts[skip[0]] + refs[k][...] * refs[k + 1][...]
        parts = [p.astype(BF16) for p in parts]
        dh_in_ref, h_ref, g_ref, mod_ref, w_ref, dh_ref, dz_ref, dmod_ref, dg_ref = refs[k + len(extra):]
        b, t = pl.program_id(0), pl.program_id(1)
        dz = jnp.concatenate(parts, axis=1)
        dz_ref[...] = dz
        da = _dot_nt(dz, w_ref[...])
        xh, r = _rms(h_ref[...])
        gg = g_ref[...]
        sc1 = 1.0 + mod_ref[1:2, :]

        @pl.when(t <= 1)
        def _():
            dmod_ref[...] = jnp.zeros_like(dmod_ref)

        @pl.when((b == 0) & (t == 0))
        def _():
            dg_ref[...] = jnp.zeros_like(dg_ref)

        dmod_ref[0:1, :] += _rowsum(da)
        dmod_ref[1:2, :] += _rowsum(da * (xh * gg))
        dg_ref[0:1, :] += _rowsum(da * sc1 * xh)
        dxh = da * gg * sc1
        dh_ref[...] = dh_in_ref[...] + r * (dxh - xh * jnp.mean(dxh * xh, axis=-1, keepdims=True))

    return pl.pallas_call(
        body, name=name, grid=(bsz, t_len // TM),
        in_specs=[_row_spec(p.shape[-1]) for p in dparts]
        + ([_row_spec(extra[0].shape[-1]), _const_spec(extra[1].shape)] if extra else [])
        + [_row_spec(D), _row_spec(D), _const_spec((1, D)), _mod_spec(), _const_spec((D, n))],
        out_specs=[pl.BlockSpec((None, TM, D), lambda b, t: (b, jnp.maximum(t - 1, 0), 0)) if latent_only
                   else _row_spec(D), _row_spec(n), _mod_spec(), _const_spec((8, D))],
        out_shape=[jax.ShapeDtypeStruct((bsz, t_len - TM if latent_only else t_len, D), F32),
                   jax.ShapeDtypeStruct((bsz, t_len, n), BF16),
                   jax.ShapeDtypeStruct((bsz, 2, 8, D), F32), jax.ShapeDtypeStruct((8, D), F32)],
        compiler_params=_cp(56, 2),
    )(*dparts, *extra, dh_in, h, g, mod, w)


def matmul_tn(a, b, m, n, name, a_col=0, b_col=0):
    rows = a.shape[0]
    tr = 512 if rows % 512 == 0 else rows
    tn = n
    for cand in (1024, 768, 512, 256, 128):
        if n > 1024 and n % cand == 0:
            tn = cand
            break
    nb = n // tn

    def body(a_ref, b_ref, o_ref):
        @pl.when(pl.program_id(1) == 0)
        def _():
            o_ref[...] = jnp.zeros_like(o_ref)
        o_ref[...] += _dot_tn(a_ref[...].astype(BF16), b_ref[...].astype(BF16))

    return pl.pallas_call(
        body, name=name, grid=(nb, rows // tr),
        in_specs=[pl.BlockSpec((tr, m), lambda j, r: (r, a_col)),
                  pl.BlockSpec((tr, tn), lambda j, r: (r, b_col * nb + j))],
        out_specs=pl.BlockSpec((m, tn), lambda j, r: (0, j)),
        out_shape=jax.ShapeDtypeStruct((m, n), F32),
        compiler_params=_cp(48, 2),
    )(a, b)


def even_out(h, mod, o_attn, g_attn, y_scan, u, d_skip, g_ssm, glu_w, glu_b, w_out, name):
    bsz, t_len, _ = h.shape

    def body(h_ref, mod_ref, oa_ref, ga_ref, ys_ref, u_ref, dk_ref, gs_ref, gw_ref, gb_ref, wo_ref,
             hn_ref, mix_ref, yo_ref):
        zz, _ = _gelu_and_grad(ys_ref[...] + u_ref[...] * dk_ref[...])
        s = _dot(zz.astype(BF16), gw_ref[...]) + gb_ref[...]
        o_ssm = zz * _sig(s)
        sa, _ = _silu_and_grad(ga_ref[...].astype(F32))
        ss, _ = _silu_and_grad(gs_ref[...].astype(F32))
        mb = jnp.concatenate([oa_ref[...] * sa, o_ssm * ss], axis=1).astype(BF16)
        mix_ref[...] = mb
        yo = _dot(mb, wo_ref[...])
        yo_ref[...] = yo.astype(BF16)
        hn_ref[...] = h_ref[...] + mod_ref[2:3, :] * yo

    return pl.pallas_call(
        body, name=name, grid=(bsz, t_len // TM),
        in_specs=[_row_spec(D), _mod_spec(), _row_spec(512), _row_spec(512), _row_spec(512), _row_spec(512),
                  _const_spec((1, 512)), _row_spec(512), _const_spec((512, 512)), _const_spec((1, 512)),
                  _const_spec((D, D))],
        out_specs=[_row_spec(D), _row_spec(D), _row_spec(D)],
        out_shape=[jax.ShapeDtypeStruct((bsz, t_len, D), F32), jax.ShapeDtypeStruct((bsz, t_len, D), BF16),
                   jax.ShapeDtypeStruct((bsz, t_len, D), BF16)],
        compiler_params=_cp(48, 2),
    )(h, mod, o_attn, g_attn, y_scan, u, d_skip, g_ssm, glu_w, glu_b, w_out)


def even_out_bwd(dh, mod, o_attn, g_attn, y_scan, u, d_skip, g_ssm, glu_w, glu_b, w_out, yout, name):
    bsz, t_len, _ = dh.shape

    def body(dh_ref, mod_ref, oa_ref, ga_ref, ys_ref, u_ref, dk_ref, gs_ref, gw_ref, gb_ref, wo_ref, yo_ref,
             doa_ref, dga_ref, dgs_ref, dys_ref, dyo_ref, zz_ref, ds_ref, dgate_ref, dgb_ref):
        b, t = pl.program_id(0), pl.program_id(1)
        dhv = dh_ref[...]

        @pl.when(t <= 1)
        def _():
            dgate_ref[...] = jnp.zeros_like(dgate_ref)

        @pl.when((b == 0) & (t == 0))
        def _():
            dgb_ref[...] = jnp.zeros_like(dgb_ref)

        dgate_ref[0:1, :] += _rowsum(dhv * yo_ref[...].astype(F32))
        dyb = (mod_ref[2:3, :] * dhv).astype(BF16)
        dyo_ref[...] = dyb
        dmix = _dot_nt(dyb, wo_ref[...])
        sa, dsa = _silu_and_grad(ga_ref[...].astype(F32))
        doa_ref[...] = dmix[:, :512] * sa
        dga_ref[...] = (dmix[:, :512] * oa_ref[...] * dsa).astype(BF16)
        uv = u_ref[...]
        zz, dzz_dy = _gelu_and_grad(ys_ref[...] + uv * dk_ref[...])
        zb = zz.astype(BF16)
        zz_ref[...] = zb
        sg = _sig(_dot(zb, gw_ref[...]) + gb_ref[...])
        ss, dss = _silu_and_grad(gs_ref[...].astype(F32))
        dm = dmix[:, 512:]
        dgs_ref[...] = (dm * (zz * sg) * dss).astype(BF16)
        do = dm * ss
        ds = do * zz * sg * (1.0 - sg)
        dsb = ds.astype(BF16)
        ds_ref[...] = dsb
        dgb_ref[0:1, :] += _rowsum(ds)
        dys = (do * sg + _dot_nt(dsb, gw_ref[...])) * dzz_dy
        dys_ref[...] = dys
        dgb_ref[1:2, :] += _rowsum(dys * uv)

    r512 = jax.ShapeDtypeStruct((bsz, t_len, 512), F32)
    b512 = jax.ShapeDtypeStruct((bsz, t_len, 512), BF16)
    return pl.pallas_call(
        body, name=name, grid=(bsz, t_len // TM),
        in_specs=[_row_spec(D), _mod_spec(), _row_spec(512), _row_spec(512), _row_spec(512), _row_spec(512),
                  _const_spec((1, 512)), _row_spec(512), _const_spec((512, 512)), _const_spec((1, 512)),
                  _const_spec((D, D)), _row_spec(D)],
        out_specs=[_row_spec(512)] * 4 + [_row_spec(D), _row_spec(512), _row_spec(512), _mod_spec(),
                                           _const_spec((8, 512))],
        out_shape=[r512, b512, b512, r512, jax.ShapeDtypeStruct((bsz, t_len, D), BF16),
                   jax.ShapeDtypeStruct((bsz, t_len, 512), BF16), jax.ShapeDtypeStruct((bsz, t_len, 512), BF16),
                   jax.ShapeDtypeStruct((bsz, 2, 8, D), F32), jax.ShapeDtypeStruct((8, 512), F32)],
        compiler_params=_cp(48, 2),
    )(dh, mod, o_attn, g_attn, y_scan, u, d_skip, g_ssm, glu_w, glu_b, w_out, yout)


def _split3_dot(band, x):
    x1 = x.astype(BF16)
    r1 = x - x1.astype(F32)
    x2 = r1.astype(BF16)
    x3 = (r1 - x2.astype(F32)).astype(BF16)
    return _dot(band, x3) + _dot(band, x2) + _dot(band, x1)


def pool_band(x, lc, transpose, name):
    bsz, t_len, _ = x.shape
    assert lc == TM
    hb = TM // HALO

    def body(xp_ref, xc_ref, xn_ref, o_ref):
        t = pl.program_id(1)
        seg_lo = jnp.where(t == 0, 0, lc)
        seg_hi = jnp.where(t == 0, lc, t_len)
        cur = xc_ref[...]
        xh = jnp.concatenate([xp_ref[...], cur, xn_ref[...]], axis=0)
        row_t = t * TM + lax.broadcasted_iota(jnp.int32, (TM, 1), 0)
        col_s = t * TM - HALO + lax.broadcasted_iota(jnp.int32, (1, TM + 2 * HALO), 1)
        row_s = t * TM - HALO + lax.broadcasted_iota(jnp.int32, (TM + 2 * HALO, 1), 0)
        s_ok = (col_s >= seg_lo) & (col_s < seg_hi)
        outs = []
        for gi, r in enumerate(POOL_R):
            band = ((jnp.abs(row_t - col_s) <= r) & s_ok).astype(BF16)
            xg = xh[:, gi * POOL_G:(gi + 1) * POOL_G]
            if transpose:
                cnt_s = jnp.minimum(row_s + r, seg_hi - 1) - jnp.maximum(row_s - r, seg_lo) + 1
                xg = xg * (1.0 / jnp.maximum(cnt_s, 1).astype(F32))
            acc = _split3_dot(band, xg)
            if not transpose:
                cnt_t = jnp.minimum(row_t + r, seg_hi - 1) - jnp.maximum(row_t - r, seg_lo) + 1
                acc = acc * (1.0 / cnt_t.astype(F32))
            outs.append(acc - cur[:, gi * POOL_G:(gi + 1) * POOL_G])
        o_ref[...] = jnp.concatenate(outs, axis=1).astype(BF16)

    return pl.pallas_call(
        body, name=name, grid=(bsz, t_len // TM),
        in_specs=[pl.BlockSpec((None, HALO, D), lambda b, t: (b, jnp.maximum(t * hb - 1, 0), 0)),
                  _row_spec(D),
                  pl.BlockSpec((None, HALO, D), lambda b, t: (b, jnp.minimum((t + 1) * hb, t_len // HALO - 1), 0))],
        out_specs=_row_spec(D),
        out_shape=jax.ShapeDtypeStruct((bsz, t_len, D), BF16),
        compiler_params=_cp(48, 2),
    )(x, x, x)


def pool_out(h, mod, pm, gate, pool_w, pool_scale, w_out, name):
    bsz, t_len, _ = h.shape

    def body(h_ref, mod_ref, pm_ref, gt_ref, pw_ref, ps_ref, wo_ref, hn_ref, mix_ref, yo_ref):
        pmv = pm_ref[...]
        ppre = jnp.concatenate([_dot(pmv[:, g * POOL_G:(g + 1) * POOL_G].astype(BF16), pw_ref[g])
                                for g in range(4)], axis=1)
        sl, _ = _silu_and_grad(gt_ref[...].astype(F32))
        mb = (ppre * ps_ref[...] * sl).astype(BF16)
        mix_ref[...] = mb
        yo = _dot(mb, wo_ref[...])
        yo_ref[...] = yo.astype(BF16)
        hn_ref[...] = h_ref[...] + mod_ref[2:3, :] * yo

    return pl.pallas_call(
        body, name=name, grid=(bsz, t_len // TM),
        in_specs=[_row_spec(D), _mod_spec(), _row_spec(D), _row_spec(D), _const_spec((4, POOL_G, POOL_G)),
                  _const_spec((1, D)), _const_spec((D, D))],
        out_specs=[_row_spec(D), _row_spec(D), _row_spec(D)],
        out_shape=[jax.ShapeDtypeStruct((bsz, t_len, D), F32), jax.ShapeDtypeStruct((bsz, t_len, D), BF16),
                   jax.ShapeDtypeStruct((bsz, t_len, D), BF16)],
        compiler_params=_cp(48, 2),
    )(h, mod, pm, gate, pool_w, pool_scale, w_out)


def pool_out_bwd(dh, mod, pm, gate, pool_w, pool_scale, w_out, yout, name):
    bsz, t_len, _ = dh.shape

    def body(dh_ref, mod_ref, pm_ref, gt_ref, pw_ref, ps_ref, wo_ref, yo_ref,
             dpm_ref, dgt_ref, dyo_ref, dpp_ref, dgate_ref, dps_ref):
        b, t = pl.program_id(0), pl.program_id(1)
        dhv = dh_ref[...]

        @pl.when(t <= 1)
        def _():
            dgate_ref[...] = jnp.zeros_like(dgate_ref)

        @pl.when((b == 0) & (t == 0))
        def _():
            dps_ref[...] = jnp.zeros_like(dps_ref)

        dgate_ref[0:1, :] += _rowsum(dhv * yo_ref[...].astype(F32))
        dyb = (mod_ref[2:3, :] * dhv).astype(BF16)
        dyo_ref[...] = dyb
        dmix = _dot_nt(dyb, wo_ref[...])
        pmv = pm_ref[...]
        ppre = jnp.concatenate([_dot(pmv[:, g * POOL_G:(g + 1) * POOL_G].astype(BF16), pw_ref[g])
                                for g in range(4)], axis=1)
        ps = ps_ref[...]
        sl, dsl = _silu_and_grad(gt_ref[...].astype(F32))
        dp = dmix * sl
        dgt_ref[...] = (dmix * (ppre * ps) * dsl).astype(BF16)
        dps_ref[0:1, :] += _rowsum(dp * ppre)
        dppb = (dp * ps).astype(BF16)
        dpp_ref[...] = dppb
        dpm_ref[...] = jnp.concatenate([_dot_nt(dppb[:, g * POOL_G:(g + 1) * POOL_G], pw_ref[g])
                                        for g in range(4)], axis=1)

    return pl.pallas_call(
        body, name=name, grid=(bsz, t_len // TM),
        in_specs=[_row_spec(D), _mod_spec(), _row_spec(D), _row_spec(D), _const_spec((4, POOL_G, POOL_G)),
                  _const_spec((1, D)), _const_spec((D, D)), _row_spec(D)],
        out_specs=[_row_spec(D), _row_spec(D), _row_spec(D), _row_spec(D), _mod_spec(), _const_spec((8, D))],
        out_shape=[jax.ShapeDtypeStruct((bsz, t_len, D), F32), jax.ShapeDtypeStruct((bsz, t_len, D), BF16),
                   jax.ShapeDtypeStruct((bsz, t_len, D), BF16), jax.ShapeDtypeStruct((bsz, t_len, D), BF16),
                   jax.ShapeDtypeStruct((bsz, 2, 8, D), F32), jax.ShapeDtypeStruct((8, D), F32)],
        compiler_params=_cp(48, 2),
    )(dh, mod, pm, gate, pool_w, pool_scale, w_out, yout)


def loss_head(h, final_g, target, name):
    bsz, t_len, _ = h.shape

    def body(h_ref, g_ref, tg_ref, dh_ref, loss_ref, dg_ref):
        b, t = pl.program_id(0), pl.program_id(1)

        @pl.when((b == 0) & (t == 0))
        def _():
            loss_ref[...] = jnp.zeros_like(loss_ref)
            dg_ref[...] = jnp.zeros_like(dg_ref)

        lat = (t > 0).astype(F32)
        xh, r = _rms(h_ref[...])
        gg = g_ref[...]
        err = (xh * gg - tg_ref[...]) * lat
        loss_ref[...] += 0.5 * jnp.sum(jnp.mean(err * err, axis=-1, keepdims=True))
        dy = err * (1.0 / D)
        dg_ref[0:1, :] += _rowsum(dy * xh)
        dxh = dy * gg
        dh_ref[...] = r * (dxh - xh * jnp.mean(dxh * xh, axis=-1, keepdims=True))

    return pl.pallas_call(
        body, name=name, grid=(bsz, t_len // TM),
        in_specs=[_row_spec(D), _const_spec((1, D)),
                  pl.BlockSpec((None, TM, D), lambda b, t: (b, jnp.maximum(t - 1, 0), 0))],
        out_specs=[_row_spec(D), _const_spec((8, 128)), _const_spec((8, D))],
        out_shape=[jax.ShapeDtypeStruct((bsz, t_len, D), F32), jax.ShapeDtypeStruct((8, 128), F32),
                   jax.ShapeDtypeStruct((8, D), F32)],
        compiler_params=_cp(48, 2),
    )(h, final_g, target)


def _swap16(x):
    n = x.shape[-1]
    ax = x.ndim - 1
    lane = lax.broadcasted_iota(jnp.int32, x.shape, ax)
    return jnp.where((lane % 32) < 16, pltpu.roll(x, n - 16, ax), pltpu.roll(x, 16, ax))


def _rope(x, cos, sin):
    return x * cos + _swap16(x) * sin


def _rope_t(dy, cos, sin):
    return dy * cos + _swap16(dy * sin)


def rope_tables(lc, seq):
    rows = seq // GRID_W
    row = jnp.repeat(jnp.arange(rows, dtype=F32), GRID_W)
    col = jnp.tile(jnp.arange(GRID_W, dtype=F32), rows)
    inv_freq = ROPE_BASE ** (-jnp.arange(ROPE_FREQS, dtype=F32) / ROPE_FREQS)
    ar, ac = row[:, None] * inv_freq, col[:, None] * inv_freq
    cos = jnp.concatenate([jnp.cos(ar), jnp.cos(ar), jnp.cos(ac), jnp.cos(ac)], axis=1)
    sin = jnp.concatenate([-jnp.sin(ar), jnp.sin(ar), -jnp.sin(ac), jnp.sin(ac)], axis=1)
    cos = jnp.concatenate([jnp.ones((lc, HEAD_DIM), F32), cos], axis=0)
    sin = jnp.concatenate([jnp.zeros((lc, HEAD_DIM), F32), sin], axis=0)
    return jnp.tile(cos, (1, 2)), jnp.tile(sin, (1, 2))


def _attn_mask(i, lc, t_len):
    qrow = i * AB + lax.broadcasted_iota(jnp.int32, (AB, 1), 0)
    kloc = (i - 1) * AB + lax.broadcasted_iota(jnp.int32, (1, 3 * AB), 1)
    valid = (qrow >= lc) & (kloc >= lc) & (kloc < t_len) & (jnp.abs(qrow - kloc) <= WINDOW)
    mask = jnp.concatenate([valid, jnp.ones((AB, lc), jnp.bool_)], axis=1)
    return jnp.concatenate([mask] * GROUP, axis=0)


def _attn_specs(t_len, lc):
    nb = t_len // AB
    prev = lambda b, i: (b, jnp.maximum(i - 1, 0), 0)
    cur = lambda b, i: (b, i, 0)
    nxt = lambda b, i: (b, jnp.minimum(i + 1, nb - 1), 0)
    kv = [pl.BlockSpec((None, AB, 2 * KV_W), f) for f in (prev, cur, nxt)]
    kv.append(pl.BlockSpec((None, lc, 2 * KV_W), lambda b, i: (b, 0, 0)))
    tab = [pl.BlockSpec((AB, 128), lambda b, i, f=f: f(b, i)[1:]) for f in (prev, cur, nxt)]
    return kv, tab


def _attn_keys(kvp, kvc, kvn, kvx, cp, cc, cn, sp, sc, sn):
    kk = jnp.concatenate([_rope(kvp[:, :KV_W], cp, sp), _rope(kvc[:, :KV_W], cc, sc),
                          _rope(kvn[:, :KV_W], cn, sn), kvx[:, :KV_W]], axis=0)
    vv = jnp.concatenate([kvp[:, KV_W:], kvc[:, KV_W:], kvn[:, KV_W:], kvx[:, KV_W:]], axis=0)
    return kk, vv


def _stack_heads(x, hk):
    return jnp.concatenate([x[:, (GROUP * hk + g) * HEAD_DIM:(GROUP * hk + g + 1) * HEAD_DIM]
                            for g in range(GROUP)], axis=0)


def _sink_col(sink_ref, hk):
    return jnp.concatenate([jnp.full((AB, 1), sink_ref[GROUP * hk + g], F32) for g in range(GROUP)], axis=0)


def attn_fwd(q, kv, cos, sin, sink, lc, name, carry=()):
    bsz, t_len, _ = q.shape
    nb = t_len // AB
    kv_specs, tab_specs = _attn_specs(t_len, lc)
    scale = HEAD_DIM ** -0.5
    nc = len(carry)

    def body(sink_ref, q_ref, kvp_ref, kvc_ref, kvn_ref, kvx_ref, cp, cc, cn, sp, sc, sn, *rest):
        o_ref, lse_ref = rest[nc:nc + 2]
        b, i = pl.program_id(0), pl.program_id(1)
        if nc:
            start, wait = _xy_copies(rest[:nc], rest[nc + 2:2 * nc + 2], *rest[2 * nc + 2:], False)
            pl.when((b == 0) & (i == 0))(start)
        mask = _attn_mask(i, lc, t_len)
        qr = _rope(q_ref[...], jnp.tile(cc[...], (1, 4)), jnp.tile(sc[...], (1, 4)))
        kk, vv = _attn_keys(kvp_ref[...], kvc_ref[...], kvn_ref[...], kvx_ref[...],
                            cp[...], cc[...], cn[...], sp[...], sc[...], sn[...])
        outs, lses = [], []
        for hk in range(N_KV):
            kh = kk[:, hk * HEAD_DIM:(hk + 1) * HEAD_DIM].astype(BF16)
            vh = vv[:, hk * HEAD_DIM:(hk + 1) * HEAD_DIM].astype(BF16)
            q4 = _stack_heads(qr, hk).astype(BF16)
            s = jnp.where(mask, _dot_nt(q4, kh) * scale, NEG_INF)
            sk = _sink_col(sink_ref, hk)
            m = jnp.maximum(jnp.max(s, axis=-1, keepdims=True), sk)
            p = jnp.exp(s - m)
            l = jnp.sum(p, axis=-1, keepdims=True) + jnp.exp(sk - m)
            o = _dot(p.astype(BF16), vh) / l
            lse = m + jnp.log(l)
            for g in range(GROUP):
                outs.append(o[g * AB:(g + 1) * AB])
                lses.append(lse[g * AB:(g + 1) * AB])
        o_ref[...] = jnp.concatenate(outs, axis=1)
        lse_ref[...] = jnp.concatenate(lses, axis=1)
        if nc:
            pl.when((b == bsz - 1) & (i == nb - 1))(wait)

    return pl.pallas_call(
        body, name=name, grid=(bsz, nb),
        in_specs=[pl.BlockSpec(memory_space=pltpu.SMEM),
                  pl.BlockSpec((None, AB, ATTN_W), lambda b, i: (b, i, 0))] + kv_specs + tab_specs + tab_specs
        + [HBM_SPEC] * nc,
        out_specs=[pl.BlockSpec((None, AB, ATTN_W), lambda b, i: (b, i, 0)),
                   pl.BlockSpec((None, AB, N_HEADS), lambda b, i: (b, i, 0))] + [HBM_SPEC] * nc,
        out_shape=[jax.ShapeDtypeStruct((bsz, t_len, ATTN_W), F32), jax.ShapeDtypeStruct((bsz, t_len, N_HEADS), F32)]
        + _xy_out_shapes(carry, False),
        scratch_shapes=_xy_sems(nc) if nc else [],
        compiler_params=_cp(48, 2),
    )(sink, q, kv, kv, kv, kv, cos, cos, cos, sin, sin, sin, *carry)


def attn_bwd(q, kv, o, lse, do, cos, sin, sink, lc, name, carry=()):
    bsz, t_len, _ = q.shape
    nb = t_len // AB
    kv_specs, tab_specs = _attn_specs(t_len, lc)
    scale = HEAD_DIM ** -0.5
    blk = lambda w: pl.BlockSpec((None, AB, w), lambda b, i: (b, i, 0))
    full_tab = pl.BlockSpec((t_len, 128), lambda b, i: (0, 0))
    nc = len(carry)

    def body(sink_ref, q_ref, kvp_ref, kvc_ref, kvn_ref, kvx_ref, cp, cc, cn, sp, sc, sn, cf, sf,
             o_ref, lse_ref, do_ref, *rest):
        dq_ref, dkv_ref, dsink_ref = rest[nc:nc + 3]
        b, i = pl.program_id(0), pl.program_id(1)
        if nc:
            start, wait = _xy_copies(rest[:nc], rest[nc + 3:2 * nc + 3], *rest[2 * nc + 3:], True)
            pl.when((b == 0) & (i == 0))(start)

        @pl.when(i == 0)
        def _():
            dkv_ref[...] = jnp.zeros_like(dkv_ref)

        @pl.when((b == 0) & (i == 0))
        def _():
            dsink_ref[...] = jnp.zeros_like(dsink_ref)

        mask = _attn_mask(i, lc, t_len)
        cq, sq = jnp.tile(cc[...], (1, 4)), jnp.tile(sc[...], (1, 4))
        qr = _rope(q_ref[...], cq, sq)
        kk, vv = _attn_keys(kvp_ref[...], kvc_ref[...], kvn_ref[...], kvx_ref[...],
                            cp[...], cc[...], cn[...], sp[...], sc[...], sn[...])
        dov, ov, lsev = do_ref[...], o_ref[...], lse_ref[...]
        dqs, dks, dvs, dsk = [], [], [], []
        for hk in range(N_KV):
            kh = kk[:, hk * HEAD_DIM:(hk + 1) * HEAD_DIM].astype(BF16)
            vh = vv[:, hk * HEAD_DIM:(hk + 1) * HEAD_DIM].astype(BF16)
            q4 = _stack_heads(qr, hk).astype(BF16)
            do4 = _stack_heads(dov, hk)
            o4 = _stack_heads(ov, hk)
            lse4 = jnp.concatenate([lsev[:, GROUP * hk + g:GROUP * hk + g + 1] for g in range(GROUP)], axis=0)
            delta = jnp.sum(do4 * o4, axis=-1, keepdims=True)
            s = jnp.where(mask, _dot_nt(q4, kh) * scale, NEG_INF)
            p = jnp.exp(s - lse4)
            do4b = do4.astype(BF16)
            dp = _dot_nt(do4b, vh)
            ds = (p * (dp - delta) * scale).astype(BF16)
            dq4 = _dot(ds, kh)
            dks.append(_dot_tn(ds, q4))
            dvs.append(_dot_tn(p.astype(BF16), do4b))
            pd = jnp.exp(_sink_col(sink_ref, hk) - lse4) * delta
            for g in range(GROUP):
                dqs.append(dq4[g * AB:(g + 1) * AB])
                dsk.append(-jnp.sum(pd[g * AB:(g + 1) * AB], axis=0, keepdims=True))
        dq_ref[...] = _rope_t(jnp.concatenate(dqs, axis=1), cq, sq).astype(BF16)
        dsink_ref[0:1, :] += jnp.concatenate(dsk, axis=1)
        dkv = jnp.concatenate(dks + dvs, axis=1)
        starts = (jnp.maximum(i - 1, 0), i, jnp.minimum(i + 1, nb - 1))
        for j, st in enumerate(starts):
            rows = pl.ds(pl.multiple_of(st * AB, AB), AB)
            dkv_ref[rows, :] += dkv[j * AB:(j + 1) * AB]
        dkv_ref[0:lc, :] += dkv[3 * AB:]

        @pl.when(i == nb - 1)
        def _():
            def unrotate(j, carry):
                rows = pl.ds(pl.multiple_of(j * AB, AB), AB)
                dkv_ref[rows, 0:KV_W] = _rope_t(dkv_ref[rows, 0:KV_W], cf[rows, :], sf[rows, :])
                return carry
            lax.fori_loop(0, nb, unrotate, 0)

        if nc:
            pl.when((b == bsz - 1) & (i == nb - 1))(wait)

    return pl.pallas_call(
        body, name=name, grid=(bsz, nb),
        in_specs=[pl.BlockSpec(memory_space=pltpu.SMEM), blk(ATTN_W)] + kv_specs + tab_specs + tab_specs
        + [full_tab, full_tab, blk(ATTN_W), blk(N_HEADS), blk(ATTN_W)] + [HBM_SPEC] * nc,
        out_specs=[blk(ATTN_W), pl.BlockSpec((None, t_len, 2 * KV_W), lambda b, i: (b, 0, 0)),
                   pl.BlockSpec((8, N_HEADS), lambda b, i: (0, 0))] + [HBM_SPEC] * nc,
        out_shape=[jax.ShapeDtypeStruct((bsz, t_len, ATTN_W), BF16), jax.ShapeDtypeStruct((bsz, t_len, 2 * KV_W), F32),
                   jax.ShapeDtypeStruct((8, N_HEADS), F32)] + _xy_out_shapes(carry, True),
        scratch_shapes=_xy_sems(nc) if nc else [],
        compiler_params=_cp(56, 2),
    )(sink, q, kv, kv, kv, kv, cos, cos, cos, sin, sin, sin, cos, sin, o, lse, do, *carry)


def _s5_mats_dir(a_re, a_im, log_dt, b_re, b_im, c_re, c_im, flip):
    hp = lax.Precision.HIGHEST
    lam = lax.complex(a_re, a_im)
    ldt = lam * jnp.exp(log_dt)[:, None]
    a_bar = jnp.exp(ldt)
    b_bar = ((a_bar - 1.0) / lam)[..., None] * lax.complex(b_re, b_im)
    cm = lax.complex(c_re, c_im)
    tt = np.arange(Q)
    powers = lambda e: jnp.exp(ldt[..., None] * jnp.asarray(e, F32))
    ca = cm[:, :, :, None] * powers(Q - 1 - tt if flip else tt)[:, None, :, :]
    ca = jnp.concatenate([jnp.real(ca), -jnp.imag(ca)], axis=2)
    bb = jnp.concatenate([jnp.real(b_bar), jnp.imag(b_bar)], axis=1)
    k = jnp.einsum('gpk,gcpt->gktc', bb, ca, precision=hp).reshape(G, C, QC)
    slabs = []
    for t1 in range(Q):
        if flip:
            sh = (Q - 1 - t1) * C
            slabs.append(jnp.pad(k, ((0, 0), (0, 0), (0, sh)))[..., sh:])
        else:
            slabs.append(jnp.pad(k, ((0, 0), (0, 0), (t1 * C, 0)))[..., :QC])
    kt = jnp.stack(slabs, axis=1).reshape(G, QC, QC)
    ws = powers(tt if flip else Q - 1 - tt)[:, :, :, None] * b_bar[:, :, None, :]
    ws = ws.transpose(0, 2, 3, 1)
    wo = cm[:, :, :, None] * powers(Q - tt if flip else tt + 1)[:, None, :, :]
    wo = wo.transpose(0, 2, 3, 1)
    ws = jnp.concatenate([jnp.real(ws), jnp.imag(ws)], axis=-1).reshape(G, QC, P2)
    wo = jnp.concatenate([jnp.real(wo), -jnp.imag(wo)], axis=1).reshape(G, P2, QC)
    a1, a2 = _pair_forms(powers([Q]))
    return kt, ws, wo, a1, a2


def _pair_forms(z):
    re, im = jnp.real(z), jnp.imag(z)
    k = z.shape[-1]
    a1 = jnp.concatenate([re, re], axis=1).transpose(2, 0, 1).reshape(k, G * P2)
    a2 = jnp.concatenate([-im, im], axis=1).transpose(2, 0, 1).reshape(k, G * P2)
    return a1, a2


def s5_mats(a_re, a_im, log_dt, b_re, b_im, c_re, c_im):
    per_dir = [_s5_mats_dir(a_re[d], a_im[d], log_dt[d], b_re[d], b_im[d], c_re[d], c_im[d], d == 1)
               for d in range(2)]
    return tuple(jnp.stack([m[i] for m in per_dir]) for i in range(5))


GH = G // 8
RT = 16 * Q


def _perm_consts():
    r = np.arange(RT)
    rows = np.zeros((RT, RT), np.float32)
    rows[(r % Q) * 16 + r // Q, r] = 1.0
    q = np.arange(8 * 128)
    lanes = np.zeros((8 * 128, 8 * 128), np.float32)
    lanes[q, ((q % 128) // C) * 128 + (q // 128) * C + q % C] = 1.0
    return jnp.asarray(rows, BF16), jnp.asarray(lanes, BF16)


def to_groups(x, name):
    bsz, t_len, _ = x.shape
    nc = t_len // Q
    rows, lanes = _perm_consts()

    def body(x_ref, r_ref, p_ref, o_ref, w_ref):
        for j in range(t_len // RT):
            pt = _dot(r_ref[...], x_ref[j * RT:(j + 1) * RT, :].astype(BF16)).astype(BF16)
            for t in range(Q):
                w_ref[j * 16:(j + 1) * 16, t * SSM_W:(t + 1) * SSM_W] = pt[t * 16:(t + 1) * 16, :]
        for gh in range(GH):
            for th in range(2):
                inp = jnp.concatenate([w_ref[:, (th * 8 + tl) * SSM_W + gh * 128:(th * 8 + tl) * SSM_W + (gh + 1) * 128]
                                       for tl in range(8)], axis=1)
                out = _dot(inp, p_ref[...]).astype(BF16)
                for gl in range(8):
                    o_ref[gh * 8 + gl, :, th * 128:(th + 1) * 128] = out[:, gl * 128:(gl + 1) * 128]

    return pl.pallas_call(
        body, name=name, grid=(bsz,),
        in_specs=[pl.BlockSpec((None, t_len, SSM_W), lambda b: (b, 0, 0)), pl.BlockSpec((RT, RT), lambda b: (0, 0)),
                  pl.BlockSpec((1024, 1024), lambda b: (0, 0))],
        out_specs=pl.BlockSpec((None, G, nc, QC), lambda b: (b, 0, 0, 0)),
        out_shape=jax.ShapeDtypeStruct((bsz, G, nc, QC), BF16),
        scratch_shapes=[pltpu.VMEM((nc, Q * SSM_W), BF16)],
        compiler_params=_cp(56, 1),
    )(x, rows, lanes)


def from_groups(xg, name):
    bsz, _, nc, _ = xg.shape
    t_len = nc * Q
    rows, lanes = _perm_consts()

    def body(x_ref, r_ref, p_ref, o_ref, whi_ref, wlo_ref):
        gh = pl.program_id(1)
        for th in range(2):
            inp = jnp.concatenate([x_ref[gl, :, th * 128:(th + 1) * 128] for gl in range(8)], axis=1)
            hi = inp.astype(BF16)
            lo = (inp - hi.astype(F32)).astype(BF16)
            whi_ref[gh, :, th * 1024:(th + 1) * 1024] = _dot(hi, p_ref[...]).astype(BF16)
            wlo_ref[gh, :, th * 1024:(th + 1) * 1024] = _dot(lo, p_ref[...]).astype(BF16)

        @pl.when(gh == GH - 1)
        def _():
            for j in range(t_len // RT):
                def tile(w_ref):
                    return jnp.concatenate(
                        [jnp.concatenate([w_ref[k, j * 16:(j + 1) * 16, t * 128:(t + 1) * 128] for k in range(GH)],
                                         axis=1) for t in range(Q)], axis=0)
                o_ref[j * RT:(j + 1) * RT, :] = _dot(r_ref[...], tile(whi_ref)) + _dot(r_ref[...], tile(wlo_ref))

    return pl.pallas_call(
        body, name=name, grid=(bsz, GH),
        in_specs=[pl.BlockSpec((None, 8, nc, QC), lambda b, k: (b, k, 0, 0)),
                  pl.BlockSpec((RT, RT), lambda b, k: (0, 0)), pl.BlockSpec((1024, 1024), lambda b, k: (0, 0))],
        out_specs=pl.BlockSpec((None, t_len, SSM_W), lambda b, k: (b, 0, 0)),
        out_shape=jax.ShapeDtypeStruct((bsz, t_len, SSM_W), F32),
        scratch_shapes=[pltpu.VMEM((GH, nc, Q * 128), BF16), pltpu.VMEM((GH, nc, Q * 128), BF16)],
        compiler_params=_cp(56, 2),
    )(xg, rows, lanes)


def _gb(shape):
    return pl.BlockSpec((None, None) + shape, lambda g, b: (b, g, 0, 0))


def _gw(shape):
    return pl.BlockSpec((2, None) + shape, lambda g, b: (0, g, 0, 0))


def _gs(nc):
    return pl.BlockSpec((2, None, nc, P2), lambda g, b: (0, b, 0, g))


def s5_chunk_fwd(ug, kt, ws, name):
    bsz, _, nc, _ = ug.shape

    def body(u_ref, kt_ref, ws_ref, y_ref, s_ref):
        ub = u_ref[...]
        y_ref[...] = _dot(ub, kt_ref[0]) + _dot(ub, kt_ref[1])
        s_ref[0] = _dot(ub, ws_ref[0])
        s_ref[1] = _dot(ub, ws_ref[1])

    return pl.pallas_call(
        body, name=name, grid=(G, bsz),
        in_specs=[_gb((nc, QC)), _gw((QC, QC)), _gw((QC, P2))],
        out_specs=[_gb((nc, QC)), _gs(nc)],
        out_shape=[jax.ShapeDtypeStruct((bsz, G, nc, QC), F32), jax.ShapeDtypeStruct((2, bsz, nc, G * P2), F32)],
        compiler_params=_cp(32, 2),
    )(ug, kt, ws)


def s5_scan(s, a1, a2, ncc, reverse, name, hp=None):
    _, bsz, nc, gw = s.shape
    as_rows = lambda v: v.reshape(v.shape[:-1] + (G, P2))
    st = pl.BlockSpec((2, None, nc, SCAN_G, P2), lambda b, w: (0, b, 0, w, 0))
    av = pl.BlockSpec((2, SCAN_G, P2), lambda b, w: (0, w, 0))
    acc = pl.BlockSpec((2, None, SCAN_G, P2), lambda b, w: (0, b, w, 0))
    with_da = hp is not None

    def body(*refs):
        if with_da:
            s_ref, a1_ref, a2_ref, hp_ref, out_ref, da1_ref, da2_ref = refs
        else:
            s_ref, a1_ref, a2_ref, out_ref = refs
        a1v = (a1_ref[0], a1_ref[1])
        a2v = (a2_ref[0], a2_ref[1])
        swap = lambda h: pltpu.roll(h, P, 1)

        def block(jb, carry):
            pb = nblk - 1 - jb if reverse else jb
            base = (pb * ncc, jnp.where(pb == 0, ncc - 1, nc - 1 - (pb - 1) * ncc))
            hs, da1, da2 = carry
            for kk in range(ncc):
                k = ncc - 1 - kk if reverse else kk
                nh, n1, n2 = [], [], []
                for d, n in enumerate((base[0] + k, base[1] - k)):
                    h, hw = hs[d]
                    out_ref[d, n] = h
                    sv = s_ref[d, n]
                    nh.append((a1v[d] * h + a2v[d] * hw + sv, a1v[d] * hw - a2v[d] * h + swap(sv)))
                    if with_da:
                        hv = hp_ref[d, n]
                        n1.append(da1[d] + h * hv)
                        n2.append(da2[d] + h * swap(hv))
                hs, da1, da2 = tuple(nh), tuple(n1), tuple(n2)
            return hs, da1, da2

        assert nc % ncc == 0
        nblk = nc // ncc
        z = jnp.zeros((SCAN_G, P2), F32)
        zz = (z, z) if with_da else ()
        _, da1, da2 = lax.fori_loop(0, nblk, block, (((z, z), (z, z)), zz, zz))
        if with_da:
            for d in range(2):
                da1_ref[d] = da1[d]
                da2_ref[d] = da2[d]

    out_shape = [jax.ShapeDtypeStruct((2, bsz, nc, G, P2), F32)]
    out_specs = [st]
    ins = [as_rows(s), as_rows(a1[:, 0]), as_rows(a2[:, 0])]
    in_specs = [st, av, av]
    if with_da:
        ins.append(as_rows(hp))
        in_specs.append(st)
        out_shape += [jax.ShapeDtypeStruct((2, bsz, G, P2), F32)] * 2
        out_specs += [acc, acc]
    res = pl.pallas_call(
        body, name=name, grid=(bsz, G // SCAN_G), in_specs=in_specs, out_specs=out_specs, out_shape=out_shape,
        compiler_params=_cp(48, 2),
    )(*ins)
    out = res[0].reshape(s.shape)
    return (out, res[1].reshape(2, bsz, gw), res[2].reshape(2, bsz, gw)) if with_da else out


def s5_out_fwd(y1, hp, wo, name):
    bsz, _, nc, _ = y1.shape

    def body(y1_ref, hp_ref, wo_ref, y_ref):
        y_ref[...] = (y1_ref[...] + _dot(hp_ref[0].astype(BF16), wo_ref[0])
                      + _dot(hp_ref[1].astype(BF16), wo_ref[1]))

    return pl.pallas_call(
        body, name=name, grid=(G, bsz),
        in_specs=[_gb((nc, QC)), _gs(nc), _gw((P2, QC))],
        out_specs=_gb((nc, QC)),
        out_shape=jax.ShapeDtypeStruct(y1.shape, F32),
        compiler_params=_cp(32, 2),
    )(y1, hp, wo)


def _acc_init(b, *refs):
    @pl.when(b == 0)
    def _():
        for r in refs:
            r[...] = jnp.zeros_like(r)


def s5_out_bwd(dyg, ug, hp, wo, name):
    bsz, _, nc, _ = dyg.shape

    def body(dy_ref, u_ref, hp_ref, wo_ref, dhp_ref, dwo_ref, dkt_ref):
        _acc_init(pl.program_id(1), dwo_ref, dkt_ref)
        dyb = dy_ref[...]
        for d in range(2):
            dhp_ref[d] = _dot_nt(dyb, wo_ref[d])
            dwo_ref[d] += _dot_tn(hp_ref[d].astype(BF16), dyb)
        dkt_ref[...] += _dot_tn(u_ref[...], dyb)

    return pl.pallas_call(
        body, name=name, grid=(G, bsz),
        in_specs=[_gb((nc, QC)), _gb((nc, QC)), _gs(nc), _gw((P2, QC))],
        out_specs=[_gs(nc), _gw((P2, QC)), pl.BlockSpec((None, QC, QC), lambda g, b: (g, 0, 0))],
        out_shape=[jax.ShapeDtypeStruct(hp.shape, F32), jax.ShapeDtypeStruct((2, G, P2, QC), F32),
                   jax.ShapeDtypeStruct((G, QC, QC), F32)],
        compiler_params=_cp(32, 2),
    )(dyg, ug, hp, wo)


def s5_chunk_bwd(dyg, ug, ds, kt, ws, name):
    bsz, _, nc, _ = dyg.shape

    def body(dy_ref, u_ref, ds_ref, kt_ref, ws_ref, du_ref, dws_ref):
        _acc_init(pl.program_id(1), dws_ref)
        dyb = dy_ref[...]
        du = _dot_nt(dyb, kt_ref[0]) + _dot_nt(dyb, kt_ref[1])
        for d in range(2):
            dsb = ds_ref[d].astype(BF16)
            du += _dot_nt(dsb, ws_ref[d])
            dws_ref[d] += _dot_tn(u_ref[...], dsb)
        du_ref[...] = du

    return pl.pallas_call(
        body, name=name, grid=(G, bsz),
        in_specs=[_gb((nc, QC)), _gb((nc, QC)), _gs(nc), _gw((QC, QC)), _gw((QC, P2))],
        out_specs=[_gb((nc, QC)), _gw((QC, P2))],
        out_shape=[jax.ShapeDtypeStruct(dyg.shape, F32), jax.ShapeDtypeStruct((2, G, QC, P2), F32)],
        compiler_params=_cp(32, 2),
    )(dyg, ug, ds, kt, ws)


def local_step(x, ctx, target, mods, norm_g, final_g, even, odd, late=None, scatter=None):
    bsz, seq, _ = x.shape
    lc = ctx.shape[1]
    t_len = lc + seq
    ncc = lc // Q
    cos, sin = rope_tables(lc, seq)
    h = jnp.concatenate([ctx, x], axis=1)
    ssm_stacked = [jnp.stack([even[0]["ssm"][k], even[1]["ssm"][k]]) for k in range(7)]
    mats_all, mats_vjp = jax.vjp(jax.vmap(s5_mats), *ssm_stacked)
    d_mats = [None, None]
    saved = []
    for i in range(DEPTH):
        j = i // 2
        g = norm_g[i].reshape(1, D)
        if i % 2 == 0:
            w = even[j]
            a, q, kv, g_attn, u, g_ssm = norm_in(h, g, mods[i], w["w_in"], EVEN_SPLITS, (2, 4), f"even_in{j}")
            if i == 0 and late is not None:
                o_attn, lse, *gathered = attn_fwd(q, kv, cos, sin, w["sink"], lc, f"attn_fwd{j}", carry=late[0])
                late[1](gathered)
            else:
                o_attn, lse = attn_fwd(q, kv, cos, sin, w["sink"], lc, f"attn_fwd{j}")
            kt, ws, wo, a1, a2 = (m[j] for m in mats_all)
            kt, ws, wo = kt.astype(BF16), ws.astype(BF16), wo.astype(BF16)
            d_skip = w["ssm"][7].reshape(1, SSM_W)
            ug = to_groups(u, f"u_to_groups{j}")
            y1, s = s5_chunk_fwd(ug, kt, ws, f"s5_chunk_fwd{j}")
            hp = s5_scan(s, a1, a2, ncc, False, f"s5_scan_fwd{j}")
            y_scan = from_groups(s5_out_fwd(y1, hp, wo, f"s5_out_fwd{j}"), f"y_from_groups{j}")
            h_new, mix, yout = even_out(h, mods[i], o_attn, g_attn, y_scan, u, d_skip, g_ssm, w["glu_w"], w["glu_b"],
                                        w["w_out"], f"even_out{j}")
            saved.append(dict(h=h, a=a, q=q, kv=kv, g_attn=g_attn, g_ssm=g_ssm, o_attn=o_attn, lse=lse, ug=ug, u=u,
                              hp=hp, y_scan=y_scan, mix=mix, yout=yout, mats=(kt, ws, wo, a1, a2), d_skip=d_skip))
        else:
            w = odd[j]
            a, u, gate = norm_in(h, g, mods[i], w["w_in"], ODD_SPLITS, (1,), f"odd_in{j}")
            pm = pool_band(u, lc, False, f"pool_band_fwd{j}")
            h_new, mix, yout = pool_out(h, mods[i], pm, gate, w["pool_w"], w["pool_scale"], w["w_out"], f"pool_out{j}")
            saved.append(dict(h=h, a=a, pm=pm, gate=gate, mix=mix, yout=yout))
        h = h_new

    dh, loss_acc, dfg = loss_head(h, final_g.reshape(1, D), target, "loss_head")
    grads = dict(final_g=dfg[0], norm_g=[None] * DEPTH, even=[None, None], odd=[None, None])
    dmods = [None] * DEPTH
    rows = bsz * t_len
    flat = lambda v: v.reshape(rows, v.shape[-1])
    for i in reversed(range(DEPTH)):
        j = i // 2
        sv = saved[i]
        g = norm_g[i].reshape(1, D)
        if i % 2 == 0:
            w = even[j]
            kt, ws, wo, a1, a2 = sv["mats"]
            (d_oattn, d_gattn, d_gssm, d_yssm, dyout, zz, dsg, dgate, dglu_b) = even_out_bwd(
                dh, mods[i], sv["o_attn"], sv["g_attn"], sv["y_scan"], sv["u"], sv["d_skip"], sv["g_ssm"], w["glu_w"],
                w["glu_b"], w["w_out"], sv["yout"], f"even_out_bwd{j}")
            g_w_out = matmul_tn(flat(sv["mix"]), flat(dyout), D, D, f"even_w_out_grad{j}")
            g_glu_w = matmul_tn(flat(zz), flat(dsg), SSM_W, SSM_W, f"glu_w_grad{j}")
            carry = scatter(grads) if i == 0 and scatter is not None else ()
            dq, dkv, dsink, *grads["landed"] = attn_bwd(sv["q"], sv["kv"], sv["o_attn"], sv["lse"], d_oattn, cos, sin,
                                                        w["sink"], lc, f"attn_bwd{j}", carry=carry)
            dyg = to_groups(d_yssm, f"dy_to_groups{j}")
            dhp, dwo, dkt = s5_out_bwd(dyg, sv["ug"], sv["hp"], wo, f"s5_out_bwd{j}")
            ds, da1, da2 = s5_scan(dhp, a1, -a2, ncc, True, f"s5_scan_bwd{j}", hp=sv["hp"])
            dug, dws = s5_chunk_bwd(dyg, sv["ug"], ds, kt, ws, f"s5_chunk_bwd{j}")
            dkt2 = jnp.stack([dkt, dkt])
            da1 = da1.sum(axis=1).reshape(2, 1, G * P2)
            da2 = da2.sum(axis=1).reshape(2, 1, G * P2)
            d_mats[j] = (dkt2, dws, dwo, da1, da2)
            dparts = [dq, dkv, d_gattn, from_groups(dug, f"du_from_groups{j}"), d_gssm]
            dh, dz, dmod, dg = norm_in_bwd(dparts, dh, sv["h"], g, mods[i], w["w_in"], f"even_in_bwd{j}",
                                           skip=(3, d_yssm, sv["d_skip"]), latent_only=(i == 0))
            g_w_in = matmul_tn(flat(sv["a"]), flat(dz), D, dz.shape[-1], f"even_w_in_grad{j}")
            grads["even"][j] = dict(w_in=g_w_in, w_out=g_w_out, sink=dsink[0], d_skip=dglu_b[1], glu_w=g_glu_w,
                                    glu_b=dglu_b[0])
        else:
            w = odd[j]
            dpm, dgt, dyout, dpp, dgate, dps = pool_out_bwd(dh, mods[i], sv["pm"], sv["gate"], w["pool_w"],
                                                            w["pool_scale"], w["w_out"], sv["yout"],
                                                            f"pool_out_bwd{j}")
            g_w_out = matmul_tn(flat(sv["mix"]), flat(dyout), D, D, f"odd_w_out_grad{j}")
            g_pool_w = jnp.stack([matmul_tn(flat(sv["pm"]), flat(dpp), POOL_G, POOL_G, f"pool_w_grad{j}_{gi}",
                                            a_col=gi, b_col=gi) for gi in range(4)])
            du = pool_band(dpm, lc, True, f"pool_band_bwd{j}")
            dh, dz, dmod, dg = norm_in_bwd([du, dgt], dh, sv["h"], g, mods[i], w["w_in"], f"odd_in_bwd{j}")
            g_w_in = matmul_tn(flat(sv["a"]), flat(dz), D, dz.shape[-1], f"odd_w_in_grad{j}")
            grads["odd"][j] = dict(w_in=g_w_in, w_out=g_w_out, pool_w=g_pool_w, pool_scale=dps[0])
        grads["norm_g"][i] = dg[0]
        dmods[i] = jnp.concatenate([dmod[:, :, 0:2, :], dgate[:, :, 0:1, :]], axis=2)
    g_ssm = mats_vjp(tuple(jnp.stack([d_mats[0][k], d_mats[1][k]]) for k in range(5)))
    for j in range(2):
        grads["even"][j]["ssm"] = tuple(gk[j] for gk in g_ssm) + (grads["even"][j].pop("d_skip"),)
    return loss_acc[0, 0], dh, dmods, grads


N_DEV = 8
HBM_SPEC = pl.BlockSpec(memory_space=pltpu.HBM)


def allgather8(x_shard, name):
    m_per, n = x_shard.shape

    def body(x_ref, out_ref, send_sems, recv_sems, local_sem):
        x, y, c = lax.axis_index("x"), lax.axis_index("y"), lax.axis_index("c")
        me, sibling = (x, y, c), (x, y, 1 - c)
        chips = [(1 - x, y), (x, 1 - y), (1 - x, 1 - y)]

        def rows(px, py, pc):
            return out_ref.at[pl.ds((4 * px + 2 * py + pc) * m_per, m_per), :]

        def copy(k, block, to, src=None):
            return pltpu.make_async_remote_copy(
                src_ref=rows(*block) if src is None else src, dst_ref=rows(*block),
                send_sem=send_sems.at[k], recv_sem=recv_sems.at[k], device_id=to, device_id_type=MESH)

        mine = pltpu.make_async_copy(x_ref, rows(*me), local_sem)
        mine.start()
        first = [copy(0, me, sibling, src=x_ref)]
        first += [copy(1 + j, me, (*chip, c), src=x_ref) for j, chip in enumerate(chips)]
        for cp in first:
            cp.start()
        passed = [copy(4 + j, (*chip, c), sibling) for j, chip in enumerate(chips)]
        for j, chip in enumerate(chips):
            copy(1 + j, (*chip, c), me).wait_recv()
            passed[j].start()
        copy(0, sibling, me).wait_recv()
        for j, chip in enumerate(chips):
            copy(4 + j, (*chip, 1 - c), me).wait_recv()
        for cp in first + passed:
            cp.wait_send()
        mine.wait()

    return pl.pallas_call(
        body, name=name,
        out_shape=jax.ShapeDtypeStruct((N_DEV * m_per, n), x_shard.dtype),
        in_specs=[pl.BlockSpec(memory_space=pltpu.VMEM)],
        out_specs=pl.BlockSpec(memory_space=pltpu.VMEM),
        scratch_shapes=[pltpu.SemaphoreType.DMA((7,)), pltpu.SemaphoreType.DMA((7,)), pltpu.SemaphoreType.DMA],
        compiler_params=_cp(56),
    )(x_shard)


def xy_exchange(srcs, scatter, name):
    n = len(srcs)

    def body(*refs):
        start, wait = _xy_copies(refs[:n], refs[n:2 * n], *refs[2 * n:], scatter)
        start()
        wait()

    return pl.pallas_call(
        body, name=name, out_shape=_xy_out_shapes(srcs, scatter),
        in_specs=[HBM_SPEC] * n, out_specs=[HBM_SPEC] * n, scratch_shapes=_xy_sems(n),
    )(*srcs)


def _xy_out_shapes(srcs, scatter):
    return [jax.ShapeDtypeStruct((4,) + (tuple(s.shape[1:]) if scatter else tuple(s.shape)), s.dtype) for s in srcs]


def _xy_sems(n):
    return [pltpu.SemaphoreType.DMA((3 * n,)), pltpu.SemaphoreType.DMA((3 * n,)), pltpu.SemaphoreType.DMA((n,))]


def _xy_copies(src_refs, out_refs, send_sems, recv_sems, local_sems, scatter):
    n = len(src_refs)

    def parts():
        x, y, c = lax.axis_index("x"), lax.axis_index("y"), lax.axis_index("c")
        my = 2 * x + y
        peers = [(1 - x, y), (x, 1 - y), (1 - x, 1 - y)]

        def piece(i, pos):
            return src_refs[i].at[pos] if scatter else src_refs[i]

        def copy(i, k, src_pos, dst_pos):
            px, py = peers[k]
            return pltpu.make_async_remote_copy(
                src_ref=piece(i, src_pos), dst_ref=out_refs[i].at[dst_pos], send_sem=send_sems.at[3 * i + k],
                recv_sem=recv_sems.at[3 * i + k], device_id=(px, py, c), device_id_type=MESH)

        local = [pltpu.make_async_copy(piece(i, my), out_refs[i].at[my], local_sems.at[i]) for i in range(n)]
        sends = [copy(i, k, 2 * px + py, my) for i in range(n) for k, (px, py) in enumerate(peers)]
        lands = [copy(i, k, my, 2 * px + py) for i in range(n) for k, (px, py) in enumerate(peers)]
        return local, sends, lands

    def start():
        local, sends, _ = parts()
        for cp in local + sends:
            cp.start()

    def wait():
        local, sends, lands = parts()
        for cp in lands:
            cp.wait_recv()
        for cp in sends:
            cp.wait_send()
        for cp in local:
            cp.wait()

    return start, wait


def sibling_exchange(srcs, name):
    n = len(srcs)

    def body(*refs):
        src_refs, out_refs = refs[:n], refs[n:2 * n]
        send_sems, recv_sems = refs[2 * n:]
        peer = (lax.axis_index("x"), lax.axis_index("y"), 1 - lax.axis_index("c"))
        cps = [pltpu.make_async_remote_copy(src_ref=src_refs[i], dst_ref=out_refs[i], send_sem=send_sems.at[i],
                                            recv_sem=recv_sems.at[i], device_id=peer, device_id_type=MESH)
               for i in range(n)]
        for cp in cps:
            cp.start()
        for cp in cps:
            cp.wait()

    return pl.pallas_call(
        body, name=name, out_shape=[jax.ShapeDtypeStruct(s.shape, s.dtype) for s in srcs],
        in_specs=[HBM_SPEC] * n, out_specs=[HBM_SPEC] * n,
        scratch_shapes=[pltpu.SemaphoreType.DMA((n,)), pltpu.SemaphoreType.DMA((n,))],
    )(*srcs)


def _row_tile(rows, bytes_per_row, limit):
    best = None
    for tr in range(8, rows + 1, 8):
        if rows % tr == 0 and tr * bytes_per_row <= limit:
            best = tr
    return best if best is not None else rows


def sum_slots(x, name):
    n, rows, cols = x.shape
    tr = _row_tile(rows, n * cols * 4, 4 * MB)

    def body(x_ref, o_ref):
        acc = x_ref[0].astype(F32)
        for k in range(1, n):
            acc = acc + x_ref[k].astype(F32)
        o_ref[...] = acc

    return pl.pallas_call(
        body, name=name, grid=(rows // tr,),
        in_specs=[pl.BlockSpec((n, tr, cols), lambda r: (0, r, 0))],
        out_specs=pl.BlockSpec((tr, cols), lambda r: (r, 0)),
        out_shape=jax.ShapeDtypeStruct((rows, cols), F32),
        compiler_params=_cp(32, 1),
    )(x)


ADA_COLS = 3 * D // 4
C_ROWS = 8


def ada_fwd(c_all, ada_w, ada_b_cols, name):
    nrow = c_all.shape[0]

    def body(c_ref, w_ref, b_ref, o_ref):
        s, _ = _silu_and_grad(c_ref[...])
        o_ref[...] = _dot(s.astype(BF16), w_ref[...].astype(BF16)) + b_ref[...]

    return pl.pallas_call(
        body, name=name, grid=(DEPTH,),
        in_specs=[pl.BlockSpec((nrow, D), lambda i: (0, 0)), pl.BlockSpec((None, D, ADA_COLS), lambda i: (i, 0, 0)),
                  pl.BlockSpec((None, 1, ADA_COLS), lambda i: (i, 0, 0))],
        out_specs=pl.BlockSpec((None, nrow, ADA_COLS), lambda i: (i, 0, 0)),
        out_shape=jax.ShapeDtypeStruct((DEPTH, nrow, ADA_COLS), F32),
        compiler_params=_cp(32, 1),
    )(c_all, ada_w, ada_b_cols)


def ada_bwd(c_all, d_cols, ada_w, name):
    nrow = c_all.shape[0]

    def body(c_ref, d_ref, w_ref, gw_ref, ds_ref):
        @pl.when(pl.program_id(0) == 0)
        def _():
            ds_ref[...] = jnp.zeros_like(ds_ref)
        s, _ = _silu_and_grad(c_ref[...])
        dl = d_ref[...]
        gw_ref[...] = _dot_tn(s.astype(BF16), dl.astype(BF16))
        rid = lax.broadcasted_iota(jnp.int32, (nrow, 1), 0) % C_ROWS
        dctx = jnp.where((rid == 2) | (rid == 3), dl, 0.0).astype(BF16)
        ds_ref[0:1, :] += _rowsum(_dot_nt(dctx, w_ref[...].astype(BF16)))

    return pl.pallas_call(
        body, name=name, grid=(DEPTH,),
        in_specs=[pl.BlockSpec((nrow, D), lambda i: (0, 0)), pl.BlockSpec((None, nrow, ADA_COLS), lambda i: (i, 0, 0)),
                  pl.BlockSpec((None, D, ADA_COLS), lambda i: (i, 0, 0))],
        out_specs=[pl.BlockSpec((None, D, ADA_COLS), lambda i: (i, 0, 0)), pl.BlockSpec((8, D), lambda i: (0, 0))],
        out_shape=[jax.ShapeDtypeStruct((DEPTH, D, ADA_COLS), F32), jax.ShapeDtypeStruct((8, D), F32)],
        compiler_params=_cp(32, 1),
    )(c_all, d_cols, ada_w)


def ada_bias_grad(d_all, name):
    nrow = d_all.shape[1]

    def body(d_ref, o_ref):
        o_ref[...] = jnp.broadcast_to(_rowsum(d_ref[...]), o_ref.shape)

    return pl.pallas_call(
        body, name=name, grid=(DEPTH,),
        in_specs=[pl.BlockSpec((None, nrow, 3 * D), lambda i: (i, 0, 0))],
        out_specs=pl.BlockSpec((None, 8, 3 * D), lambda i: (i, 0, 0)),
        out_shape=jax.ShapeDtypeStruct((DEPTH, 8, 3 * D), F32),
        compiler_params=_cp(32, 1),
    )(d_all)


def silu_chain(ds, c, name):
    def body(ds_ref, c_ref, o_ref):
        _, dsl = _silu_and_grad(c_ref[...])
        o_ref[...] = ds_ref[...] * dsl

    return pl.pallas_call(body, name=name, out_shape=jax.ShapeDtypeStruct(ds.shape, F32))(ds, c)


def _flat_cols(shape):
    size = int(np.prod(shape))
    if shape[-1] >= 128:
        return shape[-1]
    for cols in (1024, 128):
        if size % cols == 0:
            return cols
    return shape[-1]


def adamw(w, m, v, grads, name):
    shape = w.shape
    cols = _flat_cols(shape)
    as2d = lambda a: a.reshape(-1, cols)
    rows = w.size // cols
    tr = _row_tile(rows, cols * 4, MB)
    k = len(grads)

    def body(*refs):
        w_ref, m_ref, v_ref = refs[:3]
        g_refs = refs[3:3 + k]
        g_out, d_out, m_out, v_out = refs[3 + k:]
        g = g_refs[0][...]
        for r in g_refs[1:]:
            g = g + r[...]
        g_out[...] = g
        mn = ADAM_B1 * m_ref[...] + (1.0 - ADAM_B1) * g
        vn = ADAM_B2 * v_ref[...] + (1.0 - ADAM_B2) * (g * g)
        m_out[...] = mn
        v_out[...] = vn
        m_hat = mn / (1.0 - ADAM_B1 ** ADAM_STEP)
        v_hat = vn / (1.0 - ADAM_B2 ** ADAM_STEP)
        d_out[...] = -ADAM_LR * (m_hat / (jnp.sqrt(v_hat) + ADAM_EPS) + ADAM_WD * w_ref[...])

    spec = pl.BlockSpec((tr, cols), lambda r: (r, 0))
    outs = pl.pallas_call(
        body, name=name, grid=(rows // tr,),
        in_specs=[spec] * (3 + k), out_specs=[spec] * 4,
        out_shape=[jax.ShapeDtypeStruct((rows, cols), F32)] * 4,
        compiler_params=_cp(32, 1),
    )(as2d(w), as2d(m), as2d(v), *[as2d(g) for g in grads])
    return tuple(o.reshape(shape) for o in outs)


BIG = (("even_w_in", (2, D, 576), 2), ("even_w_out", (2, 256, D), 1), ("glu_w", (2, 128, SSM_W), 1),
       ("odd_w_in", (2, D, 512), 2), ("odd_w_out", (2, 256, D), 1), ("pool_w", (2, 4, 64, POOL_G), 2))


def _full_shape(shard, axis):
    return tuple(4 * s if a == axis else s for a, s in enumerate(shard))


def _to_shards(full, shard, axis):
    return jnp.moveaxis(full.reshape(shard[:axis] + (4,) + shard[axis:]), axis, 0)


def _from_shards(stacked, shard, axis):
    return jnp.moveaxis(stacked, 0, axis).reshape(_full_shape(shard, axis))


GRAD_KEY = dict(even_w_in=("even", "w_in"), even_w_out=("even", "w_out"), glu_w=("even", "glu_w"),
                odd_w_in=("odd", "w_in"), odd_w_out=("odd", "w_out"), pool_w=("odd", "pool_w"))
RIDE = tuple((n, 1) for n, _, _ in BIG[:3]) + tuple((n, j) for n, _, _ in BIG[3:] for j in range(2))
LAST = tuple((n, 0) for n, _, _ in BIG[:3])


def _grad_piece(grads, name, j):
    kind, key = GRAD_KEY[name]
    shard, axis = next((s, a) for n, s, a in BIG if n == name)
    return _to_shards(grads[kind][j][key], shard[1:], axis - 1).astype(BF16)


SMALL = (("ds_ctx", (D,)), ("norm_g", (DEPTH, D)), ("final_g", (D,)), ("attn_sink", (2, N_HEADS)),
         ("ssm_a_re", (2, 2, G, P)), ("ssm_a_im", (2, 2, G, P)), ("ssm_log_dt", (2, 2, G)),
         ("ssm_b_re", (2, 2, G, P, C)), ("ssm_b_im", (2, 2, G, P, C)), ("ssm_c_re", (2, 2, G, C, P)),
         ("ssm_c_im", (2, 2, G, C, P)), ("ssm_d", (2, SSM_W)), ("glu_b", (2, SSM_W)), ("pool_scale", (2, D)))
SMALL_PAD = 8 * 128


def pack_small(vals):
    flat = jnp.concatenate([vals[n].reshape(-1) for n, _ in SMALL])
    pad = (-flat.shape[0]) % SMALL_PAD
    return jnp.pad(flat, (0, pad)).reshape(-1, 128)


def unpack_small(packed):
    flat, out, off = packed.reshape(-1), {}, 0
    for n, shape in SMALL:
        size = int(np.prod(shape))
        out[n] = flat[off:off + size].reshape(shape)
        off += size
    return out


WEIGHT_NAMES = ('c_ctx', 'ada_w', 'ada_b', 'norm_g', 'even_w_in', 'even_w_out', 'attn_sink', 'ssm_a_re', 'ssm_a_im',
                'ssm_log_dt', 'ssm_b_re', 'ssm_b_im', 'ssm_c_re', 'ssm_c_im', 'ssm_d', 'glu_w', 'glu_b', 'odd_w_in',
                'odd_w_out', 'pool_w', 'pool_scale', 'final_g')
SSM_NAMES = ('ssm_a_re', 'ssm_a_im', 'ssm_log_dt', 'ssm_b_re', 'ssm_b_im', 'ssm_c_re', 'ssm_c_im', 'ssm_d')


def kernel(x, c, ctx, c_ctx, ada_w, ada_b, norm_g, even_w_in, even_w_out, attn_sink, ssm_a_re, ssm_a_im, ssm_log_dt, ssm_b_re, ssm_b_im, ssm_c_re, ssm_c_im, ssm_d, glu_w, glu_b, odd_w_in, odd_w_out, pool_w, pool_scale, final_g, loss_target, m_c_ctx, m_ada_w, m_ada_b, m_norm_g, m_even_w_in, m_even_w_out, m_attn_sink, m_ssm_a_re, m_ssm_a_im, m_ssm_log_dt, m_ssm_b_re, m_ssm_b_im, m_ssm_c_re, m_ssm_c_im, m_ssm_d, m_glu_w, m_glu_b, m_odd_w_in, m_odd_w_out, m_pool_w, m_pool_scale, m_final_g, v_c_ctx, v_ada_w, v_ada_b, v_norm_g, v_even_w_in, v_even_w_out, v_attn_sink, v_ssm_a_re, v_ssm_a_im, v_ssm_log_dt, v_ssm_b_re, v_ssm_b_im, v_ssm_c_re, v_ssm_c_im, v_ssm_d, v_glu_w, v_glu_b, v_odd_w_in, v_odd_w_out, v_pool_w, v_pool_scale, v_final_g):
    env = dict(locals())
    weights = {n: env[n] for n in WEIGHT_NAMES}
    bsz = x.shape[0]
    ax, ay, ac = lax.axis_index("x"), lax.axis_index("y"), lax.axis_index("c")
    pos = 2 * ax + ay
    dev = 2 * pos + ac

    c_rows = jnp.concatenate([c, c_ctx.reshape(1, D), c_ctx.reshape(1, D), jnp.zeros((C_ROWS - bsz - 2, D), F32)])
    c_all = allgather8(c_rows, "gather_c")
    ada_b_cols = lax.dynamic_slice(ada_b, (0, pos * ADA_COLS), (DEPTH, ADA_COLS)).reshape(DEPTH, 1, ADA_COLS)
    mod_cols = ada_fwd(c_all, ada_w, ada_b_cols, "ada_fwd")
    nrow = N_DEV * C_ROWS
    misc = jnp.concatenate([mod_cols.reshape(DEPTH * nrow, ADA_COLS),
                            jnp.pad(pool_scale, ((0, 6), (0, ADA_COLS - pool_scale.shape[1])))])
    misc_all = allgather8(misc, "gather_mod").reshape(4, 2, DEPTH * nrow + 8, ADA_COLS)[:, 0]
    mod_full = misc_all[:, :DEPTH * nrow].reshape(4, DEPTH, nrow, ADA_COLS).transpose(1, 2, 0, 3)
    mod_mine = lax.dynamic_slice(mod_full.reshape(DEPTH, nrow, 3 * D), (0, dev * C_ROWS, 0), (DEPTH, C_ROWS, 3 * D))
    mods = []
    for i in range(DEPTH):
        lat = mod_mine[i, :bsz].reshape(bsz, 1, 3, D)
        con = jnp.broadcast_to(mod_mine[i, bsz].reshape(1, 1, 3, D), (bsz, 1, 3, D))
        mods.append(jnp.pad(jnp.concatenate([con, lat], axis=1), ((0, 0), (0, 0), (0, 5), (0, 0))))
    pool_scale_full = misc_all[:, DEPTH * nrow:DEPTH * nrow + 2, :pool_scale.shape[1]].transpose(1, 0, 2).reshape(2, D)

    first, shard0, axis0 = BIG[0]
    w_in = weights[first].astype(BF16)
    even = [dict(sink=attn_sink[j], ssm=tuple(weights[n][j] for n in SSM_NAMES), glu_b=glu_b[j].reshape(1, SSM_W))
            for j in range(2)]
    even[0]["w_in"] = _from_shards(xy_exchange([w_in[0]], False, "gather_w_in")[0], shard0[1:], axis0 - 1)
    odd = [dict(pool_scale=pool_scale_full[j].reshape(1, D)) for j in range(2)]

    def fill(gathered):
        even[1]["w_in"] = _from_shards(gathered[0], shard0[1:], axis0 - 1)
        full = {n: _from_shards(g, shard, axis) for (n, shard, axis), g in zip(BIG[1:], gathered[1:])}
        for j in range(2):
            even[j].update(w_out=full["even_w_out"][j], glu_w=full["glu_w"][j])
            odd[j].update(w_in=full["odd_w_in"][j], w_out=full["odd_w_out"][j], pool_w=full["pool_w"][j])

    late = ([w_in[1]] + [weights[n].astype(BF16) for n, _, _ in BIG[1:]], fill)
    loss_local, grad_x, dmods, grads = local_step(
        x, ctx, loss_target, mods, norm_g, final_g, even, odd, late,
        scatter=lambda grads: [_grad_piece(grads, n, j) for n, j in RIDE])
    loss = lax.psum(loss_local, ("x", "y", "c"))

    d_rows = jnp.stack([jnp.concatenate([dm[:, 1].reshape(bsz, 3 * D), dm[:, 0].reshape(bsz, 3 * D),
                                         jnp.zeros((C_ROWS - 2 * bsz, 3 * D), F32)]) for dm in dmods])
    d_all = allgather8(d_rows.reshape(DEPTH * C_ROWS, 3 * D), "gather_dmod")
    d_all = d_all.reshape(N_DEV, DEPTH, C_ROWS, 3 * D).transpose(1, 0, 2, 3).reshape(DEPTH, nrow, 3 * D)
    d_cols = lax.dynamic_slice(d_all, (0, 0, pos * ADA_COLS), (DEPTH, nrow, ADA_COLS))
    g_ada_w, ds_ctx = ada_bwd(c_all, d_cols, ada_w, "ada_bwd")
    g_ada_b = ada_bias_grad(d_all, "ada_bias_grad")[:, 0]

    small = dict(ds_ctx=ds_ctx[0] * (ac == 0).astype(F32), norm_g=jnp.stack(grads["norm_g"]), final_g=grads["final_g"],
                 attn_sink=jnp.stack([grads["even"][j]["sink"] for j in range(2)]),
                 glu_b=jnp.stack([grads["even"][j]["glu_b"] for j in range(2)]),
                 pool_scale=jnp.stack([grads["odd"][j]["pool_scale"] for j in range(2)]))
    for k, n in enumerate(SSM_NAMES):
        small[n] = jnp.stack([grads["even"][j]["ssm"][k] for j in range(2)])
    packed = pack_small(small)
    small_sum = sum_slots(allgather8(packed, "gather_small").reshape(N_DEV, packed.shape[0], 128), "sum_small")
    g_small = unpack_small(small_sum)
    g_small["c_ctx"] = silu_chain(g_small.pop("ds_ctx").reshape(1, D), c_ctx.reshape(1, D), "c_ctx_grad").reshape(D)
    g_small["ada_b"] = g_ada_b
    g_small["pool_scale"] = lax.dynamic_slice(g_small["pool_scale"], (0, pos * 256), (2, 256))

    landed = dict(zip(RIDE, grads["landed"]))
    landed.update(zip(LAST, xy_exchange([_grad_piece(grads, n, j) for n, j in LAST], True, "scatter_grads")))
    mine4 = [jnp.stack([sum_slots(landed[n, j].reshape(4, -1, shard[-1]), f"sum_positions_{n}{j}").reshape(shard[1:])
                        for j in range(2)]) for n, shard, _ in BIG]
    other4 = sibling_exchange(mine4, "swap_cores")
    g_mine = dict(zip([n for n, _, _ in BIG], mine4))
    g_other = dict(zip([n for n, _, _ in BIG], other4))

    results = {}
    for n in WEIGHT_NAMES:
        if n in g_mine:
            gs = [g_mine[n], g_other[n]]
        elif n == "ada_w":
            gs = [g_ada_w]
        else:
            gs = [g_small[n]]
        results[n] = adamw(weights[n], env["m_" + n], env["v_" + n], gs, "adamw_" + n)
    outs = [loss, grad_x]
    for k in range(4):
        outs += [results[n][k] for n in WEIGHT_NAMES]
    return tuple(outs)
```
